```python
import math
import jax, jax.numpy as jnp
from jax import lax
import numpy as np

D_MODEL = 1024
BATCH = 16
SEQ = 2048
DEPTH = 1

ATT_HEAD_DIM = 64
ATT_HEADS_PER_GROUP = 12
DILATED_PATTERNS = ((128, 1), (512, 4), (2048, 16))
N_ATT_GROUPS = 3
ATT_HEADS = N_ATT_GROUPS * ATT_HEADS_PER_GROUP
ATT_QKV = ATT_HEADS * ATT_HEAD_DIM
ATT_OUT = ATT_HEADS_PER_GROUP * ATT_HEAD_DIM
BAND_BLOCK = 128
NUM_BUCKETS = 32
MAX_DISTANCE = 2048
SSM_EXPAND = 2
D_INNER = SSM_EXPAND * D_MODEL
SSM_HEAD_DIM = 64
SSM_HEADS = D_INNER // SSM_HEAD_DIM
SSM_GROUPS = 4
D_STATE = 128
CONV_WIDTH = 4
CONV_DIM = D_INNER + 2 * SSM_GROUPS * D_STATE
SSD_CHUNK = 128
PLE_DIM = 256
ALPHA = (2.0 * DEPTH) ** 0.25
BETA = (8.0 * DEPTH) ** -0.25
LN_EPS = 1e-5
RMS_EPS = 1e-5
Q_END = ATT_QKV
K_END = Q_END + ATT_QKV
V_END = K_END + ATT_QKV
GATT_END = V_END + ATT_OUT
Z_END = GATT_END + D_INNER
XBC_END = Z_END + CONV_DIM
DT_END = XBC_END + SSM_HEADS
GMERGE_END = DT_END + 2 * D_MODEL
IN_COLS = GMERGE_END + D_MODEL
BRANCH_ROWS = ATT_OUT + D_INNER

kernel_name = "hybrid_dilated_attn_mamba2_deepnorm"


def _layer_norm(x, g, b):
    xf = x.astype(jnp.float32)
    mu = jnp.mean(xf, -1, keepdims=True)
    var = jnp.mean(jnp.square(xf - mu), -1, keepdims=True)
    return ((xf - mu) * lax.rsqrt(var + LN_EPS) * g.astype(jnp.float32) + b.astype(jnp.float32)).astype(x.dtype)


def _t5_bucket(dist):
    max_exact = NUM_BUCKETS // 2
    d_f = jnp.maximum(dist, 1).astype(jnp.float32)
    large = max_exact + (jnp.log(d_f / max_exact) / math.log(MAX_DISTANCE / max_exact)
                         * (NUM_BUCKETS - max_exact)).astype(jnp.int32)
    large = jnp.minimum(large, NUM_BUCKETS - 1)
    return jnp.where(dist < max_exact, dist, large)


def _dilated_group(q, k, v, bias_table, window, dilation):
    b, s, h, dh = q.shape
    span = dilation * BAND_BLOCK
    s_pad = -(-s // span) * span
    sub_len = s_pad // dilation
    nb = sub_len // BAND_BLOCK

    def to_blocks(t):
        t = jnp.pad(t, ((0, 0), (0, s_pad - s), (0, 0), (0, 0)))
        t = t.reshape(b, sub_len, dilation, h, dh).transpose(0, 2, 3, 1, 4)
        return t.reshape(b, dilation, h, nb, BAND_BLOCK, dh)

    def band(t):
        prev = jnp.pad(t, ((0, 0), (0, 0), (0, 0), (1, 0), (0, 0), (0, 0)))[:, :, :, :-1]
        return jnp.concatenate([prev, t], axis=4)

    qb = to_blocks(q)
    kk = band(to_blocks(k))
    vv = band(to_blocks(v))
    scores = jnp.einsum('brhnqd,brhnkd->brhnqk', qb, kk,
                        preferred_element_type=jnp.float32) * (dh ** -0.5)
    qi = jnp.arange(BAND_BLOCK)[:, None]
    kj = jnp.arange(2 * BAND_BLOCK)[None, :]
    delta = qi + BAND_BLOCK - kj
    blk = jnp.arange(nb)[:, None, None]
    valid = (delta >= 0) & (delta <= window // dilation) & (blk * BAND_BLOCK + kj - BAND_BLOCK >= 0)
    bucket = _t5_bucket(jnp.maximum(delta, 0) * dilation)
    bias = bias_table[bucket].astype(jnp.float32).transpose(2, 0, 1)
    scores = jnp.where(valid, scores + bias[:, None], -jnp.inf)
    m = jnp.max(scores, -1, keepdims=True)
    e = jnp.exp(scores - m)
    den = jnp.sum(e, -1)
    out = jnp.einsum('brhnqk,brhnkd->brhnqd', e, vv.astype(jnp.float32)) / den[..., None]
    lse = m[..., 0] + jnp.log(den)
    out = out.reshape(b, dilation, h, sub_len, dh).transpose(0, 3, 1, 2, 4).reshape(b, s_pad, h, dh)[:, :s]
    lse = lse.reshape(b, dilation, h, sub_len).transpose(0, 3, 1, 2).reshape(b, s_pad, h)[:, :s]
    return out, lse


def _causal_conv(u, w, bias):
    c = u.shape[-1]
    out = lax.conv_general_dilated(u, w[:, None, :].astype(u.dtype), window_strides=(1,),
                                   padding=[(CONV_WIDTH - 1, 0)],
                                   dimension_numbers=('NWC', 'WIO', 'NWC'),
                                   feature_group_count=c)
    return out + bias.astype(u.dtype)


def _ssd(xh, dt, a_head, bm, cm):
    b, s, h, p = xh.shape
    g, n = bm.shape[2], bm.shape[3]
    r = h // g
    nc = s // SSD_CHUNK
    lq = SSD_CHUNK
    a = (dt * a_head).reshape(b, nc, lq, h)
    a_cs = jnp.cumsum(a, axis=2)
    xdt = (xh * dt[..., None]).reshape(b, nc, lq, g, r, p)
    bc = bm.reshape(b, nc, lq, g, n)
    cc = cm.reshape(b, nc, lq, g, n)
    seg = a_cs[:, :, :, None, :] - a_cs[:, :, None, :, :]
    causal = jnp.tril(jnp.ones((lq, lq), dtype=bool))
    lmat = jnp.exp(jnp.where(causal[:, :, None], seg, -jnp.inf)).reshape(b, nc, lq, lq, g, r)
    cb = jnp.einsum('bclgn,bcsgn->bclsg', cc, bc)
    y_diag = jnp.einsum('bclsgr,bcsgrp->bclgrp', cb[..., None] * lmat, xdt)
    decay_to_end = jnp.exp(a_cs[:, :, -1:, :] - a_cs).reshape(b, nc, lq, g, r)
    states = jnp.einsum('bclgn,bclgrp->bcgrpn', bc, xdt * decay_to_end[..., None])
    chunk_decay = jnp.exp(a_cs[:, :, -1, :]).reshape(b, nc, g, r)

    def step(carry, inp):
        st, dec = inp
        return carry * dec[..., None, None] + st, carry

    init = jnp.zeros((b, g, r, p, n), jnp.float32)
    _, prev_states = lax.scan(step, init, (jnp.moveaxis(states, 1, 0), jnp.moveaxis(chunk_decay, 1, 0)))
    prev_states = jnp.moveaxis(prev_states, 0, 1)
    decay_from_start = jnp.exp(a_cs).reshape(b, nc, lq, g, r)
    y_off = jnp.einsum('bclgn,bcgrpn->bclgrp', cc, prev_states) * decay_from_start[..., None]
    return (y_diag + y_off).reshape(b, s, h, p)


def _hybrid_layer(x, p_i, w_in, b_gate, conv_w, conv_b, dt_bias, a_log, d_skip, ssm_norm_w,
                  w_branch, w_out, w_ple, ln_g, ln_b, rel_bias):
    b, s, _ = x.shape
    hcat = jnp.einsum('bsd,de->bse', x, w_in)
    q, k, v, g_att, z, xbc, dt_raw, g_merge, g_ple = jnp.split(
        hcat, [Q_END, K_END, V_END, GATT_END, Z_END, XBC_END, DT_END, GMERGE_END], axis=-1)

    q = q.reshape(b, s, N_ATT_GROUPS, ATT_HEADS_PER_GROUP, ATT_HEAD_DIM)
    k = k.reshape(b, s, N_ATT_GROUPS, ATT_HEADS_PER_GROUP, ATT_HEAD_DIM)
    v = v.reshape(b, s, N_ATT_GROUPS, ATT_HEADS_PER_GROUP, ATT_HEAD_DIM)
    outs, lses = [], []
    for gi, (win, dil) in enumerate(DILATED_PATTERNS):
        hs = slice(gi * ATT_HEADS_PER_GROUP, (gi + 1) * ATT_HEADS_PER_GROUP)
        o, l = _dilated_group(q[:, :, gi], k[:, :, gi], v[:, :, gi], rel_bias[:, hs], win, dil)
        outs.append(o)
        lses.append(l)
    wts = jax.nn.softmax(jnp.stack(lses), axis=0)
    o_att = jnp.sum(wts[..., None] * jnp.stack(outs), axis=0).reshape(b, s, ATT_OUT).astype(x.dtype)
    o_att = o_att * jax.nn.silu(g_att)

    xbc = jax.nn.silu(_causal_conv(xbc, conv_w, conv_b))
    xs, bm, cm = jnp.split(xbc, [D_INNER, D_INNER + SSM_GROUPS * D_STATE], axis=-1)
    xh = xs.astype(jnp.float32).reshape(b, s, SSM_HEADS, SSM_HEAD_DIM)
    bm = bm.astype(jnp.float32).reshape(b, s, SSM_GROUPS, D_STATE)
    cm = cm.astype(jnp.float32).reshape(b, s, SSM_GROUPS, D_STATE)
    dt = jax.nn.softplus(dt_raw.astype(jnp.float32) + dt_bias.astype(jnp.float32))
    a_head = -jnp.exp(a_log.astype(jnp.float32))
    y = _ssd(xh, dt, a_head, bm, cm) + d_skip.astype(jnp.float32)[:, None] * xh
    u = (y.reshape(b, s, D_INNER) * jax.nn.silu(z.astype(jnp.float32))).reshape(b, s, SSM_GROUPS, -1)
    u = u * lax.rsqrt(jnp.mean(jnp.square(u), -1, keepdims=True) + RMS_EPS)
    y_ssm = (u.reshape(b, s, D_INNER) * ssm_norm_w.astype(jnp.float32)).astype(x.dtype)

    y_a = jnp.einsum('bse,ed->bsd', o_att, w_branch[:ATT_OUT])
    y_b = jnp.einsum('bse,ed->bsd', y_ssm, w_branch[ATT_OUT:])
    g_a, g_b = jnp.split(g_merge, 2, axis=-1)
    merged = jax.nn.sigmoid(g_a + b_gate[0]) * y_a + jax.nn.sigmoid(g_b + b_gate[1]) * y_b
    mix = jnp.einsum('bsd,de->bse', merged, w_out)
    ple = jax.nn.sigmoid(g_ple + b_gate[2]) * jnp.einsum('bsq,qd->bsd', p_i, w_ple)
    return _layer_norm(ALPHA * x + mix + ple, ln_g, ln_b)


def _fwd_setup_inputs(seed: int = 0) -> dict:
    key = jax.random.key(seed)
    ks = jax.random.split(key, 16)
    f32 = jnp.float32
    x = jax.random.normal(ks[0], (BATCH, SEQ, D_MODEL), f32)
    p = jax.random.normal(ks[1], (DEPTH, BATCH, SEQ, PLE_DIM), f32)
    col_scale = jnp.ones((IN_COLS,), f32).at[K_END:V_END].set(BETA).at[Z_END:Z_END + D_INNER].set(BETA)
    w_in = jax.random.normal(ks[2], (DEPTH, D_MODEL, IN_COLS), f32) * (D_MODEL ** -0.5) * col_scale
    b_gate = 0.1 * jax.random.normal(ks[3], (DEPTH, 3, D_MODEL), f32)
    conv_w = 0.5 * jax.random.normal(ks[4], (DEPTH, CONV_WIDTH, CONV_DIM), f32)
    conv_b = 0.05 * jax.random.normal(ks[5], (DEPTH, CONV_DIM), f32)
    dt0 = jnp.exp(jax.random.uniform(ks[6], (DEPTH, SSM_HEADS), f32, math.log(1e-3), math.log(1e-1)))
    dt_bias = dt0 + jnp.log(-jnp.expm1(-dt0))
    a_log = jnp.log(jax.random.uniform(ks[7], (DEPTH, SSM_HEADS), f32, 1.0, 16.0))
    d_skip = 1.0 + 0.1 * jax.random.normal(ks[8], (DEPTH, SSM_HEADS), f32)
    ssm_norm_w = 1.0 + 0.05 * jax.random.normal(ks[9], (DEPTH, D_INNER), f32)
    w_branch = jnp.concatenate([
        jax.random.normal(ks[10], (DEPTH, ATT_OUT, D_MODEL), f32) * (ATT_OUT ** -0.5),
        jax.random.normal(ks[11], (DEPTH, D_INNER, D_MODEL), f32) * (D_INNER ** -0.5)], axis=1) * BETA
    w_out = jax.random.normal(ks[12], (DEPTH, D_MODEL, D_MODEL), f32) * (D_MODEL ** -0.5) * BETA
    w_ple = jax.random.normal(ks[13], (DEPTH, PLE_DIM, D_MODEL), f32) * (PLE_DIM ** -0.5) * BETA
    kg, kb = jax.random.split(ks[14])
    ln_g = 1.0 + 0.05 * jax.random.normal(kg, (DEPTH, D_MODEL), f32)
    ln_b = 0.02 * jax.random.normal(kb, (DEPTH, D_MODEL), f32)
    rel_bias = 0.2 * jax.random.normal(ks[15], (NUM_BUCKETS, ATT_HEADS), f32)
    return {"x": x, "p": p, "w_in": w_in, "b_gate": b_gate, "conv_w": conv_w, "conv_b": conv_b,
            "dt_bias": dt_bias, "a_log": a_log, "d_skip": d_skip, "ssm_norm_w": ssm_norm_w,
            "w_branch": w_branch, "w_out": w_out, "w_ple": w_ple, "ln_g": ln_g, "ln_b": ln_b,
            "rel_bias": rel_bias}


def _fwd_reference(x, p, w_in, b_gate, conv_w, conv_b, dt_bias, a_log, d_skip, ssm_norm_w,
              w_branch, w_out, w_ple, ln_g, ln_b, rel_bias):
    for i in range(DEPTH):
        x = _hybrid_layer(x, p[i], w_in[i], b_gate[i], conv_w[i], conv_b[i], dt_bias[i], a_log[i],
                          d_skip[i], ssm_norm_w[i], w_branch[i], w_out[i], w_ple[i], ln_g[i], ln_b[i],
                          rel_bias)
    return x


import jax as _jax
import jax.numpy as _jnp

TWIN_FORMAT = 'train_step'
FWD_PARAMS = ['x', 'p', 'w_in', 'b_gate', 'conv_w', 'conv_b', 'dt_bias', 'a_log', 'd_skip', 'ssm_norm_w', 'w_branch', 'w_out', 'w_ple', 'ln_g', 'ln_b', 'rel_bias']
TWIN_WEIGHTS = ['w_in', 'b_gate', 'conv_w', 'conv_b', 'dt_bias', 'a_log', 'd_skip', 'ssm_norm_w', 'w_branch', 'w_out', 'w_ple', 'ln_g', 'ln_b', 'rel_bias']
TWIN_DIFF_INPUT = 'x'
TWIN_INPUTS = ['x', 'p', 'w_in', 'b_gate', 'conv_w', 'conv_b', 'dt_bias', 'a_log', 'd_skip', 'ssm_norm_w', 'w_branch', 'w_out', 'w_ple', 'ln_g', 'ln_b', 'rel_bias', 'loss_target', 'm_w_in', 'm_b_gate', 'm_conv_w', 'm_conv_b', 'm_dt_bias', 'm_a_log', 'm_d_skip', 'm_ssm_norm_w', 'm_w_branch', 'm_w_out', 'm_w_ple', 'm_ln_g', 'm_ln_b', 'm_rel_bias', 'v_w_in', 'v_b_gate', 'v_conv_w', 'v_conv_b', 'v_dt_bias', 'v_a_log', 'v_d_skip', 'v_ssm_norm_w', 'v_w_branch', 'v_w_out', 'v_w_ple', 'v_ln_g', 'v_ln_b', 'v_rel_bias']
TWIN_OUTPUTS = ['loss', 'grad_x', 'grad_w_in', 'grad_b_gate', 'grad_conv_w', 'grad_conv_b', 'grad_dt_bias', 'grad_a_log', 'grad_d_skip', 'grad_ssm_norm_w', 'grad_w_branch', 'grad_w_out', 'grad_w_ple', 'grad_ln_g', 'grad_ln_b', 'grad_rel_bias', 'delta_w_in', 'delta_b_gate', 'delta_conv_w', 'delta_conv_b', 'delta_dt_bias', 'delta_a_log', 'delta_d_skip', 'delta_ssm_norm_w', 'delta_w_branch', 'delta_w_out', 'delta_w_ple', 'delta_ln_g', 'delta_ln_b', 'delta_rel_bias', 'new_m_w_in', 'new_m_b_gate', 'new_m_conv_w', 'new_m_conv_b', 'new_m_dt_bias', 'new_m_a_log', 'new_m_d_skip', 'new_m_ssm_norm_w', 'new_m_w_branch', 'new_m_w_out', 'new_m_w_ple', 'new_m_ln_g', 'new_m_ln_b', 'new_m_rel_bias', 'new_v_w_in', 'new_v_b_gate', 'new_v_conv_w', 'new_v_conv_b', 'new_v_dt_bias', 'new_v_a_log', 'new_v_d_skip', 'new_v_ssm_norm_w', 'new_v_w_branch', 'new_v_w_out', 'new_v_w_ple', 'new_v_ln_g', 'new_v_ln_b', 'new_v_rel_bias']
TWIN_LEAF_KINDS = {'loss': 'loss', 'grad_x': 'grad_x', 'grad_w_in': 'grad_w', 'grad_b_gate': 'grad_w', 'grad_conv_w': 'grad_w', 'grad_conv_b': 'grad_w', 'grad_dt_bias': 'grad_w', 'grad_a_log': 'grad_w', 'grad_d_skip': 'grad_w', 'grad_ssm_norm_w': 'grad_w', 'grad_w_branch': 'grad_w', 'grad_w_out': 'grad_w', 'grad_w_ple': 'grad_w', 'grad_ln_g': 'grad_w', 'grad_ln_b': 'grad_w', 'grad_rel_bias': 'grad_w', 'delta_w_in': 'delta_w', 'delta_b_gate': 'delta_w', 'delta_conv_w': 'delta_w', 'delta_conv_b': 'delta_w', 'delta_dt_bias': 'delta_w', 'delta_a_log': 'delta_w', 'delta_d_skip': 'delta_w', 'delta_ssm_norm_w': 'delta_w', 'delta_w_branch': 'delta_w', 'delta_w_out': 'delta_w', 'delta_w_ple': 'delta_w', 'delta_ln_g': 'delta_w', 'delta_ln_b': 'delta_w', 'delta_rel_bias': 'delta_w', 'new_m_w_in': 'new_m', 'new_m_b_gate': 'new_m', 'new_m_conv_w': 'new_m', 'new_m_conv_b': 'new_m', 'new_m_dt_bias': 'new_m', 'new_m_a_log': 'new_m', 'new_m_d_skip': 'new_m', 'new_m_ssm_norm_w': 'new_m', 'new_m_w_branch': 'new_m', 'new_m_w_out': 'new_m', 'new_m_w_ple': 'new_m', 'new_m_ln_g': 'new_m', 'new_m_ln_b': 'new_m', 'new_m_rel_bias': 'new_m', 'new_v_w_in': 'new_v', 'new_v_b_gate': 'new_v', 'new_v_conv_w': 'new_v', 'new_v_conv_b': 'new_v', 'new_v_dt_bias': 'new_v', 'new_v_a_log': 'new_v', 'new_v_d_skip': 'new_v', 'new_v_ssm_norm_w': 'new_v', 'new_v_w_branch': 'new_v', 'new_v_w_out': 'new_v', 'new_v_w_ple': 'new_v', 'new_v_ln_g': 'new_v', 'new_v_ln_b': 'new_v', 'new_v_rel_bias': 'new_v'}


def _forward(args):
    return _fwd_reference(*[args[k] for k in FWD_PARAMS])


def _output_shape():
    out = _jax.eval_shape(lambda: _forward(_fwd_setup_inputs(0)))
    return out.shape, out.dtype

N_MICROBATCH = 1
ADAM_LR = 0.001
ADAM_B1 = 0.9
ADAM_B2 = 0.999
ADAM_EPS = 1e-08
ADAM_WD = 0.01
ADAM_STEP = 10
PER_EXAMPLE_BATCH_AXIS = {'x': 0, 'p': 1, 'loss_target': 0}
SHARED_INPUTS = []
_WEIGHT_DTYPES = {'w_in': _jnp.float32, 'b_gate': _jnp.float32, 'conv_w': _jnp.float32, 'conv_b': _jnp.float32, 'dt_bias': _jnp.float32, 'a_log': _jnp.float32, 'd_skip': _jnp.float32, 'ssm_norm_w': _jnp.float32, 'w_branch': _jnp.float32, 'w_out': _jnp.float32, 'w_ple': _jnp.float32, 'ln_g': _jnp.float32, 'ln_b': _jnp.float32, 'rel_bias': _jnp.float32}
MOMENT_SCALE = {'w_in': 1.496837e-02, 'b_gate': 3.243592e-02, 'conv_w': 1.697356e-02, 'conv_b': 5.435886e-02, 'dt_bias': 5.577908e-02, 'a_log': 6.495707e-02, 'd_skip': 1.017205e-01, 'ssm_norm_w': 2.073553e-02, 'w_branch': 4.077721e-02, 'w_out': 4.803110e-02, 'w_ple': 7.757451e-02, 'ln_g': 3.187174e+01, 'ln_b': 7.532662e-01, 'rel_bias': 1.430785e-03}


def _to_microbatches(a, axis):
    t = _jnp.moveaxis(a, axis, 0)
    t = t.reshape((N_MICROBATCH, t.shape[0] // N_MICROBATCH) + t.shape[1:])
    return _jnp.moveaxis(t, 1, axis + 1)


def setup_inputs(seed: int = 0) -> dict:
    inp = _fwd_setup_inputs(seed)
    key = _jax.random.fold_in(_jax.random.key(seed), 7919)
    shape, _ = _output_shape()
    out = dict(inp)
    out["loss_target"] = _jax.random.normal(_jax.random.fold_in(key, 0), shape, _jnp.float32)
    for i, name in enumerate(TWIN_WEIGHTS):
        w = inp[name].astype(_jnp.float32)
        if MOMENT_SCALE is None:
            s = _jnp.sqrt(_jnp.mean(_jnp.square(w)) + 1e-30)
        else:
            s = MOMENT_SCALE[name]
        km, kv = _jax.random.split(_jax.random.fold_in(key, i + 1))
        out[name] = w
        out["m_" + name] = s * _jax.random.normal(km, w.shape, _jnp.float32)
        out["v_" + name] = (s * s) * _jax.random.uniform(kv, w.shape, _jnp.float32, 0.5, 1.5)
    if N_MICROBATCH > 1:
        for name, axis in PER_EXAMPLE_BATCH_AXIS.items():
            out[name] = _to_microbatches(out[name], axis)
    return {'x': out['x'], 'p': out['p'], 'w_in': out['w_in'], 'b_gate': out['b_gate'], 'conv_w': out['conv_w'], 'conv_b': out['conv_b'], 'dt_bias': out['dt_bias'], 'a_log': out['a_log'], 'd_skip': out['d_skip'], 'ssm_norm_w': out['ssm_norm_w'], 'w_branch': out['w_branch'], 'w_out': out['w_out'], 'w_ple': out['w_ple'], 'ln_g': out['ln_g'], 'ln_b': out['ln_b'], 'rel_bias': out['rel_bias'], 'loss_target': out['loss_target'], 'm_w_in': out['m_w_in'], 'm_b_gate': out['m_b_gate'], 'm_conv_w': out['m_conv_w'], 'm_conv_b': out['m_conv_b'], 'm_dt_bias': out['m_dt_bias'], 'm_a_log': out['m_a_log'], 'm_d_skip': out['m_d_skip'], 'm_ssm_norm_w': out['m_ssm_norm_w'], 'm_w_branch': out['m_w_branch'], 'm_w_out': out['m_w_out'], 'm_w_ple': out['m_w_ple'], 'm_ln_g': out['m_ln_g'], 'm_ln_b': out['m_ln_b'], 'm_rel_bias': out['m_rel_bias'], 'v_w_in': out['v_w_in'], 'v_b_gate': out['v_b_gate'], 'v_conv_w': out['v_conv_w'], 'v_conv_b': out['v_conv_b'], 'v_dt_bias': out['v_dt_bias'], 'v_a_log': out['v_a_log'], 'v_d_skip': out['v_d_skip'], 'v_ssm_norm_w': out['v_ssm_norm_w'], 'v_w_branch': out['v_w_branch'], 'v_w_out': out['v_w_out'], 'v_w_ple': out['v_w_ple'], 'v_ln_g': out['v_ln_g'], 'v_ln_b': out['v_ln_b'], 'v_rel_bias': out['v_rel_bias']}


def _loss(weights, diff, rest, loss_target):
    with _jax.named_scope("forward"):
        args = {**rest, TWIN_DIFF_INPUT: diff, **{k: w.astype(_WEIGHT_DTYPES[k]) for k, w in weights.items()}}
        y = _forward(args)
    with _jax.named_scope("loss_head"):
        err = _jnp.square(y.astype(_jnp.float32) - loss_target)
        return 0.5 * _jnp.sum(_jnp.mean(err, axis=-1)) if err.ndim else 0.5 * err


def _adamw(w, g, m, v):
    m = ADAM_B1 * m + (1.0 - ADAM_B1) * g
    v = ADAM_B2 * v + (1.0 - ADAM_B2) * _jnp.square(g)
    m_hat = m / (1.0 - ADAM_B1 ** ADAM_STEP)
    v_hat = v / (1.0 - ADAM_B2 ** ADAM_STEP)
    delta = -ADAM_LR * (m_hat / (_jnp.sqrt(v_hat) + ADAM_EPS) + ADAM_WD * w)
    return delta, m, v


def reference(x, p, w_in, b_gate, conv_w, conv_b, dt_bias, a_log, d_skip, ssm_norm_w, w_branch, w_out, w_ple, ln_g, ln_b, rel_bias, loss_target, m_w_in, m_b_gate, m_conv_w, m_conv_b, m_dt_bias, m_a_log, m_d_skip, m_ssm_norm_w, m_w_branch, m_w_out, m_w_ple, m_ln_g, m_ln_b, m_rel_bias, v_w_in, v_b_gate, v_conv_w, v_conv_b, v_dt_bias, v_a_log, v_d_skip, v_ssm_norm_w, v_w_branch, v_w_out, v_w_ple, v_ln_g, v_ln_b, v_rel_bias):
    given = dict(x=x, p=p, w_in=w_in, b_gate=b_gate, conv_w=conv_w, conv_b=conv_b, dt_bias=dt_bias, a_log=a_log, d_skip=d_skip, ssm_norm_w=ssm_norm_w, w_branch=w_branch, w_out=w_out, w_ple=w_ple, ln_g=ln_g, ln_b=ln_b, rel_bias=rel_bias, loss_target=loss_target, m_w_in=m_w_in, m_b_gate=m_b_gate, m_conv_w=m_conv_w, m_conv_b=m_conv_b, m_dt_bias=m_dt_bias, m_a_log=m_a_log, m_d_skip=m_d_skip, m_ssm_norm_w=m_ssm_norm_w, m_w_branch=m_w_branch, m_w_out=m_w_out, m_w_ple=m_w_ple, m_ln_g=m_ln_g, m_ln_b=m_ln_b, m_rel_bias=m_rel_bias, v_w_in=v_w_in, v_b_gate=v_b_gate, v_conv_w=v_conv_w, v_conv_b=v_conv_b, v_dt_bias=v_dt_bias, v_a_log=v_a_log, v_d_skip=v_d_skip, v_ssm_norm_w=v_ssm_norm_w, v_w_branch=v_w_branch, v_w_out=v_w_out, v_w_ple=v_w_ple, v_ln_g=v_ln_g, v_ln_b=v_ln_b, v_rel_bias=v_rel_bias)
    weights = {n: given[n] for n in TWIN_WEIGHTS}
    shared = {n: given[n] for n in SHARED_INPUTS}
    per_example = {n: given[n] for n in ['x', 'p']}
    grad_fn = _jax.value_and_grad(_loss, argnums=(0, 1))

    def one_microbatch(ex, loss_target):
        ex = dict(ex)
        diff = ex.pop(TWIN_DIFF_INPUT)
        return grad_fn(weights, diff, {**shared, **ex}, loss_target)

    if N_MICROBATCH == 1:
        loss, (grad_w, grad_x) = one_microbatch(per_example, given["loss_target"])
    else:
        def body(carry, xs):
            loss_sum, grad_sum = carry
            l_k, (gw_k, gx_k) = one_microbatch(xs[0], xs[1])
            with _jax.named_scope("update"):
                return (loss_sum + l_k, _jax.tree.map(_jnp.add, grad_sum, gw_k)), gx_k

        init = (_jnp.zeros((), _jnp.float32), _jax.tree.map(_jnp.zeros_like, weights))
        (loss, grad_w), grad_x = _jax.lax.scan(body, init, (per_example, given["loss_target"]))
    with _jax.named_scope("update"):
        delta_w, new_m, new_v = {}, {}, {}
        for n in TWIN_WEIGHTS:
            delta_w[n], new_m[n], new_v[n] = _adamw(weights[n], grad_w[n], given["m_" + n], given["v_" + n])
    return (loss, grad_x, *[grad_w[n] for n in TWIN_WEIGHTS], *[delta_w[n] for n in TWIN_WEIGHTS],
            *[new_m[n] for n in TWIN_WEIGHTS], *[new_v[n] for n in TWIN_WEIGHTS])
```

```python
import functools
import math

import jax
import jax.numpy as jnp
from jax import lax
from jax.experimental import pallas as pl
from jax.experimental.pallas import tpu as pltpu

F32 = jnp.float32
BF16 = jnp.bfloat16

N_DEV = 8
D_MODEL = 1024
SEQ = 2048
HEAD_DIM = 64
HEADS_PER_GROUP = 12
DILATED_PATTERNS = ((128, 1), (512, 4), (2048, 16))
N_GROUPS = 3
ATT_HEADS = N_GROUPS * HEADS_PER_GROUP
GROUP_COLS = HEADS_PER_GROUP * HEAD_DIM
ATT_QKV = ATT_HEADS * HEAD_DIM
BLK = 128
NUM_BUCKETS = 32
MAX_DISTANCE = 2048
D_INNER = 2048
SSM_HEADS = 32
SSM_GROUPS = 4
HEADS_PER_SSM_GROUP = SSM_HEADS // SSM_GROUPS
D_STATE = 128
CONV_WIDTH = 4
CONV_DIM = D_INNER + 2 * SSM_GROUPS * D_STATE
PLE_DIM = 256
ALPHA = 2.0 ** 0.25
LN_EPS = 1e-5
RMS_EPS = 1e-5
IN_COLS = 15904
IN_SHARD = IN_COLS // N_DEV
BRANCH_ROWS = GROUP_COLS + D_INNER

Q_END = ATT_QKV
K_END = 2 * ATT_QKV
V_END = 3 * ATT_QKV
GATT_END = V_END + GROUP_COLS
Z_END = GATT_END + D_INNER
XBC_END = Z_END + CONV_DIM
DT_END = XBC_END + SSM_HEADS
GMERGE_END = DT_END + 2 * D_MODEL

QKV_G = 3 * GROUP_COLS
REST_OFF = N_GROUPS * QKV_G
R_XBC, R_Z, R_GM, R_GPLE, R_GATT, R_DT = 0, 3072, 5120, 7168, 8192, 8960
REST_COLS = 9216
HCAT_COLS = REST_OFF + REST_COLS

ADAM_LR, ADAM_B1, ADAM_B2, ADAM_EPS, ADAM_WD, ADAM_STEP = 0.001, 0.9, 0.999, 1e-08, 0.01, 10

VMEM_LIMIT_BYTES = 56 * 1024 * 1024


def _params(*semantics):
    return pltpu.CompilerParams(dimension_semantics=semantics, vmem_limit_bytes=VMEM_LIMIT_BYTES)


def _sigmoid(v):
    return 1.0 / (1.0 + jnp.exp(-v))


def _silu_and_grad(v):
    s = _sigmoid(v)
    return v * s, s * (1.0 + v * (1.0 - s))


def _matmul(a, b, *, mode, out_dtype, name, tm, tn, tk, n_off=0, n=None, add=None, add_scale=1.0):
    if mode == "nn":
        (m, k), n_full = a.shape, b.shape[1]
        assert b.shape[0] == k
    elif mode == "nt":
        (m, k), n_full = a.shape, b.shape[0]
        assert b.shape[1] == k
    else:
        (k, m), n_full = a.shape, b.shape[1]
        assert b.shape[0] == k
    n = n_full if n is None else n
    assert m % tm == 0 and n % tn == 0 and k % tk == 0 and n_off % tn == 0, (name, m, n, k)
    nk = k // tk
    jo = n_off // tn
    dims = {"nn": (((1,), (0,)), ((), ())), "nt": (((1,), (1,)), ((), ())), "tn": (((0,), (0,)), ((), ()))}[mode]

    def body(*refs):
        if add is None:
            a_ref, b_ref, o_ref = refs[:3]
            add_ref = None
        else:
            a_ref, b_ref, add_ref, o_ref = refs[:4]
        acc_ref = refs[-1] if nk > 1 else None
        prod = lax.dot_general(a_ref[...].astype(BF16), b_ref[...].astype(BF16), dims, preferred_element_type=F32)

        def finish(total):
            if add_ref is not None:
                total = total + add_scale * add_ref[...]
            o_ref[...] = total.astype(out_dtype)

        if nk == 1:
            finish(prod)
        else:
            kk = pl.program_id(2)

            @pl.when(kk == 0)
            def _():
                acc_ref[...] = prod

            @pl.when(jnp.logical_and(kk > 0, kk < nk - 1))
            def _():
                acc_ref[...] += prod

            @pl.when(kk == nk - 1)
            def _():
                finish(acc_ref[...] + prod)

    if mode == "nn":
        a_spec = pl.BlockSpec((tm, tk), lambda i, j, kk: (i, kk))
        b_spec = pl.BlockSpec((tk, tn), lambda i, j, kk: (kk, j + jo))
    elif mode == "nt":
        a_spec = pl.BlockSpec((tm, tk), lambda i, j, kk: (i, kk))
        b_spec = pl.BlockSpec((tn, tk), lambda i, j, kk: (j, kk))
    else:
        a_spec = pl.BlockSpec((tk, tm), lambda i, j, kk: (kk, i))
        b_spec = pl.BlockSpec((tk, tn), lambda i, j, kk: (kk, j))
    in_specs = [a_spec, b_spec]
    args = [a, b]
    if add is not None:
        in_specs.append(pl.BlockSpec((tm, tn), lambda i, j, kk: (i, j)))
        args.append(add)
    return pl.pallas_call(
        body, name=name,
        grid=(m // tm, n // tn, nk),
        in_specs=in_specs,
        out_specs=pl.BlockSpec((tm, tn), lambda i, j, kk: (i, j)),
        out_shape=jax.ShapeDtypeStruct((m, n), out_dtype),
        scratch_shapes=[pltpu.VMEM((tm, tn), F32)] if nk > 1 else [],
        compiler_params=_params("parallel", "parallel", "arbitrary"),
    )(*args)


def _band_mask(first):
    qi = lax.broadcasted_iota(jnp.int32, (BLK, BLK if first else 2 * BLK), 0)
    kj = lax.broadcasted_iota(jnp.int32, (BLK, BLK if first else 2 * BLK), 1)
    delta = qi - kj if first else qi + BLK - kj
    return jnp.logical_and(delta >= 0, delta <= BLK)


def _attn_specs(sub_len, dil):
    hp_per = GROUP_COLS // BLK
    qkv = [pl.BlockSpec((1, sub_len, BLK), functools.partial(
        lambda hp, b, r, part: (b, 0, r * 3 * hp_per + part * hp_per + hp), part=part)) for part in range(3)]
    one = pl.BlockSpec((1, sub_len, BLK), lambda hp, b, r: (b, 0, r * hp_per + hp))
    bias = pl.BlockSpec((2, BLK, 2 * BLK), lambda hp, b, r: (hp, 0, 0))
    return qkv, one, bias


def _attn_fwd(qkv_view, bias, dil, name):
    bsz, sub_len, _ = qkv_view.shape
    nb = sub_len // BLK
    scale = HEAD_DIM ** -0.5

    def body(q_ref, k_ref, v_ref, bias_ref, o_ref, lse_ref):
        def block(n, first):
            rows = pl.ds(pl.multiple_of(n * BLK, BLK), BLK)
            keys = rows if first else pl.ds(pl.multiple_of((n - 1) * BLK, BLK), 2 * BLK)
            mask = _band_mask(first)
            for j in range(2):
                lanes = pl.ds(j * HEAD_DIM, HEAD_DIM)
                q = q_ref[0, rows, lanes]
                kb = k_ref[0, keys, lanes]
                vb = v_ref[0, keys, lanes]
                bias_j = bias_ref[j, :, BLK:] if first else bias_ref[j]
                s = lax.dot_general(q, kb, (((1,), (1,)), ((), ())), preferred_element_type=F32) * scale + bias_j
                s = jnp.where(mask, s, -jnp.inf)
                mx = jnp.max(s, axis=-1, keepdims=True)
                e = jnp.exp(s - mx)
                den = jnp.sum(e, axis=-1, keepdims=True)
                out = jnp.dot(e.astype(BF16), vb, preferred_element_type=F32) / den
                o_ref[0, rows, lanes] = out
                lse_ref[0, rows, lanes] = jnp.broadcast_to(mx + jnp.log(den), (BLK, HEAD_DIM))

        block(0, True)
        if nb > 1:
            def loop(n, carry):
                block(n, False)
                return carry
            lax.fori_loop(1, nb, loop, 0)

    qkv_specs, one, bias_spec = _attn_specs(sub_len, dil)
    shape = jax.ShapeDtypeStruct((bsz, sub_len, dil * GROUP_COLS), F32)
    return pl.pallas_call(
        body, name=name,
        grid=(GROUP_COLS // BLK, bsz, dil),
        in_specs=qkv_specs + [bias_spec],
        out_specs=[one, one],
        out_shape=[shape, shape],
        compiler_params=_params("parallel", "parallel", "parallel"),
    )(qkv_view, qkv_view, qkv_view, bias)


def _attn_bwd(qkv_view, bias, lse, d_out, dd, dil, name):
    bsz, sub_len, _ = qkv_view.shape
    nb = sub_len // BLK
    scale = HEAD_DIM ** -0.5

    def body(q_ref, k_ref, v_ref, bias_ref, lse_ref, do_ref, dd_ref, dq_ref, dk_ref, dv_ref, dbias_ref, dk_acc, dv_acc):
        b, r = pl.program_id(1), pl.program_id(2)

        @pl.when(jnp.logical_and(b == 0, r == 0))
        def _():
            dbias_ref[...] = jnp.zeros_like(dbias_ref)

        dk_acc[...] = jnp.zeros_like(dk_acc)
        dv_acc[...] = jnp.zeros_like(dv_acc)

        def block(n, first):
            rows = pl.ds(pl.multiple_of(n * BLK, BLK), BLK)
            keys = rows if first else pl.ds(pl.multiple_of((n - 1) * BLK, BLK), 2 * BLK)
            mask = _band_mask(first)
            for j in range(2):
                lanes = pl.ds(j * HEAD_DIM, HEAD_DIM)
                q = q_ref[0, rows, lanes]
                kb = k_ref[0, keys, lanes]
                vb = v_ref[0, keys, lanes]
                do = do_ref[0, rows, lanes]
                lse_col = lse_ref[0, rows, pl.ds(j * HEAD_DIM, 1)]
                delta = jnp.sum(dd_ref[0, rows, lanes], axis=-1, keepdims=True)
                bias_j = bias_ref[j, :, BLK:] if first else bias_ref[j]
                s = lax.dot_general(q, kb, (((1,), (1,)), ((), ())), preferred_element_type=F32) * scale + bias_j
                p = jnp.where(mask, jnp.exp(s - lse_col), 0.0)
                dp = lax.dot_general(do, vb, (((1,), (1,)), ((), ())), preferred_element_type=F32)
                ds = p * (dp - delta)
                ds16 = ds.astype(BF16)
                dq_ref[0, rows, lanes] = (jnp.dot(ds16, kb, preferred_element_type=F32) * scale).astype(BF16)
                dk_acc[keys, lanes] += lax.dot_general(ds16, q, (((0,), (0,)), ((), ())), preferred_element_type=F32) * scale
                dv_acc[keys, lanes] += lax.dot_general(p.astype(BF16), do, (((0,), (0,)), ((), ())), preferred_element_type=F32)
                if first:
                    dbias_ref[j, :, BLK:] += ds
                else:
                    dbias_ref[j] += ds

        block(0, True)
        if nb > 1:
            def loop(n, carry):
                block(n, False)
                return carry
            lax.fori_loop(1, nb, loop, 0)
        dk_ref[0] = dk_acc[...].astype(BF16)
        dv_ref[0] = dv_acc[...].astype(BF16)

    qkv_specs, one, bias_spec = _attn_specs(sub_len, dil)
    shape = jax.ShapeDtypeStruct((bsz, sub_len, dil * GROUP_COLS), BF16)
    return pl.pallas_call(
        body, name=name,
        grid=(GROUP_COLS // BLK, bsz, dil),
        in_specs=qkv_specs + [bias_spec, one, one, one],
        out_specs=[one, one, one, bias_spec],
        out_shape=[shape, shape, shape, jax.ShapeDtypeStruct((HEADS_PER_GROUP, BLK, 2 * BLK), F32)],
        scratch_shapes=[pltpu.VMEM((sub_len, BLK), F32), pltpu.VMEM((sub_len, BLK), F32)],
        compiler_params=_params("arbitrary", "arbitrary", "arbitrary"),
    )(qkv_view, qkv_view, qkv_view, bias, lse, d_out, dd)


def _t5_buckets(dil):
    import numpy as np
    qi = np.arange(BLK)[:, None]
    kj = np.arange(2 * BLK)[None, :]
    dist = np.maximum(qi + BLK - kj, 0) * dil
    max_exact = NUM_BUCKETS // 2
    d_f = np.maximum(dist, 1).astype(np.float32)
    large = max_exact + (np.log(d_f / np.float32(max_exact)) / np.float32(math.log(MAX_DISTANCE / max_exact))
                         * np.float32(NUM_BUCKETS - max_exact)).astype(np.int32)
    large = np.minimum(large, NUM_BUCKETS - 1)
    return np.where(dist < max_exact, dist, large).astype(np.int32).reshape(-1)


def _bias_matrix(table_g, dil):
    buckets = jnp.asarray(_t5_buckets(dil))
    bias = jnp.take(table_g.T, buckets, axis=1).reshape(HEADS_PER_GROUP, BLK, 2 * BLK)
    onehot = (buckets[None, :] == lax.broadcasted_iota(jnp.int32, (NUM_BUCKETS, 1), 0)).astype(F32)
    return bias, onehot


def _bias_grad(dbias, onehot, name):
    flat = dbias.reshape(HEADS_PER_GROUP, 2 * BLK * BLK)
    tk = 4096

    def body(oh_ref, g_ref, o_ref):
        @pl.when(pl.program_id(0) == 0)
        def _():
            o_ref[...] = jnp.zeros_like(o_ref)
        o_ref[...] += lax.dot_general(oh_ref[...], g_ref[...], (((1,), (1,)), ((), ())),
                                      preferred_element_type=F32, precision=lax.Precision.HIGHEST)

    return pl.pallas_call(
        body, name=name,
        grid=(flat.shape[1] // tk,),
        in_specs=[pl.BlockSpec((NUM_BUCKETS, tk), lambda kk: (0, kk)), pl.BlockSpec((HEADS_PER_GROUP, tk), lambda kk: (0, kk))],
        out_specs=pl.BlockSpec((NUM_BUCKETS, HEADS_PER_GROUP), lambda kk: (0, 0)),
        out_shape=jax.ShapeDtypeStruct((NUM_BUCKETS, HEADS_PER_GROUP), F32),
        compiler_params=_params("arbitrary"),
    )(onehot, flat)


def _shift_rows(u, s, down):
    if s == 0:
        return u
    rows = lax.broadcasted_iota(jnp.int32, u.shape, 0)
    n = u.shape[0]
    if down:
        return jnp.where(rows >= s, pltpu.roll(u, s, 0), 0.0)
    return jnp.where(rows < n - s, pltpu.roll(u, n - s, 0), 0.0)


def _conv_pre(u, w_ref, b_ref):
    acc = b_ref[0:1, :] + w_ref[CONV_WIDTH - 1:CONV_WIDTH, :] * u
    for s in range(1, CONV_WIDTH):
        acc = acc + w_ref[CONV_WIDTH - 1 - s:CONV_WIDTH - s, :] * _shift_rows(u, s, True)
    return acc


def _conv_fwd(rest3, conv_w, conv_b):
    bsz, seq, _ = rest3.shape

    def body(u_ref, w_ref, b_ref, o_ref):
        pre = _conv_pre(u_ref[0], w_ref, b_ref)
        o_ref[0] = pre * _sigmoid(pre)

    return pl.pallas_call(
        body, name="conv_fwd",
        grid=(bsz, CONV_DIM // BLK),
        in_specs=[pl.BlockSpec((1, seq, BLK), lambda b, c: (b, 0, R_XBC // BLK + c)),
                  pl.BlockSpec((CONV_WIDTH, BLK), lambda b, c: (0, c)),
                  pl.BlockSpec((1, BLK), lambda b, c: (0, c))],
        out_specs=pl.BlockSpec((1, seq, BLK), lambda b, c: (b, 0, c)),
        out_shape=jax.ShapeDtypeStruct((bsz, seq, CONV_DIM), F32),
        compiler_params=_params("parallel", "parallel"),
    )(rest3, conv_w, conv_b)


def _conv_bwd(rest3, conv_w, conv_b, d_act):
    bsz, seq, _ = rest3.shape

    def body(u_ref, w_ref, b_ref, g_ref, du_ref, dw_ref):
        @pl.when(pl.program_id(1) == 0)
        def _():
            dw_ref[...] = jnp.zeros_like(dw_ref)

        u = u_ref[0]
        pre = _conv_pre(u, w_ref, b_ref)
        d_pre = g_ref[0] * _silu_and_grad(pre)[1]
        du = w_ref[CONV_WIDTH - 1:CONV_WIDTH, :] * d_pre
        dw_ref[CONV_WIDTH - 1:CONV_WIDTH, :] += jnp.sum(d_pre * u, axis=0, keepdims=True)
        for s in range(1, CONV_WIDTH):
            du = du + w_ref[CONV_WIDTH - 1 - s:CONV_WIDTH - s, :] * _shift_rows(d_pre, s, False)
            dw_ref[CONV_WIDTH - 1 - s:CONV_WIDTH - s, :] += jnp.sum(d_pre * _shift_rows(u, s, True), axis=0, keepdims=True)
        dw_ref[CONV_WIDTH:CONV_WIDTH + 1, :] += jnp.sum(d_pre, axis=0, keepdims=True)
        du_ref[0] = du.astype(BF16)

    return pl.pallas_call(
        body, name="conv_bwd",
        grid=(CONV_DIM // BLK, bsz),
        in_specs=[pl.BlockSpec((1, seq, BLK), lambda c, b: (b, 0, R_XBC // BLK + c)),
                  pl.BlockSpec((CONV_WIDTH, BLK), lambda c, b: (0, c)),
                  pl.BlockSpec((1, BLK), lambda c, b: (0, c)),
                  pl.BlockSpec((1, seq, BLK), lambda c, b: (b, 0, c))],
        out_specs=[pl.BlockSpec((1, seq, BLK), lambda c, b: (b, 0, c)),
                   pl.BlockSpec((8, BLK), lambda c, b: (0, c))],
        out_shape=[jax.ShapeDtypeStruct((bsz, seq, CONV_DIM), BF16), jax.ShapeDtypeStruct((8, CONV_DIM), F32)],
        compiler_params=_params("parallel", "arbitrary"),
    )(rest3, conv_w, conv_b, d_act)


P_DTB, P_ALOG, P_DSKIP = 0, 1, 2


def _softplus(v):
    return jnp.maximum(v, 0.0) + jnp.log(1.0 + jnp.exp(-jnp.abs(v)))


def _dot_hi(a, b):
    return jnp.dot(a, b, preferred_element_type=F32, precision=lax.Precision.HIGHEST)


def _nt(a, b):
    return lax.dot_general(a, b, (((1,), (1,)), ((), ())), preferred_element_type=F32)


def _tn(a, b):
    return lax.dot_general(a, b, (((0,), (0,)), ((), ())), preferred_element_type=F32)


def _sum_all(v):
    return jnp.sum(jnp.sum(v, axis=1, keepdims=True), axis=0, keepdims=True)


def _ssd_decays(dtr, dtr_t, prow_ref, pcol_ref):
    ri = lax.broadcasted_iota(jnp.int32, (BLK, BLK), 0)
    ci = lax.broadcasted_iota(jnp.int32, (BLK, BLK), 1)
    tri = (ri >= ci).astype(F32)
    tri_u = (ri <= ci).astype(F32)
    pre = dtr + prow_ref[P_DTB:P_DTB + 1, :]
    dt = _softplus(pre)
    ah_row = -jnp.exp(prow_ref[P_ALOG:P_ALOG + 1, :])
    acs = _dot_hi(tri, dt * ah_row)
    pre_t = dtr_t + pcol_ref[:, P_DTB:P_DTB + 1]
    dt_t = _softplus(pre_t)
    ah_col = -jnp.exp(pcol_ref[:, P_ALOG:P_ALOG + 1])
    acs_t = _dot_hi(dt_t * ah_col, tri_u)
    return dict(tri=tri, tri_u=tri_u, pre=pre, dt=dt, ah_row=ah_row, acs=acs, pre_t=pre_t, dt_t=dt_t, ah_col=ah_col,
                acs_t=acs_t, causal=ri >= ci, last_row=ri[:, 0:1] == BLK - 1)


def _ssd_head(h, d, x_ref, s_in, g_mat):
    col = d["acs"][:, h:h + 1]
    row = d["acs_t"][h:h + 1, :]
    lm = jnp.exp(jnp.where(d["causal"], col - row, -jnp.inf))
    m = g_mat * lm
    xh = x_ref[0, :, pl.ds(h * HEAD_DIM, HEAD_DIM)]
    dtc = d["dt"][:, h:h + 1]
    xd = xh * dtc
    e = jnp.exp(col)
    clast = d["acs"][BLK - 1:BLK, h:h + 1]
    f = jnp.exp(clast - col)
    return dict(col=col, lm=lm, m=m, xh=xh, dtc=dtc, xd=xd, e=e, ecl=jnp.exp(clast), f=f, xf=xd * f)


def _ssd_specs(nc, rev):
    cidx = (lambda c: nc - 1 - c) if rev else (lambda c: c)
    x_spec = pl.BlockSpec((1, BLK, D_INNER), lambda b, c: (b, cidx(c), 0))
    bm_spec = pl.BlockSpec((1, BLK, SSM_GROUPS * D_STATE), lambda b, c: (b, cidx(c), D_INNER // (SSM_GROUPS * D_STATE)))
    cm_spec = pl.BlockSpec((1, BLK, SSM_GROUPS * D_STATE), lambda b, c: (b, cidx(c), D_INNER // (SSM_GROUPS * D_STATE) + 1))
    dt_spec = pl.BlockSpec((1, BLK, BLK), lambda b, c: (b, cidx(c), R_DT // BLK))
    dtt_spec = pl.BlockSpec((1, BLK, BLK), lambda b, c: (b, 0, cidx(c)))
    prow_spec = pl.BlockSpec((8, BLK), lambda b, c: (0, 0))
    pcol_spec = pl.BlockSpec((BLK, 8), lambda b, c: (0, 0))
    st_spec = pl.BlockSpec((1, 1, SSM_HEADS, HEAD_DIM, D_STATE), lambda b, c: (b, cidx(c), 0, 0, 0))
    y_spec = pl.BlockSpec((1, BLK, D_INNER), lambda b, c: (b, cidx(c), 0))
    return x_spec, bm_spec, cm_spec, dt_spec, dtt_spec, prow_spec, pcol_spec, st_spec, y_spec


def _ssd_fwd(xa, rest3, dtr_t, prow, pcol):
    bsz, seq, _ = xa.shape
    nc = seq // BLK

    def body(x_ref, bm_ref, cm_ref, dt_ref, dtt_ref, prow_ref, pcol_ref, y_ref, sin_ref, s_ref):
        @pl.when(pl.program_id(1) == 0)
        def _():
            s_ref[...] = jnp.zeros_like(s_ref)

        d = _ssd_decays(dt_ref[0], dtt_ref[0], prow_ref, pcol_ref)
        for g in range(SSM_GROUPS):
            lanes = pl.ds(g * D_STATE, D_STATE)
            bg = bm_ref[0, :, lanes].astype(BF16)
            cg = cm_ref[0, :, lanes].astype(BF16)
            g_mat = _nt(cg, bg)
            for hh in range(HEADS_PER_SSM_GROUP):
                h = g * HEADS_PER_SSM_GROUP + hh
                s_in = s_ref[h]
                sin_ref[0, 0, h] = s_in
                q = _ssd_head(h, d, x_ref, s_in, g_mat)
                y_diag = jnp.dot(q["m"].astype(BF16), q["xd"].astype(BF16), preferred_element_type=F32)
                y_off = _nt(cg, s_in.astype(BF16)) * q["e"]
                s_ref[h] = s_in * q["ecl"] + _tn(q["xf"].astype(BF16), bg)
                y_ref[0, :, pl.ds(h * HEAD_DIM, HEAD_DIM)] = y_diag + y_off + prow_ref[P_DSKIP:P_DSKIP + 1, h:h + 1] * q["xh"]

    x_spec, bm_spec, cm_spec, dt_spec, dtt_spec, prow_spec, pcol_spec, st_spec, y_spec = _ssd_specs(nc, False)
    return pl.pallas_call(
        body, name="ssd_fwd",
        grid=(bsz, nc),
        in_specs=[x_spec, bm_spec, cm_spec, dt_spec, dtt_spec, prow_spec, pcol_spec],
        out_specs=[y_spec, st_spec],
        out_shape=[jax.ShapeDtypeStruct((bsz, seq, D_INNER), F32),
                   jax.ShapeDtypeStruct((bsz, nc, SSM_HEADS, HEAD_DIM, D_STATE), F32)],
        scratch_shapes=[pltpu.VMEM((SSM_HEADS, HEAD_DIM, D_STATE), F32)],
        compiler_params=_params("parallel", "arbitrary"),
    )(xa, xa, xa, rest3, dtr_t, prow, pcol)


def _ssd_bwd(xa, rest3, dtr_t, prow, pcol, s_in_all, dy):
    bsz, seq, _ = xa.shape
    nc = seq // BLK

    def body(x_ref, bm_ref, cm_ref, dt_ref, dtt_ref, prow_ref, pcol_ref, sin_ref, dy_ref,
             dxa_ref, ddt_ref, ddtt_ref, gprow_ref, gpcol_ref, ds_ref, dc_ref, ddtc_ref, drt_ref):
        first = jnp.logical_and(pl.program_id(0) == 0, pl.program_id(1) == 0)

        @pl.when(first)
        def _():
            gprow_ref[...] = jnp.zeros_like(gprow_ref)
            gpcol_ref[...] = jnp.zeros_like(gpcol_ref)

        @pl.when(pl.program_id(1) == 0)
        def _():
            ds_ref[...] = jnp.zeros_like(ds_ref)

        dc_ref[...] = jnp.zeros_like(dc_ref)
        ddtc_ref[...] = jnp.zeros_like(ddtc_ref)
        drt_ref[...] = jnp.zeros_like(drt_ref)
        d = _ssd_decays(dt_ref[0], dtt_ref[0], prow_ref, pcol_ref)
        for g in range(SSM_GROUPS):
            lanes = pl.ds(g * D_STATE, D_STATE)
            bg = bm_ref[0, :, lanes].astype(BF16)
            cg = cm_ref[0, :, lanes].astype(BF16)
            g_mat = _nt(cg, bg)
            d_g = jnp.zeros((BLK, BLK), F32)
            d_bg = jnp.zeros((BLK, D_STATE), F32)
            d_cg = jnp.zeros((BLK, D_STATE), F32)
            for hh in range(HEADS_PER_SSM_GROUP):
                h = g * HEADS_PER_SSM_GROUP + hh
                head_lanes = pl.ds(h * HEAD_DIM, HEAD_DIM)
                s_in = sin_ref[0, 0, h]
                s_in16 = s_in.astype(BF16)
                q = _ssd_head(h, d, x_ref, s_in, g_mat)
                m16, xd16 = q["m"].astype(BF16), q["xd"].astype(BF16)
                d_y = dy_ref[0, :, head_lanes]
                d_y16 = d_y.astype(BF16)
                d_so = ds_ref[h]
                d_so16 = d_so.astype(BF16)
                d_x = prow_ref[P_DSKIP:P_DSKIP + 1, h:h + 1] * d_y
                gprow_ref[P_DSKIP:P_DSKIP + 1, h:h + 1] += _sum_all(d_y * q["xh"])
                d_m = _nt(d_y16, xd16)
                d_xd = _tn(m16, d_y16)
                w = d_m * q["m"]
                d_g = d_g + d_m * q["lm"]
                d_col = jnp.sum(w, axis=1, keepdims=True)
                drt_ref[h:h + 1, :] = -jnp.sum(w, axis=0, keepdims=True)
                qmat = _nt(cg, s_in16)
                d_q16 = (d_y * q["e"]).astype(BF16)
                d_col = d_col + jnp.sum(d_y * qmat, axis=1, keepdims=True) * q["e"]
                d_cg = d_cg + jnp.dot(d_q16, s_in16, preferred_element_type=F32)
                d_sin = _tn(d_q16, cg) + d_so * q["ecl"]
                d_clast = _sum_all(d_so * s_in) * q["ecl"]
                d_xf = _nt(bg, d_so16)
                d_bg = d_bg + jnp.dot(q["xf"].astype(BF16), d_so16, preferred_element_type=F32)
                d_xd = d_xd + d_xf * q["f"]
                d_f = jnp.sum(d_xf * q["xd"], axis=1, keepdims=True) * q["f"]
                d_clast = d_clast + jnp.sum(d_f, axis=0, keepdims=True)
                d_col = d_col - d_f + jnp.where(d["last_row"], d_clast, 0.0)
                dxa_ref[0, :, head_lanes] = d_x + d_xd * q["dtc"]
                dc_ref[:, h:h + 1] = d_col
                ddtc_ref[:, h:h + 1] = jnp.sum(d_xd * q["xh"], axis=1, keepdims=True)
                ds_ref[h] = d_sin
            d_g16 = d_g.astype(BF16)
            dxa_ref[0, :, pl.ds(D_INNER + g * D_STATE, D_STATE)] = d_bg + _tn(d_g16, cg)
            dxa_ref[0, :, pl.ds(D_INNER + (SSM_GROUPS + g) * D_STATE, D_STATE)] = d_cg + jnp.dot(d_g16, bg, preferred_element_type=F32)
        d_a = _dot_hi(d["tri_u"], dc_ref[...])
        d_pre = (ddtc_ref[...] + d_a * d["ah_row"]) * _sigmoid(d["pre"])
        ddt_ref[0] = d_pre
        gprow_ref[P_DTB:P_DTB + 1, :] += jnp.sum(d_pre, axis=0, keepdims=True)
        gprow_ref[P_ALOG:P_ALOG + 1, :] += jnp.sum(d_a * d["dt"], axis=0, keepdims=True) * d["ah_row"]
        d_at = _dot_hi(drt_ref[...], d["tri"])
        d_pre_t = d_at * d["ah_col"] * _sigmoid(d["pre_t"])
        ddtt_ref[0] = d_pre_t
        gpcol_ref[:, P_DTB:P_DTB + 1] += jnp.sum(d_pre_t, axis=1, keepdims=True)
        gpcol_ref[:, P_ALOG:P_ALOG + 1] += jnp.sum(d_at * d["dt_t"], axis=1, keepdims=True) * d["ah_col"]

    x_spec, bm_spec, cm_spec, dt_spec, dtt_spec, prow_spec, pcol_spec, st_spec, y_spec = _ssd_specs(nc, True)
    return pl.pallas_call(
        body, name="ssd_bwd",
        grid=(bsz, nc),
        in_specs=[x_spec, bm_spec, cm_spec, dt_spec, dtt_spec, prow_spec, pcol_spec, st_spec, y_spec],
        out_specs=[pl.BlockSpec((1, BLK, CONV_DIM), lambda b, c: (b, nc - 1 - c, 0)),
                   pl.BlockSpec((1, BLK, BLK), lambda b, c: (b, nc - 1 - c, 0)),
                   pl.BlockSpec((1, BLK, BLK), lambda b, c: (b, 0, nc - 1 - c)),
                   prow_spec, pcol_spec],
        out_shape=[jax.ShapeDtypeStruct((bsz, seq, CONV_DIM), F32),
                   jax.ShapeDtypeStruct((bsz, seq, BLK), F32),
                   jax.ShapeDtypeStruct((bsz, BLK, seq), F32),
                   jax.ShapeDtypeStruct((8, BLK), F32),
                   jax.ShapeDtypeStruct((BLK, 8), F32)],
        scratch_shapes=[pltpu.VMEM((SSM_HEADS, HEAD_DIM, D_STATE), F32), pltpu.VMEM((BLK, BLK), F32),
                        pltpu.VMEM((BLK, BLK), F32), pltpu.VMEM((BLK, BLK), F32)],
        compiler_params=_params("arbitrary", "arbitrary"),
    )(xa, xa, xa, rest3, dtr_t, prow, pcol, s_in_all, dy)


ROW_TILE = 256
CMB_COLS = 256
RMS_COLS = D_INNER // SSM_GROUPS


def _combine_weights(l_refs):
    ls = [r[...] for r in l_refs]
    mx = jnp.maximum(jnp.maximum(ls[0], ls[1]), ls[2])
    es = [jnp.exp(l - mx) for l in ls]
    inv = 1.0 / (es[0] + es[1] + es[2])
    return [e * inv for e in es]


def _combine_specs():
    a = pl.BlockSpec((ROW_TILE, CMB_COLS), lambda i, j: (i, j))
    gatt = pl.BlockSpec((ROW_TILE, CMB_COLS), lambda i, j: (i, R_GATT // CMB_COLS + j))
    return a, gatt


def _combine_fwd(outs, lses, rest):
    t = rest.shape[0]

    def body(o0, o1, o2, l0, l1, l2, ga_ref, oa_ref):
        ws = _combine_weights((l0, l1, l2))
        o = ws[0] * o0[...] + ws[1] * o1[...] + ws[2] * o2[...]
        oa_ref[...] = (o * _silu_and_grad(ga_ref[...])[0]).astype(BF16)

    a, gatt = _combine_specs()
    return pl.pallas_call(
        body, name="combine_fwd",
        grid=(t // ROW_TILE, GROUP_COLS // CMB_COLS),
        in_specs=[a] * 6 + [gatt],
        out_specs=a,
        out_shape=jax.ShapeDtypeStruct((t, GROUP_COLS), BF16),
        compiler_params=_params("parallel", "parallel"),
    )(*outs, *lses, rest)


def _combine_bwd(outs, lses, rest, d_oa):
    t = rest.shape[0]

    def body(o0, o1, o2, l0, l1, l2, ga_ref, doa_ref, do0, do1, do2, dd0, dd1, dd2, dga_ref):
        ws = _combine_weights((l0, l1, l2))
        o = ws[0] * o0[...] + ws[1] * o1[...] + ws[2] * o2[...]
        sg, dsg = _silu_and_grad(ga_ref[...])
        d_oa_v = doa_ref[...]
        d_o = d_oa_v * sg
        dga_ref[...] = (d_oa_v * o * dsg).astype(BF16)
        for w, do_ref, dd_ref in zip(ws, (do0, do1, do2), (dd0, dd1, dd2)):
            d_out = w * d_o
            do_ref[...] = d_out.astype(BF16)
            dd_ref[...] = d_out * o

    a, gatt = _combine_specs()
    s16 = jax.ShapeDtypeStruct((t, GROUP_COLS), BF16)
    s32 = jax.ShapeDtypeStruct((t, GROUP_COLS), F32)
    return pl.pallas_call(
        body, name="combine_bwd",
        grid=(t // ROW_TILE, GROUP_COLS // CMB_COLS),
        in_specs=[a] * 6 + [gatt, a],
        out_specs=[a] * 7,
        out_shape=[s16, s16, s16, s32, s32, s32, s16],
        compiler_params=_params("parallel", "parallel"),
    )(*outs, *lses, rest, d_oa)


def _gatenorm_fwd(y, rest, norm_w):
    t = rest.shape[0]

    def body(y_ref, z_ref, w_ref, o_ref):
        u = y_ref[...] * _silu_and_grad(z_ref[...])[0]
        rs = lax.rsqrt(jnp.mean(u * u, axis=-1, keepdims=True) + RMS_EPS)
        o_ref[...] = (u * rs * w_ref[...]).astype(BF16)

    return pl.pallas_call(
        body, name="gatenorm_fwd",
        grid=(t // ROW_TILE, SSM_GROUPS),
        in_specs=[pl.BlockSpec((ROW_TILE, RMS_COLS), lambda i, j: (i, j)),
                  pl.BlockSpec((ROW_TILE, RMS_COLS), lambda i, j: (i, R_Z // RMS_COLS + j)),
                  pl.BlockSpec((1, RMS_COLS), lambda i, j: (0, j))],
        out_specs=pl.BlockSpec((ROW_TILE, RMS_COLS), lambda i, j: (i, j)),
        out_shape=jax.ShapeDtypeStruct((t, D_INNER), BF16),
        compiler_params=_params("parallel", "parallel"),
    )(y, rest, norm_w)


def _gatenorm_bwd(y, rest, norm_w, d_ys):
    t = rest.shape[0]

    def body(y_ref, z_ref, w_ref, g_ref, dy_ref, dz_ref, dw_ref):
        @pl.when(pl.program_id(1) == 0)
        def _():
            dw_ref[...] = jnp.zeros_like(dw_ref)

        yv = y_ref[...]
        sz, dsz = _silu_and_grad(z_ref[...])
        u = yv * sz
        rs = lax.rsqrt(jnp.mean(u * u, axis=-1, keepdims=True) + RMS_EPS)
        un = u * rs
        g = g_ref[...]
        dw_ref[0:1, :] += jnp.sum(g * un, axis=0, keepdims=True)
        d_un = g * w_ref[...]
        d_u = rs * (d_un - un * jnp.mean(d_un * un, axis=-1, keepdims=True))
        dy_ref[...] = d_u * sz
        dz_ref[...] = (d_u * yv * dsz).astype(BF16)

    blk = pl.BlockSpec((ROW_TILE, RMS_COLS), lambda j, i: (i, j))
    return pl.pallas_call(
        body, name="gatenorm_bwd",
        grid=(SSM_GROUPS, t // ROW_TILE),
        in_specs=[blk, pl.BlockSpec((ROW_TILE, RMS_COLS), lambda j, i: (i, R_Z // RMS_COLS + j)),
                  pl.BlockSpec((1, RMS_COLS), lambda j, i: (0, j)), blk],
        out_specs=[blk, blk, pl.BlockSpec((8, RMS_COLS), lambda j, i: (0, j))],
        out_shape=[jax.ShapeDtypeStruct((t, D_INNER), F32), jax.ShapeDtypeStruct((t, D_INNER), BF16),
                   jax.ShapeDtypeStruct((8, D_INNER), F32)],
        compiler_params=_params("parallel", "arbitrary"),
    )(y, rest, norm_w, d_ys)


def _row_specs():
    full = pl.BlockSpec((ROW_TILE, D_MODEL), lambda i: (i, 0))
    vec = pl.BlockSpec((8, D_MODEL), lambda i: (0, 0))
    at = lambda off: pl.BlockSpec((ROW_TILE, D_MODEL), lambda i: (i, off // D_MODEL))
    return full, vec, at


def _merge_fwd(y_a, y_b, rest, b_gate):
    t = rest.shape[0]

    def body(ya_ref, yb_ref, ga_ref, gb_ref, bg_ref, o_ref):
        sa = _sigmoid(ga_ref[...] + bg_ref[0:1, :])
        sb = _sigmoid(gb_ref[...] + bg_ref[1:2, :])
        o_ref[...] = (sa * ya_ref[...] + sb * yb_ref[...]).astype(BF16)

    full, vec, at = _row_specs()
    return pl.pallas_call(
        body, name="merge_fwd",
        grid=(t // ROW_TILE,),
        in_specs=[full, full, at(R_GM), at(R_GM + D_MODEL), vec],
        out_specs=full,
        out_shape=jax.ShapeDtypeStruct((t, D_MODEL), BF16),
        compiler_params=_params("parallel"),
    )(y_a, y_b, rest, rest, b_gate)


def _merge_bwd(y_a, y_b, rest, b_gate, d_merged):
    t = rest.shape[0]

    def body(ya_ref, yb_ref, ga_ref, gb_ref, bg_ref, dm_ref, dya_ref, dyb_ref, dga_ref, dgb_ref, dbg_ref):
        @pl.when(pl.program_id(0) == 0)
        def _():
            dbg_ref[...] = jnp.zeros_like(dbg_ref)

        dm = dm_ref[...]
        for row, y_ref, g_ref, dy_ref, dg_ref in ((0, ya_ref, ga_ref, dya_ref, dga_ref), (1, yb_ref, gb_ref, dyb_ref, dgb_ref)):
            s = _sigmoid(g_ref[...] + bg_ref[row:row + 1, :])
            dy_ref[...] = (dm * s).astype(BF16)
            dg = dm * y_ref[...] * s * (1.0 - s)
            dg_ref[...] = dg.astype(BF16)
            dbg_ref[row:row + 1, :] += jnp.sum(dg, axis=0, keepdims=True)

    full, vec, at = _row_specs()
    s16 = jax.ShapeDtypeStruct((t, D_MODEL), BF16)
    return pl.pallas_call(
        body, name="merge_bwd",
        grid=(t // ROW_TILE,),
        in_specs=[full, full, at(R_GM), at(R_GM + D_MODEL), vec, full],
        out_specs=[full, full, full, full, vec],
        out_shape=[s16, s16, s16, s16, jax.ShapeDtypeStruct((8, D_MODEL), F32)],
        compiler_params=_params("arbitrary"),
    )(y_a, y_b, rest, rest, b_gate, d_merged)


ST_LNG, ST_LNB, ST_BG2, ST_LOSS = 0, 1, 2, 3


def _final(x, mix, pw, rest, b_gate, ln_gb, target):
    t = rest.shape[0]

    def body(x_ref, mix_ref, pw_ref, gp_ref, bg_ref, ln_ref, tgt_ref, dpre_ref, dpre16_ref, dgp_ref, dpw_ref, st_ref):
        @pl.when(pl.program_id(0) == 0)
        def _():
            st_ref[...] = jnp.zeros_like(st_ref)

        sp = _sigmoid(gp_ref[...] + bg_ref[2:3, :])
        pw = pw_ref[...]
        pre = ALPHA * x_ref[...] + mix_ref[...] + sp * pw
        xc = pre - jnp.mean(pre, axis=-1, keepdims=True)
        rstd = lax.rsqrt(jnp.mean(xc * xc, axis=-1, keepdims=True) + LN_EPS)
        xhat = xc * rstd
        gain = ln_ref[0:1, :]
        err = xhat * gain + ln_ref[1:2, :] - tgt_ref[...]
        d_yo = err * (1.0 / D_MODEL)
        d_xhat = d_yo * gain
        d_pre = rstd * (d_xhat - jnp.mean(d_xhat, axis=-1, keepdims=True)
                        - xhat * jnp.mean(d_xhat * xhat, axis=-1, keepdims=True))
        dpre_ref[...] = d_pre
        dpre16_ref[...] = d_pre.astype(BF16)
        dgp = d_pre * pw * sp * (1.0 - sp)
        dgp_ref[...] = dgp.astype(BF16)
        dpw_ref[...] = (d_pre * sp).astype(BF16)
        st_ref[ST_LNG:ST_LNG + 1, :] += jnp.sum(d_yo * xhat, axis=0, keepdims=True)
        st_ref[ST_LNB:ST_LNB + 1, :] += jnp.sum(d_yo, axis=0, keepdims=True)
        st_ref[ST_BG2:ST_BG2 + 1, :] += jnp.sum(dgp, axis=0, keepdims=True)
        st_ref[ST_LOSS:ST_LOSS + 1, :] += jnp.sum(err * err, axis=0, keepdims=True) * (0.5 / D_MODEL)

    full, vec, at = _row_specs()
    s16 = jax.ShapeDtypeStruct((t, D_MODEL), BF16)
    return pl.pallas_call(
        body, name="final",
        grid=(t // ROW_TILE,),
        in_specs=[full, full, full, at(R_GPLE), vec, vec, full],
        out_specs=[full, full, full, full, vec],
        out_shape=[jax.ShapeDtypeStruct((t, D_MODEL), F32), s16, s16, s16, jax.ShapeDtypeStruct((8, D_MODEL), F32)],
        compiler_params=_params("arbitrary"),
    )(x, mix, pw, rest, b_gate, ln_gb, target)


def _mesh_position():
    return lax.axis_index("x"), lax.axis_index("y"), lax.axis_index("c")


def _flip(pos, k):
    x, y, c = pos
    return ((1 - x) if k & 4 else x, (1 - y) if k & 2 else y, (1 - c) if k & 1 else c)


def _linear(pos):
    return 4 * pos[0] + 2 * pos[1] + pos[2]


def _exchange(arrays, scatter, name):
    n = len(arrays)

    def body(*refs):
        ins, outs = refs[:n], refs[n:2 * n]
        send_sems, recv_sems, local_sems = refs[2 * n:]
        me = _mesh_position()
        me_i = _linear(me)

        def src_for(i, dest_i):
            return ins[i].at[dest_i] if scatter[i] else ins[i]

        local = [pltpu.make_async_copy(src_for(i, me_i), outs[i].at[me_i], local_sems.at[i]) for i in range(n)]
        for cp in local:
            cp.start()
        started = []
        for k in range(1, N_DEV):
            peer = _flip(me, k)
            peer_i = _linear(peer)
            for i in range(n):
                sem = i * (N_DEV - 1) + k - 1
                cp = pltpu.make_async_remote_copy(
                    src_ref=src_for(i, peer_i), dst_ref=outs[i].at[me_i], send_sem=send_sems.at[sem],
                    recv_sem=recv_sems.at[sem], device_id=peer, device_id_type=pl.DeviceIdType.MESH)
                cp.start()
                started.append(cp)
        for k in range(1, N_DEV):
            peer = _flip(me, k)
            peer_i = _linear(peer)
            for i in range(n):
                sem = i * (N_DEV - 1) + k - 1
                pltpu.make_async_remote_copy(
                    src_ref=src_for(i, peer_i), dst_ref=outs[i].at[peer_i], send_sem=send_sems.at[sem],
                    recv_sem=recv_sems.at[sem], device_id=peer, device_id_type=pl.DeviceIdType.MESH).wait_recv()
        for cp in started:
            cp.wait_send()
        for cp in local:
            cp.wait()

    any_spec = pl.BlockSpec(memory_space=pl.ANY)
    out_shape = [jax.ShapeDtypeStruct(a.shape if s else (N_DEV,) + a.shape, a.dtype) for a, s in zip(arrays, scatter)]
    return pl.pallas_call(
        body, name=name,
        in_specs=[any_spec] * n,
        out_specs=[any_spec] * n,
        out_shape=out_shape,
        scratch_shapes=[pltpu.SemaphoreType.DMA((n * (N_DEV - 1),)), pltpu.SemaphoreType.DMA((n * (N_DEV - 1),)),
                        pltpu.SemaphoreType.DMA((n,))],
        compiler_params=pltpu.CompilerParams(has_side_effects=True),
    )(*arrays)


def _adam_reduce(parts, w, m, v, name, rows):
    r, c = w.shape
    assert r % rows == 0
    c1 = 1.0 - ADAM_B1 ** ADAM_STEP
    c2 = 1.0 - ADAM_B2 ** ADAM_STEP

    def body(p_ref, w_ref, m_ref, v_ref, g_ref, d_ref, nm_ref, nv_ref):
        g = p_ref[0]
        for s in range(1, N_DEV):
            g = g + p_ref[s]
        g_ref[...] = g
        nm = ADAM_B1 * m_ref[...] + (1.0 - ADAM_B1) * g
        nv = ADAM_B2 * v_ref[...] + (1.0 - ADAM_B2) * (g * g)
        nm_ref[...] = nm
        nv_ref[...] = nv
        d_ref[...] = -ADAM_LR * ((nm / c1) / (jnp.sqrt(nv / c2) + ADAM_EPS) + ADAM_WD * w_ref[...])

    blk = pl.BlockSpec((rows, c), lambda i: (i, 0))
    shape = jax.ShapeDtypeStruct((r, c), F32)
    return pl.pallas_call(
        body, name=name,
        grid=(r // rows,),
        in_specs=[pl.BlockSpec((N_DEV, rows, c), lambda i: (0, i, 0)), blk, blk, blk],
        out_specs=[blk] * 4,
        out_shape=[shape] * 4,
        compiler_params=_params("parallel"),
    )(parts, w, m, v)


def _lane_total(rows8):
    def body(a_ref, o_ref):
        o_ref[...] = _sum_all(a_ref[...])

    return pl.pallas_call(body, name="loss_total", out_shape=jax.ShapeDtypeStruct((1, 1), F32))(rows8)


def _permute_w_in(w):
    pieces = []
    for g in range(N_GROUPS):
        for part in range(3):
            start = part * ATT_QKV + g * GROUP_COLS
            pieces.append(w[:, start:start + GROUP_COLS])
    pieces += [w[:, Z_END:XBC_END], w[:, GATT_END:Z_END], w[:, DT_END:GMERGE_END], w[:, GMERGE_END:],
               w[:, V_END:GATT_END], w[:, XBC_END:DT_END],
               jnp.zeros((w.shape[0], HCAT_COLS - REST_OFF - R_DT - SSM_HEADS), w.dtype)]
    return jnp.concatenate(pieces, axis=1)


def _unpermute_w_in(g):
    rest = g[:, REST_OFF:]
    parts = [jnp.concatenate([g[:, k * QKV_G + part * GROUP_COLS:k * QKV_G + (part + 1) * GROUP_COLS]
                              for k in range(N_GROUPS)], axis=1) for part in range(3)]
    parts += [rest[:, R_GATT:R_GATT + GROUP_COLS], rest[:, R_Z:R_Z + D_INNER], rest[:, R_XBC:R_XBC + CONV_DIM],
              rest[:, R_DT:R_DT + SSM_HEADS], rest[:, R_GM:R_GM + 2 * D_MODEL], rest[:, R_GPLE:R_GPLE + D_MODEL]]
    return jnp.concatenate(parts, axis=1)


SMALL_ROWS = 80
_SMALL_LAYOUT = (("conv_b", CONV_DIM), ("dt_bias", BLK), ("a_log", BLK), ("d_skip", BLK), ("ssm_norm_w", D_INNER),
                 ("ln_g", D_MODEL), ("ln_b", D_MODEL), ("rel_bias", NUM_BUCKETS * ATT_HEADS), ("loss", D_MODEL))


def _pack_small(vals):
    flat = []
    for name, width in _SMALL_LAYOUT:
        v = vals.get(name)
        v = jnp.zeros((width,), F32) if v is None else v.reshape(-1).astype(F32)
        flat.append(jnp.pad(v, (0, width - v.shape[0])))
    flat = jnp.concatenate(flat)
    return jnp.pad(flat, (0, SMALL_ROWS * BLK - flat.shape[0])).reshape(SMALL_ROWS, BLK)


def _unpack_small(packed):
    flat = packed.reshape(-1)
    out, pos = {}, 0
    for name, width in _SMALL_LAYOUT:
        out[name] = flat[pos:pos + width]
        pos += width
    for name in ("dt_bias", "a_log", "d_skip"):
        out[name] = out[name][:SSM_HEADS]
    out["rel_bias"] = out["rel_bias"].reshape(NUM_BUCKETS, ATT_HEADS)
    return out


def _pack_cols(b_gate_part, conv_w_part):
    return jnp.concatenate([jnp.pad(b_gate_part, ((0, 5), (0, 0))), jnp.pad(conv_w_part, ((0, 4), (0, 0)))], axis=1)


def _pack_cols_all(b_gate_full, conv_w_full):
    bg = b_gate_full.reshape(3, N_DEV, BLK).transpose(1, 0, 2)
    cw = conv_w_full.reshape(CONV_WIDTH, N_DEV, CONV_DIM // N_DEV).transpose(1, 0, 2)
    return jnp.concatenate([jnp.pad(bg, ((0, 0), (0, 5), (0, 0))), jnp.pad(cw, ((0, 0), (0, 4), (0, 0)))], axis=2)


def _unpack_cols_all(packed):
    bg = packed[:, :3, :BLK].transpose(1, 0, 2).reshape(3, D_MODEL)
    cw = packed[:, :CONV_WIDTH, BLK:].transpose(1, 0, 2).reshape(CONV_WIDTH, CONV_DIM)
    return bg, cw


def _local_step(x, p, target, wp16, wb16, wo16, wple16, b_gate, conv_w, small):
    bsz, seq, _ = x.shape
    t = bsz * seq
    x2 = x.reshape(t, D_MODEL)
    x16 = x2.astype(BF16)
    p16 = p.reshape(t, PLE_DIM).astype(BF16)
    tgt2 = target.reshape(t, D_MODEL)
    b_gate8 = jnp.pad(b_gate, ((0, 5), (0, 0)))
    ln_gb = jnp.pad(jnp.stack([small["ln_g"], small["ln_b"]]), ((0, 6), (0, 0)))
    conv_b = small["conv_b"].reshape(1, CONV_DIM)
    norm_w = small["ssm_norm_w"].reshape(1, D_INNER)
    pad_heads = lambda v: jnp.pad(v, (0, BLK - SSM_HEADS))
    prow = jnp.pad(jnp.stack([pad_heads(small["dt_bias"]), pad_heads(small["a_log"]), pad_heads(small["d_skip"])]), ((0, 5), (0, 0)))
    pcol = prow.T
    wa16, wbb16 = wb16[:GROUP_COLS], wb16[GROUP_COLS:]

    qkv = [_matmul(x16, wp16, mode="nn", out_dtype=BF16, name=f"inproj_qkv{g}", tm=512, tn=768, tk=D_MODEL,
                   n_off=g * QKV_G, n=QKV_G) for g in range(N_GROUPS)]
    rest = _matmul(x16, wp16, mode="nn", out_dtype=F32, name="inproj_rest", tm=512, tn=768, tk=D_MODEL,
                   n_off=REST_OFF, n=REST_COLS)
    rest3 = rest.reshape(bsz, seq, REST_COLS)
    views, biases, onehots, outs, lses, lse_views = [], [], [], [], [], []
    for g, (_, dil) in enumerate(DILATED_PATTERNS):
        view = qkv[g].reshape(bsz, seq // dil, dil * QKV_G)
        bias, onehot = _bias_matrix(small["rel_bias"][:, g * HEADS_PER_GROUP:(g + 1) * HEADS_PER_GROUP], dil)
        out_v, lse_v = _attn_fwd(view, bias, dil, f"attn_fwd{g}")
        views.append(view), biases.append(bias), onehots.append(onehot), lse_views.append(lse_v)
        outs.append(out_v.reshape(t, GROUP_COLS)), lses.append(lse_v.reshape(t, GROUP_COLS))
    oa = _combine_fwd(outs, lses, rest)
    xa = _conv_fwd(rest3, conv_w, conv_b)
    dtr_t = jnp.swapaxes(rest3[:, :, R_DT:R_DT + BLK], 1, 2)
    y, s_in = _ssd_fwd(xa, rest3, dtr_t, prow, pcol)
    y2 = y.reshape(t, D_INNER)
    ys = _gatenorm_fwd(y2, rest, norm_w)
    y_a = _matmul(oa, wa16, mode="nn", out_dtype=F32, name="branch_a", tm=512, tn=D_MODEL, tk=GROUP_COLS)
    y_b = _matmul(ys, wbb16, mode="nn", out_dtype=F32, name="branch_b", tm=512, tn=D_MODEL, tk=D_INNER)
    merged = _merge_fwd(y_a, y_b, rest, b_gate8)
    mix = _matmul(merged, wo16, mode="nn", out_dtype=F32, name="out_proj", tm=512, tn=D_MODEL, tk=D_MODEL)
    pw = _matmul(p16, wple16, mode="nn", out_dtype=F32, name="ple_proj", tm=512, tn=D_MODEL, tk=PLE_DIM)
    d_pre, d_pre16, d_gple, d_pw, stats = _final(x2, mix, pw, rest, b_gate8, ln_gb, tgt2)

    d_merged = _matmul(d_pre16, wo16, mode="nt", out_dtype=F32, name="d_merged", tm=512, tn=D_MODEL, tk=D_MODEL)
    g_w_out = _matmul(merged, d_pre16, mode="tn", out_dtype=F32, name="g_w_out", tm=512, tn=D_MODEL, tk=1024)
    g_w_ple = _matmul(p16, d_pw, mode="tn", out_dtype=F32, name="g_w_ple", tm=PLE_DIM, tn=D_MODEL, tk=1024)
    d_ya, d_yb, d_ga, d_gb, dbg01 = _merge_bwd(y_a, y_b, rest, b_gate8, d_merged)
    d_oa = _matmul(d_ya, wa16, mode="nt", out_dtype=F32, name="d_oa", tm=512, tn=GROUP_COLS, tk=D_MODEL)
    d_ys = _matmul(d_yb, wbb16, mode="nt", out_dtype=F32, name="d_ys", tm=512, tn=1024, tk=D_MODEL)
    g_wa = _matmul(oa, d_ya, mode="tn", out_dtype=F32, name="g_w_branch_a", tm=GROUP_COLS, tn=D_MODEL, tk=1024)
    g_wb = _matmul(ys, d_yb, mode="tn", out_dtype=F32, name="g_w_branch_b", tm=512, tn=D_MODEL, tk=1024)
    d_outs_dd_ga = _combine_bwd(outs, lses, rest, d_oa)
    d_outs, dds, d_gatt = d_outs_dd_ga[:3], d_outs_dd_ga[3:6], d_outs_dd_ga[6]
    d_y, d_z, d_nw = _gatenorm_bwd(y2, rest, norm_w, d_ys)
    d_xa, ddt, ddt_t, gprow, gpcol = _ssd_bwd(xa, rest3, dtr_t, prow, pcol, s_in, d_y.reshape(bsz, seq, D_INNER))
    d_xbc, d_conv = _conv_bwd(rest3, conv_w, conv_b, d_xa)
    pieces, g_tables = [], []
    for g, (_, dil) in enumerate(DILATED_PATTERNS):
        vshape = (bsz, seq // dil, dil * GROUP_COLS)
        dq, dk, dv, dbias = _attn_bwd(views[g], biases[g], lse_views[g], d_outs[g].reshape(vshape), dds[g].reshape(vshape),
                                      dil, f"attn_bwd{g}")
        pieces += [dq.reshape(t, GROUP_COLS), dk.reshape(t, GROUP_COLS), dv.reshape(t, GROUP_COLS)]
        g_tables.append(_bias_grad(dbias, onehots[g], f"bias_grad{g}"))
    d_dt = (ddt + jnp.swapaxes(ddt_t, 1, 2)).reshape(t, BLK).astype(BF16)
    pieces += [d_xbc.reshape(t, CONV_DIM), d_z, d_ga, d_gb, d_gple, d_gatt, d_dt,
               jnp.zeros((t, HCAT_COLS - REST_OFF - R_DT - BLK), BF16)]
    d_hcat = jnp.concatenate(pieces, axis=1)
    grad_x = _matmul(d_hcat, wp16, mode="nt", out_dtype=F32, name="grad_x", tm=512, tn=D_MODEL, tk=768,
                     add=d_pre, add_scale=ALPHA)
    g_wp = _matmul(x16, d_hcat, mode="tn", out_dtype=F32, name="g_w_in", tm=D_MODEL, tn=768, tk=512)

    grads = dict(
        w_in=_unpermute_w_in(g_wp),
        b_gate=jnp.stack([dbg01[0], dbg01[1], stats[ST_BG2]]),
        conv_w=d_conv[:CONV_WIDTH],
        w_branch=jnp.concatenate([g_wa, g_wb], axis=0),
        w_out=g_w_out,
        w_ple=g_w_ple,
    )
    small_grads = dict(
        conv_b=d_conv[CONV_WIDTH],
        dt_bias=gprow[P_DTB, :SSM_HEADS] + gpcol[:SSM_HEADS, P_DTB],
        a_log=gprow[P_ALOG, :SSM_HEADS] + gpcol[:SSM_HEADS, P_ALOG],
        d_skip=gprow[P_DSKIP, :SSM_HEADS],
        ssm_norm_w=d_nw[0],
        ln_g=stats[ST_LNG],
        ln_b=stats[ST_LNB],
        rel_bias=jnp.concatenate(g_tables, axis=1),
        loss=stats[ST_LOSS],
    )
    return grad_x.reshape(bsz, seq, D_MODEL), grads, small_grads


WEIGHT_ORDER = ("w_in", "b_gate", "conv_w", "conv_b", "dt_bias", "a_log", "d_skip", "ssm_norm_w", "w_branch", "w_out",
                "w_ple", "ln_g", "ln_b", "rel_bias")
SMALL_NAMES = ("conv_b", "dt_bias", "a_log", "d_skip", "ssm_norm_w", "ln_g", "ln_b", "rel_bias")


def kernel(x, p, w_in, b_gate, conv_w, conv_b, dt_bias, a_log, d_skip, ssm_norm_w, w_branch, w_out, w_ple, ln_g, ln_b, rel_bias, loss_target, m_w_in, m_b_gate, m_conv_w, m_conv_b, m_dt_bias, m_a_log, m_d_skip, m_ssm_norm_w, m_w_branch, m_w_out, m_w_ple, m_ln_g, m_ln_b, m_rel_bias, v_w_in, v_b_gate, v_conv_w, v_conv_b, v_dt_bias, v_a_log, v_d_skip, v_ssm_norm_w, v_w_branch, v_w_out, v_w_ple, v_ln_g, v_ln_b, v_rel_bias):
    given = dict(w_in=w_in, b_gate=b_gate, conv_w=conv_w, conv_b=conv_b, dt_bias=dt_bias, a_log=a_log, d_skip=d_skip,
                 ssm_norm_w=ssm_norm_w, w_branch=w_branch, w_out=w_out, w_ple=w_ple, ln_g=ln_g, ln_b=ln_b)
    moments_m = dict(w_in=m_w_in, b_gate=m_b_gate, conv_w=m_conv_w, conv_b=m_conv_b, dt_bias=m_dt_bias, a_log=m_a_log,
                     d_skip=m_d_skip, ssm_norm_w=m_ssm_norm_w, w_branch=m_w_branch, w_out=m_w_out, w_ple=m_w_ple,
                     ln_g=m_ln_g, ln_b=m_ln_b)
    moments_v = dict(w_in=v_w_in, b_gate=v_b_gate, conv_w=v_conv_w, conv_b=v_conv_b, dt_bias=v_dt_bias, a_log=v_a_log,
                     d_skip=v_d_skip, ssm_norm_w=v_ssm_norm_w, w_branch=v_w_branch, w_out=v_w_out, w_ple=v_w_ple,
                     ln_g=v_ln_g, ln_b=v_ln_b)
    w = {k: a[0] for k, a in given.items()} | {"rel_bias": rel_bias}
    mm = {k: a[0] for k, a in moments_m.items()} | {"rel_bias": m_rel_bias}
    vv = {k: a[0] for k, a in moments_v.items()} | {"rel_bias": v_rel_bias}

    gathered = _exchange(
        [w["w_in"].astype(BF16), w["w_branch"].astype(BF16), w["w_out"].astype(BF16), w["w_ple"].astype(BF16),
         _pack_cols(w["b_gate"], w["conv_w"])], [False] * 5, "gather_weights")
    wp16 = _permute_w_in(gathered[0].transpose(1, 0, 2).reshape(D_MODEL, IN_COLS))
    wb16 = gathered[1].reshape(BRANCH_ROWS, D_MODEL)
    wo16 = gathered[2].reshape(D_MODEL, D_MODEL)
    wple16 = gathered[3].transpose(1, 0, 2).reshape(PLE_DIM, D_MODEL)
    b_gate_full, conv_w_full = _unpack_cols_all(gathered[4])
    small = {k: w[k] for k in SMALL_NAMES}

    grad_x, grads, small_grads = _local_step(x, p[0], loss_target, wp16, wb16, wo16, wple16, b_gate_full, conv_w_full, small)

    parts = _exchange(
        [grads["w_in"].reshape(D_MODEL, N_DEV, IN_SHARD).transpose(1, 0, 2),
         grads["w_branch"].reshape(N_DEV, BRANCH_ROWS // N_DEV, D_MODEL),
         grads["w_out"].reshape(N_DEV, D_MODEL // N_DEV, D_MODEL),
         grads["w_ple"].reshape(PLE_DIM, N_DEV, BLK).transpose(1, 0, 2),
         _pack_cols_all(grads["b_gate"], grads["conv_w"]),
         _pack_small(small_grads)],
        [True, True, True, True, True, False], "exchange_grads")

    out = {}
    out["w_in"] = _adam_reduce(parts[0], w["w_in"], mm["w_in"], vv["w_in"], "adam_w_in", 64)
    out["w_branch"] = _adam_reduce(parts[1], w["w_branch"], mm["w_branch"], vv["w_branch"], "adam_w_branch", 88)
    out["w_out"] = _adam_reduce(parts[2], w["w_out"], mm["w_out"], vv["w_out"], "adam_w_out", 128)
    out["w_ple"] = _adam_reduce(parts[3], w["w_ple"], mm["w_ple"], vv["w_ple"], "adam_w_ple", 256)
    cols = _adam_reduce(parts[4], _pack_cols(w["b_gate"], w["conv_w"]), _pack_cols(mm["b_gate"], mm["conv_w"]),
                        _pack_cols(vv["b_gate"], vv["conv_w"]), "adam_cols", 8)
    out["b_gate"] = [a[:3, :BLK] for a in cols]
    out["conv_w"] = [a[:CONV_WIDTH, BLK:] for a in cols]
    packed = _adam_reduce(parts[5], _pack_small({k: w[k] for k in SMALL_NAMES}), _pack_small({k: mm[k] for k in SMALL_NAMES}),
                          _pack_small({k: vv[k] for k in SMALL_NAMES}), "adam_small", SMALL_ROWS)
    unpacked = [_unpack_small(a) for a in packed]
    for k in SMALL_NAMES:
        out[k] = [u[k] for u in unpacked]
    loss_rows = unpacked[0]["loss"].reshape(D_MODEL // BLK, BLK)
    loss = _lane_total(loss_rows).reshape(())

    def shaped(k, a):
        return a if k == "rel_bias" else a[None]

    results = [loss, grad_x]
    for i in range(4):
        results += [shaped(k, out[k][i]) for k in WEIGHT_ORDER]
    return tuple(results)
```

```python
import functools
import math

import jax
import jax.numpy as jnp
from jax import lax
from jax.experimental import pallas as pl
from jax.experimental.pallas import tpu as pltpu

F32 = jnp.float32
BF16 = jnp.bfloat16

N_DEV = 8
D_MODEL = 1024
SEQ = 2048
HEAD_DIM = 64
HEADS_PER_GROUP = 12
DILATED_PATTERNS = ((128, 1), (512, 4), (2048, 16))
N_GROUPS = 3
ATT_HEADS = N_GROUPS * HEADS_PER_GROUP
GROUP_COLS = HEADS_PER_GROUP * HEAD_DIM
ATT_QKV = ATT_HEADS * HEAD_DIM
BLK = 128
NUM_BUCKETS = 32
MAX_DISTANCE = 2048
D_INNER = 2048
SSM_HEADS = 32
SSM_GROUPS = 4
HEADS_PER_SSM_GROUP = SSM_HEADS // SSM_GROUPS
D_STATE = 128
CONV_WIDTH = 4
CONV_DIM = D_INNER + 2 * SSM_GROUPS * D_STATE
PLE_DIM = 256
ALPHA = 2.0 ** 0.25
LN_EPS = 1e-5
RMS_EPS = 1e-5
IN_COLS = 15904
IN_SHARD = IN_COLS // N_DEV
BRANCH_ROWS = GROUP_COLS + D_INNER

Q_END = ATT_QKV
K_END = 2 * ATT_QKV
V_END = 3 * ATT_QKV
GATT_END = V_END + GROUP_COLS
Z_END = GATT_END + D_INNER
XBC_END = Z_END + CONV_DIM
DT_END = XBC_END + SSM_HEADS
GMERGE_END = DT_END + 2 * D_MODEL

R_GM, R_Z, R_XBC, R_GPLE, R_GATT, R_DT = 0, 2048, 4096, 7168, 8192, 8960
QKV_OFF = 9216
HEAD_PAIRS = GROUP_COLS // BLK
QKV_G = 3 * GROUP_COLS
HCAT_COLS = QKV_OFF + N_GROUPS * QKV_G

ADAM_LR, ADAM_B1, ADAM_B2, ADAM_EPS, ADAM_WD, ADAM_STEP = 0.001, 0.9, 0.999, 1e-08, 0.01, 10

VMEM_LIMIT_BYTES = 56 * 1024 * 1024


def _params(*semantics):
    return pltpu.CompilerParams(dimension_semantics=semantics, vmem_limit_bytes=VMEM_LIMIT_BYTES)


def _sigmoid(v):
    return 1.0 / (1.0 + jnp.exp(-v))


def _silu_and_grad(v):
    s = _sigmoid(v)
    return v * s, s * (1.0 + v * (1.0 - s))


def _matmul(a, b, *, mode, out_dtype, name, tm, tn, tk, n_off=0, n=None, add=None, add_scale=1.0, n_outer=False):
    if mode == "nn":
        (m, k), n_full = a.shape, b.shape[1]
        assert b.shape[0] == k
    elif mode == "nt":
        (m, k), n_full = a.shape, b.shape[0]
        assert b.shape[1] == k
    else:
        (k, m), n_full = a.shape, b.shape[1]
        assert b.shape[0] == k
    n = n_full if n is None else n
    assert m % tm == 0 and n % tn == 0 and k % tk == 0 and n_off % tn == 0, (name, m, n, k)
    nk = k // tk
    jo = n_off // tn
    dims = {"nn": (((1,), (0,)), ((), ())), "nt": (((1,), (1,)), ((), ())), "tn": (((0,), (0,)), ((), ()))}[mode]

    def body(*refs):
        if add is None:
            a_ref, b_ref, o_ref = refs[:3]
            add_ref = None
        else:
            a_ref, b_ref, add_ref, o_ref = refs[:4]
        acc_ref = refs[-1] if nk > 1 else None
        prod = lax.dot_general(a_ref[...].astype(BF16), b_ref[...].astype(BF16), dims, preferred_element_type=F32)

        def finish(total):
            if add_ref is not None:
                total = total + add_scale * add_ref[...]
            o_ref[...] = total.astype(out_dtype)

        if nk == 1:
            finish(prod)
        else:
            kk = pl.program_id(2)

            @pl.when(kk == 0)
            def _():
                acc_ref[...] = prod

            @pl.when(jnp.logical_and(kk > 0, kk < nk - 1))
            def _():
                acc_ref[...] += prod

            @pl.when(kk == nk - 1)
            def _():
                finish(acc_ref[...] + prod)

    def ij(f):
        return (lambda g0, g1, kk: f(g1, g0, kk)) if n_outer else f

    if mode == "nn":
        a_spec = pl.BlockSpec((tm, tk), ij(lambda i, j, kk: (i, kk)))
        b_spec = pl.BlockSpec((tk, tn), ij(lambda i, j, kk: (kk, j + jo)))
    elif mode == "nt":
        a_spec = pl.BlockSpec((tm, tk), ij(lambda i, j, kk: (i, kk)))
        b_spec = pl.BlockSpec((tn, tk), ij(lambda i, j, kk: (j, kk)))
    else:
        a_spec = pl.BlockSpec((tk, tm), ij(lambda i, j, kk: (kk, i)))
        b_spec = pl.BlockSpec((tk, tn), ij(lambda i, j, kk: (kk, j)))
    in_specs = [a_spec, b_spec]
    args = [a, b]
    if add is not None:
        in_specs.append(pl.BlockSpec((tm, tn), ij(lambda i, j, kk: (i, j))))
        args.append(add)
    return pl.pallas_call(
        body, name=name,
        grid=(n // tn, m // tm, nk) if n_outer else (m // tm, n // tn, nk),
        in_specs=in_specs,
        out_specs=pl.BlockSpec((tm, tn), ij(lambda i, j, kk: (i, j))),
        out_shape=jax.ShapeDtypeStruct((m, n), out_dtype),
        scratch_shapes=[pltpu.VMEM((tm, tn), F32)] if nk > 1 else [],
        compiler_params=_params("parallel", "parallel", "arbitrary"),
    )(*args)


UNITS_PER_ITER = 4


def _band_mask(first):
    qi = lax.broadcasted_iota(jnp.int32, (BLK, BLK if first else 2 * BLK), 0)
    kj = lax.broadcasted_iota(jnp.int32, (BLK, BLK if first else 2 * BLK), 1)
    delta = qi - kj if first else qi + BLK - kj
    return jnp.logical_and(delta >= 0, delta <= BLK)


def _attn_specs(seq, g):
    qkv = [pl.BlockSpec((1, seq, BLK), functools.partial(
        lambda hp, b, part: (b, 0, QKV_OFF // BLK + (g * HEAD_PAIRS + hp) * 3 + part), part=part)) for part in range(3)]
    one = pl.BlockSpec((1, seq, BLK), lambda hp, b: (b, 0, hp))
    bias = pl.BlockSpec((2, BLK, 2 * BLK), lambda hp, b: (hp, 0, 0))
    return qkv, one, bias


def _attn_rows(dil, r, n, first):
    start = r + (dil * BLK) * n
    if dil == 1:
        start = pl.multiple_of(start, BLK)
        rows = pl.ds(start, BLK)
        keys = rows if first else pl.ds(pl.multiple_of(start - BLK, BLK), 2 * BLK)
    else:
        rows = pl.ds(start, BLK, stride=dil)
        keys = rows if first else pl.ds(start - dil * BLK, 2 * BLK, stride=dil)
    return rows, keys


def _attn_schedule(dil, nb, unit):
    def blocks_of(r):
        unit(r, 0, True)
        for n in range(1, UNITS_PER_ITER):
            unit(r, n, False)
        if nb > UNITS_PER_ITER:
            def more(i, carry):
                for jj in range(UNITS_PER_ITER):
                    unit(r, i * UNITS_PER_ITER + jj, False)
                return carry
            lax.fori_loop(1, nb // UNITS_PER_ITER, more, 0)

    if nb >= UNITS_PER_ITER:
        assert nb % UNITS_PER_ITER == 0
        if dil == 1:
            blocks_of(0)
        else:
            def per_residue(r, carry):
                blocks_of(r)
                return carry
            lax.fori_loop(0, dil, per_residue, 0)
    else:
        assert nb == 1 and dil % UNITS_PER_ITER == 0

        def residues(i, carry):
            for jj in range(UNITS_PER_ITER):
                unit(i * UNITS_PER_ITER + jj, 0, True)
            return carry
        lax.fori_loop(0, dil // UNITS_PER_ITER, residues, 0)


def _attn_fwd(hcat3, bias, g, dil, name):
    bsz, seq, _ = hcat3.shape
    nb = seq // dil // BLK
    scale = HEAD_DIM ** -0.5

    def body(q_ref, k_ref, v_ref, bias_ref, o_ref, lse_ref):
        masks = {True: _band_mask(True), False: _band_mask(False)}

        def unit(r, n, first):
            rows, keys = _attn_rows(dil, r, n, first)
            q2 = q_ref[0, rows, :].astype(BF16)
            k2 = k_ref[0, keys, :].astype(BF16)
            v2 = v_ref[0, keys, :].astype(BF16)
            outs, lses = [], []
            for j in range(2):
                lanes = slice(j * HEAD_DIM, (j + 1) * HEAD_DIM)
                bias_j = bias_ref[j, :, BLK:] if first else bias_ref[j]
                s = _nt(q2[:, lanes], k2[:, lanes]) * scale + bias_j
                s = jnp.where(masks[first], s, -jnp.inf)
                mx = jnp.max(s, axis=-1, keepdims=True)
                e = jnp.exp(s - mx)
                den = jnp.sum(e, axis=-1, keepdims=True)
                outs.append(jnp.dot(e.astype(BF16), v2[:, lanes], preferred_element_type=F32) / den)
                lses.append(jnp.broadcast_to(mx + jnp.log(den), (BLK, HEAD_DIM)))
            o_ref[0, rows, :] = jnp.concatenate(outs, axis=1)
            lse_ref[0, rows, :] = jnp.concatenate(lses, axis=1)

        _attn_schedule(dil, nb, unit)

    qkv_specs, one, bias_spec = _attn_specs(seq, g)
    shape = jax.ShapeDtypeStruct((bsz, seq, GROUP_COLS), F32)
    return pl.pallas_call(
        body, name=name,
        grid=(HEAD_PAIRS, bsz),
        in_specs=qkv_specs + [bias_spec],
        out_specs=[one, one],
        out_shape=[shape, shape],
        compiler_params=_params("parallel", "parallel"),
    )(hcat3, hcat3, hcat3, bias)


def _attn_bwd(hcat3, bias, lse, d_out, dd, g, dil, name):
    bsz, seq, _ = hcat3.shape
    nb = seq // dil // BLK
    scale = HEAD_DIM ** -0.5

    def body(q_ref, k_ref, v_ref, bias_ref, lse_ref, do_ref, dd_ref, dqkv_ref, dbias_ref, dq_acc, dk_acc, dv_acc):
        @pl.when(pl.program_id(1) == 0)
        def _():
            dbias_ref[...] = jnp.zeros_like(dbias_ref)

        dk_acc[...] = jnp.zeros_like(dk_acc)
        dv_acc[...] = jnp.zeros_like(dv_acc)
        masks = {True: _band_mask(True), False: _band_mask(False)}

        def unit(r, n, first):
            rows, keys = _attn_rows(dil, r, n, first)
            q2 = q_ref[0, rows, :].astype(BF16)
            k2 = k_ref[0, keys, :].astype(BF16)
            v2 = v_ref[0, keys, :].astype(BF16)
            do2 = do_ref[0, rows, :].astype(BF16)
            lse2 = lse_ref[0, rows, :]
            dd2 = dd_ref[0, rows, :]
            dqs, dks, dvs = [], [], []
            for j in range(2):
                lanes = slice(j * HEAD_DIM, (j + 1) * HEAD_DIM)
                q, kb, vb, do = q2[:, lanes], k2[:, lanes], v2[:, lanes], do2[:, lanes]
                delta = jnp.sum(dd2[:, lanes], axis=-1, keepdims=True)
                bias_j = bias_ref[j, :, BLK:] if first else bias_ref[j]
                s = _nt(q, kb) * scale + bias_j
                p = jnp.where(masks[first], jnp.exp(s - lse2[:, j * HEAD_DIM:j * HEAD_DIM + 1]), 0.0)
                ds = p * (_nt(do, vb) - delta)
                ds16 = ds.astype(BF16)
                dqs.append(jnp.dot(ds16, kb, preferred_element_type=F32) * scale)
                dks.append(_tn(ds16, q) * scale)
                dvs.append(_tn(p.astype(BF16), do))
                if first:
                    dbias_ref[j, :, BLK:] += ds
                else:
                    dbias_ref[j] += ds
            dq_acc[rows, :] = jnp.concatenate(dqs, axis=1)
            dk_acc[keys, :] += jnp.concatenate(dks, axis=1)
            dv_acc[keys, :] += jnp.concatenate(dvs, axis=1)

        _attn_schedule(dil, nb, unit)
        dqkv_ref[0, :, 0:BLK] = dq_acc[...].astype(BF16)
        dqkv_ref[0, :, BLK:2 * BLK] = dk_acc[...].astype(BF16)
        dqkv_ref[0, :, 2 * BLK:3 * BLK] = dv_acc[...].astype(BF16)

    qkv_specs, one, bias_spec = _attn_specs(seq, g)
    return pl.pallas_call(
        body, name=name,
        grid=(HEAD_PAIRS, bsz),
        in_specs=qkv_specs + [bias_spec, one, one, one],
        out_specs=[pl.BlockSpec((1, seq, 3 * BLK), lambda hp, b: (b, 0, hp)), bias_spec],
        out_shape=[jax.ShapeDtypeStruct((bsz, seq, QKV_G), BF16), jax.ShapeDtypeStruct((HEADS_PER_GROUP, BLK, 2 * BLK), F32)],
        scratch_shapes=[pltpu.VMEM((seq, BLK), F32)] * 3,
        compiler_params=_params("arbitrary", "arbitrary"),
    )(hcat3, hcat3, hcat3, bias, lse, d_out, dd)


def _t5_buckets(dil):
    import numpy as np
    qi = np.arange(BLK)[:, None]
    kj = np.arange(2 * BLK)[None, :]
    dist = np.maximum(qi + BLK - kj, 0) * dil
    max_exact = NUM_BUCKETS // 2
    d_f = np.maximum(dist, 1).astype(np.float32)
    large = max_exact + (np.log(d_f / np.float32(max_exact)) / np.float32(math.log(MAX_DISTANCE / max_exact))
                         * np.float32(NUM_BUCKETS - max_exact)).astype(np.int32)
    large = np.minimum(large, NUM_BUCKETS - 1)
    return np.where(dist < max_exact, dist, large).astype(np.int32).reshape(-1)


def _bias_matrix(table_g, dil):
    buckets = jnp.asarray(_t5_buckets(dil))
    onehot = (buckets[None, :] == lax.broadcasted_iota(jnp.int32, (NUM_BUCKETS, 1), 0)).astype(F32)
    tk = 4096

    def body(t_ref, oh_ref, o_ref):
        o_ref[...] = _dot_hi(t_ref[...], oh_ref[...])

    bias = pl.pallas_call(
        body, name=f"bias_matrix{dil}",
        grid=(onehot.shape[1] // tk,),
        in_specs=[pl.BlockSpec((HEADS_PER_GROUP, NUM_BUCKETS), lambda kk: (0, 0)), pl.BlockSpec((NUM_BUCKETS, tk), lambda kk: (0, kk))],
        out_specs=pl.BlockSpec((HEADS_PER_GROUP, tk), lambda kk: (0, kk)),
        out_shape=jax.ShapeDtypeStruct((HEADS_PER_GROUP, onehot.shape[1]), F32),
        compiler_params=_params("parallel"),
    )(table_g.T, onehot)
    return bias.reshape(HEADS_PER_GROUP, BLK, 2 * BLK), onehot


def _bias_grad(dbias, onehot, name):
    flat = dbias.reshape(HEADS_PER_GROUP, 2 * BLK * BLK)
    tk = 4096

    def body(oh_ref, g_ref, o_ref):
        @pl.when(pl.program_id(0) == 0)
        def _():
            o_ref[...] = jnp.zeros_like(o_ref)
        o_ref[...] += lax.dot_general(oh_ref[...], g_ref[...], (((1,), (1,)), ((), ())),
                                      preferred_element_type=F32, precision=lax.Precision.HIGHEST)

    return pl.pallas_call(
        body, name=name,
        grid=(flat.shape[1] // tk,),
        in_specs=[pl.BlockSpec((NUM_BUCKETS, tk), lambda kk: (0, kk)), pl.BlockSpec((HEADS_PER_GROUP, tk), lambda kk: (0, kk))],
        out_specs=pl.BlockSpec((NUM_BUCKETS, HEADS_PER_GROUP), lambda kk: (0, 0)),
        out_shape=jax.ShapeDtypeStruct((NUM_BUCKETS, HEADS_PER_GROUP), F32),
        compiler_params=_params("arbitrary"),
    )(onehot, flat)


def _shift_rows(u, s, down):
    if s == 0:
        return u
    rows = lax.broadcasted_iota(jnp.int32, u.shape, 0)
    n = u.shape[0]
    if down:
        return jnp.where(rows >= s, pltpu.roll(u, s, 0), 0.0)
    return jnp.where(rows < n - s, pltpu.roll(u, n - s, 0), 0.0)


def _conv_pre(u, w_ref, b_ref):
    acc = b_ref[0:1, :] + w_ref[CONV_WIDTH - 1:CONV_WIDTH, :] * u
    for s in range(1, CONV_WIDTH):
        acc = acc + w_ref[CONV_WIDTH - 1 - s:CONV_WIDTH - s, :] * _shift_rows(u, s, True)
    return acc


def _conv_fwd(rest3, conv_w, conv_b):
    bsz, seq, _ = rest3.shape

    def body(u_ref, w_ref, b_ref, o_ref):
        pre = _conv_pre(u_ref[0], w_ref, b_ref)
        o_ref[0] = pre * _sigmoid(pre)

    return pl.pallas_call(
        body, name="conv_fwd",
        grid=(bsz, CONV_DIM // BLK),
        in_specs=[pl.BlockSpec((1, seq, BLK), lambda b, c: (b, 0, R_XBC // BLK + c)),
                  pl.BlockSpec((CONV_WIDTH, BLK), lambda b, c: (0, c)),
                  pl.BlockSpec((1, BLK), lambda b, c: (0, c))],
        out_specs=pl.BlockSpec((1, seq, BLK), lambda b, c: (b, 0, c)),
        out_shape=jax.ShapeDtypeStruct((bsz, seq, CONV_DIM), F32),
        compiler_params=_params("parallel", "parallel"),
    )(rest3, conv_w, conv_b)


def _conv_bwd(rest3, conv_w, conv_b, d_act):
    bsz, seq, _ = rest3.shape

    def body(u_ref, w_ref, b_ref, g_ref, du_ref, dw_ref):
        @pl.when(pl.program_id(1) == 0)
        def _():
            dw_ref[...] = jnp.zeros_like(dw_ref)

        u = u_ref[0]
        pre = _conv_pre(u, w_ref, b_ref)
        d_pre = g_ref[0] * _silu_and_grad(pre)[1]
        du = w_ref[CONV_WIDTH - 1:CONV_WIDTH, :] * d_pre
        dw_ref[CONV_WIDTH - 1:CONV_WIDTH, :] += jnp.sum(d_pre * u, axis=0, keepdims=True)
        for s in range(1, CONV_WIDTH):
            du = du + w_ref[CONV_WIDTH - 1 - s:CONV_WIDTH - s, :] * _shift_rows(d_pre, s, False)
            dw_ref[CONV_WIDTH - 1 - s:CONV_WIDTH - s, :] += jnp.sum(d_pre * _shift_rows(u, s, True), axis=0, keepdims=True)
        dw_ref[CONV_WIDTH:CONV_WIDTH + 1, :] += jnp.sum(d_pre, axis=0, keepdims=True)
        du_ref[0] = du.astype(BF16)

    return pl.pallas_call(
        body, name="conv_bwd",
        grid=(CONV_DIM // BLK, bsz),
        in_specs=[pl.BlockSpec((1, seq, BLK), lambda c, b: (b, 0, R_XBC // BLK + c)),
                  pl.BlockSpec((CONV_WIDTH, BLK), lambda c, b: (0, c)),
                  pl.BlockSpec((1, BLK), lambda c, b: (0, c)),
                  pl.BlockSpec((1, seq, BLK), lambda c, b: (b, 0, c))],
        out_specs=[pl.BlockSpec((1, seq, BLK), lambda c, b: (b, 0, c)),
                   pl.BlockSpec((8, BLK), lambda c, b: (0, c))],
        out_shape=[jax.ShapeDtypeStruct((bsz, seq, CONV_DIM), BF16), jax.ShapeDtypeStruct((8, CONV_DIM), F32)],
        compiler_params=_params("parallel", "arbitrary"),
    )(rest3, conv_w, conv_b, d_act)


P_DTB, P_ALOG, P_DSKIP = 0, 1, 2


def _softplus(v):
    return jnp.maximum(v, 0.0) + jnp.log(1.0 + jnp.exp(-jnp.abs(v)))


def _dot_hi(a, b):
    return jnp.dot(a, b, preferred_element_type=F32, precision=lax.Precision.HIGHEST)


def _nt(a, b):
    return lax.dot_general(a, b, (((1,), (1,)), ((), ())), preferred_element_type=F32)


def _tn(a, b):
    return lax.dot_general(a, b, (((0,), (0,)), ((), ())), preferred_element_type=F32)


def _sum_all(v):
    return jnp.sum(jnp.sum(v, axis=1, keepdims=True), axis=0, keepdims=True)


def _ssd_decays(dtr, dtr_t, prow_ref, pcol_ref):
    ri = lax.broadcasted_iota(jnp.int32, (BLK, BLK), 0)
    ci = lax.broadcasted_iota(jnp.int32, (BLK, BLK), 1)
    tri = (ri >= ci).astype(F32)
    tri_u = (ri <= ci).astype(F32)
    pre = dtr + prow_ref[P_DTB:P_DTB + 1, :]
    dt = _softplus(pre)
    ah_row = -jnp.exp(prow_ref[P_ALOG:P_ALOG + 1, :])
    acs = _dot_hi(tri, dt * ah_row)
    pre_t = dtr_t + pcol_ref[:, P_DTB:P_DTB + 1]
    dt_t = _softplus(pre_t)
    ah_col = -jnp.exp(pcol_ref[:, P_ALOG:P_ALOG + 1])
    acs_t = _dot_hi(dt_t * ah_col, tri_u)
    return dict(tri=tri, tri_u=tri_u, pre=pre, dt=dt, ah_row=ah_row, acs=acs, pre_t=pre_t, dt_t=dt_t, ah_col=ah_col,
                acs_t=acs_t, causal=ri >= ci, last_row=ri[:, 0:1] == BLK - 1)


def _ssd_head(h, d, x_ref, s_in, g_mat):
    col = d["acs"][:, h:h + 1]
    row = d["acs_t"][h:h + 1, :]
    lm = jnp.exp(jnp.where(d["causal"], col - row, -jnp.inf))
    m = g_mat * lm
    xh = x_ref[0, :, pl.ds(h * HEAD_DIM, HEAD_DIM)]
    dtc = d["dt"][:, h:h + 1]
    xd = xh * dtc
    e = jnp.exp(col)
    clast = d["acs"][BLK - 1:BLK, h:h + 1]
    f = jnp.exp(clast - col)
    return dict(col=col, lm=lm, m=m, xh=xh, dtc=dtc, xd=xd, e=e, ecl=jnp.exp(clast), f=f, xf=xd * f)


def _ssd_specs(nc, rev):
    cidx = (lambda c: nc - 1 - c) if rev else (lambda c: c)
    x_spec = pl.BlockSpec((1, BLK, D_INNER), lambda b, c: (b, cidx(c), 0))
    bm_spec = pl.BlockSpec((1, BLK, SSM_GROUPS * D_STATE), lambda b, c: (b, cidx(c), D_INNER // (SSM_GROUPS * D_STATE)))
    cm_spec = pl.BlockSpec((1, BLK, SSM_GROUPS * D_STATE), lambda b, c: (b, cidx(c), D_INNER // (SSM_GROUPS * D_STATE) + 1))
    dt_spec = pl.BlockSpec((1, BLK, BLK), lambda b, c: (b, cidx(c), R_DT // BLK))
    dtt_spec = pl.BlockSpec((1, BLK, BLK), lambda b, c: (b, 0, cidx(c)))
    prow_spec = pl.BlockSpec((8, BLK), lambda b, c: (0, 0))
    pcol_spec = pl.BlockSpec((BLK, 8), lambda b, c: (0, 0))
    st_spec = pl.BlockSpec((1, 1, SSM_HEADS, HEAD_DIM, D_STATE), lambda b, c: (b, cidx(c), 0, 0, 0))
    y_spec = pl.BlockSpec((1, BLK, D_INNER), lambda b, c: (b, cidx(c), 0))
    return x_spec, bm_spec, cm_spec, dt_spec, dtt_spec, prow_spec, pcol_spec, st_spec, y_spec


def _ssd_fwd(xa, rest3, dtr_t, prow, pcol):
    bsz, seq, _ = xa.shape
    nc = seq // BLK

    def body(x_ref, bm_ref, cm_ref, dt_ref, dtt_ref, prow_ref, pcol_ref, y_ref, sin_ref, s_ref):
        @pl.when(pl.program_id(1) == 0)
        def _():
            s_ref[...] = jnp.zeros_like(s_ref)

        d = _ssd_decays(dt_ref[0], dtt_ref[0], prow_ref, pcol_ref)
        for g in range(SSM_GROUPS):
            lanes = pl.ds(g * D_STATE, D_STATE)
            bg = bm_ref[0, :, lanes].astype(BF16)
            cg = cm_ref[0, :, lanes].astype(BF16)
            g_mat = _nt(cg, bg)
            for hh in range(HEADS_PER_SSM_GROUP):
                h = g * HEADS_PER_SSM_GROUP + hh
                s_in = s_ref[h]
                sin_ref[0, 0, h] = s_in
                q = _ssd_head(h, d, x_ref, s_in, g_mat)
                y_diag = jnp.dot(q["m"].astype(BF16), q["xd"].astype(BF16), preferred_element_type=F32)
                y_off = _nt(cg, s_in.astype(BF16)) * q["e"]
                s_ref[h] = s_in * q["ecl"] + _tn(q["xf"].astype(BF16), bg)
                y_ref[0, :, pl.ds(h * HEAD_DIM, HEAD_DIM)] = y_diag + y_off + prow_ref[P_DSKIP:P_DSKIP + 1, h:h + 1] * q["xh"]

    x_spec, bm_spec, cm_spec, dt_spec, dtt_spec, prow_spec, pcol_spec, st_spec, y_spec = _ssd_specs(nc, False)
    return pl.pallas_call(
        body, name="ssd_fwd",
        grid=(bsz, nc),
        in_specs=[x_spec, bm_spec, cm_spec, dt_spec, dtt_spec, prow_spec, pcol_spec],
        out_specs=[y_spec, st_spec],
        out_shape=[jax.ShapeDtypeStruct((bsz, seq, D_INNER), F32),
                   jax.ShapeDtypeStruct((bsz, nc, SSM_HEADS, HEAD_DIM, D_STATE), F32)],
        scratch_shapes=[pltpu.VMEM((SSM_HEADS, HEAD_DIM, D_STATE), F32)],
        compiler_params=_params("parallel", "arbitrary"),
    )(xa, xa, xa, rest3, dtr_t, prow, pcol)


def _ssd_bwd(xa, rest3, dtr_t, prow, pcol, s_in_all, dy):
    bsz, seq, _ = xa.shape
    nc = seq // BLK

    def body(x_ref, bm_ref, cm_ref, dt_ref, dtt_ref, prow_ref, pcol_ref, sin_ref, dy_ref,
             dxa_ref, ddt_ref, ddtt_ref, gprow_ref, gpcol_ref, ds_ref, dc_ref, ddtc_ref, drt_ref):
        first = jnp.logical_and(pl.program_id(0) == 0, pl.program_id(1) == 0)

        @pl.when(first)
        def _():
            gprow_ref[...] = jnp.zeros_like(gprow_ref)
            gpcol_ref[...] = jnp.zeros_like(gpcol_ref)

        @pl.when(pl.program_id(1) == 0)
        def _():
            ds_ref[...] = jnp.zeros_like(ds_ref)

        dc_ref[...] = jnp.zeros_like(dc_ref)
        ddtc_ref[...] = jnp.zeros_like(ddtc_ref)
        drt_ref[...] = jnp.zeros_like(drt_ref)
        d = _ssd_decays(dt_ref[0], dtt_ref[0], prow_ref, pcol_ref)
        for g in range(SSM_GROUPS):
            lanes = pl.ds(g * D_STATE, D_STATE)
            bg = bm_ref[0, :, lanes].astype(BF16)
            cg = cm_ref[0, :, lanes].astype(BF16)
            g_mat = _nt(cg, bg)
            d_g = jnp.zeros((BLK, BLK), F32)
            d_bg = jnp.zeros((BLK, D_STATE), F32)
            d_cg = jnp.zeros((BLK, D_STATE), F32)
            for hh in range(HEADS_PER_SSM_GROUP):
                h = g * HEADS_PER_SSM_GROUP + hh
                head_lanes = pl.ds(h * HEAD_DIM, HEAD_DIM)
                s_in = sin_ref[0, 0, h]
                s_in16 = s_in.astype(BF16)
                q = _ssd_head(h, d, x_ref, s_in, g_mat)
                m16, xd16 = q["m"].astype(BF16), q["xd"].astype(BF16)
                d_y = dy_ref[0, :, head_lanes]
                d_y16 = d_y.astype(BF16)
                d_so = ds_ref[h]
                d_so16 = d_so.astype(BF16)
                d_x = prow_ref[P_DSKIP:P_DSKIP + 1, h:h + 1] * d_y
                gprow_ref[P_DSKIP:P_DSKIP + 1, h:h + 1] += _sum_all(d_y * q["xh"])
                d_m = _nt(d_y16, xd16)
                d_xd = _tn(m16, d_y16)
                w = d_m * q["m"]
                d_g = d_g + d_m * q["lm"]
                d_col = jnp.sum(w, axis=1, keepdims=True)
                drt_ref[h:h + 1, :] = -jnp.sum(w, axis=0, keepdims=True)
                qmat = _nt(cg, s_in16)
                d_q16 = (d_y * q["e"]).astype(BF16)
                d_col = d_col + jnp.sum(d_y * qmat, axis=1, keepdims=True) * q["e"]
                d_cg = d_cg + jnp.dot(d_q16, s_in16, preferred_element_type=F32)
                d_sin = _tn(d_q16, cg) + d_so * q["ecl"]
                d_clast = _sum_all(d_so * s_in) * q["ecl"]
                d_xf = _nt(bg, d_so16)
                d_bg = d_bg + jnp.dot(q["xf"].astype(BF16), d_so16, preferred_element_type=F32)
                d_xd = d_xd + d_xf * q["f"]
                d_f = jnp.sum(d_xf * q["xd"], axis=1, keepdims=True) * q["f"]
                d_clast = d_clast + jnp.sum(d_f, axis=0, keepdims=True)
                d_col = d_col - d_f + jnp.where(d["last_row"], d_clast, 0.0)
                dxa_ref[0, :, head_lanes] = d_x + d_xd * q["dtc"]
                dc_ref[:, h:h + 1] = d_col
                ddtc_ref[:, h:h + 1] = jnp.sum(d_xd * q["xh"], axis=1, keepdims=True)
                ds_ref[h] = d_sin
            d_g16 = d_g.astype(BF16)
            dxa_ref[0, :, pl.ds(D_INNER + g * D_STATE, D_STATE)] = d_bg + _tn(d_g16, cg)
            dxa_ref[0, :, pl.ds(D_INNER + (SSM_GROUPS + g) * D_STATE, D_STATE)] = d_cg + jnp.dot(d_g16, bg, preferred_element_type=F32)
        d_a = _dot_hi(d["tri_u"], dc_ref[...])
        d_pre = (ddtc_ref[...] + d_a * d["ah_row"]) * _sigmoid(d["pre"])
        ddt_ref[0] = d_pre
        gprow_ref[P_DTB:P_DTB + 1, :] += jnp.sum(d_pre, axis=0, keepdims=True)
        gprow_ref[P_ALOG:P_ALOG + 1, :] += jnp.sum(d_a * d["dt"], axis=0, keepdims=True) * d["ah_row"]
        d_at = _dot_hi(drt_ref[...], d["tri"])
        d_pre_t = d_at * d["ah_col"] * _sigmoid(d["pre_t"])
        ddtt_ref[0] = d_pre_t
        gpcol_ref[:, P_DTB:P_DTB + 1] += jnp.sum(d_pre_t, axis=1, keepdims=True)
        gpcol_ref[:, P_ALOG:P_ALOG + 1] += jnp.sum(d_at * d["dt_t"], axis=1, keepdims=True) * d["ah_col"]

    x_spec, bm_spec, cm_spec, dt_spec, dtt_spec, prow_spec, pcol_spec, st_spec, y_spec = _ssd_specs(nc, True)
    return pl.pallas_call(
        body, name="ssd_bwd",
        grid=(bsz, nc),
        in_specs=[x_spec, bm_spec, cm_spec, dt_spec, dtt_spec, prow_spec, pcol_spec, st_spec, y_spec],
        out_specs=[pl.BlockSpec((1, BLK, CONV_DIM), lambda b, c: (b, nc - 1 - c, 0)),
                   pl.BlockSpec((1, BLK, BLK), lambda b, c: (b, nc - 1 - c, 0)),
                   pl.BlockSpec((1, BLK, BLK), lambda b, c: (b, 0, nc - 1 - c)),
                   prow_spec, pcol_spec],
        out_shape=[jax.ShapeDtypeStruct((bsz, seq, CONV_DIM), F32),
                   jax.ShapeDtypeStruct((bsz, seq, BLK), F32),
                   jax.ShapeDtypeStruct((bsz, BLK, seq), F32),
                   jax.ShapeDtypeStruct((8, BLK), F32),
                   jax.ShapeDtypeStruct((BLK, 8), F32)],
        scratch_shapes=[pltpu.VMEM((SSM_HEADS, HEAD_DIM, D_STATE), F32), pltpu.VMEM((BLK, BLK), F32),
                        pltpu.VMEM((BLK, BLK), F32), pltpu.VMEM((BLK, BLK), F32)],
        compiler_params=_params("arbitrary", "arbitrary"),
    )(xa, xa, xa, rest3, dtr_t, prow, pcol, s_in_all, dy)


ROW_TILE = 256
CMB_COLS = 256
RMS_COLS = D_INNER // SSM_GROUPS


def _combine_weights(l_refs):
    ls = [r[...] for r in l_refs]
    mx = jnp.maximum(jnp.maximum(ls[0], ls[1]), ls[2])
    es = [jnp.exp(l - mx) for l in ls]
    inv = 1.0 / (es[0] + es[1] + es[2])
    return [e * inv for e in es]


def _combine_specs():
    a = pl.BlockSpec((ROW_TILE, CMB_COLS), lambda i, j: (i, j))
    gatt = pl.BlockSpec((ROW_TILE, CMB_COLS), lambda i, j: (i, R_GATT // CMB_COLS + j))
    return a, gatt


def _combine_fwd(outs, lses, rest):
    t = rest.shape[0]

    def body(o0, o1, o2, l0, l1, l2, ga_ref, oa_ref):
        ws = _combine_weights((l0, l1, l2))
        o = ws[0] * o0[...] + ws[1] * o1[...] + ws[2] * o2[...]
        oa_ref[...] = (o * _silu_and_grad(ga_ref[...])[0]).astype(BF16)

    a, gatt = _combine_specs()
    return pl.pallas_call(
        body, name="combine_fwd",
        grid=(t // ROW_TILE, GROUP_COLS // CMB_COLS),
        in_specs=[a] * 6 + [gatt],
        out_specs=a,
        out_shape=jax.ShapeDtypeStruct((t, GROUP_COLS), BF16),
        compiler_params=_params("parallel", "parallel"),
    )(*outs, *lses, rest)


def _combine_bwd(outs, lses, rest, d_oa):
    t = rest.shape[0]

    def body(o0, o1, o2, l0, l1, l2, ga_ref, doa_ref, do0, do1, do2, dd0, dd1, dd2, dga_ref):
        ws = _combine_weights((l0, l1, l2))
        o = ws[0] * o0[...] + ws[1] * o1[...] + ws[2] * o2[...]
        sg, dsg = _silu_and_grad(ga_ref[...])
        d_oa_v = doa_ref[...]
        d_o = d_oa_v * sg
        dga_ref[...] = (d_oa_v * o * dsg).astype(BF16)
        for w, do_ref, dd_ref in zip(ws, (do0, do1, do2), (dd0, dd1, dd2)):
            d_out = w * d_o
            do_ref[...] = d_out
            dd_ref[...] = d_out * o

    a, gatt = _combine_specs()
    s16 = jax.ShapeDtypeStruct((t, GROUP_COLS), BF16)
    s32 = jax.ShapeDtypeStruct((t, GROUP_COLS), F32)
    return pl.pallas_call(
        body, name="combine_bwd",
        grid=(t // ROW_TILE, GROUP_COLS // CMB_COLS),
        in_specs=[a] * 6 + [gatt, a],
        out_specs=[a] * 7,
        out_shape=[s32, s32, s32, s32, s32, s32, s16],
        compiler_params=_params("parallel", "parallel"),
    )(*outs, *lses, rest, d_oa)


def _gatenorm_fwd(y, rest, norm_w):
    t = rest.shape[0]

    def body(y_ref, z_ref, w_ref, o_ref):
        u = y_ref[...] * _silu_and_grad(z_ref[...])[0]
        rs = lax.rsqrt(jnp.mean(u * u, axis=-1, keepdims=True) + RMS_EPS)
        o_ref[...] = (u * rs * w_ref[...]).astype(BF16)

    return pl.pallas_call(
        body, name="gatenorm_fwd",
        grid=(t // ROW_TILE, SSM_GROUPS),
        in_specs=[pl.BlockSpec((ROW_TILE, RMS_COLS), lambda i, j: (i, j)),
                  pl.BlockSpec((ROW_TILE, RMS_COLS), lambda i, j: (i, R_Z // RMS_COLS + j)),
                  pl.BlockSpec((1, RMS_COLS), lambda i, j: (0, j))],
        out_specs=pl.BlockSpec((ROW_TILE, RMS_COLS), lambda i, j: (i, j)),
        out_shape=jax.ShapeDtypeStruct((t, D_INNER), BF16),
        compiler_params=_params("parallel", "parallel"),
    )(y, rest, norm_w)


def _gatenorm_bwd(y, rest, norm_w, d_ys):
    t = rest.shape[0]

    def body(y_ref, z_ref, w_ref, g_ref, dy_ref, dz_ref, dw_ref):
        @pl.when(pl.program_id(1) == 0)
        def _():
            dw_ref[...] = jnp.zeros_like(dw_ref)

        yv = y_ref[...]
        sz, dsz = _silu_and_grad(z_ref[...])
        u = yv * sz
        rs = lax.rsqrt(jnp.mean(u * u, axis=-1, keepdims=True) + RMS_EPS)
        un = u * rs
        g = g_ref[...]
        dw_ref[0:1, :] += jnp.sum(g * un, axis=0, keepdims=True)
        d_un = g * w_ref[...]
        d_u = rs * (d_un - un * jnp.mean(d_un * un, axis=-1, keepdims=True))
        dy_ref[...] = d_u * sz
        dz_ref[...] = (d_u * yv * dsz).astype(BF16)

    blk = pl.BlockSpec((ROW_TILE, RMS_COLS), lambda j, i: (i, j))
    return pl.pallas_call(
        body, name="gatenorm_bwd",
        grid=(SSM_GROUPS, t // ROW_TILE),
        in_specs=[blk, pl.BlockSpec((ROW_TILE, RMS_COLS), lambda j, i: (i, R_Z // RMS_COLS + j)),
                  pl.BlockSpec((1, RMS_COLS), lambda j, i: (0, j)), blk],
        out_specs=[blk, blk, pl.BlockSpec((8, RMS_COLS), lambda j, i: (0, j))],
        out_shape=[jax.ShapeDtypeStruct((t, D_INNER), F32), jax.ShapeDtypeStruct((t, D_INNER), BF16),
                   jax.ShapeDtypeStruct((8, D_INNER), F32)],
        compiler_params=_params("parallel", "arbitrary"),
    )(y, rest, norm_w, d_ys)


def _row_specs():
    full = pl.BlockSpec((ROW_TILE, D_MODEL), lambda i: (i, 0))
    vec = pl.BlockSpec((8, D_MODEL), lambda i: (0, 0))
    at = lambda off: pl.BlockSpec((ROW_TILE, D_MODEL), lambda i: (i, off // D_MODEL))
    return full, vec, at


def _merge_fwd(y_a, y_b, rest, b_gate):
    t = rest.shape[0]

    def body(ya_ref, yb_ref, ga_ref, gb_ref, bg_ref, o_ref):
        sa = _sigmoid(ga_ref[...] + bg_ref[0:1, :])
        sb = _sigmoid(gb_ref[...] + bg_ref[1:2, :])
        o_ref[...] = (sa * ya_ref[...] + sb * yb_ref[...]).astype(BF16)

    full, vec, at = _row_specs()
    return pl.pallas_call(
        body, name="merge_fwd",
        grid=(t // ROW_TILE,),
        in_specs=[full, full, at(R_GM), at(R_GM + D_MODEL), vec],
        out_specs=full,
        out_shape=jax.ShapeDtypeStruct((t, D_MODEL), BF16),
        compiler_params=_params("parallel"),
    )(y_a, y_b, rest, rest, b_gate)


def _merge_bwd(y_a, y_b, rest, b_gate, d_merged):
    t = rest.shape[0]

    def body(ya_ref, yb_ref, ga_ref, gb_ref, bg_ref, dm_ref, dya_ref, dyb_ref, dga_ref, dgb_ref, dbg_ref):
        @pl.when(pl.program_id(0) == 0)
        def _():
            dbg_ref[...] = jnp.zeros_like(dbg_ref)

        dm = dm_ref[...]
        for row, y_ref, g_ref, dy_ref, dg_ref in ((0, ya_ref, ga_ref, dya_ref, dga_ref), (1, yb_ref, gb_ref, dyb_ref, dgb_ref)):
            s = _sigmoid(g_ref[...] + bg_ref[row:row + 1, :])
            dy_ref[...] = (dm * s).astype(BF16)
            dg = dm * y_ref[...] * s * (1.0 - s)
            dg_ref[...] = dg.astype(BF16)
            dbg_ref[row:row + 1, :] += jnp.sum(dg, axis=0, keepdims=True)

    full, vec, at = _row_specs()
    s16 = jax.ShapeDtypeStruct((t, D_MODEL), BF16)
    return pl.pallas_call(
        body, name="merge_bwd",
        grid=(t // ROW_TILE,),
        in_specs=[full, full, at(R_GM), at(R_GM + D_MODEL), vec, full],
        out_specs=[full, full, full, full, vec],
        out_shape=[s16, s16, s16, s16, jax.ShapeDtypeStruct((8, D_MODEL), F32)],
        compiler_params=_params("arbitrary"),
    )(y_a, y_b, rest, rest, b_gate, d_merged)


ST_LNG, ST_LNB, ST_BG2, ST_LOSS = 0, 1, 2, 3


def _final(x, mix, pw, rest, b_gate, ln_gb, target):
    t = rest.shape[0]

    def body(x_ref, mix_ref, pw_ref, gp_ref, bg_ref, ln_ref, tgt_ref, dpre_ref, dpre16_ref, dgp_ref, dpw_ref, st_ref):
        @pl.when(pl.program_id(0) == 0)
        def _():
            st_ref[...] = jnp.zeros_like(st_ref)

        sp = _sigmoid(gp_ref[...] + bg_ref[2:3, :])
        pw = pw_ref[...]
        pre = ALPHA * x_ref[...] + mix_ref[...] + sp * pw
        xc = pre - jnp.mean(pre, axis=-1, keepdims=True)
        rstd = lax.rsqrt(jnp.mean(xc * xc, axis=-1, keepdims=True) + LN_EPS)
        xhat = xc * rstd
        gain = ln_ref[0:1, :]
        err = xhat * gain + ln_ref[1:2, :] - tgt_ref[...]
        d_yo = err * (1.0 / D_MODEL)
        d_xhat = d_yo * gain
        d_pre = rstd * (d_xhat - jnp.mean(d_xhat, axis=-1, keepdims=True)
                        - xhat * jnp.mean(d_xhat * xhat, axis=-1, keepdims=True))
        dpre_ref[...] = d_pre
        dpre16_ref[...] = d_pre.astype(BF16)
        dgp = d_pre * pw * sp * (1.0 - sp)
        dgp_ref[...] = dgp.astype(BF16)
        dpw_ref[...] = (d_pre * sp).astype(BF16)
        st_ref[ST_LNG:ST_LNG + 1, :] += jnp.sum(d_yo * xhat, axis=0, keepdims=True)
        st_ref[ST_LNB:ST_LNB + 1, :] += jnp.sum(d_yo, axis=0, keepdims=True)
        st_ref[ST_BG2:ST_BG2 + 1, :] += jnp.sum(dgp, axis=0, keepdims=True)
        st_ref[ST_LOSS:ST_LOSS + 1, :] += jnp.sum(err * err, axis=0, keepdims=True) * (0.5 / D_MODEL)

    full, vec, at = _row_specs()
    s16 = jax.ShapeDtypeStruct((t, D_MODEL), BF16)
    return pl.pallas_call(
        body, name="final",
        grid=(t // ROW_TILE,),
        in_specs=[full, full, full, at(R_GPLE), vec, vec, full],
        out_specs=[full, full, full, full, vec],
        out_shape=[jax.ShapeDtypeStruct((t, D_MODEL), F32), s16, s16, s16, jax.ShapeDtypeStruct((8, D_MODEL), F32)],
        compiler_params=_params("arbitrary"),
    )(x, mix, pw, rest, b_gate, ln_gb, target)


def _mesh_position():
    return lax.axis_index("x"), lax.axis_index("y"), lax.axis_index("c")


def _flip(pos, k):
    x, y, c = pos
    return ((1 - x) if k & 4 else x, (1 - y) if k & 2 else y, (1 - c) if k & 1 else c)


def _linear(pos):
    return 4 * pos[0] + 2 * pos[1] + pos[2]


def _exchange(arrays, scatter, name):
    n = len(arrays)

    def body(*refs):
        ins, outs = refs[:n], refs[n:2 * n]
        send_sems, recv_sems, local_sems = refs[2 * n:]
        me = _mesh_position()
        me_i = _linear(me)

        def src_for(i, dest_i):
            return ins[i].at[dest_i] if scatter[i] else ins[i]

        local = [pltpu.make_async_copy(src_for(i, me_i), outs[i].at[me_i], local_sems.at[i]) for i in range(n)]
        for cp in local:
            cp.start()
        started = []
        for k in range(1, N_DEV):
            peer = _flip(me, k)
            peer_i = _linear(peer)
            for i in range(n):
                sem = i * (N_DEV - 1) + k - 1
                cp = pltpu.make_async_remote_copy(
                    src_ref=src_for(i, peer_i), dst_ref=outs[i].at[me_i], send_sem=send_sems.at[sem],
                    recv_sem=recv_sems.at[sem], device_id=peer, device_id_type=pl.DeviceIdType.MESH)
                cp.start()
                started.append(cp)
        for k in range(1, N_DEV):
            peer = _flip(me, k)
            peer_i = _linear(peer)
            for i in range(n):
                sem = i * (N_DEV - 1) + k - 1
                pltpu.make_async_remote_copy(
                    src_ref=src_for(i, peer_i), dst_ref=outs[i].at[peer_i], send_sem=send_sems.at[sem],
                    recv_sem=recv_sems.at[sem], device_id=peer, device_id_type=pl.DeviceIdType.MESH).wait_recv()
        for cp in started:
            cp.wait_send()
        for cp in local:
            cp.wait()

    any_spec = pl.BlockSpec(memory_space=pl.ANY)
    out_shape = [jax.ShapeDtypeStruct(a.shape if s else (N_DEV,) + a.shape, a.dtype) for a, s in zip(arrays, scatter)]
    return pl.pallas_call(
        body, name=name,
        in_specs=[any_spec] * n,
        out_specs=[any_spec] * n,
        out_shape=out_shape,
        scratch_shapes=[pltpu.SemaphoreType.DMA((n * (N_DEV - 1),)), pltpu.SemaphoreType.DMA((n * (N_DEV - 1),)),
                        pltpu.SemaphoreType.DMA((n,))],
        compiler_params=pltpu.CompilerParams(has_side_effects=True),
    )(*arrays)


N_CHIPS = N_DEV // 2


def _other_chips(x, y):
    return [(1 - x, y), (x, 1 - y), (1 - x, 1 - y)]


def _gather_two_level(arrays, name):
    n = len(arrays)
    per = N_DEV - 1

    def body(*refs):
        ins, outs = refs[:n], refs[n:2 * n]
        send_sems, recv_sems, local_sems = refs[2 * n:]
        x, y, c = _mesh_position()
        me, sibling = (x, y, c), (x, y, 1 - c)
        chips = _other_chips(x, y)

        def copy(i, k, block, to, src=None):
            slot = outs[i].at[_linear(block)]
            return pltpu.make_async_remote_copy(
                src_ref=slot if src is None else src, dst_ref=slot, send_sem=send_sems.at[i * per + k],
                recv_sem=recv_sems.at[i * per + k], device_id=to, device_id_type=pl.DeviceIdType.MESH)

        local = [pltpu.make_async_copy(ins[i], outs[i].at[_linear(me)], local_sems.at[i]) for i in range(n)]
        for cp in local:
            cp.start()
        started = []
        for i in range(n):
            first = [copy(i, 0, me, sibling, src=ins[i])]
            first += [copy(i, 1 + j, me, (*chip, c), src=ins[i]) for j, chip in enumerate(chips)]
            for cp in first:
                cp.start()
            started += first
        for j, chip in enumerate(chips):
            for i in range(n):
                copy(i, 1 + j, (*chip, c), me).wait_recv()
                passed = copy(i, 4 + j, (*chip, c), sibling)
                passed.start()
                started.append(passed)
        for i in range(n):
            copy(i, 0, sibling, me).wait_recv()
            for j, chip in enumerate(chips):
                copy(i, 4 + j, (*chip, 1 - c), me).wait_recv()
        for cp in started:
            cp.wait_send()
        for cp in local:
            cp.wait()

    any_spec = pl.BlockSpec(memory_space=pl.ANY)
    return pl.pallas_call(
        body, name=name,
        in_specs=[any_spec] * n,
        out_specs=[any_spec] * n,
        out_shape=[jax.ShapeDtypeStruct((N_DEV,) + a.shape, a.dtype) for a in arrays],
        scratch_shapes=[pltpu.SemaphoreType.DMA((n * per,)), pltpu.SemaphoreType.DMA((n * per,)), pltpu.SemaphoreType.DMA((n,))],
        compiler_params=pltpu.CompilerParams(has_side_effects=True),
    )(*arrays)


def _pair_exchange(arrays, name):
    n = len(arrays)

    def body(*refs):
        ins, kept, got = refs[:n], refs[n:2 * n], refs[2 * n:3 * n]
        send_sems, recv_sems, local_sems = refs[3 * n:]
        x, y, c = _mesh_position()
        sibling = (x, y, 1 - c)
        local, sent = [], []
        for i in range(n):
            for q in range(N_CHIPS):
                sem = i * N_CHIPS + q
                cp = pltpu.make_async_copy(ins[i].at[2 * q + c], kept[i].at[q], local_sems.at[sem])
                cp.start()
                local.append(cp)
                rc = pltpu.make_async_remote_copy(
                    src_ref=ins[i].at[2 * q + 1 - c], dst_ref=got[i].at[q], send_sem=send_sems.at[sem],
                    recv_sem=recv_sems.at[sem], device_id=sibling, device_id_type=pl.DeviceIdType.MESH)
                rc.start()
                sent.append(rc)
        for rc in sent:
            rc.wait_recv()
        for rc in sent:
            rc.wait_send()
        for cp in local:
            cp.wait()

    any_spec = pl.BlockSpec(memory_space=pl.ANY)
    shapes = [jax.ShapeDtypeStruct((N_CHIPS,) + a.shape[1:], a.dtype) for a in arrays]
    res = pl.pallas_call(
        body, name=name,
        in_specs=[any_spec] * n,
        out_specs=[any_spec] * (2 * n),
        out_shape=shapes + shapes,
        scratch_shapes=[pltpu.SemaphoreType.DMA((n * N_CHIPS,))] * 3,
        compiler_params=pltpu.CompilerParams(has_side_effects=True),
    )(*arrays)
    return res[:n], res[n:]


def _pair_add(kept, got, name, rows):
    _, r, c = kept.shape
    assert r % rows == 0

    def body(a_ref, b_ref, o_ref):
        o_ref[...] = (a_ref[...].astype(F32) + b_ref[...].astype(F32)).astype(o_ref.dtype)

    blk = pl.BlockSpec((N_CHIPS, rows, c), lambda i: (0, i, 0))
    return pl.pallas_call(
        body, name=name, grid=(r // rows,), in_specs=[blk, blk], out_specs=blk,
        out_shape=jax.ShapeDtypeStruct(kept.shape, kept.dtype), compiler_params=_params("parallel"),
    )(kept, got)


def _chip_exchange(arrays, name):
    n = len(arrays)
    per = N_CHIPS - 1

    def body(*refs):
        ins, outs = refs[:n], refs[n:2 * n]
        send_sems, recv_sems, local_sems = refs[2 * n:]
        x, y, c = _mesh_position()
        me_q = 2 * x + y
        local = [pltpu.make_async_copy(ins[i].at[me_q], outs[i].at[me_q], local_sems.at[i]) for i in range(n)]
        for cp in local:
            cp.start()
        started = []
        for j, (px, py) in enumerate(_other_chips(x, y)):
            for i in range(n):
                cp = pltpu.make_async_remote_copy(
                    src_ref=ins[i].at[2 * px + py], dst_ref=outs[i].at[me_q], send_sem=send_sems.at[i * per + j],
                    recv_sem=recv_sems.at[i * per + j], device_id=(px, py, c), device_id_type=pl.DeviceIdType.MESH)
                cp.start()
                started.append(cp)
        for j, (px, py) in enumerate(_other_chips(x, y)):
            for i in range(n):
                pltpu.make_async_remote_copy(
                    src_ref=ins[i].at[2 * px + py], dst_ref=outs[i].at[2 * px + py], send_sem=send_sems.at[i * per + j],
                    recv_sem=recv_sems.at[i * per + j], device_id=(px, py, c), device_id_type=pl.DeviceIdType.MESH).wait_recv()
        for cp in started:
            cp.wait_send()
        for cp in local:
            cp.wait()

    any_spec = pl.BlockSpec(memory_space=pl.ANY)
    return pl.pallas_call(
        body, name=name,
        in_specs=[any_spec] * n,
        out_specs=[any_spec] * n,
        out_shape=[jax.ShapeDtypeStruct(a.shape, a.dtype) for a in arrays],
        scratch_shapes=[pltpu.SemaphoreType.DMA((n * per,)), pltpu.SemaphoreType.DMA((n * per,)), pltpu.SemaphoreType.DMA((n,))],
        compiler_params=pltpu.CompilerParams(has_side_effects=True),
    )(*arrays)


def _adam_reduce(parts, w, m, v, name, rows):
    r, c = w.shape
    n_parts = parts.shape[0]
    assert r % rows == 0
    c1 = 1.0 - ADAM_B1 ** ADAM_STEP
    c2 = 1.0 - ADAM_B2 ** ADAM_STEP

    def body(p_ref, w_ref, m_ref, v_ref, g_ref, d_ref, nm_ref, nv_ref):
        g = p_ref[0].astype(F32)
        for s in range(1, n_parts):
            g = g + p_ref[s].astype(F32)
        g_ref[...] = g
        nm = ADAM_B1 * m_ref[...] + (1.0 - ADAM_B1) * g
        nv = ADAM_B2 * v_ref[...] + (1.0 - ADAM_B2) * (g * g)
        nm_ref[...] = nm
        nv_ref[...] = nv
        d_ref[...] = -ADAM_LR * ((nm / c1) / (jnp.sqrt(nv / c2) + ADAM_EPS) + ADAM_WD * w_ref[...])

    blk = pl.BlockSpec((rows, c), lambda i: (i, 0))
    shape = jax.ShapeDtypeStruct((r, c), F32)
    return pl.pallas_call(
        body, name=name,
        grid=(r // rows,),
        in_specs=[pl.BlockSpec((n_parts, rows, c), lambda i: (0, i, 0)), blk, blk, blk],
        out_specs=[blk] * 4,
        out_shape=[shape] * 4,
        compiler_params=_params("parallel"),
    )(parts, w, m, v)


def _lane_total(rows8):
    def body(a_ref, o_ref):
        o_ref[...] = _sum_all(a_ref[...])

    return pl.pallas_call(body, name="loss_total", out_shape=jax.ShapeDtypeStruct((1, 1), F32))(rows8)


def _permute_w_in(w):
    rows = w.shape[0]
    n_pairs = N_GROUPS * HEAD_PAIRS
    qkv = w[:, :V_END].reshape(rows, 3, n_pairs, BLK).transpose(0, 2, 1, 3).reshape(rows, V_END)
    return jnp.concatenate(
        [w[:, DT_END:GMERGE_END], w[:, GATT_END:Z_END], w[:, Z_END:XBC_END], w[:, GMERGE_END:], w[:, V_END:GATT_END],
         w[:, XBC_END:DT_END], jnp.zeros((rows, QKV_OFF - R_DT - SSM_HEADS), w.dtype), qkv], axis=1)


def _unpermute_w_in(g):
    rows = g.shape[0]
    n_pairs = N_GROUPS * HEAD_PAIRS
    qkv = g[:, QKV_OFF:].reshape(rows, n_pairs, 3, BLK).transpose(0, 2, 1, 3).reshape(rows, V_END)
    return jnp.concatenate(
        [qkv, g[:, R_GATT:R_GATT + GROUP_COLS], g[:, R_Z:R_Z + D_INNER], g[:, R_XBC:R_XBC + CONV_DIM],
         g[:, R_DT:R_DT + SSM_HEADS], g[:, R_GM:R_GM + 2 * D_MODEL], g[:, R_GPLE:R_GPLE + D_MODEL]], axis=1)


SMALL_ROWS = 80
_SMALL_LAYOUT = (("conv_b", CONV_DIM), ("dt_bias", BLK), ("a_log", BLK), ("d_skip", BLK), ("ssm_norm_w", D_INNER),
                 ("ln_g", D_MODEL), ("ln_b", D_MODEL), ("rel_bias", NUM_BUCKETS * ATT_HEADS), ("loss", D_MODEL))


def _pack_small(vals):
    flat = []
    for name, width in _SMALL_LAYOUT:
        v = vals.get(name)
        v = jnp.zeros((width,), F32) if v is None else v.reshape(-1).astype(F32)
        flat.append(jnp.pad(v, (0, width - v.shape[0])))
    flat = jnp.concatenate(flat)
    return jnp.pad(flat, (0, SMALL_ROWS * BLK - flat.shape[0])).reshape(SMALL_ROWS, BLK)


def _unpack_small(packed):
    flat = packed.reshape(-1)
    out, pos = {}, 0
    for name, width in _SMALL_LAYOUT:
        out[name] = flat[pos:pos + width]
        pos += width
    for name in ("dt_bias", "a_log", "d_skip"):
        out[name] = out[name][:SSM_HEADS]
    out["rel_bias"] = out["rel_bias"].reshape(NUM_BUCKETS, ATT_HEADS)
    return out


def _pack_cols(b_gate_part, conv_w_part):
    return jnp.concatenate([jnp.pad(b_gate_part, ((0, 5), (0, 0))), jnp.pad(conv_w_part, ((0, 4), (0, 0)))], axis=1)


def _pack_cols_all(b_gate_full, conv_w_full):
    bg = b_gate_full.reshape(3, N_DEV, BLK).transpose(1, 0, 2)
    cw = conv_w_full.reshape(CONV_WIDTH, N_DEV, CONV_DIM // N_DEV).transpose(1, 0, 2)
    return jnp.concatenate([jnp.pad(bg, ((0, 0), (0, 5), (0, 0))), jnp.pad(cw, ((0, 0), (0, 4), (0, 0)))], axis=2)


def _unpack_cols_all(packed):
    bg = packed[:, :3, :BLK].transpose(1, 0, 2).reshape(3, D_MODEL)
    cw = packed[:, :CONV_WIDTH, BLK:].transpose(1, 0, 2).reshape(CONV_WIDTH, CONV_DIM)
    return bg, cw


def _local_step(x, p, target, wp16, wb16, wo16, wple16, b_gate, conv_w, small):
    bsz, seq, _ = x.shape
    t = bsz * seq
    x2 = x.reshape(t, D_MODEL)
    x16 = x2.astype(BF16)
    p16 = p.reshape(t, PLE_DIM).astype(BF16)
    tgt2 = target.reshape(t, D_MODEL)
    b_gate8 = jnp.pad(b_gate, ((0, 5), (0, 0)))
    ln_gb = jnp.pad(jnp.stack([small["ln_g"], small["ln_b"]]), ((0, 6), (0, 0)))
    conv_b = small["conv_b"].reshape(1, CONV_DIM)
    norm_w = small["ssm_norm_w"].reshape(1, D_INNER)
    pad_heads = lambda v: jnp.pad(v, (0, BLK - SSM_HEADS))
    prow = jnp.pad(jnp.stack([pad_heads(small["dt_bias"]), pad_heads(small["a_log"]), pad_heads(small["d_skip"])]), ((0, 5), (0, 0)))
    pcol = prow.T
    wa16, wbb16 = wb16[:GROUP_COLS], wb16[GROUP_COLS:]

    rest = _matmul(x16, wp16, mode="nn", out_dtype=F32, name="inproj", tm=512, tn=2304, tk=D_MODEL, n_outer=True)
    rest3 = rest.reshape(bsz, seq, HCAT_COLS)
    biases, onehots, outs, lses = [], [], [], []
    for g, (_, dil) in enumerate(DILATED_PATTERNS):
        bias, onehot = _bias_matrix(small["rel_bias"][:, g * HEADS_PER_GROUP:(g + 1) * HEADS_PER_GROUP], dil)
        out3, lse3 = _attn_fwd(rest3, bias, g, dil, f"attn_fwd{g}")
        biases.append(bias), onehots.append(onehot)
        outs.append(out3.reshape(t, GROUP_COLS)), lses.append(lse3.reshape(t, GROUP_COLS))
    oa = _combine_fwd(outs, lses, rest)
    xa = _conv_fwd(rest3, conv_w, conv_b)
    dtr_t = jnp.swapaxes(rest3[:, :, R_DT:R_DT + BLK], 1, 2)
    y, s_in = _ssd_fwd(xa, rest3, dtr_t, prow, pcol)
    y2 = y.reshape(t, D_INNER)
    ys = _gatenorm_fwd(y2, rest, norm_w)
    y_a = _matmul(oa, wa16, mode="nn", out_dtype=F32, name="branch_a", tm=512, tn=D_MODEL, tk=GROUP_COLS)
    y_b = _matmul(ys, wbb16, mode="nn", out_dtype=F32, name="branch_b", tm=512, tn=D_MODEL, tk=D_INNER)
    merged = _merge_fwd(y_a, y_b, rest, b_gate8)
    mix = _matmul(merged, wo16, mode="nn", out_dtype=F32, name="out_proj", tm=512, tn=D_MODEL, tk=D_MODEL)
    pw = _matmul(p16, wple16, mode="nn", out_dtype=F32, name="ple_proj", tm=512, tn=D_MODEL, tk=PLE_DIM)
    d_pre, d_pre16, d_gple, d_pw, stats = _final(x2, mix, pw, rest, b_gate8, ln_gb, tgt2)

    d_merged = _matmul(d_pre16, wo16, mode="nt", out_dtype=F32, name="d_merged", tm=512, tn=D_MODEL, tk=D_MODEL)
    g_w_out = _matmul(merged, d_pre16, mode="tn", out_dtype=BF16, name="g_w_out", tm=512, tn=D_MODEL, tk=1024)
    g_w_ple = _matmul(p16, d_pw, mode="tn", out_dtype=BF16, name="g_w_ple", tm=PLE_DIM, tn=D_MODEL, tk=1024)
    d_ya, d_yb, d_ga, d_gb, dbg01 = _merge_bwd(y_a, y_b, rest, b_gate8, d_merged)
    d_oa = _matmul(d_ya, wa16, mode="nt", out_dtype=F32, name="d_oa", tm=512, tn=GROUP_COLS, tk=D_MODEL)
    d_ys = _matmul(d_yb, wbb16, mode="nt", out_dtype=F32, name="d_ys", tm=512, tn=1024, tk=D_MODEL)
    g_wa = _matmul(oa, d_ya, mode="tn", out_dtype=BF16, name="g_w_branch_a", tm=GROUP_COLS, tn=D_MODEL, tk=1024)
    g_wb = _matmul(ys, d_yb, mode="tn", out_dtype=BF16, name="g_w_branch_b", tm=512, tn=D_MODEL, tk=1024)
    d_outs_dd_ga = _combine_bwd(outs, lses, rest, d_oa)
    d_outs, dds, d_gatt = d_outs_dd_ga[:3], d_outs_dd_ga[3:6], d_outs_dd_ga[6]
    d_y, d_z, d_nw = _gatenorm_bwd(y2, rest, norm_w, d_ys)
    d_xa, ddt, ddt_t, gprow, gpcol = _ssd_bwd(xa, rest3, dtr_t, prow, pcol, s_in, d_y.reshape(bsz, seq, D_INNER))
    d_xbc, d_conv = _conv_bwd(rest3, conv_w, conv_b, d_xa)
    d_dt = (ddt + jnp.swapaxes(ddt_t, 1, 2)).reshape(t, BLK).astype(BF16)
    pieces = [d_ga, d_gb, d_z, d_xbc.reshape(t, CONV_DIM), d_gple, d_gatt, d_dt, jnp.zeros((t, QKV_OFF - R_DT - BLK), BF16)]
    g_tables = []
    shape3 = (bsz, seq, GROUP_COLS)
    for g, (_, dil) in enumerate(DILATED_PATTERNS):
        d_qkv, dbias = _attn_bwd(rest3, biases[g], lses[g].reshape(shape3), d_outs[g].reshape(shape3), dds[g].reshape(shape3),
                                 g, dil, f"attn_bwd{g}")
        pieces.append(d_qkv.reshape(t, QKV_G))
        g_tables.append(_bias_grad(dbias, onehots[g], f"bias_grad{g}"))
    d_hcat = jnp.concatenate(pieces, axis=1)
    grad_x = _matmul(d_hcat, wp16, mode="nt", out_dtype=F32, name="grad_x", tm=1024, tn=D_MODEL, tk=768,
                     add=d_pre, add_scale=ALPHA)
    g_wp = _matmul(x16, d_hcat, mode="tn", out_dtype=BF16, name="g_w_in", tm=D_MODEL, tn=2304, tk=1024, n_outer=True)

    grads = dict(
        w_in=_unpermute_w_in(g_wp),
        b_gate=jnp.stack([dbg01[0], dbg01[1], stats[ST_BG2]]),
        conv_w=d_conv[:CONV_WIDTH],
        w_branch=jnp.concatenate([g_wa, g_wb], axis=0),
        w_out=g_w_out,
        w_ple=g_w_ple,
    )
    small_grads = dict(
        conv_b=d_conv[CONV_WIDTH],
        dt_bias=gprow[P_DTB, :SSM_HEADS] + gpcol[:SSM_HEADS, P_DTB],
        a_log=gprow[P_ALOG, :SSM_HEADS] + gpcol[:SSM_HEADS, P_ALOG],
        d_skip=gprow[P_DSKIP, :SSM_HEADS],
        ssm_norm_w=d_nw[0],
        ln_g=stats[ST_LNG],
        ln_b=stats[ST_LNB],
        rel_bias=jnp.concatenate(g_tables, axis=1),
        loss=stats[ST_LOSS],
    )
    return grad_x.reshape(bsz, seq, D_MODEL), grads, small_grads


WEIGHT_ORDER = ("w_in", "b_gate", "conv_w", "conv_b", "dt_bias", "a_log", "d_skip", "ssm_norm_w", "w_branch", "w_out",
                "w_ple", "ln_g", "ln_b", "rel_bias")
SMALL_NAMES = ("conv_b", "dt_bias", "a_log", "d_skip", "ssm_norm_w", "ln_g", "ln_b", "rel_bias")


def kernel(x, p, w_in, b_gate, conv_w, conv_b, dt_bias, a_log, d_skip, ssm_norm_w, w_branch, w_out, w_ple, ln_g, ln_b, rel_bias, loss_target, m_w_in, m_b_gate, m_conv_w, m_conv_b, m_dt_bias, m_a_log, m_d_skip, m_ssm_norm_w, m_w_branch, m_w_out, m_w_ple, m_ln_g, m_ln_b, m_rel_bias, v_w_in, v_b_gate, v_conv_w, v_conv_b, v_dt_bias, v_a_log, v_d_skip, v_ssm_norm_w, v_w_branch, v_w_out, v_w_ple, v_ln_g, v_ln_b, v_rel_bias):
    given = dict(w_in=w_in, b_gate=b_gate, conv_w=conv_w, conv_b=conv_b, dt_bias=dt_bias, a_log=a_log, d_skip=d_skip,
                 ssm_norm_w=ssm_norm_w, w_branch=w_branch, w_out=w_out, w_ple=w_ple, ln_g=ln_g, ln_b=ln_b)
    moments_m = dict(w_in=m_w_in, b_gate=m_b_gate, conv_w=m_conv_w, conv_b=m_conv_b, dt_bias=m_dt_bias, a_log=m_a_log,
                     d_skip=m_d_skip, ssm_norm_w=m_ssm_norm_w, w_branch=m_w_branch, w_out=m_w_out, w_ple=m_w_ple,
                     ln_g=m_ln_g, ln_b=m_ln_b)
    moments_v = dict(w_in=v_w_in, b_gate=v_b_gate, conv_w=v_conv_w, conv_b=v_conv_b, dt_bias=v_dt_bias, a_log=v_a_log,
                     d_skip=v_d_skip, ssm_norm_w=v_ssm_norm_w, w_branch=v_w_branch, w_out=v_w_out, w_ple=v_w_ple,
                     ln_g=v_ln_g, ln_b=v_ln_b)
    w = {k: a[0] for k, a in given.items()} | {"rel_bias": rel_bias}
    mm = {k: a[0] for k, a in moments_m.items()} | {"rel_bias": m_rel_bias}
    vv = {k: a[0] for k, a in moments_v.items()} | {"rel_bias": v_rel_bias}

    gathered = _gather_two_level(
        [w["w_in"].astype(BF16), w["w_branch"].astype(BF16), w["w_out"].astype(BF16), w["w_ple"].astype(BF16),
         _pack_cols(w["b_gate"], w["conv_w"])], "gather_weights")
    wp16 = _permute_w_in(gathered[0].transpose(1, 0, 2).reshape(D_MODEL, IN_COLS))
    wb16 = gathered[1].reshape(BRANCH_ROWS, D_MODEL)
    wo16 = gathered[2].reshape(D_MODEL, D_MODEL)
    wple16 = gathered[3].transpose(1, 0, 2).reshape(PLE_DIM, D_MODEL)
    b_gate_full, conv_w_full = _unpack_cols_all(gathered[4])
    small = {k: w[k] for k in SMALL_NAMES}

    grad_x, grads, small_grads = _local_step(x, p[0], loss_target, wp16, wb16, wo16, wple16, b_gate_full, conv_w_full, small)

    big = [grads["w_in"].astype(BF16).reshape(D_MODEL, N_DEV, IN_SHARD).transpose(1, 0, 2),
           grads["w_branch"].astype(BF16).reshape(N_DEV, BRANCH_ROWS // N_DEV, D_MODEL),
           grads["w_out"].astype(BF16).reshape(N_DEV, D_MODEL // N_DEV, D_MODEL),
           grads["w_ple"].astype(BF16).reshape(PLE_DIM, N_DEV, BLK).transpose(1, 0, 2)]
    kept, got = _pair_exchange(big, "pair_exchange")
    sums = [_pair_add(kk, gg, f"pair_add{i}", rows) for i, (kk, gg, rows) in enumerate(zip(kept, got, (128, 176, 128, 256)))]
    parts = _chip_exchange(sums, "chip_exchange")
    small_parts = _exchange([_pack_cols_all(grads["b_gate"], grads["conv_w"]), _pack_small(small_grads)], [True, False],
                            "exchange_small")

    out = {}
    out["w_in"] = _adam_reduce(parts[0], w["w_in"], mm["w_in"], vv["w_in"], "adam_w_in", 128)
    out["w_branch"] = _adam_reduce(parts[1], w["w_branch"], mm["w_branch"], vv["w_branch"], "adam_w_branch", 176)
    out["w_out"] = _adam_reduce(parts[2], w["w_out"], mm["w_out"], vv["w_out"], "adam_w_out", 128)
    out["w_ple"] = _adam_reduce(parts[3], w["w_ple"], mm["w_ple"], vv["w_ple"], "adam_w_ple", 256)
    cols = _adam_reduce(small_parts[0], _pack_cols(w["b_gate"], w["conv_w"]), _pack_cols(mm["b_gate"], mm["conv_w"]),
                        _pack_cols(vv["b_gate"], vv["conv_w"]), "adam_cols", 8)
    out["b_gate"] = [a[:3, :BLK] for a in cols]
    out["conv_w"] = [a[:CONV_WIDTH, BLK:] for a in cols]
    packed = _adam_reduce(small_parts[1], _pack_small({k: w[k] for k in SMALL_NAMES}), _pack_small({k: mm[k] for k in SMALL_NAMES}),
                          _pack_small({k: vv[k] for k in SMALL_NAMES}), "adam_small", SMALL_ROWS)
    unpacked = [_unpack_small(a) for a in packed]
    for k in SMALL_NAMES:
        out[k] = [u[k] for u in unpacked]
    loss_rows = unpacked[0]["loss"].reshape(D_MODEL // BLK, BLK)
    loss = _lane_total(loss_rows).reshape(())

    def shaped(k, a):
        return a if k == "rel_bias" else a[None]

    results = [loss, grad_x]
    for i in range(4):
        results += [shaped(k, out[k][i]) for k in WEIGHT_ORDER]
    return tuple(results)
```

```python
import functools
import math

import jax
import jax.numpy as jnp
from jax import lax
from jax.experimental import pallas as pl
from jax.experimental.pallas import tpu as pltpu

F32 = jnp.float32
BF16 = jnp.bfloat16

N_DEV = 8
D_MODEL = 1024
SEQ = 2048
HEAD_DIM = 64
HEADS_PER_GROUP = 12
DILATED_PATTERNS = ((128, 1), (512, 4), (2048, 16))
N_GROUPS = 3
ATT_HEADS = N_GROUPS * HEADS_PER_GROUP
GROUP_COLS = HEADS_PER_GROUP * HEAD_DIM
ATT_QKV = ATT_HEADS * HEAD_DIM
BLK = 128
NUM_BUCKETS = 32
MAX_DISTANCE = 2048
D_INNER = 2048
SSM_HEADS = 32
SSM_GROUPS = 4
HEADS_PER_SSM_GROUP = SSM_HEADS // SSM_GROUPS
D_STATE = 128
CONV_WIDTH = 4
CONV_DIM = D_INNER + 2 * SSM_GROUPS * D_STATE
PLE_DIM = 256
ALPHA = 2.0 ** 0.25
LN_EPS = 1e-5
RMS_EPS = 1e-5
IN_COLS = 15904
IN_SHARD = IN_COLS // N_DEV
BRANCH_ROWS = GROUP_COLS + D_INNER

Q_END = ATT_QKV
K_END = 2 * ATT_QKV
V_END = 3 * ATT_QKV
GATT_END = V_END + GROUP_COLS
Z_END = GATT_END + D_INNER
XBC_END = Z_END + CONV_DIM
DT_END = XBC_END + SSM_HEADS
GMERGE_END = DT_END + 2 * D_MODEL

R_GM, R_Z, R_XBC, R_GPLE, R_GATT, R_DT = 0, 2048, 4096, 7168, 8192, 8960
QKV_OFF = 9216
HEAD_PAIRS = GROUP_COLS // BLK
QKV_G = 3 * GROUP_COLS
HCAT_COLS = QKV_OFF + N_GROUPS * QKV_G

ADAM_LR, ADAM_B1, ADAM_B2, ADAM_EPS, ADAM_WD, ADAM_STEP = 0.001, 0.9, 0.999, 1e-08, 0.01, 10

VMEM_LIMIT_BYTES = 56 * 1024 * 1024


def _params(*semantics):
    return pltpu.CompilerParams(dimension_semantics=semantics, vmem_limit_bytes=VMEM_LIMIT_BYTES)


def _sigmoid(v):
    return 1.0 / (1.0 + jnp.exp(-v))


def _silu_and_grad(v):
    s = _sigmoid(v)
    return v * s, s * (1.0 + v * (1.0 - s))


def _matmul(a, b, *, mode, out_dtype, name, tm, tn, tk, n_off=0, n=None, add=None, add_scale=1.0, n_outer=False):
    if mode == "nn":
        (m, k), n_full = a.shape, b.shape[1]
        assert b.shape[0] == k
    elif mode == "nt":
        (m, k), n_full = a.shape, b.shape[0]
        assert b.shape[1] == k
    else:
        (k, m), n_full = a.shape, b.shape[1]
        assert b.shape[0] == k
    n = n_full if n is None else n
    assert m % tm == 0 and n % tn == 0 and k % tk == 0 and n_off % tn == 0, (name, m, n, k)
    nk = k // tk
    jo = n_off // tn
    dims = {"nn": (((1,), (0,)), ((), ())), "nt": (((1,), (1,)), ((), ())), "tn": (((0,), (0,)), ((), ()))}[mode]

    def body(*refs):
        if add is None:
            a_ref, b_ref, o_ref = refs[:3]
            add_ref = None
        else:
            a_ref, b_ref, add_ref, o_ref = refs[:4]
        acc_ref = refs[-1] if nk > 1 else None
        prod = lax.dot_general(a_ref[...].astype(BF16), b_ref[...].astype(BF16), dims, preferred_element_type=F32)

        def finish(total):
            if add_ref is not None:
                total = total + add_scale * add_ref[...]
            o_ref[...] = total.astype(out_dtype)

        if nk == 1:
            finish(prod)
        else:
            kk = pl.program_id(2)

            @pl.when(kk == 0)
            def _():
                acc_ref[...] = prod

            @pl.when(jnp.logical_and(kk > 0, kk < nk - 1))
            def _():
                acc_ref[...] += prod

            @pl.when(kk == nk - 1)
            def _():
                finish(acc_ref[...] + prod)

    def ij(f):
        return (lambda g0, g1, kk: f(g1, g0, kk)) if n_outer else f

    if mode == "nn":
        a_spec = pl.BlockSpec((tm, tk), ij(lambda i, j, kk: (i, kk)))
        b_spec = pl.BlockSpec((tk, tn), ij(lambda i, j, kk: (kk, j + jo)))
    elif mode == "nt":
        a_spec = pl.BlockSpec((tm, tk), ij(lambda i, j, kk: (i, kk)))
        b_spec = pl.BlockSpec((tn, tk), ij(lambda i, j, kk: (j, kk)))
    else:
        a_spec = pl.BlockSpec((tk, tm), ij(lambda i, j, kk: (kk, i)))
        b_spec = pl.BlockSpec((tk, tn), ij(lambda i, j, kk: (kk, j)))
    in_specs = [a_spec, b_spec]
    args = [a, b]
    if add is not None:
        in_specs.append(pl.BlockSpec((tm, tn), ij(lambda i, j, kk: (i, j))))
        args.append(add)
    return pl.pallas_call(
        body, name=name,
        grid=(n // tn, m // tm, nk) if n_outer else (m // tm, n // tn, nk),
        in_specs=in_specs,
        out_specs=pl.BlockSpec((tm, tn), ij(lambda i, j, kk: (i, j))),
        out_shape=jax.ShapeDtypeStruct((m, n), out_dtype),
        scratch_shapes=[pltpu.VMEM((tm, tn), F32)] if nk > 1 else [],
        compiler_params=_params("parallel", "parallel", "arbitrary"),
    )(*args)


UNITS_PER_ITER = 4


def _band_mask(first):
    qi = lax.broadcasted_iota(jnp.int32, (BLK, BLK if first else 2 * BLK), 0)
    kj = lax.broadcasted_iota(jnp.int32, (BLK, BLK if first else 2 * BLK), 1)
    delta = qi - kj if first else qi + BLK - kj
    return jnp.logical_and(delta >= 0, delta <= BLK)


def _attn_specs(seq, g):
    qkv = [pl.BlockSpec((1, seq, BLK), functools.partial(
        lambda hp, b, part: (b, 0, QKV_OFF // BLK + (g * HEAD_PAIRS + hp) * 3 + part), part=part)) for part in range(3)]
    one = pl.BlockSpec((1, seq, BLK), lambda hp, b: (b, 0, hp))
    bias = pl.BlockSpec((2, BLK, 2 * BLK), lambda hp, b: (hp, 0, 0))
    return qkv, one, bias


def _attn_rows(dil, r, n, first):
    start = r + (dil * BLK) * n
    if dil == 1:
        start = pl.multiple_of(start, BLK)
        rows = pl.ds(start, BLK)
        keys = rows if first else pl.ds(pl.multiple_of(start - BLK, BLK), 2 * BLK)
    else:
        rows = pl.ds(start, BLK, stride=dil)
        keys = rows if first else pl.ds(start - dil * BLK, 2 * BLK, stride=dil)
    return rows, keys


def _attn_schedule(dil, nb, unit):
    def blocks_of(r):
        unit(r, 0, True)
        for n in range(1, UNITS_PER_ITER):
            unit(r, n, False)
        if nb > UNITS_PER_ITER:
            def more(i, carry):
                for jj in range(UNITS_PER_ITER):
                    unit(r, i * UNITS_PER_ITER + jj, False)
                return carry
            lax.fori_loop(1, nb // UNITS_PER_ITER, more, 0)

    if nb >= UNITS_PER_ITER:
        assert nb % UNITS_PER_ITER == 0
        if dil == 1:
            blocks_of(0)
        else:
            def per_residue(r, carry):
                blocks_of(r)
                return carry
            lax.fori_loop(0, dil, per_residue, 0)
    else:
        assert nb == 1 and dil % UNITS_PER_ITER == 0

        def residues(i, carry):
            for jj in range(UNITS_PER_ITER):
                unit(i * UNITS_PER_ITER + jj, 0, True)
            return carry
        lax.fori_loop(0, dil // UNITS_PER_ITER, residues, 0)


def _attn_fwd(hcat3, bias, g, dil, name):
    bsz, seq, _ = hcat3.shape
    nb = seq // dil // BLK
    scale = HEAD_DIM ** -0.5

    def body(q_ref, k_ref, v_ref, bias_ref, o_ref, lse_ref):
        masks = {True: _band_mask(True), False: _band_mask(False)}

        def unit(r, n, first):
            rows, keys = _attn_rows(dil, r, n, first)
            q2 = q_ref[0, rows, :].astype(BF16)
            k2 = k_ref[0, keys, :].astype(BF16)
            v2 = v_ref[0, keys, :].astype(BF16)
            outs, lses = [], []
            for j in range(2):
                lanes = slice(j * HEAD_DIM, (j + 1) * HEAD_DIM)
                bias_j = bias_ref[j, :, BLK:] if first else bias_ref[j]
                s = _nt(q2[:, lanes], k2[:, lanes]) * scale + bias_j
                s = jnp.where(masks[first], s, -jnp.inf)
                mx = jnp.max(s, axis=-1, keepdims=True)
                e = jnp.exp(s - mx)
                den = jnp.sum(e, axis=-1, keepdims=True)
                outs.append(jnp.dot(e.astype(BF16), v2[:, lanes], preferred_element_type=F32) / den)
                lses.append(jnp.broadcast_to(mx + jnp.log(den), (BLK, HEAD_DIM)))
            o_ref[0, rows, :] = jnp.concatenate(outs, axis=1)
            lse_ref[0, rows, :] = jnp.concatenate(lses, axis=1)

        _attn_schedule(dil, nb, unit)

    qkv_specs, one, bias_spec = _attn_specs(seq, g)
    shape = jax.ShapeDtypeStruct((bsz, seq, GROUP_COLS), F32)
    return pl.pallas_call(
        body, name=name,
        grid=(HEAD_PAIRS, bsz),
        in_specs=qkv_specs + [bias_spec],
        out_specs=[one, one],
        out_shape=[shape, shape],
        compiler_params=_params("parallel", "parallel"),
    )(hcat3, hcat3, hcat3, bias)


def _attn_bwd(hcat3, bias, lse, d_out, dd, g, dil, name):
    bsz, seq, _ = hcat3.shape
    nb = seq // dil // BLK
    scale = HEAD_DIM ** -0.5

    def body(q_ref, k_ref, v_ref, bias_ref, lse_ref, do_ref, dd_ref, dqkv_ref, dbias_ref, dq_acc, dk_acc, dv_acc):
        @pl.when(pl.program_id(1) == 0)
        def _():
            dbias_ref[...] = jnp.zeros_like(dbias_ref)

        dk_acc[...] = jnp.zeros_like(dk_acc)
        dv_acc[...] = jnp.zeros_like(dv_acc)
        masks = {True: _band_mask(True), False: _band_mask(False)}

        def unit(r, n, first):
            rows, keys = _attn_rows(dil, r, n, first)
            q2 = q_ref[0, rows, :].astype(BF16)
            k2 = k_ref[0, keys, :].astype(BF16)
            v2 = v_ref[0, keys, :].astype(BF16)
            do2 = do_ref[0, rows, :].astype(BF16)
            lse2 = lse_ref[0, rows, :]
            dd2 = dd_ref[0, rows, :]
            dqs, dks, dvs = [], [], []
            for j in range(2):
                lanes = slice(j * HEAD_DIM, (j + 1) * HEAD_DIM)
                q, kb, vb, do = q2[:, lanes], k2[:, lanes], v2[:, lanes], do2[:, lanes]
                delta = jnp.sum(dd2[:, lanes], axis=-1, keepdims=True)
                bias_j = bias_ref[j, :, BLK:] if first else bias_ref[j]
                s = _nt(q, kb) * scale + bias_j
                p = jnp.where(masks[first], jnp.exp(s - lse2[:, j * HEAD_DIM:j * HEAD_DIM + 1]), 0.0)
                ds = p * (_nt(do, vb) - delta)
                ds16 = ds.astype(BF16)
                dqs.append(jnp.dot(ds16, kb, preferred_element_type=F32) * scale)
                dks.append(_tn(ds16, q) * scale)
                dvs.append(_tn(p.astype(BF16), do))
                if first:
                    dbias_ref[j, :, BLK:] += ds
                else:
                    dbias_ref[j] += ds
            dq_acc[rows, :] = jnp.concatenate(dqs, axis=1)
            dk_acc[keys, :] += jnp.concatenate(dks, axis=1)
            dv_acc[keys, :] += jnp.concatenate(dvs, axis=1)

        _attn_schedule(dil, nb, unit)
        dqkv_ref[0, :, 0:BLK] = dq_acc[...].astype(BF16)
        dqkv_ref[0, :, BLK:2 * BLK] = dk_acc[...].astype(BF16)
        dqkv_ref[0, :, 2 * BLK:3 * BLK] = dv_acc[...].astype(BF16)

    qkv_specs, one, bias_spec = _attn_specs(seq, g)
    return pl.pallas_call(
        body, name=name,
        grid=(HEAD_PAIRS, bsz),
        in_specs=qkv_specs + [bias_spec, one, one, one],
        out_specs=[pl.BlockSpec((1, seq, 3 * BLK), lambda hp, b: (b, 0, hp)), bias_spec],
        out_shape=[jax.ShapeDtypeStruct((bsz, seq, QKV_G), BF16), jax.ShapeDtypeStruct((HEADS_PER_GROUP, BLK, 2 * BLK), F32)],
        scratch_shapes=[pltpu.VMEM((seq, BLK), F32)] * 3,
        compiler_params=_params("arbitrary", "arbitrary"),
    )(hcat3, hcat3, hcat3, bias, lse, d_out, dd)


def _t5_buckets(dil):
    import numpy as np
    qi = np.arange(BLK)[:, None]
    kj = np.arange(2 * BLK)[None, :]
    dist = np.maximum(qi + BLK - kj, 0) * dil
    max_exact = NUM_BUCKETS // 2
    d_f = np.maximum(dist, 1).astype(np.float32)
    large = max_exact + (np.log(d_f / np.float32(max_exact)) / np.float32(math.log(MAX_DISTANCE / max_exact))
                         * np.float32(NUM_BUCKETS - max_exact)).astype(np.int32)
    large = np.minimum(large, NUM_BUCKETS - 1)
    return np.where(dist < max_exact, dist, large).astype(np.int32).reshape(-1)


def _bias_matrix(table_g, dil):
    buckets = jnp.asarray(_t5_buckets(dil))
    onehot = (buckets[None, :] == lax.broadcasted_iota(jnp.int32, (NUM_BUCKETS, 1), 0)).astype(F32)
    tk = 4096

    def body(t_ref, oh_ref, o_ref):
        o_ref[...] = _dot_hi(t_ref[...], oh_ref[...])

    bias = pl.pallas_call(
        body, name=f"bias_matrix{dil}",
        grid=(onehot.shape[1] // tk,),
        in_specs=[pl.BlockSpec((HEADS_PER_GROUP, NUM_BUCKETS), lambda kk: (0, 0)), pl.BlockSpec((NUM_BUCKETS, tk), lambda kk: (0, kk))],
        out_specs=pl.BlockSpec((HEADS_PER_GROUP, tk), lambda kk: (0, kk)),
        out_shape=jax.ShapeDtypeStruct((HEADS_PER_GROUP, onehot.shape[1]), F32),
        compiler_params=_params("parallel"),
    )(table_g.T, onehot)
    return bias.reshape(HEADS_PER_GROUP, BLK, 2 * BLK), onehot


def _bias_grad(dbias, onehot, name):
    flat = dbias.reshape(HEADS_PER_GROUP, 2 * BLK * BLK)
    tk = 4096

    def body(oh_ref, g_ref, o_ref):
        @pl.when(pl.program_id(0) == 0)
        def _():
            o_ref[...] = jnp.zeros_like(o_ref)
        o_ref[...] += lax.dot_general(oh_ref[...], g_ref[...], (((1,), (1,)), ((), ())),
                                      preferred_element_type=F32, precision=lax.Precision.HIGHEST)

    return pl.pallas_call(
        body, name=name,
        grid=(flat.shape[1] // tk,),
        in_specs=[pl.BlockSpec((NUM_BUCKETS, tk), lambda kk: (0, kk)), pl.BlockSpec((HEADS_PER_GROUP, tk), lambda kk: (0, kk))],
        out_specs=pl.BlockSpec((NUM_BUCKETS, HEADS_PER_GROUP), lambda kk: (0, 0)),
        out_shape=jax.ShapeDtypeStruct((NUM_BUCKETS, HEADS_PER_GROUP), F32),
        compiler_params=_params("arbitrary"),
    )(onehot, flat)


def _shift_rows(u, s, down):
    if s == 0:
        return u
    rows = lax.broadcasted_iota(jnp.int32, u.shape, 0)
    n = u.shape[0]
    if down:
        return jnp.where(rows >= s, pltpu.roll(u, s, 0), 0.0)
    return jnp.where(rows < n - s, pltpu.roll(u, n - s, 0), 0.0)


def _conv_pre(u, w_ref, b_ref):
    acc = b_ref[0:1, :] + w_ref[CONV_WIDTH - 1:CONV_WIDTH, :] * u
    for s in range(1, CONV_WIDTH):
        acc = acc + w_ref[CONV_WIDTH - 1 - s:CONV_WIDTH - s, :] * _shift_rows(u, s, True)
    return acc


def _conv_fwd(rest3, conv_w, conv_b):
    bsz, seq, _ = rest3.shape

    def body(u_ref, w_ref, b_ref, o_ref):
        pre = _conv_pre(u_ref[0], w_ref, b_ref)
        o_ref[0] = pre * _sigmoid(pre)

    return pl.pallas_call(
        body, name="conv_fwd",
        grid=(bsz, CONV_DIM // BLK),
        in_specs=[pl.BlockSpec((1, seq, BLK), lambda b, c: (b, 0, R_XBC // BLK + c)),
                  pl.BlockSpec((CONV_WIDTH, BLK), lambda b, c: (0, c)),
                  pl.BlockSpec((1, BLK), lambda b, c: (0, c))],
        out_specs=pl.BlockSpec((1, seq, BLK), lambda b, c: (b, 0, c)),
        out_shape=jax.ShapeDtypeStruct((bsz, seq, CONV_DIM), F32),
        compiler_params=_params("parallel", "parallel"),
    )(rest3, conv_w, conv_b)


def _conv_bwd(rest3, conv_w, conv_b, d_act):
    bsz, seq, _ = rest3.shape

    def body(u_ref, w_ref, b_ref, g_ref, du_ref, dw_ref):
        @pl.when(pl.program_id(1) == 0)
        def _():
            dw_ref[...] = jnp.zeros_like(dw_ref)

        u = u_ref[0]
        pre = _conv_pre(u, w_ref, b_ref)
        d_pre = g_ref[0] * _silu_and_grad(pre)[1]
        du = w_ref[CONV_WIDTH - 1:CONV_WIDTH, :] * d_pre
        dw_ref[CONV_WIDTH - 1:CONV_WIDTH, :] += jnp.sum(d_pre * u, axis=0, keepdims=True)
        for s in range(1, CONV_WIDTH):
            du = du + w_ref[CONV_WIDTH - 1 - s:CONV_WIDTH - s, :] * _shift_rows(d_pre, s, False)
            dw_ref[CONV_WIDTH - 1 - s:CONV_WIDTH - s, :] += jnp.sum(d_pre * _shift_rows(u, s, True), axis=0, keepdims=True)
        dw_ref[CONV_WIDTH:CONV_WIDTH + 1, :] += jnp.sum(d_pre, axis=0, keepdims=True)
        du_ref[0] = du.astype(BF16)

    return pl.pallas_call(
        body, name="conv_bwd",
        grid=(CONV_DIM // BLK, bsz),
        in_specs=[pl.BlockSpec((1, seq, BLK), lambda c, b: (b, 0, R_XBC // BLK + c)),
                  pl.BlockSpec((CONV_WIDTH, BLK), lambda c, b: (0, c)),
                  pl.BlockSpec((1, BLK), lambda c, b: (0, c)),
                  pl.BlockSpec((1, seq, BLK), lambda c, b: (b, 0, c))],
        out_specs=[pl.BlockSpec((1, seq, BLK), lambda c, b: (b, 0, c)),
                   pl.BlockSpec((8, BLK), lambda c, b: (0, c))],
        out_shape=[jax.ShapeDtypeStruct((bsz, seq, CONV_DIM), BF16), jax.ShapeDtypeStruct((8, CONV_DIM), F32)],
        compiler_params=_params("parallel", "arbitrary"),
    )(rest3, conv_w, conv_b, d_act)


P_DTB, P_ALOG, P_DSKIP = 0, 1, 2


def _softplus(v):
    return jnp.maximum(v, 0.0) + jnp.log(1.0 + jnp.exp(-jnp.abs(v)))


def _dot_hi(a, b):
    return jnp.dot(a, b, preferred_element_type=F32, precision=lax.Precision.HIGHEST)


def _nt(a, b):
    return lax.dot_general(a, b, (((1,), (1,)), ((), ())), preferred_element_type=F32)


def _tn(a, b):
    return lax.dot_general(a, b, (((0,), (0,)), ((), ())), preferred_element_type=F32)


def _sum_all(v):
    return jnp.sum(jnp.sum(v, axis=0, keepdims=True), axis=1, keepdims=True)


def _ssd_decays(dtr, dtr_t, prow_ref, pcol_ref):
    ri = lax.broadcasted_iota(jnp.int32, (BLK, BLK), 0)
    ci = lax.broadcasted_iota(jnp.int32, (BLK, BLK), 1)
    tri = (ri >= ci).astype(F32)
    tri_u = (ri <= ci).astype(F32)
    pre = dtr + prow_ref[P_DTB:P_DTB + 1, :]
    dt = _softplus(pre)
    ah_row = -jnp.exp(prow_ref[P_ALOG:P_ALOG + 1, :])
    acs = _dot_hi(tri, dt * ah_row)
    pre_t = dtr_t + pcol_ref[:, P_DTB:P_DTB + 1]
    dt_t = _softplus(pre_t)
    ah_col = -jnp.exp(pcol_ref[:, P_ALOG:P_ALOG + 1])
    acs_t = _dot_hi(dt_t * ah_col, tri_u)
    return dict(tri=tri, tri_u=tri_u, pre=pre, dt=dt, ah_row=ah_row, acs=acs, pre_t=pre_t, dt_t=dt_t, ah_col=ah_col,
                acs_t=acs_t, causal=ri >= ci, last_row=ri[:, 0:1] == BLK - 1)


def _ssd_head(h, d, x_ref, s_in, g_mat):
    col = d["acs"][:, h:h + 1]
    row = d["acs_t"][h:h + 1, :]
    lm = jnp.exp(jnp.where(d["causal"], col - row, -jnp.inf))
    m = g_mat * lm
    xh = x_ref[0, :, pl.ds(h * HEAD_DIM, HEAD_DIM)]
    dtc = d["dt"][:, h:h + 1]
    xd = xh * dtc
    e = jnp.exp(col)
    clast = d["acs"][BLK - 1:BLK, h:h + 1]
    f = jnp.exp(clast - col)
    return dict(col=col, lm=lm, m=m, xh=xh, dtc=dtc, xd=xd, e=e, ecl=jnp.exp(clast), f=f, xf=xd * f)


def _ssd_specs(nc, rev):
    cidx = (lambda c: nc - 1 - c) if rev else (lambda c: c)
    x_spec = pl.BlockSpec((1, BLK, D_INNER), lambda b, c: (b, cidx(c), 0))
    bm_spec = pl.BlockSpec((1, BLK, SSM_GROUPS * D_STATE), lambda b, c: (b, cidx(c), D_INNER // (SSM_GROUPS * D_STATE)))
    cm_spec = pl.BlockSpec((1, BLK, SSM_GROUPS * D_STATE), lambda b, c: (b, cidx(c), D_INNER // (SSM_GROUPS * D_STATE) + 1))
    dt_spec = pl.BlockSpec((1, BLK, BLK), lambda b, c: (b, cidx(c), R_DT // BLK))
    dtt_spec = pl.BlockSpec((1, BLK, BLK), lambda b, c: (b, 0, cidx(c)))
    prow_spec = pl.BlockSpec((8, BLK), lambda b, c: (0, 0))
    pcol_spec = pl.BlockSpec((BLK, 8), lambda b, c: (0, 0))
    st_spec = pl.BlockSpec((1, 1, SSM_HEADS, HEAD_DIM, D_STATE), lambda b, c: (b, cidx(c), 0, 0, 0))
    y_spec = pl.BlockSpec((1, BLK, D_INNER), lambda b, c: (b, cidx(c), 0))
    return x_spec, bm_spec, cm_spec, dt_spec, dtt_spec, prow_spec, pcol_spec, st_spec, y_spec


def _ssd_fwd(xa, rest3, dtr_t, prow, pcol):
    bsz, seq, _ = xa.shape
    nc = seq // BLK

    def body(x_ref, bm_ref, cm_ref, dt_ref, dtt_ref, prow_ref, pcol_ref, y_ref, sin_ref, s_ref):
        @pl.when(pl.program_id(1) == 0)
        def _():
            s_ref[...] = jnp.zeros_like(s_ref)

        d = _ssd_decays(dt_ref[0], dtt_ref[0], prow_ref, pcol_ref)
        for g in range(SSM_GROUPS):
            lanes = pl.ds(g * D_STATE, D_STATE)
            bg = bm_ref[0, :, lanes].astype(BF16)
            cg = cm_ref[0, :, lanes].astype(BF16)
            g_mat = _nt(cg, bg)
            for hh in range(HEADS_PER_SSM_GROUP):
                h = g * HEADS_PER_SSM_GROUP + hh
                s_in = s_ref[h]
                sin_ref[0, 0, h] = s_in
                q = _ssd_head(h, d, x_ref, s_in, g_mat)
                y_diag = jnp.dot(q["m"].astype(BF16), q["xd"].astype(BF16), preferred_element_type=F32)
                y_off = _nt(cg, s_in.astype(BF16)) * q["e"]
                s_ref[h] = s_in * q["ecl"] + _tn(q["xf"].astype(BF16), bg)
                y_ref[0, :, pl.ds(h * HEAD_DIM, HEAD_DIM)] = y_diag + y_off + prow_ref[P_DSKIP:P_DSKIP + 1, h:h + 1] * q["xh"]

    x_spec, bm_spec, cm_spec, dt_spec, dtt_spec, prow_spec, pcol_spec, st_spec, y_spec = _ssd_specs(nc, False)
    return pl.pallas_call(
        body, name="ssd_fwd",
        grid=(bsz, nc),
        in_specs=[x_spec, bm_spec, cm_spec, dt_spec, dtt_spec, prow_spec, pcol_spec],
        out_specs=[y_spec, st_spec],
        out_shape=[jax.ShapeDtypeStruct((bsz, seq, D_INNER), F32),
                   jax.ShapeDtypeStruct((bsz, nc, SSM_HEADS, HEAD_DIM, D_STATE), F32)],
        scratch_shapes=[pltpu.VMEM((SSM_HEADS, HEAD_DIM, D_STATE), F32)],
        compiler_params=_params("parallel", "arbitrary"),
    )(xa, xa, xa, rest3, dtr_t, prow, pcol)


def _ssd_bwd(xa, rest3, dtr_t, prow, pcol, s_in_all, dy):
    bsz, seq, _ = xa.shape
    nc = seq // BLK

    def body(x_ref, bm_ref, cm_ref, dt_ref, dtt_ref, prow_ref, pcol_ref, sin_ref, dy_ref,
             dxa_ref, ddt_ref, ddtt_ref, gprow_ref, gpcol_ref, ds_ref, dc_ref, ddtc_ref, drt_ref):
        first = jnp.logical_and(pl.program_id(0) == 0, pl.program_id(1) == 0)

        @pl.when(first)
        def _():
            gprow_ref[...] = jnp.zeros_like(gprow_ref)
            gpcol_ref[...] = jnp.zeros_like(gpcol_ref)

        @pl.when(pl.program_id(1) == 0)
        def _():
            ds_ref[...] = jnp.zeros_like(ds_ref)

        dc_ref[...] = jnp.zeros_like(dc_ref)
        ddtc_ref[...] = jnp.zeros_like(ddtc_ref)
        drt_ref[...] = jnp.zeros_like(drt_ref)
        d = _ssd_decays(dt_ref[0], dtt_ref[0], prow_ref, pcol_ref)
        for g in range(SSM_GROUPS):
            lanes = pl.ds(g * D_STATE, D_STATE)
            bg = bm_ref[0, :, lanes].astype(BF16)
            cg = cm_ref[0, :, lanes].astype(BF16)
            g_mat = _nt(cg, bg)
            d_g = jnp.zeros((BLK, BLK), F32)
            d_bg = jnp.zeros((BLK, D_STATE), F32)
            d_cg = jnp.zeros((BLK, D_STATE), F32)
            for hh in range(HEADS_PER_SSM_GROUP):
                h = g * HEADS_PER_SSM_GROUP + hh
                head_lanes = pl.ds(h * HEAD_DIM, HEAD_DIM)
                s_in = sin_ref[0, 0, h]
                s_in16 = s_in.astype(BF16)
                q = _ssd_head(h, d, x_ref, s_in, g_mat)
                m16, xd16 = q["m"].astype(BF16), q["xd"].astype(BF16)
                d_y = dy_ref[0, :, head_lanes]
                d_y16 = d_y.astype(BF16)
                d_so = ds_ref[h]
                d_so16 = d_so.astype(BF16)
                d_x = prow_ref[P_DSKIP:P_DSKIP + 1, h:h + 1] * d_y
                gprow_ref[P_DSKIP:P_DSKIP + 1, h:h + 1] += _sum_all(d_y * q["xh"])
                d_m = _nt(d_y16, xd16)
                d_xd = _tn(m16, d_y16)
                w = d_m * q["m"]
                d_g = d_g + d_m * q["lm"]
                d_col = jnp.sum(w, axis=1, keepdims=True)
                drt_ref[h:h + 1, :] = -jnp.sum(w, axis=0, keepdims=True)
                qmat = _nt(cg, s_in16)
                d_q16 = (d_y * q["e"]).astype(BF16)
                d_col = d_col + jnp.sum(d_y * qmat, axis=1, keepdims=True) * q["e"]
                d_cg = d_cg + jnp.dot(d_q16, s_in16, preferred_element_type=F32)
                d_sin = _tn(d_q16, cg) + d_so * q["ecl"]
                d_clast = _sum_all(d_so * s_in) * q["ecl"]
                d_xf = _nt(bg, d_so16)
                d_bg = d_bg + jnp.dot(q["xf"].astype(BF16), d_so16, preferred_element_type=F32)
                d_xd = d_xd + d_xf * q["f"]
                d_f = jnp.sum(d_xf * q["xd"], axis=1, keepdims=True) * q["f"]
                d_clast = d_clast + jnp.sum(d_f, axis=0, keepdims=True)
                d_col = d_col - d_f + jnp.where(d["last_row"], d_clast, 0.0)
                dxa_ref[0, :, head_lanes] = d_x + d_xd * q["dtc"]
                dc_ref[:, h:h + 1] = d_col
                ddtc_ref[:, h:h + 1] = jnp.sum(d_xd * q["xh"], axis=1, keepdims=True)
                ds_ref[h] = d_sin
            d_g16 = d_g.astype(BF16)
            dxa_ref[0, :, pl.ds(D_INNER + g * D_STATE, D_STATE)] = d_bg + _tn(d_g16, cg)
            dxa_ref[0, :, pl.ds(D_INNER + (SSM_GROUPS + g) * D_STATE, D_STATE)] = d_cg + jnp.dot(d_g16, bg, preferred_element_type=F32)
        d_a = _dot_hi(d["tri_u"], dc_ref[...])
        d_pre = (ddtc_ref[...] + d_a * d["ah_row"]) * _sigmoid(d["pre"])
        ddt_ref[0] = d_pre
        gprow_ref[P_DTB:P_DTB + 1, :] += jnp.sum(d_pre, axis=0, keepdims=True)
        gprow_ref[P_ALOG:P_ALOG + 1, :] += jnp.sum(d_a * d["dt"], axis=0, keepdims=True) * d["ah_row"]
        d_at = _dot_hi(drt_ref[...], d["tri"])
        d_pre_t = d_at * d["ah_col"] * _sigmoid(d["pre_t"])
        ddtt_ref[0] = d_pre_t
        gpcol_ref[:, P_DTB:P_DTB + 1] += jnp.sum(d_pre_t, axis=1, keepdims=True)
        gpcol_ref[:, P_ALOG:P_ALOG + 1] += jnp.sum(d_at * d["dt_t"], axis=1, keepdims=True) * d["ah_col"]

    x_spec, bm_spec, cm_spec, dt_spec, dtt_spec, prow_spec, pcol_spec, st_spec, y_spec = _ssd_specs(nc, True)
    return pl.pallas_call(
        body, name="ssd_bwd",
        grid=(bsz, nc),
        in_specs=[x_spec, bm_spec, cm_spec, dt_spec, dtt_spec, prow_spec, pcol_spec, st_spec, y_spec],
        out_specs=[pl.BlockSpec((1, BLK, CONV_DIM), lambda b, c: (b, nc - 1 - c, 0)),
                   pl.BlockSpec((1, BLK, BLK), lambda b, c: (b, nc - 1 - c, 0)),
                   pl.BlockSpec((1, BLK, BLK), lambda b, c: (b, 0, nc - 1 - c)),
                   prow_spec, pcol_spec],
        out_shape=[jax.ShapeDtypeStruct((bsz, seq, CONV_DIM), F32),
                   jax.ShapeDtypeStruct((bsz, seq, BLK), F32),
                   jax.ShapeDtypeStruct((bsz, BLK, seq), F32),
                   jax.ShapeDtypeStruct((8, BLK), F32),
                   jax.ShapeDtypeStruct((BLK, 8), F32)],
        scratch_shapes=[pltpu.VMEM((SSM_HEADS, HEAD_DIM, D_STATE), F32), pltpu.VMEM((BLK, BLK), F32),
                        pltpu.VMEM((BLK, BLK), F32), pltpu.VMEM((BLK, BLK), F32)],
        compiler_params=_params("arbitrary", "arbitrary"),
    )(xa, xa, xa, rest3, dtr_t, prow, pcol, s_in_all, dy)


ROW_TILE = 256
CMB_COLS = 256
RMS_COLS = D_INNER // SSM_GROUPS


def _combine_weights(l_refs):
    ls = [r[...] for r in l_refs]
    mx = jnp.maximum(jnp.maximum(ls[0], ls[1]), ls[2])
    es = [jnp.exp(l - mx) for l in ls]
    inv = 1.0 / (es[0] + es[1] + es[2])
    return [e * inv for e in es]


def _combine_specs():
    a = pl.BlockSpec((ROW_TILE, CMB_COLS), lambda i, j: (i, j))
    gatt = pl.BlockSpec((ROW_TILE, CMB_COLS), lambda i, j: (i, R_GATT // CMB_COLS + j))
    return a, gatt


def _combine_fwd(outs, lses, rest):
    t = rest.shape[0]

    def body(o0, o1, o2, l0, l1, l2, ga_ref, oa_ref):
        ws = _combine_weights((l0, l1, l2))
        o = ws[0] * o0[...] + ws[1] * o1[...] + ws[2] * o2[...]
        oa_ref[...] = (o * _silu_and_grad(ga_ref[...])[0]).astype(BF16)

    a, gatt = _combine_specs()
    return pl.pallas_call(
        body, name="combine_fwd",
        grid=(t // ROW_TILE, GROUP_COLS // CMB_COLS),
        in_specs=[a] * 6 + [gatt],
        out_specs=a,
        out_shape=jax.ShapeDtypeStruct((t, GROUP_COLS), BF16),
        compiler_params=_params("parallel", "parallel"),
    )(*outs, *lses, rest)


def _combine_bwd(outs, lses, rest, d_oa):
    t = rest.shape[0]

    def body(o0, o1, o2, l0, l1, l2, ga_ref, doa_ref, do0, do1, do2, dd0, dd1, dd2, dga_ref):
        ws = _combine_weights((l0, l1, l2))
        o = ws[0] * o0[...] + ws[1] * o1[...] + ws[2] * o2[...]
        sg, dsg = _silu_and_grad(ga_ref[...])
        d_oa_v = doa_ref[...]
        d_o = d_oa_v * sg
        dga_ref[...] = (d_oa_v * o * dsg).astype(BF16)
        for w, do_ref, dd_ref in zip(ws, (do0, do1, do2), (dd0, dd1, dd2)):
            d_out = w * d_o
            do_ref[...] = d_out
            dd_ref[...] = d_out * o

    a, gatt = _combine_specs()
    s16 = jax.ShapeDtypeStruct((t, GROUP_COLS), BF16)
    s32 = jax.ShapeDtypeStruct((t, GROUP_COLS), F32)
    return pl.pallas_call(
        body, name="combine_bwd",
        grid=(t // ROW_TILE, GROUP_COLS // CMB_COLS),
        in_specs=[a] * 6 + [gatt, a],
        out_specs=[a] * 7,
        out_shape=[s32, s32, s32, s32, s32, s32, s16],
        compiler_params=_params("parallel", "parallel"),
    )(*outs, *lses, rest, d_oa)


def _gatenorm_fwd(y, rest, norm_w):
    t = rest.shape[0]

    def body(y_ref, z_ref, w_ref, o_ref):
        u = y_ref[...] * _silu_and_grad(z_ref[...])[0]
        rs = lax.rsqrt(jnp.mean(u * u, axis=-1, keepdims=True) + RMS_EPS)
        o_ref[...] = (u * rs * w_ref[...]).astype(BF16)

    return pl.pallas_call(
        body, name="gatenorm_fwd",
        grid=(t // ROW_TILE, SSM_GROUPS),
        in_specs=[pl.BlockSpec((ROW_TILE, RMS_COLS), lambda i, j: (i, j)),
                  pl.BlockSpec((ROW_TILE, RMS_COLS), lambda i, j: (i, R_Z // RMS_COLS + j)),
                  pl.BlockSpec((1, RMS_COLS), lambda i, j: (0, j))],
        out_specs=pl.BlockSpec((ROW_TILE, RMS_COLS), lambda i, j: (i, j)),
        out_shape=jax.ShapeDtypeStruct((t, D_INNER), BF16),
        compiler_params=_params("parallel", "parallel"),
    )(y, rest, norm_w)


def _gatenorm_bwd(y, rest, norm_w, d_ys):
    t = rest.shape[0]

    def body(y_ref, z_ref, w_ref, g_ref, dy_ref, dz_ref, dw_ref):
        @pl.when(pl.program_id(1) == 0)
        def _():
            dw_ref[...] = jnp.zeros_like(dw_ref)

        yv = y_ref[...]
        sz, dsz = _silu_and_grad(z_ref[...])
        u = yv * sz
        rs = lax.rsqrt(jnp.mean(u * u, axis=-1, keepdims=True) + RMS_EPS)
        un = u * rs
        g = g_ref[...]
        dw_ref[0:1, :] += jnp.sum(g * un, axis=0, keepdims=True)
        d_un = g * w_ref[...]
        d_u = rs * (d_un - un * jnp.mean(d_un * un, axis=-1, keepdims=True))
        dy_ref[...] = d_u * sz
        dz_ref[...] = (d_u * yv * dsz).astype(BF16)

    blk = pl.BlockSpec((ROW_TILE, RMS_COLS), lambda j, i: (i, j))
    return pl.pallas_call(
        body, name="gatenorm_bwd",
        grid=(SSM_GROUPS, t // ROW_TILE),
        in_specs=[blk, pl.BlockSpec((ROW_TILE, RMS_COLS), lambda j, i: (i, R_Z // RMS_COLS + j)),
                  pl.BlockSpec((1, RMS_COLS), lambda j, i: (0, j)), blk],
        out_specs=[blk, blk, pl.BlockSpec((8, RMS_COLS), lambda j, i: (0, j))],
        out_shape=[jax.ShapeDtypeStruct((t, D_INNER), F32), jax.ShapeDtypeStruct((t, D_INNER), BF16),
                   jax.ShapeDtypeStruct((8, D_INNER), F32)],
        compiler_params=_params("parallel", "arbitrary"),
    )(y, rest, norm_w, d_ys)


def _row_specs():
    full = pl.BlockSpec((ROW_TILE, D_MODEL), lambda i: (i, 0))
    vec = pl.BlockSpec((8, D_MODEL), lambda i: (0, 0))
    at = lambda off: pl.BlockSpec((ROW_TILE, D_MODEL), lambda i: (i, off // D_MODEL))
    return full, vec, at


def _merge_fwd(y_a, y_b, rest, b_gate):
    t = rest.shape[0]

    def body(ya_ref, yb_ref, ga_ref, gb_ref, bg_ref, o_ref):
        sa = _sigmoid(ga_ref[...] + bg_ref[0:1, :])
        sb = _sigmoid(gb_ref[...] + bg_ref[1:2, :])
        o_ref[...] = (sa * ya_ref[...] + sb * yb_ref[...]).astype(BF16)

    full, vec, at = _row_specs()
    return pl.pallas_call(
        body, name="merge_fwd",
        grid=(t // ROW_TILE,),
        in_specs=[full, full, at(R_GM), at(R_GM + D_MODEL), vec],
        out_specs=full,
        out_shape=jax.ShapeDtypeStruct((t, D_MODEL), BF16),
        compiler_params=_params("parallel"),
    )(y_a, y_b, rest, rest, b_gate)


def _merge_bwd(y_a, y_b, rest, b_gate, d_merged):
    t = rest.shape[0]

    def body(ya_ref, yb_ref, ga_ref, gb_ref, bg_ref, dm_ref, dya_ref, dyb_ref, dga_ref, dgb_ref, dbg_ref):
        @pl.when(pl.program_id(0) == 0)
        def _():
            dbg_ref[...] = jnp.zeros_like(dbg_ref)

        dm = dm_ref[...]
        for row, y_ref, g_ref, dy_ref, dg_ref in ((0, ya_ref, ga_ref, dya_ref, dga_ref), (1, yb_ref, gb_ref, dyb_ref, dgb_ref)):
            s = _sigmoid(g_ref[...] + bg_ref[row:row + 1, :])
            dy_ref[...] = (dm * s).astype(BF16)
            dg = dm * y_ref[...] * s * (1.0 - s)
            dg_ref[...] = dg.astype(BF16)
            dbg_ref[row:row + 1, :] += jnp.sum(dg, axis=0, keepdims=True)

    full, vec, at = _row_specs()
    s16 = jax.ShapeDtypeStruct((t, D_MODEL), BF16)
    return pl.pallas_call(
        body, name="merge_bwd",
        grid=(t // ROW_TILE,),
        in_specs=[full, full, at(R_GM), at(R_GM + D_MODEL), vec, full],
        out_specs=[full, full, full, full, vec],
        out_shape=[s16, s16, s16, s16, jax.ShapeDtypeStruct((8, D_MODEL), F32)],
        compiler_params=_params("arbitrary"),
    )(y_a, y_b, rest, rest, b_gate, d_merged)


ST_LNG, ST_LNB, ST_BG2, ST_LOSS = 0, 1, 2, 3


def _final(x, mix, pw, rest, b_gate, ln_gb, target):
    t = rest.shape[0]

    def body(x_ref, mix_ref, pw_ref, gp_ref, bg_ref, ln_ref, tgt_ref, dpre_ref, dpre16_ref, dgp_ref, dpw_ref, st_ref):
        @pl.when(pl.program_id(0) == 0)
        def _():
            st_ref[...] = jnp.zeros_like(st_ref)

        sp = _sigmoid(gp_ref[...] + bg_ref[2:3, :])
        pw = pw_ref[...]
        pre = ALPHA * x_ref[...] + mix_ref[...] + sp * pw
        xc = pre - jnp.mean(pre, axis=-1, keepdims=True)
        rstd = lax.rsqrt(jnp.mean(xc * xc, axis=-1, keepdims=True) + LN_EPS)
        xhat = xc * rstd
        gain = ln_ref[0:1, :]
        err = xhat * gain + ln_ref[1:2, :] - tgt_ref[...]
        d_yo = err * (1.0 / D_MODEL)
        d_xhat = d_yo * gain
        d_pre = rstd * (d_xhat - jnp.mean(d_xhat, axis=-1, keepdims=True)
                        - xhat * jnp.mean(d_xhat * xhat, axis=-1, keepdims=True))
        dpre_ref[...] = d_pre
        dpre16_ref[...] = d_pre.astype(BF16)
        dgp = d_pre * pw * sp * (1.0 - sp)
        dgp_ref[...] = dgp.astype(BF16)
        dpw_ref[...] = (d_pre * sp).astype(BF16)
        st_ref[ST_LNG:ST_LNG + 1, :] += jnp.sum(d_yo * xhat, axis=0, keepdims=True)
        st_ref[ST_LNB:ST_LNB + 1, :] += jnp.sum(d_yo, axis=0, keepdims=True)
        st_ref[ST_BG2:ST_BG2 + 1, :] += jnp.sum(dgp, axis=0, keepdims=True)
        st_ref[ST_LOSS:ST_LOSS + 1, :] += jnp.sum(err * err, axis=0, keepdims=True) * (0.5 / D_MODEL)

    full, vec, at = _row_specs()
    s16 = jax.ShapeDtypeStruct((t, D_MODEL), BF16)
    return pl.pallas_call(
        body, name="final",
        grid=(t // ROW_TILE,),
        in_specs=[full, full, full, at(R_GPLE), vec, vec, full],
        out_specs=[full, full, full, full, vec],
        out_shape=[jax.ShapeDtypeStruct((t, D_MODEL), F32), s16, s16, s16, jax.ShapeDtypeStruct((8, D_MODEL), F32)],
        compiler_params=_params("arbitrary"),
    )(x, mix, pw, rest, b_gate, ln_gb, target)


def _mesh_position():
    return lax.axis_index("x"), lax.axis_index("y"), lax.axis_index("c")


def _flip(pos, k):
    x, y, c = pos
    return ((1 - x) if k & 4 else x, (1 - y) if k & 2 else y, (1 - c) if k & 1 else c)


def _linear(pos):
    return 4 * pos[0] + 2 * pos[1] + pos[2]


def _exchange(arrays, scatter, name):
    n = len(arrays)

    def body(*refs):
        ins, outs = refs[:n], refs[n:2 * n]
        send_sems, recv_sems, local_sems = refs[2 * n:]
        me = _mesh_position()
        me_i = _linear(me)

        def src_for(i, dest_i):
            return ins[i].at[dest_i] if scatter[i] else ins[i]

        local = [pltpu.make_async_copy(src_for(i, me_i), outs[i].at[me_i], local_sems.at[i]) for i in range(n)]
        for cp in local:
            cp.start()
        started = []
        for k in range(1, N_DEV):
            peer = _flip(me, k)
            peer_i = _linear(peer)
            for i in range(n):
                sem = i * (N_DEV - 1) + k - 1
                cp = pltpu.make_async_remote_copy(
                    src_ref=src_for(i, peer_i), dst_ref=outs[i].at[me_i], send_sem=send_sems.at[sem],
                    recv_sem=recv_sems.at[sem], device_id=peer, device_id_type=pl.DeviceIdType.MESH)
                cp.start()
                started.append(cp)
        for k in range(1, N_DEV):
            peer = _flip(me, k)
            peer_i = _linear(peer)
            for i in range(n):
                sem = i * (N_DEV - 1) + k - 1
                pltpu.make_async_remote_copy(
                    src_ref=src_for(i, peer_i), dst_ref=outs[i].at[peer_i], send_sem=send_sems.at[sem],
                    recv_sem=recv_sems.at[sem], device_id=peer, device_id_type=pl.DeviceIdType.MESH).wait_recv()
        for cp in started:
            cp.wait_send()
        for cp in local:
            cp.wait()

    any_spec = pl.BlockSpec(memory_space=pl.ANY)
    out_shape = [jax.ShapeDtypeStruct(a.shape if s else (N_DEV,) + a.shape, a.dtype) for a, s in zip(arrays, scatter)]
    return pl.pallas_call(
        body, name=name,
        in_specs=[any_spec] * n,
        out_specs=[any_spec] * n,
        out_shape=out_shape,
        scratch_shapes=[pltpu.SemaphoreType.DMA((n * (N_DEV - 1),)), pltpu.SemaphoreType.DMA((n * (N_DEV - 1),)),
                        pltpu.SemaphoreType.DMA((n,))],
        compiler_params=pltpu.CompilerParams(has_side_effects=True),
    )(*arrays)


N_CHIPS = N_DEV // 2


def _other_chips(x, y):
    return [(1 - x, y), (x, 1 - y), (1 - x, 1 - y)]


def _gather_two_level(arrays, name):
    n = len(arrays)
    per = N_DEV - 1

    def body(*refs):
        ins, outs = refs[:n], refs[n:2 * n]
        send_sems, recv_sems, local_sems = refs[2 * n:]
        x, y, c = _mesh_position()
        me, sibling = (x, y, c), (x, y, 1 - c)
        chips = _other_chips(x, y)

        def copy(i, k, block, to, src=None):
            slot = outs[i].at[_linear(block)]
            return pltpu.make_async_remote_copy(
                src_ref=slot if src is None else src, dst_ref=slot, send_sem=send_sems.at[i * per + k],
                recv_sem=recv_sems.at[i * per + k], device_id=to, device_id_type=pl.DeviceIdType.MESH)

        local = [pltpu.make_async_copy(ins[i], outs[i].at[_linear(me)], local_sems.at[i]) for i in range(n)]
        for cp in local:
            cp.start()
        started = []
        for i in range(n):
            first = [copy(i, 0, me, sibling, src=ins[i])]
            first += [copy(i, 1 + j, me, (*chip, c), src=ins[i]) for j, chip in enumerate(chips)]
            for cp in first:
                cp.start()
            started += first
        for j, chip in enumerate(chips):
            for i in range(n):
                copy(i, 1 + j, (*chip, c), me).wait_recv()
                passed = copy(i, 4 + j, (*chip, c), sibling)
                passed.start()
                started.append(passed)
        for i in range(n):
            copy(i, 0, sibling, me).wait_recv()
            for j, chip in enumerate(chips):
                copy(i, 4 + j, (*chip, 1 - c), me).wait_recv()
        for cp in started:
            cp.wait_send()
        for cp in local:
            cp.wait()

    any_spec = pl.BlockSpec(memory_space=pl.ANY)
    return pl.pallas_call(
        body, name=name,
        in_specs=[any_spec] * n,
        out_specs=[any_spec] * n,
        out_shape=[jax.ShapeDtypeStruct((N_DEV,) + a.shape, a.dtype) for a in arrays],
        scratch_shapes=[pltpu.SemaphoreType.DMA((n * per,)), pltpu.SemaphoreType.DMA((n * per,)), pltpu.SemaphoreType.DMA((n,))],
        compiler_params=pltpu.CompilerParams(has_side_effects=True),
    )(*arrays)


def _pair_reduce(a, name, rows):
    _, r, c = a.shape
    assert r % rows == 0
    n_steps = r // rows
    a5 = a.reshape(N_CHIPS, 2, r, c)
    core = lax.axis_index("c").astype(jnp.int32).reshape(1)

    def body(core_ref, keep_ref, send_ref, o_ref, land, send_sems, recv_sems, credits):
        i = pl.program_id(0)
        slot = i % 2
        x, y, cc = _mesh_position()
        sibling = (x, y, 1 - cc)

        @pl.when(i >= 2)
        def _():
            pl.semaphore_wait(credits.at[slot], 1)

        rdma = pltpu.make_async_remote_copy(
            src_ref=send_ref, dst_ref=land.at[slot], send_sem=send_sems.at[slot], recv_sem=recv_sems.at[slot],
            device_id=sibling, device_id_type=pl.DeviceIdType.MESH)
        rdma.start()
        rdma.wait_recv()
        o_ref[...] = (keep_ref[:, 0].astype(F32) + land[slot, :, 0].astype(F32)).astype(o_ref.dtype)
        rdma.wait_send()

        @pl.when(i + 2 < n_steps)
        def _():
            pl.semaphore_signal(credits.at[slot], inc=1, device_id=sibling, device_id_type=pl.DeviceIdType.MESH)

    grid_spec = pltpu.PrefetchScalarGridSpec(
        num_scalar_prefetch=1,
        grid=(n_steps,),
        in_specs=[pl.BlockSpec((N_CHIPS, 1, rows, c), lambda i, core_ref: (0, core_ref[0], i, 0)),
                  pl.BlockSpec((N_CHIPS, 1, rows, c), lambda i, core_ref: (0, 1 - core_ref[0], i, 0))],
        out_specs=pl.BlockSpec((N_CHIPS, rows, c), lambda i, core_ref: (0, i, 0)),
        scratch_shapes=[pltpu.VMEM((2, N_CHIPS, 1, rows, c), a.dtype), pltpu.SemaphoreType.DMA((2,)),
                        pltpu.SemaphoreType.DMA((2,)), pltpu.SemaphoreType.REGULAR((2,))],
    )
    return pl.pallas_call(
        body, name=name, grid_spec=grid_spec,
        out_shape=jax.ShapeDtypeStruct((N_CHIPS, r, c), a.dtype),
        compiler_params=pltpu.CompilerParams(dimension_semantics=("arbitrary",), vmem_limit_bytes=VMEM_LIMIT_BYTES,
                                             has_side_effects=True),
    )(core, a5, a5)


def _chip_exchange(arrays, name):
    n = len(arrays)
    per = N_CHIPS - 1

    def body(*refs):
        ins, outs = refs[:n], refs[n:2 * n]
        send_sems, recv_sems, local_sems = refs[2 * n:]
        x, y, c = _mesh_position()
        me_q = 2 * x + y
        local = [pltpu.make_async_copy(ins[i].at[me_q], outs[i].at[me_q], local_sems.at[i]) for i in range(n)]
        for cp in local:
            cp.start()
        started = []
        for j, (px, py) in enumerate(_other_chips(x, y)):
            for i in range(n):
                cp = pltpu.make_async_remote_copy(
                    src_ref=ins[i].at[2 * px + py], dst_ref=outs[i].at[me_q], send_sem=send_sems.at[i * per + j],
                    recv_sem=recv_sems.at[i * per + j], device_id=(px, py, c), device_id_type=pl.DeviceIdType.MESH)
                cp.start()
                started.append(cp)
        for j, (px, py) in enumerate(_other_chips(x, y)):
            for i in range(n):
                pltpu.make_async_remote_copy(
                    src_ref=ins[i].at[2 * px + py], dst_ref=outs[i].at[2 * px + py], send_sem=send_sems.at[i * per + j],
                    recv_sem=recv_sems.at[i * per + j], device_id=(px, py, c), device_id_type=pl.DeviceIdType.MESH).wait_recv()
        for cp in started:
            cp.wait_send()
        for cp in local:
            cp.wait()

    any_spec = pl.BlockSpec(memory_space=pl.ANY)
    return pl.pallas_call(
        body, name=name,
        in_specs=[any_spec] * n,
        out_specs=[any_spec] * n,
        out_shape=[jax.ShapeDtypeStruct(a.shape, a.dtype) for a in arrays],
        scratch_shapes=[pltpu.SemaphoreType.DMA((n * per,)), pltpu.SemaphoreType.DMA((n * per,)), pltpu.SemaphoreType.DMA((n,))],
        compiler_params=pltpu.CompilerParams(has_side_effects=True),
    )(*arrays)


def _adam_reduce(parts, w, m, v, name, rows):
    r, c = w.shape
    n_parts = parts.shape[0]
    assert r % rows == 0
    c1 = 1.0 - ADAM_B1 ** ADAM_STEP
    c2 = 1.0 - ADAM_B2 ** ADAM_STEP

    def body(p_ref, w_ref, m_ref, v_ref, g_ref, d_ref, nm_ref, nv_ref):
        g = p_ref[0].astype(F32)
        for s in range(1, n_parts):
            g = g + p_ref[s].astype(F32)
        g_ref[...] = g
        nm = ADAM_B1 * m_ref[...] + (1.0 - ADAM_B1) * g
        nv = ADAM_B2 * v_ref[...] + (1.0 - ADAM_B2) * (g * g)
        nm_ref[...] = nm
        nv_ref[...] = nv
        d_ref[...] = -ADAM_LR * ((nm / c1) / (jnp.sqrt(nv / c2) + ADAM_EPS) + ADAM_WD * w_ref[...])

    blk = pl.BlockSpec((rows, c), lambda i: (i, 0))
    shape = jax.ShapeDtypeStruct((r, c), F32)
    return pl.pallas_call(
        body, name=name,
        grid=(r // rows,),
        in_specs=[pl.BlockSpec((n_parts, rows, c), lambda i: (0, i, 0)), blk, blk, blk],
        out_specs=[blk] * 4,
        out_shape=[shape] * 4,
        compiler_params=_params("parallel"),
    )(parts, w, m, v)


def _lane_total(rows8):
    def body(a_ref, o_ref):
        o_ref[...] = _sum_all(a_ref[...])

    return pl.pallas_call(body, name="loss_total", out_shape=jax.ShapeDtypeStruct((1, 1), F32))(rows8)


def _permute_w_in(w):
    rows = w.shape[0]
    n_pairs = N_GROUPS * HEAD_PAIRS
    qkv = w[:, :V_END].reshape(rows, 3, n_pairs, BLK).transpose(0, 2, 1, 3).reshape(rows, V_END)
    return jnp.concatenate(
        [w[:, DT_END:GMERGE_END], w[:, GATT_END:Z_END], w[:, Z_END:XBC_END], w[:, GMERGE_END:], w[:, V_END:GATT_END],
         w[:, XBC_END:DT_END], jnp.zeros((rows, QKV_OFF - R_DT - SSM_HEADS), w.dtype), qkv], axis=1)


def _unpermute_w_in(g):
    rows = g.shape[0]
    n_pairs = N_GROUPS * HEAD_PAIRS
    qkv = g[:, QKV_OFF:].reshape(rows, n_pairs, 3, BLK).transpose(0, 2, 1, 3).reshape(rows, V_END)
    return jnp.concatenate(
        [qkv, g[:, R_GATT:R_GATT + GROUP_COLS], g[:, R_Z:R_Z + D_INNER], g[:, R_XBC:R_XBC + CONV_DIM],
         g[:, R_DT:R_DT + SSM_HEADS], g[:, R_GM:R_GM + 2 * D_MODEL], g[:, R_GPLE:R_GPLE + D_MODEL]], axis=1)


SMALL_ROWS = 80
_SMALL_LAYOUT = (("conv_b", CONV_DIM), ("dt_bias", BLK), ("a_log", BLK), ("d_skip", BLK), ("ssm_norm_w", D_INNER),
                 ("ln_g", D_MODEL), ("ln_b", D_MODEL), ("rel_bias", NUM_BUCKETS * ATT_HEADS), ("loss", D_MODEL))


def _pack_small(vals):
    flat = []
    for name, width in _SMALL_LAYOUT:
        v = vals.get(name)
        v = jnp.zeros((width,), F32) if v is None else v.reshape(-1).astype(F32)
        flat.append(jnp.pad(v, (0, width - v.shape[0])))
    flat = jnp.concatenate(flat)
    return jnp.pad(flat, (0, SMALL_ROWS * BLK - flat.shape[0])).reshape(SMALL_ROWS, BLK)


def _unpack_small(packed):
    flat = packed.reshape(-1)
    out, pos = {}, 0
    for name, width in _SMALL_LAYOUT:
        out[name] = flat[pos:pos + width]
        pos += width
    for name in ("dt_bias", "a_log", "d_skip"):
        out[name] = out[name][:SSM_HEADS]
    out["rel_bias"] = out["rel_bias"].reshape(NUM_BUCKETS, ATT_HEADS)
    return out


def _pack_cols(b_gate_part, conv_w_part):
    return jnp.concatenate([jnp.pad(b_gate_part, ((0, 5), (0, 0))), jnp.pad(conv_w_part, ((0, 4), (0, 0)))], axis=1)


def _pack_cols_all(b_gate_full, conv_w_full):
    bg = b_gate_full.reshape(3, N_DEV, BLK).transpose(1, 0, 2)
    cw = conv_w_full.reshape(CONV_WIDTH, N_DEV, CONV_DIM // N_DEV).transpose(1, 0, 2)
    return jnp.concatenate([jnp.pad(bg, ((0, 0), (0, 5), (0, 0))), jnp.pad(cw, ((0, 0), (0, 4), (0, 0)))], axis=2)


def _unpack_cols_all(packed):
    bg = packed[:, :3, :BLK].transpose(1, 0, 2).reshape(3, D_MODEL)
    cw = packed[:, :CONV_WIDTH, BLK:].transpose(1, 0, 2).reshape(CONV_WIDTH, CONV_DIM)
    return bg, cw


def _local_step(x, p, target, wp16, wb16, wo16, wple16, b_gate, conv_w, small):
    bsz, seq, _ = x.shape
    t = bsz * seq
    x2 = x.reshape(t, D_MODEL)
    x16 = x2.astype(BF16)
    p16 = p.reshape(t, PLE_DIM).astype(BF16)
    tgt2 = target.reshape(t, D_MODEL)
    b_gate8 = jnp.pad(b_gate, ((0, 5), (0, 0)))
    ln_gb = jnp.pad(jnp.stack([small["ln_g"], small["ln_b"]]), ((0, 6), (0, 0)))
    conv_b = small["conv_b"].reshape(1, CONV_DIM)
    norm_w = small["ssm_norm_w"].reshape(1, D_INNER)
    pad_heads = lambda v: jnp.pad(v, (0, BLK - SSM_HEADS))
    prow = jnp.pad(jnp.stack([pad_heads(small["dt_bias"]), pad_heads(small["a_log"]), pad_heads(small["d_skip"])]), ((0, 5), (0, 0)))
    pcol = prow.T
    wa16, wbb16 = wb16[:GROUP_COLS], wb16[GROUP_COLS:]

    rest = _matmul(x16, wp16, mode="nn", out_dtype=F32, name="inproj", tm=512, tn=2304, tk=D_MODEL, n_outer=True)
    rest3 = rest.reshape(bsz, seq, HCAT_COLS)
    biases, onehots, outs, lses = [], [], [], []
    for g, (_, dil) in enumerate(DILATED_PATTERNS):
        bias, onehot = _bias_matrix(small["rel_bias"][:, g * HEADS_PER_GROUP:(g + 1) * HEADS_PER_GROUP], dil)
        out3, lse3 = _attn_fwd(rest3, bias, g, dil, f"attn_fwd{g}")
        biases.append(bias), onehots.append(onehot)
        outs.append(out3.reshape(t, GROUP_COLS)), lses.append(lse3.reshape(t, GROUP_COLS))
    oa = _combine_fwd(outs, lses, rest)
    xa = _conv_fwd(rest3, conv_w, conv_b)
    dtr_t = jnp.swapaxes(rest3[:, :, R_DT:R_DT + BLK], 1, 2)
    y, s_in = _ssd_fwd(xa, rest3, dtr_t, prow, pcol)
    y2 = y.reshape(t, D_INNER)
    ys = _gatenorm_fwd(y2, rest, norm_w)
    y_a = _matmul(oa, wa16, mode="nn", out_dtype=F32, name="branch_a", tm=512, tn=D_MODEL, tk=GROUP_COLS)
    y_b = _matmul(ys, wbb16, mode="nn", out_dtype=F32, name="branch_b", tm=512, tn=D_MODEL, tk=D_INNER)
    merged = _merge_fwd(y_a, y_b, rest, b_gate8)
    mix = _matmul(merged, wo16, mode="nn", out_dtype=F32, name="out_proj", tm=512, tn=D_MODEL, tk=D_MODEL)
    pw = _matmul(p16, wple16, mode="nn", out_dtype=F32, name="ple_proj", tm=512, tn=D_MODEL, tk=PLE_DIM)
    d_pre, d_pre16, d_gple, d_pw, stats = _final(x2, mix, pw, rest, b_gate8, ln_gb, tgt2)

    d_merged = _matmul(d_pre16, wo16, mode="nt", out_dtype=F32, name="d_merged", tm=512, tn=D_MODEL, tk=D_MODEL)
    g_w_out = _matmul(merged, d_pre16, mode="tn", out_dtype=BF16, name="g_w_out", tm=512, tn=D_MODEL, tk=1024)
    g_w_ple = _matmul(p16, d_pw, mode="tn", out_dtype=BF16, name="g_w_ple", tm=PLE_DIM, tn=D_MODEL, tk=1024)
    d_ya, d_yb, d_ga, d_gb, dbg01 = _merge_bwd(y_a, y_b, rest, b_gate8, d_merged)
    d_oa = _matmul(d_ya, wa16, mode="nt", out_dtype=F32, name="d_oa", tm=512, tn=GROUP_COLS, tk=D_MODEL)
    d_ys = _matmul(d_yb, wbb16, mode="nt", out_dtype=F32, name="d_ys", tm=512, tn=1024, tk=D_MODEL)
    g_wa = _matmul(oa, d_ya, mode="tn", out_dtype=BF16, name="g_w_branch_a", tm=GROUP_COLS, tn=D_MODEL, tk=1024)
    g_wb = _matmul(ys, d_yb, mode="tn", out_dtype=BF16, name="g_w_branch_b", tm=512, tn=D_MODEL, tk=1024)
    d_outs_dd_ga = _combine_bwd(outs, lses, rest, d_oa)
    d_outs, dds, d_gatt = d_outs_dd_ga[:3], d_outs_dd_ga[3:6], d_outs_dd_ga[6]
    d_y, d_z, d_nw = _gatenorm_bwd(y2, rest, norm_w, d_ys)
    d_xa, ddt, ddt_t, gprow, gpcol = _ssd_bwd(xa, rest3, dtr_t, prow, pcol, s_in, d_y.reshape(bsz, seq, D_INNER))
    d_xbc, d_conv = _conv_bwd(rest3, conv_w, conv_b, d_xa)
    d_dt = (ddt + jnp.swapaxes(ddt_t, 1, 2)).reshape(t, BLK).astype(BF16)
    pieces = [d_ga, d_gb, d_z, d_xbc.reshape(t, CONV_DIM), d_gple, d_gatt, d_dt, jnp.zeros((t, QKV_OFF - R_DT - BLK), BF16)]
    g_tables = []
    shape3 = (bsz, seq, GROUP_COLS)
    for g, (_, dil) in enumerate(DILATED_PATTERNS):
        d_qkv, dbias = _attn_bwd(rest3, biases[g], lses[g].reshape(shape3), d_outs[g].reshape(shape3), dds[g].reshape(shape3),
                                 g, dil, f"attn_bwd{g}")
        pieces.append(d_qkv.reshape(t, QKV_G))
        g_tables.append(_bias_grad(dbias, onehots[g], f"bias_grad{g}"))
    d_hcat = jnp.concatenate(pieces, axis=1)
    grad_x = _matmul(d_hcat, wp16, mode="nt", out_dtype=F32, name="grad_x", tm=1024, tn=D_MODEL, tk=768,
                     add=d_pre, add_scale=ALPHA)
    g_wp = _matmul(x16, d_hcat, mode="tn", out_dtype=BF16, name="g_w_in", tm=D_MODEL, tn=2304, tk=1024, n_outer=True)

    grads = dict(
        w_in=_unpermute_w_in(g_wp),
        b_gate=jnp.stack([dbg01[0], dbg01[1], stats[ST_BG2]]),
        conv_w=d_conv[:CONV_WIDTH],
        w_branch=jnp.concatenate([g_wa, g_wb], axis=0),
        w_out=g_w_out,
        w_ple=g_w_ple,
    )
    small_grads = dict(
        conv_b=d_conv[CONV_WIDTH],
        dt_bias=gprow[P_DTB, :SSM_HEADS] + gpcol[:SSM_HEADS, P_DTB],
        a_log=gprow[P_ALOG, :SSM_HEADS] + gpcol[:SSM_HEADS, P_ALOG],
        d_skip=gprow[P_DSKIP, :SSM_HEADS],
        ssm_norm_w=d_nw[0],
        ln_g=stats[ST_LNG],
        ln_b=stats[ST_LNB],
        rel_bias=jnp.concatenate(g_tables, axis=1),
        loss=stats[ST_LOSS],
    )
    return grad_x.reshape(bsz, seq, D_MODEL), grads, small_grads


WEIGHT_ORDER = ("w_in", "b_gate", "conv_w", "conv_b", "dt_bias", "a_log", "d_skip", "ssm_norm_w", "w_branch", "w_out",
                "w_ple", "ln_g", "ln_b", "rel_bias")
SMALL_NAMES = ("conv_b", "dt_bias", "a_log", "d_skip", "ssm_norm_w", "ln_g", "ln_b", "rel_bias")


def kernel(x, p, w_in, b_gate, conv_w, conv_b, dt_bias, a_log, d_skip, ssm_norm_w, w_branch, w_out, w_ple, ln_g, ln_b, rel_bias, loss_target, m_w_in, m_b_gate, m_conv_w, m_conv_b, m_dt_bias, m_a_log, m_d_skip, m_ssm_norm_w, m_w_branch, m_w_out, m_w_ple, m_ln_g, m_ln_b, m_rel_bias, v_w_in, v_b_gate, v_conv_w, v_conv_b, v_dt_bias, v_a_log, v_d_skip, v_ssm_norm_w, v_w_branch, v_w_out, v_w_ple, v_ln_g, v_ln_b, v_rel_bias):
    given = dict(w_in=w_in, b_gate=b_gate, conv_w=conv_w, conv_b=conv_b, dt_bias=dt_bias, a_log=a_log, d_skip=d_skip,
                 ssm_norm_w=ssm_norm_w, w_branch=w_branch, w_out=w_out, w_ple=w_ple, ln_g=ln_g, ln_b=ln_b)
    moments_m = dict(w_in=m_w_in, b_gate=m_b_gate, conv_w=m_conv_w, conv_b=m_conv_b, dt_bias=m_dt_bias, a_log=m_a_log,
                     d_skip=m_d_skip, ssm_norm_w=m_ssm_norm_w, w_branch=m_w_branch, w_out=m_w_out, w_ple=m_w_ple,
                     ln_g=m_ln_g, ln_b=m_ln_b)
    moments_v = dict(w_in=v_w_in, b_gate=v_b_gate, conv_w=v_conv_w, conv_b=v_conv_b, dt_bias=v_dt_bias, a_log=v_a_log,
                     d_skip=v_d_skip, ssm_norm_w=v_ssm_norm_w, w_branch=v_w_branch, w_out=v_w_out, w_ple=v_w_ple,
                     ln_g=v_ln_g, ln_b=v_ln_b)
    w = {k: a[0] for k, a in given.items()} | {"rel_bias": rel_bias}
    mm = {k: a[0] for k, a in moments_m.items()} | {"rel_bias": m_rel_bias}
    vv = {k: a[0] for k, a in moments_v.items()} | {"rel_bias": v_rel_bias}

    gathered = _gather_two_level(
        [w["w_in"].astype(BF16), w["w_branch"].astype(BF16), w["w_out"].astype(BF16), w["w_ple"].astype(BF16),
         _pack_cols(w["b_gate"], w["conv_w"])], "gather_weights")
    wp16 = _permute_w_in(gathered[0].transpose(1, 0, 2).reshape(D_MODEL, IN_COLS))
    wb16 = gathered[1].reshape(BRANCH_ROWS, D_MODEL)
    wo16 = gathered[2].reshape(D_MODEL, D_MODEL)
    wple16 = gathered[3].transpose(1, 0, 2).reshape(PLE_DIM, D_MODEL)
    b_gate_full, conv_w_full = _unpack_cols_all(gathered[4])
    small = {k: w[k] for k in SMALL_NAMES}

    grad_x, grads, small_grads = _local_step(x, p[0], loss_target, wp16, wb16, wo16, wple16, b_gate_full, conv_w_full, small)

    big = [grads["w_in"].astype(BF16).reshape(D_MODEL, N_DEV, IN_SHARD).transpose(1, 0, 2),
           grads["w_branch"].astype(BF16).reshape(N_DEV, BRANCH_ROWS // N_DEV, D_MODEL),
           grads["w_out"].astype(BF16).reshape(N_DEV, D_MODEL // N_DEV, D_MODEL),
           grads["w_ple"].astype(BF16).reshape(PLE_DIM, N_DEV, BLK).transpose(1, 0, 2)]
    sums = [_pair_reduce(a, f"pair_reduce{i}", rows) for i, (a, rows) in enumerate(zip(big, (128, 176, 128, 256)))]
    parts = _chip_exchange(sums, "chip_exchange")
    small_parts = _exchange([_pack_cols_all(grads["b_gate"], grads["conv_w"]), _pack_small(small_grads)], [True, False],
                            "exchange_small")

    out = {}
    out["w_in"] = _adam_reduce(parts[0], w["w_in"], mm["w_in"], vv["w_in"], "adam_w_in", 128)
    out["w_branch"] = _adam_reduce(parts[1], w["w_branch"], mm["w_branch"], vv["w_branch"], "adam_w_branch", 176)
    out["w_out"] = _adam_reduce(parts[2], w["w_out"], mm["w_out"], vv["w_out"], "adam_w_out", 128)
    out["w_ple"] = _adam_reduce(parts[3], w["w_ple"], mm["w_ple"], vv["w_ple"], "adam_w_ple", 256)
    cols = _adam_reduce(small_parts[0], _pack_cols(w["b_gate"], w["conv_w"]), _pack_cols(mm["b_gate"], mm["conv_w"]),
                        _pack_cols(vv["b_gate"], vv["conv_w"]), "adam_cols", 8)
    out["b_gate"] = [a[:3, :BLK] for a in cols]
    out["conv_w"] = [a[:CONV_WIDTH, BLK:] for a in cols]
    packed = _adam_reduce(small_parts[1], _pack_small({k: w[k] for k in SMALL_NAMES}), _pack_small({k: mm[k] for k in SMALL_NAMES}),
                          _pack_small({k: vv[k] for k in SMALL_NAMES}), "adam_small", SMALL_ROWS)
    unpacked = [_unpack_small(a) for a in packed]
    for k in SMALL_NAMES:
        out[k] = [u[k] for u in unpacked]
    loss_rows = unpacked[0]["loss"].reshape(D_MODEL // BLK, BLK)
    loss = _lane_total(loss_rows).reshape(())

    def shaped(k, a):
        return a if k == "rel_bias" else a[None]

    results = [loss, grad_x]
    for i in range(4):
        results += [shaped(k, out[k][i]) for k in WEIGHT_ORDER]
    return tuple(results)
```

```python
import functools
import math

import jax
import jax.numpy as jnp
from jax import lax
from jax.experimental import pallas as pl
from jax.experimental.pallas import tpu as pltpu

F32 = jnp.float32
BF16 = jnp.bfloat16

N_DEV = 8
D_MODEL = 1024
SEQ = 2048
HEAD_DIM = 64
HEADS_PER_GROUP = 12
DILATED_PATTERNS = ((128, 1), (512, 4), (2048, 16))
N_GROUPS = 3
ATT_HEADS = N_GROUPS * HEADS_PER_GROUP
GROUP_COLS = HEADS_PER_GROUP * HEAD_DIM
ATT_QKV = ATT_HEADS * HEAD_DIM
BLK = 128
NUM_BUCKETS = 32
MAX_DISTANCE = 2048
D_INNER = 2048
SSM_HEADS = 32
SSM_GROUPS = 4
HEADS_PER_SSM_GROUP = SSM_HEADS // SSM_GROUPS
D_STATE = 128
CONV_WIDTH = 4
CONV_DIM = D_INNER + 2 * SSM_GROUPS * D_STATE
PLE_DIM = 256
ALPHA = 2.0 ** 0.25
LN_EPS = 1e-5
RMS_EPS = 1e-5
IN_COLS = 15904
IN_SHARD = IN_COLS // N_DEV
BRANCH_ROWS = GROUP_COLS + D_INNER

Q_END = ATT_QKV
K_END = 2 * ATT_QKV
V_END = 3 * ATT_QKV
GATT_END = V_END + GROUP_COLS
Z_END = GATT_END + D_INNER
XBC_END = Z_END + CONV_DIM
DT_END = XBC_END + SSM_HEADS
GMERGE_END = DT_END + 2 * D_MODEL

R_GM, R_Z, R_XBC, R_GPLE, R_GATT, R_DT = 0, 2048, 4096, 7168, 8192, 8960
QKV_OFF = 9216
HEAD_PAIRS = GROUP_COLS // BLK
QKV_G = 3 * GROUP_COLS
HCAT_COLS = QKV_OFF + N_GROUPS * QKV_G

ADAM_LR, ADAM_B1, ADAM_B2, ADAM_EPS, ADAM_WD, ADAM_STEP = 0.001, 0.9, 0.999, 1e-08, 0.01, 10

VMEM_LIMIT_BYTES = 56 * 1024 * 1024


def _params(*semantics):
    return pltpu.CompilerParams(dimension_semantics=semantics, vmem_limit_bytes=VMEM_LIMIT_BYTES)


def _sigmoid(v):
    return 1.0 / (1.0 + jnp.exp(-v))


def _silu_and_grad(v):
    s = _sigmoid(v)
    return v * s, s * (1.0 + v * (1.0 - s))


def _matmul(a, b, *, mode, out_dtype, name, tm, tn, tk, n_off=0, n=None, add=None, add_scale=1.0, n_outer=False):
    if mode == "nn":
        (m, k), n_full = a.shape, b.shape[1]
        assert b.shape[0] == k
    elif mode == "nt":
        (m, k), n_full = a.shape, b.shape[0]
        assert b.shape[1] == k
    else:
        (k, m), n_full = a.shape, b.shape[1]
        assert b.shape[0] == k
    n = n_full if n is None else n
    assert m % tm == 0 and n % tn == 0 and k % tk == 0 and n_off % tn == 0, (name, m, n, k)
    nk = k // tk
    jo = n_off // tn
    dims = {"nn": (((1,), (0,)), ((), ())), "nt": (((1,), (1,)), ((), ())), "tn": (((0,), (0,)), ((), ()))}[mode]

    def body(*refs):
        if add is None:
            a_ref, b_ref, o_ref = refs[:3]
            add_ref = None
        else:
            a_ref, b_ref, add_ref, o_ref = refs[:4]
        acc_ref = refs[-1] if nk > 1 else None
        prod = lax.dot_general(a_ref[...].astype(BF16), b_ref[...].astype(BF16), dims, preferred_element_type=F32)

        def finish(total):
            if add_ref is not None:
                total = total + add_scale * add_ref[...]
            o_ref[...] = total.astype(out_dtype)

        if nk == 1:
            finish(prod)
        else:
            kk = pl.program_id(2)

            @pl.when(kk == 0)
            def _():
                acc_ref[...] = prod

            @pl.when(jnp.logical_and(kk > 0, kk < nk - 1))
            def _():
                acc_ref[...] += prod

            @pl.when(kk == nk - 1)
            def _():
                finish(acc_ref[...] + prod)

    def ij(f):
        return (lambda g0, g1, kk: f(g1, g0, kk)) if n_outer else f

    if mode == "nn":
        a_spec = pl.BlockSpec((tm, tk), ij(lambda i, j, kk: (i, kk)))
        b_spec = pl.BlockSpec((tk, tn), ij(lambda i, j, kk: (kk, j + jo)))
    elif mode == "nt":
        a_spec = pl.BlockSpec((tm, tk), ij(lambda i, j, kk: (i, kk)))
        b_spec = pl.BlockSpec((tn, tk), ij(lambda i, j, kk: (j, kk)))
    else:
        a_spec = pl.BlockSpec((tk, tm), ij(lambda i, j, kk: (kk, i)))
        b_spec = pl.BlockSpec((tk, tn), ij(lambda i, j, kk: (kk, j)))
    in_specs = [a_spec, b_spec]
    args = [a, b]
    if add is not None:
        in_specs.append(pl.BlockSpec((tm, tn), ij(lambda i, j, kk: (i, j))))
        args.append(add)
    return pl.pallas_call(
        body, name=name,
        grid=(n // tn, m // tm, nk) if n_outer else (m // tm, n // tn, nk),
        in_specs=in_specs,
        out_specs=pl.BlockSpec((tm, tn), ij(lambda i, j, kk: (i, j))),
        out_shape=jax.ShapeDtypeStruct((m, n), out_dtype),
        scratch_shapes=[pltpu.VMEM((tm, tn), F32)] if nk > 1 else [],
        compiler_params=_params("parallel", "parallel", "arbitrary"),
    )(*args)


UNITS_PER_ITER = 4


def _band_mask(first):
    qi = lax.broadcasted_iota(jnp.int32, (BLK, BLK if first else 2 * BLK), 0)
    kj = lax.broadcasted_iota(jnp.int32, (BLK, BLK if first else 2 * BLK), 1)
    delta = qi - kj if first else qi + BLK - kj
    return jnp.logical_and(delta >= 0, delta <= BLK)


def _attn_specs(seq, g):
    qkv = [pl.BlockSpec((1, seq, BLK), functools.partial(
        lambda hp, b, part: (b, 0, QKV_OFF // BLK + (g * HEAD_PAIRS + hp) * 3 + part), part=part)) for part in range(3)]
    one = pl.BlockSpec((1, seq, BLK), lambda hp, b: (b, 0, hp))
    bias = pl.BlockSpec((2, BLK, 2 * BLK), lambda hp, b: (hp, 0, 0))
    return qkv, one, bias


def _attn_rows(dil, r, n, first):
    start = r + (dil * BLK) * n
    if dil == 1:
        start = pl.multiple_of(start, BLK)
        rows = pl.ds(start, BLK)
        keys = rows if first else pl.ds(pl.multiple_of(start - BLK, BLK), 2 * BLK)
    else:
        rows = pl.ds(start, BLK, stride=dil)
        keys = rows if first else pl.ds(start - dil * BLK, 2 * BLK, stride=dil)
    return rows, keys


def _attn_schedule(dil, nb, unit):
    def blocks_of(r):
        unit(r, 0, True)
        for n in range(1, UNITS_PER_ITER):
            unit(r, n, False)
        if nb > UNITS_PER_ITER:
            def more(i, carry):
                for jj in range(UNITS_PER_ITER):
                    unit(r, i * UNITS_PER_ITER + jj, False)
                return carry
            lax.fori_loop(1, nb // UNITS_PER_ITER, more, 0)

    if nb >= UNITS_PER_ITER:
        assert nb % UNITS_PER_ITER == 0
        if dil == 1:
            blocks_of(0)
        else:
            def per_residue(r, carry):
                blocks_of(r)
                return carry
            lax.fori_loop(0, dil, per_residue, 0)
    else:
        per_iter = UNITS_PER_ITER // nb
        assert UNITS_PER_ITER % nb == 0 and dil % per_iter == 0

        def residues(i, carry):
            for jj in range(per_iter):
                for n in range(nb):
                    unit(i * per_iter + jj, n, n == 0)
            return carry
        lax.fori_loop(0, dil // per_iter, residues, 0)


def _attn_fwd(hcat3, bias, g, dil, name):
    bsz, seq, _ = hcat3.shape
    nb = seq // dil // BLK
    scale = HEAD_DIM ** -0.5

    def body(q_ref, k_ref, v_ref, bias_ref, o_ref, lse_ref):
        masks = {True: _band_mask(True), False: _band_mask(False)}

        def unit(r, n, first):
            rows, keys = _attn_rows(dil, r, n, first)
            q2 = q_ref[0, rows, :].astype(BF16)
            k2 = k_ref[0, keys, :].astype(BF16)
            v2 = v_ref[0, keys, :].astype(BF16)
            outs, lses = [], []
            for j in range(2):
                lanes = slice(j * HEAD_DIM, (j + 1) * HEAD_DIM)
                bias_j = bias_ref[j, :, BLK:] if first else bias_ref[j]
                s = _nt(q2[:, lanes], k2[:, lanes]) * scale + bias_j
                s = jnp.where(masks[first], s, -jnp.inf)
                mx = jnp.max(s, axis=-1, keepdims=True)
                e = jnp.exp(s - mx)
                den = jnp.sum(e, axis=-1, keepdims=True)
                outs.append(jnp.dot(e.astype(BF16), v2[:, lanes], preferred_element_type=F32) / den)
                lses.append(jnp.broadcast_to(mx + jnp.log(den), (BLK, HEAD_DIM)))
            o_ref[0, rows, :] = jnp.concatenate(outs, axis=1)
            lse_ref[0, rows, :] = jnp.concatenate(lses, axis=1)

        _attn_schedule(dil, nb, unit)

    qkv_specs, one, bias_spec = _attn_specs(seq, g)
    shape = jax.ShapeDtypeStruct((bsz, seq, GROUP_COLS), F32)
    return pl.pallas_call(
        body, name=name,
        grid=(HEAD_PAIRS, bsz),
        in_specs=qkv_specs + [bias_spec],
        out_specs=[one, one],
        out_shape=[shape, shape],
        compiler_params=_params("parallel", "parallel"),
    )(hcat3, hcat3, hcat3, bias)


def _attn_bwd(hcat3, bias, lse, d_out, dd, g, dil, name):
    bsz, seq, _ = hcat3.shape
    nb = seq // dil // BLK
    scale = HEAD_DIM ** -0.5

    def body(q_ref, k_ref, v_ref, bias_ref, lse_ref, do_ref, dd_ref, dqkv_ref, dbias_ref, dq_acc, dk_acc, dv_acc):
        @pl.when(pl.program_id(1) == 0)
        def _():
            dbias_ref[...] = jnp.zeros_like(dbias_ref)

        dk_acc[...] = jnp.zeros_like(dk_acc)
        dv_acc[...] = jnp.zeros_like(dv_acc)
        masks = {True: _band_mask(True), False: _band_mask(False)}

        def unit(r, n, first):
            rows, keys = _attn_rows(dil, r, n, first)
            q2 = q_ref[0, rows, :].astype(BF16)
            k2 = k_ref[0, keys, :].astype(BF16)
            v2 = v_ref[0, keys, :].astype(BF16)
            do2 = do_ref[0, rows, :].astype(BF16)
            lse2 = lse_ref[0, rows, :]
            dd2 = dd_ref[0, rows, :]
            dqs, dks, dvs = [], [], []
            for j in range(2):
                lanes = slice(j * HEAD_DIM, (j + 1) * HEAD_DIM)
                q, kb, vb, do = q2[:, lanes], k2[:, lanes], v2[:, lanes], do2[:, lanes]
                delta = jnp.sum(dd2[:, lanes], axis=-1, keepdims=True)
                bias_j = bias_ref[j, :, BLK:] if first else bias_ref[j]
                s = _nt(q, kb) * scale + bias_j
                p = jnp.where(masks[first], jnp.exp(s - lse2[:, j * HEAD_DIM:j * HEAD_DIM + 1]), 0.0)
                ds = p * (_nt(do, vb) - delta)
                ds16 = ds.astype(BF16)
                dqs.append(jnp.dot(ds16, kb, preferred_element_type=F32) * scale)
                dks.append(_tn(ds16, q) * scale)
                dvs.append(_tn(p.astype(BF16), do))
                if first:
                    dbias_ref[j, :, BLK:] += ds
                else:
                    dbias_ref[j] += ds
            dq_acc[rows, :] = jnp.concatenate(dqs, axis=1)
            dk_acc[keys, :] += jnp.concatenate(dks, axis=1)
            dv_acc[keys, :] += jnp.concatenate(dvs, axis=1)

        _attn_schedule(dil, nb, unit)
        dqkv_ref[0, :, 0:BLK] = dq_acc[...].astype(BF16)
        dqkv_ref[0, :, BLK:2 * BLK] = dk_acc[...].astype(BF16)
        dqkv_ref[0, :, 2 * BLK:3 * BLK] = dv_acc[...].astype(BF16)

    qkv_specs, one, bias_spec = _attn_specs(seq, g)
    return pl.pallas_call(
        body, name=name,
        grid=(HEAD_PAIRS, bsz),
        in_specs=qkv_specs + [bias_spec, one, one, one],
        out_specs=[pl.BlockSpec((1, seq, 3 * BLK), lambda hp, b: (b, 0, hp)), bias_spec],
        out_shape=[jax.ShapeDtypeStruct((bsz, seq, QKV_G), BF16), jax.ShapeDtypeStruct((HEADS_PER_GROUP, BLK, 2 * BLK), F32)],
        scratch_shapes=[pltpu.VMEM((seq, BLK), F32)] * 3,
        compiler_params=_params("arbitrary", "arbitrary"),
    )(hcat3, hcat3, hcat3, bias, lse, d_out, dd)


def _t5_buckets(dil):
    import numpy as np
    qi = np.arange(BLK)[:, None]
    kj = np.arange(2 * BLK)[None, :]
    dist = np.maximum(qi + BLK - kj, 0) * dil
    max_exact = NUM_BUCKETS // 2
    d_f = np.maximum(dist, 1).astype(np.float32)
    large = max_exact + (np.log(d_f / np.float32(max_exact)) / np.float32(math.log(MAX_DISTANCE / max_exact))
                         * np.float32(NUM_BUCKETS - max_exact)).astype(np.int32)
    large = np.minimum(large, NUM_BUCKETS - 1)
    return np.where(dist < max_exact, dist, large).astype(np.int32).reshape(-1)


def _bias_matrix(table_g, dil):
    buckets = jnp.asarray(_t5_buckets(dil))
    onehot = (buckets[None, :] == lax.broadcasted_iota(jnp.int32, (NUM_BUCKETS, 1), 0)).astype(F32)
    tk = 4096

    def body(t_ref, oh_ref, o_ref):
        o_ref[...] = _dot_hi(t_ref[...], oh_ref[...])

    bias = pl.pallas_call(
        body, name=f"bias_matrix{dil}",
        grid=(onehot.shape[1] // tk,),
        in_specs=[pl.BlockSpec((HEADS_PER_GROUP, NUM_BUCKETS), lambda kk: (0, 0)), pl.BlockSpec((NUM_BUCKETS, tk), lambda kk: (0, kk))],
        out_specs=pl.BlockSpec((HEADS_PER_GROUP, tk), lambda kk: (0, kk)),
        out_shape=jax.ShapeDtypeStruct((HEADS_PER_GROUP, onehot.shape[1]), F32),
        compiler_params=_params("parallel"),
    )(table_g.T, onehot)
    return bias.reshape(HEADS_PER_GROUP, BLK, 2 * BLK), onehot


def _bias_grad(dbias, onehot, name):
    flat = dbias.reshape(HEADS_PER_GROUP, 2 * BLK * BLK)
    tk = 4096

    def body(oh_ref, g_ref, o_ref):
        @pl.when(pl.program_id(0) == 0)
        def _():
            o_ref[...] = jnp.zeros_like(o_ref)
        o_ref[...] += lax.dot_general(oh_ref[...], g_ref[...], (((1,), (1,)), ((), ())),
                                      preferred_element_type=F32, precision=lax.Precision.HIGHEST)

    return pl.pallas_call(
        body, name=name,
        grid=(flat.shape[1] // tk,),
        in_specs=[pl.BlockSpec((NUM_BUCKETS, tk), lambda kk: (0, kk)), pl.BlockSpec((HEADS_PER_GROUP, tk), lambda kk: (0, kk))],
        out_specs=pl.BlockSpec((NUM_BUCKETS, HEADS_PER_GROUP), lambda kk: (0, 0)),
        out_shape=jax.ShapeDtypeStruct((NUM_BUCKETS, HEADS_PER_GROUP), F32),
        compiler_params=_params("arbitrary"),
    )(onehot, flat)


def _shift_rows(u, s, down):
    if s == 0:
        return u
    rows = lax.broadcasted_iota(jnp.int32, u.shape, 0)
    n = u.shape[0]
    if down:
        return jnp.where(rows >= s, pltpu.roll(u, s, 0), 0.0)
    return jnp.where(rows < n - s, pltpu.roll(u, n - s, 0), 0.0)


def _conv_pre(u, w_ref, b_ref):
    acc = b_ref[0:1, :] + w_ref[CONV_WIDTH - 1:CONV_WIDTH, :] * u
    for s in range(1, CONV_WIDTH):
        acc = acc + w_ref[CONV_WIDTH - 1 - s:CONV_WIDTH - s, :] * _shift_rows(u, s, True)
    return acc


def _conv_fwd(rest3, conv_w, conv_b):
    bsz, seq, _ = rest3.shape

    def body(u_ref, w_ref, b_ref, o_ref):
        pre = _conv_pre(u_ref[0], w_ref, b_ref)
        o_ref[0] = pre * _sigmoid(pre)

    return pl.pallas_call(
        body, name="conv_fwd",
        grid=(bsz, CONV_DIM // BLK),
        in_specs=[pl.BlockSpec((1, seq, BLK), lambda b, c: (b, 0, R_XBC // BLK + c)),
                  pl.BlockSpec((CONV_WIDTH, BLK), lambda b, c: (0, c)),
                  pl.BlockSpec((1, BLK), lambda b, c: (0, c))],
        out_specs=pl.BlockSpec((1, seq, BLK), lambda b, c: (b, 0, c)),
        out_shape=jax.ShapeDtypeStruct((bsz, seq, CONV_DIM), F32),
        compiler_params=_params("parallel", "parallel"),
    )(rest3, conv_w, conv_b)


def _conv_bwd(rest3, conv_w, conv_b, d_act):
    bsz, seq, _ = rest3.shape

    def body(u_ref, w_ref, b_ref, g_ref, du_ref, dw_ref):
        @pl.when(pl.program_id(1) == 0)
        def _():
            dw_ref[...] = jnp.zeros_like(dw_ref)

        u = u_ref[0]
        pre = _conv_pre(u, w_ref, b_ref)
        d_pre = g_ref[0] * _silu_and_grad(pre)[1]
        du = w_ref[CONV_WIDTH - 1:CONV_WIDTH, :] * d_pre
        dw_ref[CONV_WIDTH - 1:CONV_WIDTH, :] += jnp.sum(d_pre * u, axis=0, keepdims=True)
        for s in range(1, CONV_WIDTH):
            du = du + w_ref[CONV_WIDTH - 1 - s:CONV_WIDTH - s, :] * _shift_rows(d_pre, s, False)
            dw_ref[CONV_WIDTH - 1 - s:CONV_WIDTH - s, :] += jnp.sum(d_pre * _shift_rows(u, s, True), axis=0, keepdims=True)
        dw_ref[CONV_WIDTH:CONV_WIDTH + 1, :] += jnp.sum(d_pre, axis=0, keepdims=True)
        du_ref[0] = du.astype(BF16)

    return pl.pallas_call(
        body, name="conv_bwd",
        grid=(CONV_DIM // BLK, bsz),
        in_specs=[pl.BlockSpec((1, seq, BLK), lambda c, b: (b, 0, R_XBC // BLK + c)),
                  pl.BlockSpec((CONV_WIDTH, BLK), lambda c, b: (0, c)),
                  pl.BlockSpec((1, BLK), lambda c, b: (0, c)),
                  pl.BlockSpec((1, seq, BLK), lambda c, b: (b, 0, c))],
        out_specs=[pl.BlockSpec((1, seq, BLK), lambda c, b: (b, 0, c)),
                   pl.BlockSpec((8, BLK), lambda c, b: (0, c))],
        out_shape=[jax.ShapeDtypeStruct((bsz, seq, CONV_DIM), BF16), jax.ShapeDtypeStruct((8, CONV_DIM), F32)],
        compiler_params=_params("parallel", "arbitrary"),
    )(rest3, conv_w, conv_b, d_act)


P_DTB, P_ALOG, P_DSKIP = 0, 1, 2


def _softplus(v):
    return jnp.maximum(v, 0.0) + jnp.log(1.0 + jnp.exp(-jnp.abs(v)))


def _dot_hi(a, b):
    return jnp.dot(a, b, preferred_element_type=F32, precision=lax.Precision.HIGHEST)


def _nt(a, b):
    return lax.dot_general(a, b, (((1,), (1,)), ((), ())), preferred_element_type=F32)


def _tn(a, b):
    return lax.dot_general(a, b, (((0,), (0,)), ((), ())), preferred_element_type=F32)


def _sum_all(v):
    return jnp.sum(jnp.sum(v, axis=0, keepdims=True), axis=1, keepdims=True)


def _ssd_decays(dtr, dtr_t, prow_ref, pcol_ref):
    ri = lax.broadcasted_iota(jnp.int32, (BLK, BLK), 0)
    ci = lax.broadcasted_iota(jnp.int32, (BLK, BLK), 1)
    tri = (ri >= ci).astype(F32)
    tri_u = (ri <= ci).astype(F32)
    pre = dtr + prow_ref[P_DTB:P_DTB + 1, :]
    dt = _softplus(pre)
    ah_row = -jnp.exp(prow_ref[P_ALOG:P_ALOG + 1, :])
    acs = _dot_hi(tri, dt * ah_row)
    pre_t = dtr_t + pcol_ref[:, P_DTB:P_DTB + 1]
    dt_t = _softplus(pre_t)
    ah_col = -jnp.exp(pcol_ref[:, P_ALOG:P_ALOG + 1])
    acs_t = _dot_hi(dt_t * ah_col, tri_u)
    return dict(tri=tri, tri_u=tri_u, pre=pre, dt=dt, ah_row=ah_row, acs=acs, pre_t=pre_t, dt_t=dt_t, ah_col=ah_col,
                acs_t=acs_t, causal=ri >= ci, last_row=ri[:, 0:1] == BLK - 1)


def _ssd_head(h, d, x_ref, s_in, g_mat):
    col = d["acs"][:, h:h + 1]
    row = d["acs_t"][h:h + 1, :]
    lm = jnp.exp(jnp.where(d["causal"], col - row, -jnp.inf))
    m = g_mat * lm
    xh = x_ref[0, :, pl.ds(h * HEAD_DIM, HEAD_DIM)]
    dtc = d["dt"][:, h:h + 1]
    xd = xh * dtc
    e = jnp.exp(col)
    clast = d["acs"][BLK - 1:BLK, h:h + 1]
    f = jnp.exp(clast - col)
    return dict(col=col, lm=lm, m=m, xh=xh, dtc=dtc, xd=xd, e=e, ecl=jnp.exp(clast), f=f, xf=xd * f)


def _ssd_specs(nc, rev):
    cidx = (lambda c: nc - 1 - c) if rev else (lambda c: c)
    x_spec = pl.BlockSpec((1, BLK, D_INNER), lambda b, c: (b, cidx(c), 0))
    bm_spec = pl.BlockSpec((1, BLK, SSM_GROUPS * D_STATE), lambda b, c: (b, cidx(c), D_INNER // (SSM_GROUPS * D_STATE)))
    cm_spec = pl.BlockSpec((1, BLK, SSM_GROUPS * D_STATE), lambda b, c: (b, cidx(c), D_INNER // (SSM_GROUPS * D_STATE) + 1))
    dt_spec = pl.BlockSpec((1, BLK, BLK), lambda b, c: (b, cidx(c), R_DT // BLK))
    dtt_spec = pl.BlockSpec((1, BLK, BLK), lambda b, c: (b, 0, cidx(c)))
    prow_spec = pl.BlockSpec((8, BLK), lambda b, c: (0, 0))
    pcol_spec = pl.BlockSpec((BLK, 8), lambda b, c: (0, 0))
    st_spec = pl.BlockSpec((1, 1, SSM_HEADS, HEAD_DIM, D_STATE), lambda b, c: (b, cidx(c), 0, 0, 0))
    y_spec = pl.BlockSpec((1, BLK, D_INNER), lambda b, c: (b, cidx(c), 0))
    return x_spec, bm_spec, cm_spec, dt_spec, dtt_spec, prow_spec, pcol_spec, st_spec, y_spec


def _ssd_fwd(xa, rest3, dtr_t, prow, pcol):
    bsz, seq, _ = xa.shape
    nc = seq // BLK

    def body(x_ref, bm_ref, cm_ref, dt_ref, dtt_ref, prow_ref, pcol_ref, y_ref, sin_ref, s_ref):
        @pl.when(pl.program_id(1) == 0)
        def _():
            s_ref[...] = jnp.zeros_like(s_ref)

        d = _ssd_decays(dt_ref[0], dtt_ref[0], prow_ref, pcol_ref)
        for g in range(SSM_GROUPS):
            lanes = pl.ds(g * D_STATE, D_STATE)
            bg = bm_ref[0, :, lanes].astype(BF16)
            cg = cm_ref[0, :, lanes].astype(BF16)
            g_mat = _nt(cg, bg)
            for hh in range(HEADS_PER_SSM_GROUP):
                h = g * HEADS_PER_SSM_GROUP + hh
                s_in = s_ref[h]
                sin_ref[0, 0, h] = s_in
                q = _ssd_head(h, d, x_ref, s_in, g_mat)
                y_diag = jnp.dot(q["m"].astype(BF16), q["xd"].astype(BF16), preferred_element_type=F32)
                y_off = _nt(cg, s_in.astype(BF16)) * q["e"]
                s_ref[h] = s_in * q["ecl"] + _tn(q["xf"].astype(BF16), bg)
                y_ref[0, :, pl.ds(h * HEAD_DIM, HEAD_DIM)] = y_diag + y_off + prow_ref[P_DSKIP:P_DSKIP + 1, h:h + 1] * q["xh"]

    x_spec, bm_spec, cm_spec, dt_spec, dtt_spec, prow_spec, pcol_spec, st_spec, y_spec = _ssd_specs(nc, False)
    return pl.pallas_call(
        body, name="ssd_fwd",
        grid=(bsz, nc),
        in_specs=[x_spec, bm_spec, cm_spec, dt_spec, dtt_spec, prow_spec, pcol_spec],
        out_specs=[y_spec, st_spec],
        out_shape=[jax.ShapeDtypeStruct((bsz, seq, D_INNER), F32),
                   jax.ShapeDtypeStruct((bsz, nc, SSM_HEADS, HEAD_DIM, D_STATE), F32)],
        scratch_shapes=[pltpu.VMEM((SSM_HEADS, HEAD_DIM, D_STATE), F32)],
        compiler_params=_params("parallel", "arbitrary"),
    )(xa, xa, xa, rest3, dtr_t, prow, pcol)


def _ssd_bwd(xa, rest3, dtr_t, prow, pcol, s_in_all, dy):
    bsz, seq, _ = xa.shape
    nc = seq // BLK

    def body(x_ref, bm_ref, cm_ref, dt_ref, dtt_ref, prow_ref, pcol_ref, sin_ref, dy_ref,
             dxa_ref, ddt_ref, ddtt_ref, gprow_ref, gpcol_ref, ds_ref, dc_ref, ddtc_ref, drt_ref):
        first = jnp.logical_and(pl.program_id(0) == 0, pl.program_id(1) == 0)

        @pl.when(first)
        def _():
            gprow_ref[...] = jnp.zeros_like(gprow_ref)
            gpcol_ref[...] = jnp.zeros_like(gpcol_ref)

        @pl.when(pl.program_id(1) == 0)
        def _():
            ds_ref[...] = jnp.zeros_like(ds_ref)

        dc_ref[...] = jnp.zeros_like(dc_ref)
        ddtc_ref[...] = jnp.zeros_like(ddtc_ref)
        drt_ref[...] = jnp.zeros_like(drt_ref)
        d = _ssd_decays(dt_ref[0], dtt_ref[0], prow_ref, pcol_ref)
        for g in range(SSM_GROUPS):
            lanes = pl.ds(g * D_STATE, D_STATE)
            bg = bm_ref[0, :, lanes].astype(BF16)
            cg = cm_ref[0, :, lanes].astype(BF16)
            g_mat = _nt(cg, bg)
            d_g = jnp.zeros((BLK, BLK), F32)
            d_bg = jnp.zeros((BLK, D_STATE), F32)
            d_cg = jnp.zeros((BLK, D_STATE), F32)
            for hh in range(HEADS_PER_SSM_GROUP):
                h = g * HEADS_PER_SSM_GROUP + hh
                head_lanes = pl.ds(h * HEAD_DIM, HEAD_DIM)
                s_in = sin_ref[0, 0, h]
                s_in16 = s_in.astype(BF16)
                q = _ssd_head(h, d, x_ref, s_in, g_mat)
                m16, xd16 = q["m"].astype(BF16), q["xd"].astype(BF16)
                d_y = dy_ref[0, :, head_lanes]
                d_y16 = d_y.astype(BF16)
                d_so = ds_ref[h]
                d_so16 = d_so.astype(BF16)
                d_x = prow_ref[P_DSKIP:P_DSKIP + 1, h:h + 1] * d_y
                gprow_ref[P_DSKIP:P_DSKIP + 1, h:h + 1] += _sum_all(d_y * q["xh"])
                d_m = _nt(d_y16, xd16)
                d_xd = _tn(m16, d_y16)
                w = d_m * q["m"]
                d_g = d_g + d_m * q["lm"]
                d_col = jnp.sum(w, axis=1, keepdims=True)
                drt_ref[h:h + 1, :] = -jnp.sum(w, axis=0, keepdims=True)
                qmat = _nt(cg, s_in16)
                d_q16 = (d_y * q["e"]).astype(BF16)
                d_col = d_col + jnp.sum(d_y * qmat, axis=1, keepdims=True) * q["e"]
                d_cg = d_cg + jnp.dot(d_q16, s_in16, preferred_element_type=F32)
                d_sin = _tn(d_q16, cg) + d_so * q["ecl"]
                d_clast = _sum_all(d_so * s_in) * q["ecl"]
                d_xf = _nt(bg, d_so16)
                d_bg = d_bg + jnp.dot(q["xf"].astype(BF16), d_so16, preferred_element_type=F32)
                d_xd = d_xd + d_xf * q["f"]
                d_f = jnp.sum(d_xf * q["xd"], axis=1, keepdims=True) * q["f"]
                d_clast = d_clast + jnp.sum(d_f, axis=0, keepdims=True)
                d_col = d_col - d_f + jnp.where(d["last_row"], d_clast, 0.0)
                dxa_ref[0, :, head_lanes] = d_x + d_xd * q["dtc"]
                dc_ref[:, h:h + 1] = d_col
                ddtc_ref[:, h:h + 1] = jnp.sum(d_xd * q["xh"], axis=1, keepdims=True)
                ds_ref[h] = d_sin
            d_g16 = d_g.astype(BF16)
            dxa_ref[0, :, pl.ds(D_INNER + g * D_STATE, D_STATE)] = d_bg + _tn(d_g16, cg)
            dxa_ref[0, :, pl.ds(D_INNER + (SSM_GROUPS + g) * D_STATE, D_STATE)] = d_cg + jnp.dot(d_g16, bg, preferred_element_type=F32)
        d_a = _dot_hi(d["tri_u"], dc_ref[...])
        d_pre = (ddtc_ref[...] + d_a * d["ah_row"]) * _sigmoid(d["pre"])
        ddt_ref[0] = d_pre
        gprow_ref[P_DTB:P_DTB + 1, :] += jnp.sum(d_pre, axis=0, keepdims=True)
        gprow_ref[P_ALOG:P_ALOG + 1, :] += jnp.sum(d_a * d["dt"], axis=0, keepdims=True) * d["ah_row"]
        d_at = _dot_hi(drt_ref[...], d["tri"])
        d_pre_t = d_at * d["ah_col"] * _sigmoid(d["pre_t"])
        ddtt_ref[0] = d_pre_t
        gpcol_ref[:, P_DTB:P_DTB + 1] += jnp.sum(d_pre_t, axis=1, keepdims=True)
        gpcol_ref[:, P_ALOG:P_ALOG + 1] += jnp.sum(d_at * d["dt_t"], axis=1, keepdims=True) * d["ah_col"]

    x_spec, bm_spec, cm_spec, dt_spec, dtt_spec, prow_spec, pcol_spec, st_spec, y_spec = _ssd_specs(nc, True)
    return pl.pallas_call(
        body, name="ssd_bwd",
        grid=(bsz, nc),
        in_specs=[x_spec, bm_spec, cm_spec, dt_spec, dtt_spec, prow_spec, pcol_spec, st_spec, y_spec],
        out_specs=[pl.BlockSpec((1, BLK, CONV_DIM), lambda b, c: (b, nc - 1 - c, 0)),
                   pl.BlockSpec((1, BLK, BLK), lambda b, c: (b, nc - 1 - c, 0)),
                   pl.BlockSpec((1, BLK, BLK), lambda b, c: (b, 0, nc - 1 - c)),
                   prow_spec, pcol_spec],
        out_shape=[jax.ShapeDtypeStruct((bsz, seq, CONV_DIM), F32),
                   jax.ShapeDtypeStruct((bsz, seq, BLK), F32),
                   jax.ShapeDtypeStruct((bsz, BLK, seq), F32),
                   jax.ShapeDtypeStruct((8, BLK), F32),
                   jax.ShapeDtypeStruct((BLK, 8), F32)],
        scratch_shapes=[pltpu.VMEM((SSM_HEADS, HEAD_DIM, D_STATE), F32), pltpu.VMEM((BLK, BLK), F32),
                        pltpu.VMEM((BLK, BLK), F32), pltpu.VMEM((BLK, BLK), F32)],
        compiler_params=_params("arbitrary", "arbitrary"),
    )(xa, xa, xa, rest3, dtr_t, prow, pcol, s_in_all, dy)


GROUP_W = HEADS_PER_SSM_GROUP * HEAD_DIM


def _select_matrix(shape, g, head_axis, per_head):
    h = lax.broadcasted_iota(jnp.int32, shape, head_axis)
    j = lax.broadcasted_iota(jnp.int32, shape, 1 - head_axis)
    return (h == g * HEADS_PER_SSM_GROUP + lax.shift_right_logical(j, per_head.bit_length() - 1)).astype(BF16)


def _split16(v, terms):
    parts, rem = [], v
    for _ in range(terms):
        p = rem.astype(BF16)
        parts.append(p)
        rem = rem - p.astype(F32)
    return parts


def _sel_dot(a, b, terms=2):
    if a.dtype == BF16:
        return sum(jnp.dot(a, p, preferred_element_type=F32) for p in _split16(b, terms))
    return sum(jnp.dot(p, b, preferred_element_type=F32) for p in _split16(a, terms))


def _ssd_group(g, d, e_all, f_all, ecl_b, x_ref, prow_ref):
    spread = _select_matrix((BLK, GROUP_W), g, 0, HEAD_DIM)
    gather = _select_matrix((GROUP_W, BLK), g, 1, HEAD_DIM)
    xg = x_ref[0, :, pl.ds(g * GROUP_W, GROUP_W)]
    dt_g = _sel_dot(d["dt"], spread)
    e_g = _sel_dot(e_all, spread)
    f_g = _sel_dot(f_all, spread)
    dsk_g = _sel_dot(prow_ref[...], spread, 3)[P_DSKIP:P_DSKIP + 1, :]
    sc_g = _sel_dot(gather, ecl_b, 3)
    xd = xg * dt_g
    return dict(spread=spread, gather=gather, xg=xg, dt_g=dt_g, e_g=e_g, f_g=f_g, dsk_g=dsk_g, sc_g=sc_g, xd=xd,
                xd16=xd.astype(BF16), xf16=(xd * f_g).astype(BF16))


def _ssd_common(d):
    e_all = jnp.exp(d["acs"])
    f_all = jnp.exp(d["acs"][BLK - 1:BLK, :] - d["acs"])
    ecl_b = jnp.broadcast_to(jnp.exp(d["acs_t"][:, BLK - 1:BLK]), (BLK, BLK))
    return e_all, f_all, ecl_b


def _ssd_mask_decay(d, h, g_mat):
    col = d["acs"][:, h:h + 1]
    row = d["acs_t"][h:h + 1, :]
    lm = jnp.exp(jnp.where(d["causal"], col - row, -jnp.inf))
    return lm, g_mat * lm


def _ssd_state_spec(nc, rev):
    cidx = (lambda c: nc - 1 - c) if rev else (lambda c: c)
    return pl.BlockSpec((1, 1, SSM_GROUPS, GROUP_W, D_STATE), lambda b, c: (b, cidx(c), 0, 0, 0))


def _ssd_fwd(xa, rest3, dtr_t, prow, pcol):
    bsz, seq, _ = xa.shape
    nc = seq // BLK

    def body(x_ref, bm_ref, cm_ref, dt_ref, dtt_ref, prow_ref, pcol_ref, y_ref, sin_ref, s_ref):
        @pl.when(pl.program_id(1) == 0)
        def _():
            s_ref[...] = jnp.zeros_like(s_ref)

        d = _ssd_decays(dt_ref[0], dtt_ref[0], prow_ref, pcol_ref)
        e_all, f_all, ecl_b = _ssd_common(d)
        for g in range(SSM_GROUPS):
            lanes = pl.ds(g * D_STATE, D_STATE)
            bg = bm_ref[0, :, lanes].astype(BF16)
            cg = cm_ref[0, :, lanes].astype(BF16)
            g_mat = _nt(cg, bg)
            q = _ssd_group(g, d, e_all, f_all, ecl_b, x_ref, prow_ref)
            s_in = s_ref[g]
            sin_ref[0, 0, g] = s_in
            y_diag = []
            for j in range(HEADS_PER_SSM_GROUP):
                _, m = _ssd_mask_decay(d, g * HEADS_PER_SSM_GROUP + j, g_mat)
                y_diag.append(jnp.dot(m.astype(BF16), q["xd16"][:, j * HEAD_DIM:(j + 1) * HEAD_DIM], preferred_element_type=F32))
            y_off = _nt(cg, s_in.astype(BF16)) * q["e_g"]
            y_ref[0, :, pl.ds(g * GROUP_W, GROUP_W)] = jnp.concatenate(y_diag, axis=1) + y_off + q["dsk_g"] * q["xg"]
            s_ref[g] = s_in * q["sc_g"] + _tn(q["xf16"], bg)

    x_spec, bm_spec, cm_spec, dt_spec, dtt_spec, prow_spec, pcol_spec, _, y_spec = _ssd_specs(nc, False)
    return pl.pallas_call(
        body, name="ssd_fwd",
        grid=(bsz, nc),
        in_specs=[x_spec, bm_spec, cm_spec, dt_spec, dtt_spec, prow_spec, pcol_spec],
        out_specs=[y_spec, _ssd_state_spec(nc, False)],
        out_shape=[jax.ShapeDtypeStruct((bsz, seq, D_INNER), F32),
                   jax.ShapeDtypeStruct((bsz, nc, SSM_GROUPS, GROUP_W, D_STATE), F32)],
        scratch_shapes=[pltpu.VMEM((SSM_GROUPS, GROUP_W, D_STATE), F32)],
        compiler_params=_params("parallel", "arbitrary"),
    )(xa, xa, xa, rest3, dtr_t, prow, pcol)


def _ssd_bwd(xa, rest3, dtr_t, prow, pcol, s_in_all, dy):
    bsz, seq, _ = xa.shape
    nc = seq // BLK

    def body(x_ref, bm_ref, cm_ref, dt_ref, dtt_ref, prow_ref, pcol_ref, sin_ref, dy_ref,
             dxa_ref, ddt_ref, ddtt_ref, gprow_ref, gpcol_ref, ds_ref, drt_ref):
        first = jnp.logical_and(pl.program_id(0) == 0, pl.program_id(1) == 0)

        @pl.when(first)
        def _():
            gprow_ref[...] = jnp.zeros_like(gprow_ref)
            gpcol_ref[...] = jnp.zeros_like(gpcol_ref)

        @pl.when(pl.program_id(1) == 0)
        def _():
            ds_ref[...] = jnp.zeros_like(ds_ref)

        drt_ref[...] = jnp.zeros_like(drt_ref)
        d = _ssd_decays(dt_ref[0], dtt_ref[0], prow_ref, pcol_ref)
        e_all, f_all, ecl_b = _ssd_common(d)
        d_c = jnp.zeros((BLK, BLK), F32)
        d_dtc = jnp.zeros((BLK, BLK), F32)
        d_clast_col = jnp.zeros((BLK, 1), F32)
        skip_rows = lax.broadcasted_iota(jnp.int32, (8, GROUP_W), 0) == P_DSKIP
        for g in range(SSM_GROUPS):
            lanes = pl.ds(g * D_STATE, D_STATE)
            bg = bm_ref[0, :, lanes].astype(BF16)
            cg = cm_ref[0, :, lanes].astype(BF16)
            g_mat = _nt(cg, bg)
            q = _ssd_group(g, d, e_all, f_all, ecl_b, x_ref, prow_ref)
            s_in = sin_ref[0, 0, g]
            s_in16 = s_in.astype(BF16)
            d_y = dy_ref[0, :, pl.ds(g * GROUP_W, GROUP_W)]
            d_y16 = d_y.astype(BF16)
            d_so = ds_ref[g]
            d_so16 = d_so.astype(BF16)
            d_g = jnp.zeros((BLK, BLK), F32)
            ws, d_xds = [], []
            for j in range(HEADS_PER_SSM_GROUP):
                h = g * HEADS_PER_SSM_GROUP + j
                head = slice(j * HEAD_DIM, (j + 1) * HEAD_DIM)
                lm, m = _ssd_mask_decay(d, h, g_mat)
                d_m = _nt(d_y16[:, head], q["xd16"][:, head])
                d_xds.append(_tn(m.astype(BF16), d_y16[:, head]))
                w = d_m * m
                d_g = d_g + d_m * lm
                drt_ref[h:h + 1, :] = -jnp.sum(w, axis=0, keepdims=True)
                ws.append(w)
            d_c = d_c + _sel_dot(jnp.concatenate(ws, axis=1), _select_matrix((HEADS_PER_SSM_GROUP * BLK, BLK), g, 1, BLK))
            d_xd = jnp.concatenate(d_xds, axis=1)
            d_g16 = d_g.astype(BF16)
            qmat = _nt(cg, s_in16)
            d_q16 = (d_y * q["e_g"]).astype(BF16)
            d_cg = jnp.dot(d_q16, s_in16, preferred_element_type=F32) + jnp.dot(d_g16, bg, preferred_element_type=F32)
            d_sin = _tn(d_q16, cg) + d_so * q["sc_g"]
            d_clast_col = d_clast_col + jnp.sum(_sel_dot(q["spread"], d_so * s_in * q["sc_g"]), axis=1, keepdims=True)
            d_xf = _nt(bg, d_so16)
            d_bg = jnp.dot(q["xf16"], d_so16, preferred_element_type=F32) + _tn(d_g16, cg)
            d_xd = d_xd + d_xf * q["f_g"]
            r_e = _sel_dot(d_y * qmat * q["e_g"], q["gather"])
            r_f = _sel_dot(d_xf * q["xd"] * q["f_g"], q["gather"])
            d_c = d_c + r_e - r_f + jnp.where(d["last_row"], jnp.sum(r_f, axis=0, keepdims=True), 0.0)
            d_dtc = d_dtc + _sel_dot(d_xd * q["xg"], q["gather"])
            skip_sum = jnp.where(skip_rows, jnp.sum(d_y * q["xg"], axis=0, keepdims=True), 0.0)
            gprow_ref[...] += _sel_dot(skip_sum, q["gather"])
            dxa_ref[0, :, pl.ds(g * GROUP_W, GROUP_W)] = q["dsk_g"] * d_y + d_xd * q["dt_g"]
            dxa_ref[0, :, pl.ds(D_INNER + g * D_STATE, D_STATE)] = d_bg
            dxa_ref[0, :, pl.ds(D_INNER + (SSM_GROUPS + g) * D_STATE, D_STATE)] = d_cg
            ds_ref[g] = d_sin
        drt_ref[:, BLK - 1:BLK] += d_clast_col
        d_a = _dot_hi(d["tri_u"], d_c)
        d_pre = (d_dtc + d_a * d["ah_row"]) * _sigmoid(d["pre"])
        ddt_ref[0] = d_pre
        gprow_ref[P_DTB:P_DTB + 1, :] += jnp.sum(d_pre, axis=0, keepdims=True)
        gprow_ref[P_ALOG:P_ALOG + 1, :] += jnp.sum(d_a * d["dt"], axis=0, keepdims=True) * d["ah_row"]
        d_at = _dot_hi(drt_ref[...], d["tri"])
        d_pre_t = d_at * d["ah_col"] * _sigmoid(d["pre_t"])
        ddtt_ref[0] = d_pre_t
        gpcol_ref[:, P_DTB:P_DTB + 1] += jnp.sum(d_pre_t, axis=1, keepdims=True)
        gpcol_ref[:, P_ALOG:P_ALOG + 1] += jnp.sum(d_at * d["dt_t"], axis=1, keepdims=True) * d["ah_col"]

    x_spec, bm_spec, cm_spec, dt_spec, dtt_spec, prow_spec, pcol_spec, _, y_spec = _ssd_specs(nc, True)
    return pl.pallas_call(
        body, name="ssd_bwd",
        grid=(bsz, nc),
        in_specs=[x_spec, bm_spec, cm_spec, dt_spec, dtt_spec, prow_spec, pcol_spec, _ssd_state_spec(nc, True), y_spec],
        out_specs=[pl.BlockSpec((1, BLK, CONV_DIM), lambda b, c: (b, nc - 1 - c, 0)),
                   pl.BlockSpec((1, BLK, BLK), lambda b, c: (b, nc - 1 - c, 0)),
                   pl.BlockSpec((1, BLK, BLK), lambda b, c: (b, 0, nc - 1 - c)),
                   prow_spec, pcol_spec],
        out_shape=[jax.ShapeDtypeStruct((bsz, seq, CONV_DIM), F32),
                   jax.ShapeDtypeStruct((bsz, seq, BLK), F32),
                   jax.ShapeDtypeStruct((bsz, BLK, seq), F32),
                   jax.ShapeDtypeStruct((8, BLK), F32),
                   jax.ShapeDtypeStruct((BLK, 8), F32)],
        scratch_shapes=[pltpu.VMEM((SSM_GROUPS, GROUP_W, D_STATE), F32), pltpu.VMEM((BLK, BLK), F32)],
        compiler_params=_params("arbitrary", "arbitrary"),
    )(xa, xa, xa, rest3, dtr_t, prow, pcol, s_in_all, dy)


ROW_TILE = 256
CMB_COLS = 256
RMS_COLS = D_INNER // SSM_GROUPS


def _combine_weights(l_refs):
    ls = [r[...] for r in l_refs]
    mx = jnp.maximum(jnp.maximum(ls[0], ls[1]), ls[2])
    es = [jnp.exp(l - mx) for l in ls]
    inv = 1.0 / (es[0] + es[1] + es[2])
    return [e * inv for e in es]


def _combine_specs():
    a = pl.BlockSpec((ROW_TILE, CMB_COLS), lambda i, j: (i, j))
    gatt = pl.BlockSpec((ROW_TILE, CMB_COLS), lambda i, j: (i, R_GATT // CMB_COLS + j))
    return a, gatt


def _combine_fwd(outs, lses, rest):
    t = rest.shape[0]

    def body(o0, o1, o2, l0, l1, l2, ga_ref, oa_ref):
        ws = _combine_weights((l0, l1, l2))
        o = ws[0] * o0[...] + ws[1] * o1[...] + ws[2] * o2[...]
        oa_ref[...] = (o * _silu_and_grad(ga_ref[...])[0]).astype(BF16)

    a, gatt = _combine_specs()
    return pl.pallas_call(
        body, name="combine_fwd",
        grid=(t // ROW_TILE, GROUP_COLS // CMB_COLS),
        in_specs=[a] * 6 + [gatt],
        out_specs=a,
        out_shape=jax.ShapeDtypeStruct((t, GROUP_COLS), BF16),
        compiler_params=_params("parallel", "parallel"),
    )(*outs, *lses, rest)


def _combine_bwd(outs, lses, rest, d_oa):
    t = rest.shape[0]

    def body(o0, o1, o2, l0, l1, l2, ga_ref, doa_ref, do0, do1, do2, dd0, dd1, dd2, dga_ref):
        ws = _combine_weights((l0, l1, l2))
        o = ws[0] * o0[...] + ws[1] * o1[...] + ws[2] * o2[...]
        sg, dsg = _silu_and_grad(ga_ref[...])
        d_oa_v = doa_ref[...]
        d_o = d_oa_v * sg
        dga_ref[...] = (d_oa_v * o * dsg).astype(BF16)
        for w, do_ref, dd_ref in zip(ws, (do0, do1, do2), (dd0, dd1, dd2)):
            d_out = w * d_o
            do_ref[...] = d_out
            dd_ref[...] = d_out * o

    a, gatt = _combine_specs()
    s16 = jax.ShapeDtypeStruct((t, GROUP_COLS), BF16)
    s32 = jax.ShapeDtypeStruct((t, GROUP_COLS), F32)
    return pl.pallas_call(
        body, name="combine_bwd",
        grid=(t // ROW_TILE, GROUP_COLS // CMB_COLS),
        in_specs=[a] * 6 + [gatt, a],
        out_specs=[a] * 7,
        out_shape=[s32, s32, s32, s32, s32, s32, s16],
        compiler_params=_params("parallel", "parallel"),
    )(*outs, *lses, rest, d_oa)


def _gatenorm_fwd(y, rest, norm_w):
    t = rest.shape[0]

    def body(y_ref, z_ref, w_ref, o_ref):
        u = y_ref[...] * _silu_and_grad(z_ref[...])[0]
        rs = lax.rsqrt(jnp.mean(u * u, axis=-1, keepdims=True) + RMS_EPS)
        o_ref[...] = (u * rs * w_ref[...]).astype(BF16)

    return pl.pallas_call(
        body, name="gatenorm_fwd",
        grid=(t // ROW_TILE, SSM_GROUPS),
        in_specs=[pl.BlockSpec((ROW_TILE, RMS_COLS), lambda i, j: (i, j)),
                  pl.BlockSpec((ROW_TILE, RMS_COLS), lambda i, j: (i, R_Z // RMS_COLS + j)),
                  pl.BlockSpec((1, RMS_COLS), lambda i, j: (0, j))],
        out_specs=pl.BlockSpec((ROW_TILE, RMS_COLS), lambda i, j: (i, j)),
        out_shape=jax.ShapeDtypeStruct((t, D_INNER), BF16),
        compiler_params=_params("parallel", "parallel"),
    )(y, rest, norm_w)


def _gatenorm_bwd(y, rest, norm_w, d_ys):
    t = rest.shape[0]

    def body(y_ref, z_ref, w_ref, g_ref, dy_ref, dz_ref, dw_ref):
        @pl.when(pl.program_id(1) == 0)
        def _():
            dw_ref[...] = jnp.zeros_like(dw_ref)

        yv = y_ref[...]
        sz, dsz = _silu_and_grad(z_ref[...])
        u = yv * sz
        rs = lax.rsqrt(jnp.mean(u * u, axis=-1, keepdims=True) + RMS_EPS)
        un = u * rs
        g = g_ref[...]
        dw_ref[0:1, :] += jnp.sum(g * un, axis=0, keepdims=True)
        d_un = g * w_ref[...]
        d_u = rs * (d_un - un * jnp.mean(d_un * un, axis=-1, keepdims=True))
        dy_ref[...] = d_u * sz
        dz_ref[...] = (d_u * yv * dsz).astype(BF16)

    blk = pl.BlockSpec((ROW_TILE, RMS_COLS), lambda j, i: (i, j))
    return pl.pallas_call(
        body, name="gatenorm_bwd",
        grid=(SSM_GROUPS, t // ROW_TILE),
        in_specs=[blk, pl.BlockSpec((ROW_TILE, RMS_COLS), lambda j, i: (i, R_Z // RMS_COLS + j)),
                  pl.BlockSpec((1, RMS_COLS), lambda j, i: (0, j)), blk],
        out_specs=[blk, blk, pl.BlockSpec((8, RMS_COLS), lambda j, i: (0, j))],
        out_shape=[jax.ShapeDtypeStruct((t, D_INNER), F32), jax.ShapeDtypeStruct((t, D_INNER), BF16),
                   jax.ShapeDtypeStruct((8, D_INNER), F32)],
        compiler_params=_params("parallel", "arbitrary"),
    )(y, rest, norm_w, d_ys)


def _row_specs():
    full = pl.BlockSpec((ROW_TILE, D_MODEL), lambda i: (i, 0))
    vec = pl.BlockSpec((8, D_MODEL), lambda i: (0, 0))
    at = lambda off: pl.BlockSpec((ROW_TILE, D_MODEL), lambda i: (i, off // D_MODEL))
    return full, vec, at


def _merge_fwd(y_a, y_b, rest, b_gate):
    t = rest.shape[0]

    def body(ya_ref, yb_ref, ga_ref, gb_ref, bg_ref, o_ref):
        sa = _sigmoid(ga_ref[...] + bg_ref[0:1, :])
        sb = _sigmoid(gb_ref[...] + bg_ref[1:2, :])
        o_ref[...] = (sa * ya_ref[...] + sb * yb_ref[...]).astype(BF16)

    full, vec, at = _row_specs()
    return pl.pallas_call(
        body, name="merge_fwd",
        grid=(t // ROW_TILE,),
        in_specs=[full, full, at(R_GM), at(R_GM + D_MODEL), vec],
        out_specs=full,
        out_shape=jax.ShapeDtypeStruct((t, D_MODEL), BF16),
        compiler_params=_params("parallel"),
    )(y_a, y_b, rest, rest, b_gate)


def _merge_bwd(y_a, y_b, rest, b_gate, d_merged):
    t = rest.shape[0]

    def body(ya_ref, yb_ref, ga_ref, gb_ref, bg_ref, dm_ref, dya_ref, dyb_ref, dga_ref, dgb_ref, dbg_ref):
        @pl.when(pl.program_id(0) == 0)
        def _():
            dbg_ref[...] = jnp.zeros_like(dbg_ref)

        dm = dm_ref[...]
        for row, y_ref, g_ref, dy_ref, dg_ref in ((0, ya_ref, ga_ref, dya_ref, dga_ref), (1, yb_ref, gb_ref, dyb_ref, dgb_ref)):
            s = _sigmoid(g_ref[...] + bg_ref[row:row + 1, :])
            dy_ref[...] = (dm * s).astype(BF16)
            dg = dm * y_ref[...] * s * (1.0 - s)
            dg_ref[...] = dg.astype(BF16)
            dbg_ref[row:row + 1, :] += jnp.sum(dg, axis=0, keepdims=True)

    full, vec, at = _row_specs()
    s16 = jax.ShapeDtypeStruct((t, D_MODEL), BF16)
    return pl.pallas_call(
        body, name="merge_bwd",
        grid=(t // ROW_TILE,),
        in_specs=[full, full, at(R_GM), at(R_GM + D_MODEL), vec, full],
        out_specs=[full, full, full, full, vec],
        out_shape=[s16, s16, s16, s16, jax.ShapeDtypeStruct((8, D_MODEL), F32)],
        compiler_params=_params("arbitrary"),
    )(y_a, y_b, rest, rest, b_gate, d_merged)


ST_LNG, ST_LNB, ST_BG2, ST_LOSS = 0, 1, 2, 3


def _final(x, mix, pw, rest, b_gate, ln_gb, target):
    t = rest.shape[0]

    def body(x_ref, mix_ref, pw_ref, gp_ref, bg_ref, ln_ref, tgt_ref, dpre_ref, dpre16_ref, dgp_ref, dpw_ref, st_ref):
        @pl.when(pl.program_id(0) == 0)
        def _():
            st_ref[...] = jnp.zeros_like(st_ref)

        sp = _sigmoid(gp_ref[...] + bg_ref[2:3, :])
        pw = pw_ref[...]
        pre = ALPHA * x_ref[...] + mix_ref[...] + sp * pw
        xc = pre - jnp.mean(pre, axis=-1, keepdims=True)
        rstd = lax.rsqrt(jnp.mean(xc * xc, axis=-1, keepdims=True) + LN_EPS)
        xhat = xc * rstd
        gain = ln_ref[0:1, :]
        err = xhat * gain + ln_ref[1:2, :] - tgt_ref[...]
        d_yo = err * (1.0 / D_MODEL)
        d_xhat = d_yo * gain
        d_pre = rstd * (d_xhat - jnp.mean(d_xhat, axis=-1, keepdims=True)
                        - xhat * jnp.mean(d_xhat * xhat, axis=-1, keepdims=True))
        dpre_ref[...] = d_pre
        dpre16_ref[...] = d_pre.astype(BF16)
        dgp = d_pre * pw * sp * (1.0 - sp)
        dgp_ref[...] = dgp.astype(BF16)
        dpw_ref[...] = (d_pre * sp).astype(BF16)
        st_ref[ST_LNG:ST_LNG + 1, :] += jnp.sum(d_yo * xhat, axis=0, keepdims=True)
        st_ref[ST_LNB:ST_LNB + 1, :] += jnp.sum(d_yo, axis=0, keepdims=True)
        st_ref[ST_BG2:ST_BG2 + 1, :] += jnp.sum(dgp, axis=0, keepdims=True)
        st_ref[ST_LOSS:ST_LOSS + 1, :] += jnp.sum(err * err, axis=0, keepdims=True) * (0.5 / D_MODEL)

    full, vec, at = _row_specs()
    s16 = jax.ShapeDtypeStruct((t, D_MODEL), BF16)
    return pl.pallas_call(
        body, name="final",
        grid=(t // ROW_TILE,),
        in_specs=[full, full, full, at(R_GPLE), vec, vec, full],
        out_specs=[full, full, full, full, vec],
        out_shape=[jax.ShapeDtypeStruct((t, D_MODEL), F32), s16, s16, s16, jax.ShapeDtypeStruct((8, D_MODEL), F32)],
        compiler_params=_params("arbitrary"),
    )(x, mix, pw, rest, b_gate, ln_gb, target)


def _mesh_position():
    return lax.axis_index("x"), lax.axis_index("y"), lax.axis_index("c")


def _flip(pos, k):
    x, y, c = pos
    return ((1 - x) if k & 4 else x, (1 - y) if k & 2 else y, (1 - c) if k & 1 else c)


def _linear(pos):
    return 4 * pos[0] + 2 * pos[1] + pos[2]


def _exchange(arrays, scatter, name):
    n = len(arrays)

    def body(*refs):
        ins, outs = refs[:n], refs[n:2 * n]
        send_sems, recv_sems, local_sems = refs[2 * n:]
        me = _mesh_position()
        me_i = _linear(me)

        def src_for(i, dest_i):
            return ins[i].at[dest_i] if scatter[i] else ins[i]

        local = [pltpu.make_async_copy(src_for(i, me_i), outs[i].at[me_i], local_sems.at[i]) for i in range(n)]
        for cp in local:
            cp.start()
        started = []
        for k in range(1, N_DEV):
            peer = _flip(me, k)
            peer_i = _linear(peer)
            for i in range(n):
                sem = i * (N_DEV - 1) + k - 1
                cp = pltpu.make_async_remote_copy(
                    src_ref=src_for(i, peer_i), dst_ref=outs[i].at[me_i], send_sem=send_sems.at[sem],
                    recv_sem=recv_sems.at[sem], device_id=peer, device_id_type=pl.DeviceIdType.MESH)
                cp.start()
                started.append(cp)
        for k in range(1, N_DEV):
            peer = _flip(me, k)
            peer_i = _linear(peer)
            for i in range(n):
                sem = i * (N_DEV - 1) + k - 1
                pltpu.make_async_remote_copy(
                    src_ref=src_for(i, peer_i), dst_ref=outs[i].at[peer_i], send_sem=send_sems.at[sem],
                    recv_sem=recv_sems.at[sem], device_id=peer, device_id_type=pl.DeviceIdType.MESH).wait_recv()
        for cp in started:
            cp.wait_send()
        for cp in local:
            cp.wait()

    any_spec = pl.BlockSpec(memory_space=pl.ANY)
    out_shape = [jax.ShapeDtypeStruct(a.shape if s else (N_DEV,) + a.shape, a.dtype) for a, s in zip(arrays, scatter)]
    return pl.pallas_call(
        body, name=name,
        in_specs=[any_spec] * n,
        out_specs=[any_spec] * n,
        out_shape=out_shape,
        scratch_shapes=[pltpu.SemaphoreType.DMA((n * (N_DEV - 1),)), pltpu.SemaphoreType.DMA((n * (N_DEV - 1),)),
                        pltpu.SemaphoreType.DMA((n,))],
        compiler_params=pltpu.CompilerParams(has_side_effects=True),
    )(*arrays)


N_CHIPS = N_DEV // 2


def _other_chips(x, y):
    return [(1 - x, y), (x, 1 - y), (1 - x, 1 - y)]


def _gather_two_level(arrays, name):
    n = len(arrays)
    per = N_DEV - 1

    def body(*refs):
        ins, outs = refs[:n], refs[n:2 * n]
        send_sems, recv_sems, local_sems = refs[2 * n:]
        x, y, c = _mesh_position()
        me, sibling = (x, y, c), (x, y, 1 - c)
        chips = _other_chips(x, y)

        def copy(i, k, block, to, src=None):
            slot = outs[i].at[_linear(block)]
            return pltpu.make_async_remote_copy(
                src_ref=slot if src is None else src, dst_ref=slot, send_sem=send_sems.at[i * per + k],
                recv_sem=recv_sems.at[i * per + k], device_id=to, device_id_type=pl.DeviceIdType.MESH)

        local = [pltpu.make_async_copy(ins[i], outs[i].at[_linear(me)], local_sems.at[i]) for i in range(n)]
        for cp in local:
            cp.start()
        started = []
        for i in range(n):
            first = [copy(i, 0, me, sibling, src=ins[i])]
            first += [copy(i, 1 + j, me, (*chip, c), src=ins[i]) for j, chip in enumerate(chips)]
            for cp in first:
                cp.start()
            started += first
        for j, chip in enumerate(chips):
            for i in range(n):
                copy(i, 1 + j, (*chip, c), me).wait_recv()
                passed = copy(i, 4 + j, (*chip, c), sibling)
                passed.start()
                started.append(passed)
        for i in range(n):
            copy(i, 0, sibling, me).wait_recv()
            for j, chip in enumerate(chips):
                copy(i, 4 + j, (*chip, 1 - c), me).wait_recv()
        for cp in started:
            cp.wait_send()
        for cp in local:
            cp.wait()

    any_spec = pl.BlockSpec(memory_space=pl.ANY)
    return pl.pallas_call(
        body, name=name,
        in_specs=[any_spec] * n,
        out_specs=[any_spec] * n,
        out_shape=[jax.ShapeDtypeStruct((N_DEV,) + a.shape, a.dtype) for a in arrays],
        scratch_shapes=[pltpu.SemaphoreType.DMA((n * per,)), pltpu.SemaphoreType.DMA((n * per,)), pltpu.SemaphoreType.DMA((n,))],
        compiler_params=pltpu.CompilerParams(has_side_effects=True),
    )(*arrays)


def _pair_reduce(a, name, rows):
    _, r, c = a.shape
    assert r % rows == 0
    n_steps = r // rows
    a5 = a.reshape(N_CHIPS, 2, r, c)
    core = lax.axis_index("c").astype(jnp.int32).reshape(1)

    def body(core_ref, keep_ref, send_ref, o_ref, land, send_sems, recv_sems, credits):
        i = pl.program_id(0)
        slot = i % 2
        x, y, cc = _mesh_position()
        sibling = (x, y, 1 - cc)

        @pl.when(i >= 2)
        def _():
            pl.semaphore_wait(credits.at[slot], 1)

        rdma = pltpu.make_async_remote_copy(
            src_ref=send_ref, dst_ref=land.at[slot], send_sem=send_sems.at[slot], recv_sem=recv_sems.at[slot],
            device_id=sibling, device_id_type=pl.DeviceIdType.MESH)
        rdma.start()
        rdma.wait_recv()
        o_ref[...] = (keep_ref[:, 0].astype(F32) + land[slot, :, 0].astype(F32)).astype(o_ref.dtype)
        rdma.wait_send()

        @pl.when(i + 2 < n_steps)
        def _():
            pl.semaphore_signal(credits.at[slot], inc=1, device_id=sibling, device_id_type=pl.DeviceIdType.MESH)

    grid_spec = pltpu.PrefetchScalarGridSpec(
        num_scalar_prefetch=1,
        grid=(n_steps,),
        in_specs=[pl.BlockSpec((N_CHIPS, 1, rows, c), lambda i, core_ref: (0, core_ref[0], i, 0)),
                  pl.BlockSpec((N_CHIPS, 1, rows, c), lambda i, core_ref: (0, 1 - core_ref[0], i, 0))],
        out_specs=pl.BlockSpec((N_CHIPS, rows, c), lambda i, core_ref: (0, i, 0)),
        scratch_shapes=[pltpu.VMEM((2, N_CHIPS, 1, rows, c), a.dtype), pltpu.SemaphoreType.DMA((2,)),
                        pltpu.SemaphoreType.DMA((2,)), pltpu.SemaphoreType.REGULAR((2,))],
    )
    return pl.pallas_call(
        body, name=name, grid_spec=grid_spec,
        out_shape=jax.ShapeDtypeStruct((N_CHIPS, r, c), a.dtype),
        compiler_params=pltpu.CompilerParams(dimension_semantics=("arbitrary",), vmem_limit_bytes=VMEM_LIMIT_BYTES,
                                             has_side_effects=True),
    )(core, a5, a5)


def _chip_exchange(arrays, name):
    n = len(arrays)
    per = N_CHIPS - 1

    def body(*refs):
        ins, outs = refs[:n], refs[n:2 * n]
        send_sems, recv_sems, local_sems = refs[2 * n:]
        x, y, c = _mesh_position()
        me_q = 2 * x + y
        local = [pltpu.make_async_copy(ins[i].at[me_q], outs[i].at[me_q], local_sems.at[i]) for i in range(n)]
        for cp in local:
            cp.start()
        started = []
        for j, (px, py) in enumerate(_other_chips(x, y)):
            for i in range(n):
                cp = pltpu.make_async_remote_copy(
                    src_ref=ins[i].at[2 * px + py], dst_ref=outs[i].at[me_q], send_sem=send_sems.at[i * per + j],
                    recv_sem=recv_sems.at[i * per + j], device_id=(px, py, c), device_id_type=pl.DeviceIdType.MESH)
                cp.start()
                started.append(cp)
        for j, (px, py) in enumerate(_other_chips(x, y)):
            for i in range(n):
                pltpu.make_async_remote_copy(
                    src_ref=ins[i].at[2 * px + py], dst_ref=outs[i].at[2 * px + py], send_sem=send_sems.at[i * per + j],
                    recv_sem=recv_sems.at[i * per + j], device_id=(px, py, c), device_id_type=pl.DeviceIdType.MESH).wait_recv()
        for cp in started:
            cp.wait_send()
        for cp in local:
            cp.wait()

    any_spec = pl.BlockSpec(memory_space=pl.ANY)
    return pl.pallas_call(
        body, name=name,
        in_specs=[any_spec] * n,
        out_specs=[any_spec] * n,
        out_shape=[jax.ShapeDtypeStruct(a.shape, a.dtype) for a in arrays],
        scratch_shapes=[pltpu.SemaphoreType.DMA((n * per,)), pltpu.SemaphoreType.DMA((n * per,)), pltpu.SemaphoreType.DMA((n,))],
        compiler_params=pltpu.CompilerParams(has_side_effects=True),
    )(*arrays)


def _adam_reduce(parts, w, m, v, name, rows):
    r, c = w.shape
    n_parts = parts.shape[0]
    assert r % rows == 0
    c1 = 1.0 - ADAM_B1 ** ADAM_STEP
    c2 = 1.0 - ADAM_B2 ** ADAM_STEP

    def body(p_ref, w_ref, m_ref, v_ref, g_ref, d_ref, nm_ref, nv_ref):
        g = p_ref[0].astype(F32)
        for s in range(1, n_parts):
            g = g + p_ref[s].astype(F32)
        g_ref[...] = g
        nm = ADAM_B1 * m_ref[...] + (1.0 - ADAM_B1) * g
        nv = ADAM_B2 * v_ref[...] + (1.0 - ADAM_B2) * (g * g)
        nm_ref[...] = nm
        nv_ref[...] = nv
        d_ref[...] = -ADAM_LR * ((nm / c1) / (jnp.sqrt(nv / c2) + ADAM_EPS) + ADAM_WD * w_ref[...])

    blk = pl.BlockSpec((rows, c), lambda i: (i, 0))
    shape = jax.ShapeDtypeStruct((r, c), F32)
    return pl.pallas_call(
        body, name=name,
        grid=(r // rows,),
        in_specs=[pl.BlockSpec((n_parts, rows, c), lambda i: (0, i, 0)), blk, blk, blk],
        out_specs=[blk] * 4,
        out_shape=[shape] * 4,
        compiler_params=_params("parallel"),
    )(parts, w, m, v)


def _lane_total(rows8):
    def body(a_ref, o_ref):
        o_ref[...] = _sum_all(a_ref[...])

    return pl.pallas_call(body, name="loss_total", out_shape=jax.ShapeDtypeStruct((1, 1), F32))(rows8)


def _permute_w_in(w):
    rows = w.shape[0]
    n_pairs = N_GROUPS * HEAD_PAIRS
    qkv = w[:, :V_END].reshape(rows, 3, n_pairs, BLK).transpose(0, 2, 1, 3).reshape(rows, V_END)
    return jnp.concatenate(
        [w[:, DT_END:GMERGE_END], w[:, GATT_END:Z_END], w[:, Z_END:XBC_END], w[:, GMERGE_END:], w[:, V_END:GATT_END],
         w[:, XBC_END:DT_END], jnp.zeros((rows, QKV_OFF - R_DT - SSM_HEADS), w.dtype), qkv], axis=1)


def _unpermute_w_in(g):
    rows = g.shape[0]
    n_pairs = N_GROUPS * HEAD_PAIRS
    qkv = g[:, QKV_OFF:].reshape(rows, n_pairs, 3, BLK).transpose(0, 2, 1, 3).reshape(rows, V_END)
    return jnp.concatenate(
        [qkv, g[:, R_GATT:R_GATT + GROUP_COLS], g[:, R_Z:R_Z + D_INNER], g[:, R_XBC:R_XBC + CONV_DIM],
         g[:, R_DT:R_DT + SSM_HEADS], g[:, R_GM:R_GM + 2 * D_MODEL], g[:, R_GPLE:R_GPLE + D_MODEL]], axis=1)


SMALL_ROWS = 80
_SMALL_LAYOUT = (("conv_b", CONV_DIM), ("dt_bias", BLK), ("a_log", BLK), ("d_skip", BLK), ("ssm_norm_w", D_INNER),
                 ("ln_g", D_MODEL), ("ln_b", D_MODEL), ("rel_bias", NUM_BUCKETS * ATT_HEADS), ("loss", D_MODEL))


def _pack_small(vals):
    flat = []
    for name, width in _SMALL_LAYOUT:
        v = vals.get(name)
        v = jnp.zeros((width,), F32) if v is None else v.reshape(-1).astype(F32)
        flat.append(jnp.pad(v, (0, width - v.shape[0])))
    flat = jnp.concatenate(flat)
    return jnp.pad(flat, (0, SMALL_ROWS * BLK - flat.shape[0])).reshape(SMALL_ROWS, BLK)


def _unpack_small(packed):
    flat = packed.reshape(-1)
    out, pos = {}, 0
    for name, width in _SMALL_LAYOUT:
        out[name] = flat[pos:pos + width]
        pos += width
    for name in ("dt_bias", "a_log", "d_skip"):
        out[name] = out[name][:SSM_HEADS]
    out["rel_bias"] = out["rel_bias"].reshape(NUM_BUCKETS, ATT_HEADS)
    return out


def _pack_cols(b_gate_part, conv_w_part):
    return jnp.concatenate([jnp.pad(b_gate_part, ((0, 5), (0, 0))), jnp.pad(conv_w_part, ((0, 4), (0, 0)))], axis=1)


def _pack_cols_all(b_gate_full, conv_w_full):
    bg = b_gate_full.reshape(3, N_DEV, BLK).transpose(1, 0, 2)
    cw = conv_w_full.reshape(CONV_WIDTH, N_DEV, CONV_DIM // N_DEV).transpose(1, 0, 2)
    return jnp.concatenate([jnp.pad(bg, ((0, 0), (0, 5), (0, 0))), jnp.pad(cw, ((0, 0), (0, 4), (0, 0)))], axis=2)


def _unpack_cols_all(packed):
    bg = packed[:, :3, :BLK].transpose(1, 0, 2).reshape(3, D_MODEL)
    cw = packed[:, :CONV_WIDTH, BLK:].transpose(1, 0, 2).reshape(CONV_WIDTH, CONV_DIM)
    return bg, cw


def _local_step(x, p, target, wp16, wb16, wo16, wple16, b_gate, conv_w, small):
    bsz, seq, _ = x.shape
    t = bsz * seq
    x2 = x.reshape(t, D_MODEL)
    x16 = x2.astype(BF16)
    p16 = p.reshape(t, PLE_DIM).astype(BF16)
    tgt2 = target.reshape(t, D_MODEL)
    b_gate8 = jnp.pad(b_gate, ((0, 5), (0, 0)))
    ln_gb = jnp.pad(jnp.stack([small["ln_g"], small["ln_b"]]), ((0, 6), (0, 0)))
    conv_b = small["conv_b"].reshape(1, CONV_DIM)
    norm_w = small["ssm_norm_w"].reshape(1, D_INNER)
    pad_heads = lambda v: jnp.pad(v, (0, BLK - SSM_HEADS))
    prow = jnp.pad(jnp.stack([pad_heads(small["dt_bias"]), pad_heads(small["a_log"]), pad_heads(small["d_skip"])]), ((0, 5), (0, 0)))
    pcol = prow.T
    wa16, wbb16 = wb16[:GROUP_COLS], wb16[GROUP_COLS:]

    rest = _matmul(x16, wp16, mode="nn", out_dtype=F32, name="inproj", tm=512, tn=2304, tk=D_MODEL, n_outer=True)
    rest3 = rest.reshape(bsz, seq, HCAT_COLS)
    biases, onehots, outs, lses = [], [], [], []
    for g, (_, dil) in enumerate(DILATED_PATTERNS):
        bias, onehot = _bias_matrix(small["rel_bias"][:, g * HEADS_PER_GROUP:(g + 1) * HEADS_PER_GROUP], dil)
        out3, lse3 = _attn_fwd(rest3, bias, g, dil, f"attn_fwd{g}")
        biases.append(bias), onehots.append(onehot)
        outs.append(out3.reshape(t, GROUP_COLS)), lses.append(lse3.reshape(t, GROUP_COLS))
    oa = _combine_fwd(outs, lses, rest)
    xa = _conv_fwd(rest3, conv_w, conv_b)
    dtr_t = jnp.swapaxes(rest3[:, :, R_DT:R_DT + BLK], 1, 2)
    y, s_in = _ssd_fwd(xa, rest3, dtr_t, prow, pcol)
    y2 = y.reshape(t, D_INNER)
    ys = _gatenorm_fwd(y2, rest, norm_w)
    y_a = _matmul(oa, wa16, mode="nn", out_dtype=F32, name="branch_a", tm=512, tn=D_MODEL, tk=GROUP_COLS)
    y_b = _matmul(ys, wbb16, mode="nn", out_dtype=F32, name="branch_b", tm=512, tn=D_MODEL, tk=D_INNER)
    merged = _merge_fwd(y_a, y_b, rest, b_gate8)
    mix = _matmul(merged, wo16, mode="nn", out_dtype=F32, name="out_proj", tm=512, tn=D_MODEL, tk=D_MODEL)
    pw = _matmul(p16, wple16, mode="nn", out_dtype=F32, name="ple_proj", tm=512, tn=D_MODEL, tk=PLE_DIM)
    d_pre, d_pre16, d_gple, d_pw, stats = _final(x2, mix, pw, rest, b_gate8, ln_gb, tgt2)

    d_merged = _matmul(d_pre16, wo16, mode="nt", out_dtype=F32, name="d_merged", tm=512, tn=D_MODEL, tk=D_MODEL)
    g_w_out = _matmul(merged, d_pre16, mode="tn", out_dtype=BF16, name="g_w_out", tm=512, tn=D_MODEL, tk=1024)
    g_w_ple = _matmul(p16, d_pw, mode="tn", out_dtype=BF16, name="g_w_ple", tm=PLE_DIM, tn=D_MODEL, tk=1024)
    d_ya, d_yb, d_ga, d_gb, dbg01 = _merge_bwd(y_a, y_b, rest, b_gate8, d_merged)
    d_oa = _matmul(d_ya, wa16, mode="nt", out_dtype=F32, name="d_oa", tm=512, tn=GROUP_COLS, tk=D_MODEL)
    d_ys = _matmul(d_yb, wbb16, mode="nt", out_dtype=F32, name="d_ys", tm=512, tn=1024, tk=D_MODEL)
    g_wa = _matmul(oa, d_ya, mode="tn", out_dtype=BF16, name="g_w_branch_a", tm=GROUP_COLS, tn=D_MODEL, tk=1024)
    g_wb = _matmul(ys, d_yb, mode="tn", out_dtype=BF16, name="g_w_branch_b", tm=512, tn=D_MODEL, tk=1024)
    d_outs_dd_ga = _combine_bwd(outs, lses, rest, d_oa)
    d_outs, dds, d_gatt = d_outs_dd_ga[:3], d_outs_dd_ga[3:6], d_outs_dd_ga[6]
    d_y, d_z, d_nw = _gatenorm_bwd(y2, rest, norm_w, d_ys)
    d_xa, ddt, ddt_t, gprow, gpcol = _ssd_bwd(xa, rest3, dtr_t, prow, pcol, s_in, d_y.reshape(bsz, seq, D_INNER))
    d_xbc, d_conv = _conv_bwd(rest3, conv_w, conv_b, d_xa)
    d_dt = (ddt + jnp.swapaxes(ddt_t, 1, 2)).reshape(t, BLK).astype(BF16)
    pieces = [d_ga, d_gb, d_z, d_xbc.reshape(t, CONV_DIM), d_gple, d_gatt, d_dt, jnp.zeros((t, QKV_OFF - R_DT - BLK), BF16)]
    g_tables = []
    shape3 = (bsz, seq, GROUP_COLS)
    for g, (_, dil) in enumerate(DILATED_PATTERNS):
        d_qkv, dbias = _attn_bwd(rest3, biases[g], lses[g].reshape(shape3), d_outs[g].reshape(shape3), dds[g].reshape(shape3),
                                 g, dil, f"attn_bwd{g}")
        pieces.append(d_qkv.reshape(t, QKV_G))
        g_tables.append(_bias_grad(dbias, onehots[g], f"bias_grad{g}"))
    d_hcat = jnp.concatenate(pieces, axis=1)
    grad_x = _matmul(d_hcat, wp16, mode="nt", out_dtype=F32, name="grad_x", tm=1024, tn=D_MODEL, tk=768,
                     add=d_pre, add_scale=ALPHA)
    g_wp = _matmul(x16, d_hcat, mode="tn", out_dtype=BF16, name="g_w_in", tm=D_MODEL, tn=2304, tk=1024, n_outer=True)

    grads = dict(
        w_in=_unpermute_w_in(g_wp),
        b_gate=jnp.stack([dbg01[0], dbg01[1], stats[ST_BG2]]),
        conv_w=d_conv[:CONV_WIDTH],
        w_branch=jnp.concatenate([g_wa, g_wb], axis=0),
        w_out=g_w_out,
        w_ple=g_w_ple,
    )
    small_grads = dict(
        conv_b=d_conv[CONV_WIDTH],
        dt_bias=gprow[P_DTB, :SSM_HEADS] + gpcol[:SSM_HEADS, P_DTB],
        a_log=gprow[P_ALOG, :SSM_HEADS] + gpcol[:SSM_HEADS, P_ALOG],
        d_skip=gprow[P_DSKIP, :SSM_HEADS],
        ssm_norm_w=d_nw[0],
        ln_g=stats[ST_LNG],
        ln_b=stats[ST_LNB],
        rel_bias=jnp.concatenate(g_tables, axis=1),
        loss=stats[ST_LOSS],
    )
    return grad_x.reshape(bsz, seq, D_MODEL), grads, small_grads


WEIGHT_ORDER = ("w_in", "b_gate", "conv_w", "conv_b", "dt_bias", "a_log", "d_skip", "ssm_norm_w", "w_branch", "w_out",
                "w_ple", "ln_g", "ln_b", "rel_bias")
SMALL_NAMES = ("conv_b", "dt_bias", "a_log", "d_skip", "ssm_norm_w", "ln_g", "ln_b", "rel_bias")


def kernel(x, p, w_in, b_gate, conv_w, conv_b, dt_bias, a_log, d_skip, ssm_norm_w, w_branch, w_out, w_ple, ln_g, ln_b, rel_bias, loss_target, m_w_in, m_b_gate, m_conv_w, m_conv_b, m_dt_bias, m_a_log, m_d_skip, m_ssm_norm_w, m_w_branch, m_w_out, m_w_ple, m_ln_g, m_ln_b, m_rel_bias, v_w_in, v_b_gate, v_conv_w, v_conv_b, v_dt_bias, v_a_log, v_d_skip, v_ssm_norm_w, v_w_branch, v_w_out, v_w_ple, v_ln_g, v_ln_b, v_rel_bias):
    given = dict(w_in=w_in, b_gate=b_gate, conv_w=conv_w, conv_b=conv_b, dt_bias=dt_bias, a_log=a_log, d_skip=d_skip,
                 ssm_norm_w=ssm_norm_w, w_branch=w_branch, w_out=w_out, w_ple=w_ple, ln_g=ln_g, ln_b=ln_b)
    moments_m = dict(w_in=m_w_in, b_gate=m_b_gate, conv_w=m_conv_w, conv_b=m_conv_b, dt_bias=m_dt_bias, a_log=m_a_log,
                     d_skip=m_d_skip, ssm_norm_w=m_ssm_norm_w, w_branch=m_w_branch, w_out=m_w_out, w_ple=m_w_ple,
                     ln_g=m_ln_g, ln_b=m_ln_b)
    moments_v = dict(w_in=v_w_in, b_gate=v_b_gate, conv_w=v_conv_w, conv_b=v_conv_b, dt_bias=v_dt_bias, a_log=v_a_log,
                     d_skip=v_d_skip, ssm_norm_w=v_ssm_norm_w, w_branch=v_w_branch, w_out=v_w_out, w_ple=v_w_ple,
                     ln_g=v_ln_g, ln_b=v_ln_b)
    w = {k: a[0] for k, a in given.items()} | {"rel_bias": rel_bias}
    mm = {k: a[0] for k, a in moments_m.items()} | {"rel_bias": m_rel_bias}
    vv = {k: a[0] for k, a in moments_v.items()} | {"rel_bias": v_rel_bias}

    gathered = _gather_two_level(
        [w["w_in"].astype(BF16), w["w_branch"].astype(BF16), w["w_out"].astype(BF16), w["w_ple"].astype(BF16),
         _pack_cols(w["b_gate"], w["conv_w"])], "gather_weights")
    wp16 = _permute_w_in(gathered[0].transpose(1, 0, 2).reshape(D_MODEL, IN_COLS))
    wb16 = gathered[1].reshape(BRANCH_ROWS, D_MODEL)
    wo16 = gathered[2].reshape(D_MODEL, D_MODEL)
    wple16 = gathered[3].transpose(1, 0, 2).reshape(PLE_DIM, D_MODEL)
    b_gate_full, conv_w_full = _unpack_cols_all(gathered[4])
    small = {k: w[k] for k in SMALL_NAMES}

    grad_x, grads, small_grads = _local_step(x, p[0], loss_target, wp16, wb16, wo16, wple16, b_gate_full, conv_w_full, small)

    big = [grads["w_in"].astype(BF16).reshape(D_MODEL, N_DEV, IN_SHARD).transpose(1, 0, 2),
           grads["w_branch"].astype(BF16).reshape(N_DEV, BRANCH_ROWS // N_DEV, D_MODEL),
           grads["w_out"].astype(BF16).reshape(N_DEV, D_MODEL // N_DEV, D_MODEL),
           grads["w_ple"].astype(BF16).reshape(PLE_DIM, N_DEV, BLK).transpose(1, 0, 2)]
    sums = [_pair_reduce(a, f"pair_reduce{i}", rows) for i, (a, rows) in enumerate(zip(big, (128, 176, 128, 256)))]
    parts = _chip_exchange(sums, "chip_exchange")
    small_parts = _exchange([_pack_cols_all(grads["b_gate"], grads["conv_w"]), _pack_small(small_grads)], [True, False],
                            "exchange_small")

    out = {}
    out["w_in"] = _adam_reduce(parts[0], w["w_in"], mm["w_in"], vv["w_in"], "adam_w_in", 128)
    out["w_branch"] = _adam_reduce(parts[1], w["w_branch"], mm["w_branch"], vv["w_branch"], "adam_w_branch", 176)
    out["w_out"] = _adam_reduce(parts[2], w["w_out"], mm["w_out"], vv["w_out"], "adam_w_out", 128)
    out["w_ple"] = _adam_reduce(parts[3], w["w_ple"], mm["w_ple"], vv["w_ple"], "adam_w_ple", 256)
    cols = _adam_reduce(small_parts[0], _pack_cols(w["b_gate"], w["conv_w"]), _pack_cols(mm["b_gate"], mm["conv_w"]),
                        _pack_cols(vv["b_gate"], vv["conv_w"]), "adam_cols", 8)
    out["b_gate"] = [a[:3, :BLK] for a in cols]
    out["conv_w"] = [a[:CONV_WIDTH, BLK:] for a in cols]
    packed = _adam_reduce(small_parts[1], _pack_small({k: w[k] for k in SMALL_NAMES}), _pack_small({k: mm[k] for k in SMALL_NAMES}),
                          _pack_small({k: vv[k] for k in SMALL_NAMES}), "adam_small", SMALL_ROWS)
    unpacked = [_unpack_small(a) for a in packed]
    for k in SMALL_NAMES:
        out[k] = [u[k] for u in unpacked]
    loss_rows = unpacked[0]["loss"].reshape(D_MODEL // BLK, BLK)
    loss = _lane_total(loss_rows).reshape(())

    def shaped(k, a):
        return a if k == "rel_bias" else a[None]

    results = [loss, grad_x]
    for i in range(4):
        results += [shaped(k, out[k][i]) for k in WEIGHT_ORDER]
    return tuple(results)
```

```python
import functools
import math

import jax
import jax.numpy as jnp
from jax import lax
from jax.experimental import pallas as pl
from jax.experimental.pallas import tpu as pltpu

F32 = jnp.float32
BF16 = jnp.bfloat16

N_DEV = 8
D_MODEL = 1024
SEQ = 2048
HEAD_DIM = 64
HEADS_PER_GROUP = 12
DILATED_PATTERNS = ((128, 1), (512, 4), (2048, 16))
N_GROUPS = 3
ATT_HEADS = N_GROUPS * HEADS_PER_GROUP
GROUP_COLS = HEADS_PER_GROUP * HEAD_DIM
ATT_QKV = ATT_HEADS * HEAD_DIM
BLK = 128
NUM_BUCKETS = 32
MAX_DISTANCE = 2048
D_INNER = 2048
SSM_HEADS = 32
SSM_GROUPS = 4
HEADS_PER_SSM_GROUP = SSM_HEADS // SSM_GROUPS
D_STATE = 128
CONV_WIDTH = 4
CONV_DIM = D_INNER + 2 * SSM_GROUPS * D_STATE
PLE_DIM = 256
ALPHA = 2.0 ** 0.25
LN_EPS = 1e-5
RMS_EPS = 1e-5
IN_COLS = 15904
IN_SHARD = IN_COLS // N_DEV
BRANCH_ROWS = GROUP_COLS + D_INNER

Q_END = ATT_QKV
K_END = 2 * ATT_QKV
V_END = 3 * ATT_QKV
GATT_END = V_END + GROUP_COLS
Z_END = GATT_END + D_INNER
XBC_END = Z_END + CONV_DIM
DT_END = XBC_END + SSM_HEADS
GMERGE_END = DT_END + 2 * D_MODEL

R_GM, R_Z, R_XBC, R_GPLE, R_GATT, R_DT = 0, 2048, 4096, 7168, 8192, 8960
QKV_OFF = 9216
HEAD_PAIRS = GROUP_COLS // BLK
QKV_G = 3 * GROUP_COLS
HCAT_COLS = QKV_OFF + N_GROUPS * QKV_G

ADAM_LR, ADAM_B1, ADAM_B2, ADAM_EPS, ADAM_WD, ADAM_STEP = 0.001, 0.9, 0.999, 1e-08, 0.01, 10

VMEM_LIMIT_BYTES = 56 * 1024 * 1024


def _params(*semantics):
    return pltpu.CompilerParams(dimension_semantics=semantics, vmem_limit_bytes=VMEM_LIMIT_BYTES)


def _sigmoid(v):
    return 1.0 / (1.0 + jnp.exp(-v))


def _silu_and_grad(v):
    s = _sigmoid(v)
    return v * s, s * (1.0 + v * (1.0 - s))


def _matmul(a, b, *, mode, out_dtype, name, tm, tn, tk, n_off=0, n=None, add=None, add_scale=1.0, n_outer=False, after=None):
    if mode == "nn":
        (m, k), n_full = a.shape, b.shape[1]
        assert b.shape[0] == k
    elif mode == "nt":
        (m, k), n_full = a.shape, b.shape[0]
        assert b.shape[1] == k
    else:
        (k, m), n_full = a.shape, b.shape[1]
        assert b.shape[0] == k
    n = n_full if n is None else n
    assert m % tm == 0 and n % tn == 0 and k % tk == 0 and n_off % tn == 0, (name, m, n, k)
    nk = k // tk
    jo = n_off // tn
    dims = {"nn": (((1,), (0,)), ((), ())), "nt": (((1,), (1,)), ((), ())), "tn": (((0,), (0,)), ((), ()))}[mode]

    def body(*refs):
        a_ref, b_ref = refs[:2]
        add_ref = refs[2] if add is not None else None
        o_ref = refs[2 + (add is not None) + (after is not None)]
        acc_ref = refs[-1] if nk > 1 else None
        prod = lax.dot_general(a_ref[...].astype(BF16), b_ref[...].astype(BF16), dims, preferred_element_type=F32)

        def finish(total):
            if add_ref is not None:
                total = total + add_scale * add_ref[...]
            o_ref[...] = total.astype(out_dtype)

        if nk == 1:
            finish(prod)
        else:
            kk = pl.program_id(2)

            @pl.when(kk == 0)
            def _():
                acc_ref[...] = prod

            @pl.when(jnp.logical_and(kk > 0, kk < nk - 1))
            def _():
                acc_ref[...] += prod

            @pl.when(kk == nk - 1)
            def _():
                finish(acc_ref[...] + prod)

    def ij(f):
        return (lambda g0, g1, kk: f(g1, g0, kk)) if n_outer else f

    if mode == "nn":
        a_spec = pl.BlockSpec((tm, tk), ij(lambda i, j, kk: (i, kk)))
        b_spec = pl.BlockSpec((tk, tn), ij(lambda i, j, kk: (kk, j + jo)))
    elif mode == "nt":
        a_spec = pl.BlockSpec((tm, tk), ij(lambda i, j, kk: (i, kk)))
        b_spec = pl.BlockSpec((tn, tk), ij(lambda i, j, kk: (j, kk)))
    else:
        a_spec = pl.BlockSpec((tk, tm), ij(lambda i, j, kk: (kk, i)))
        b_spec = pl.BlockSpec((tk, tn), ij(lambda i, j, kk: (kk, j)))
    in_specs = [a_spec, b_spec]
    args = [a, b]
    if add is not None:
        in_specs.append(pl.BlockSpec((tm, tn), ij(lambda i, j, kk: (i, j))))
        args.append(add)
    if after is not None:
        in_specs.append(pl.BlockSpec((8, BLK), lambda g0, g1, kk: (0, 0)))
        args.append(after)
    return pl.pallas_call(
        body, name=name,
        grid=(n // tn, m // tm, nk) if n_outer else (m // tm, n // tn, nk),
        in_specs=in_specs,
        out_specs=pl.BlockSpec((tm, tn), ij(lambda i, j, kk: (i, j))),
        out_shape=jax.ShapeDtypeStruct((m, n), out_dtype),
        scratch_shapes=[pltpu.VMEM((tm, tn), F32)] if nk > 1 else [],
        compiler_params=_params("parallel", "parallel", "arbitrary"),
    )(*args)


UNITS_PER_ITER = 4


def _band_mask(first):
    qi = lax.broadcasted_iota(jnp.int32, (BLK, BLK if first else 2 * BLK), 0)
    kj = lax.broadcasted_iota(jnp.int32, (BLK, BLK if first else 2 * BLK), 1)
    delta = qi - kj if first else qi + BLK - kj
    return jnp.logical_and(delta >= 0, delta <= BLK)


def _attn_specs(seq, g):
    qkv = [pl.BlockSpec((1, seq, BLK), functools.partial(
        lambda hp, b, part: (b, 0, QKV_OFF // BLK + (g * HEAD_PAIRS + hp) * 3 + part), part=part)) for part in range(3)]
    one = pl.BlockSpec((1, seq, BLK), lambda hp, b: (b, 0, hp))
    bias = pl.BlockSpec((2, BLK, 2 * BLK), lambda hp, b: (hp, 0, 0))
    return qkv, one, bias


def _attn_rows(dil, r, n, first):
    start = r + (dil * BLK) * n
    if dil == 1:
        start = pl.multiple_of(start, BLK)
        rows = pl.ds(start, BLK)
        keys = rows if first else pl.ds(pl.multiple_of(start - BLK, BLK), 2 * BLK)
    else:
        rows = pl.ds(start, BLK, stride=dil)
        keys = rows if first else pl.ds(start - dil * BLK, 2 * BLK, stride=dil)
    return rows, keys


def _attn_schedule(dil, nb, unit):
    def blocks_of(r):
        unit(r, 0, True)
        for n in range(1, UNITS_PER_ITER):
            unit(r, n, False)
        if nb > UNITS_PER_ITER:
            def more(i, carry):
                for jj in range(UNITS_PER_ITER):
                    unit(r, i * UNITS_PER_ITER + jj, False)
                return carry
            lax.fori_loop(1, nb // UNITS_PER_ITER, more, 0)

    if nb >= UNITS_PER_ITER:
        assert nb % UNITS_PER_ITER == 0
        if dil == 1:
            blocks_of(0)
        else:
            def per_residue(r, carry):
                blocks_of(r)
                return carry
            lax.fori_loop(0, dil, per_residue, 0)
    else:
        per_iter = UNITS_PER_ITER // nb
        assert UNITS_PER_ITER % nb == 0 and dil % per_iter == 0

        def residues(i, carry):
            for jj in range(per_iter):
                for n in range(nb):
                    unit(i * per_iter + jj, n, n == 0)
            return carry
        lax.fori_loop(0, dil // per_iter, residues, 0)


def _attn_fwd(hcat3, bias, g, dil, name):
    bsz, seq, _ = hcat3.shape
    nb = seq // dil // BLK
    scale = HEAD_DIM ** -0.5

    def body(q_ref, k_ref, v_ref, bias_ref, o_ref, lse_ref):
        masks = {True: _band_mask(True), False: _band_mask(False)}

        def unit(r, n, first):
            rows, keys = _attn_rows(dil, r, n, first)
            q2 = q_ref[0, rows, :].astype(BF16)
            k2 = k_ref[0, keys, :].astype(BF16)
            v2 = v_ref[0, keys, :].astype(BF16)
            outs, lses = [], []
            for j in range(2):
                lanes = slice(j * HEAD_DIM, (j + 1) * HEAD_DIM)
                bias_j = bias_ref[j, :, BLK:] if first else bias_ref[j]
                s = _nt(q2[:, lanes], k2[:, lanes]) * scale + bias_j
                s = jnp.where(masks[first], s, -jnp.inf)
                mx = jnp.max(s, axis=-1, keepdims=True)
                e = jnp.exp(s - mx)
                den = jnp.sum(e, axis=-1, keepdims=True)
                outs.append(jnp.dot(e.astype(BF16), v2[:, lanes], preferred_element_type=F32) / den)
                lses.append(jnp.broadcast_to(mx + jnp.log(den), (BLK, HEAD_DIM)))
            o_ref[0, rows, :] = jnp.concatenate(outs, axis=1)
            lse_ref[0, rows, :] = jnp.concatenate(lses, axis=1)

        _attn_schedule(dil, nb, unit)

    qkv_specs, one, bias_spec = _attn_specs(seq, g)
    shape = jax.ShapeDtypeStruct((bsz, seq, GROUP_COLS), F32)
    return pl.pallas_call(
        body, name=name,
        grid=(HEAD_PAIRS, bsz),
        in_specs=qkv_specs + [bias_spec],
        out_specs=[one, one],
        out_shape=[shape, shape],
        compiler_params=_params("parallel", "parallel"),
    )(hcat3, hcat3, hcat3, bias)


def _attn_bwd(hcat3, bias, lse, d_out, dd, g, dil, name):
    bsz, seq, _ = hcat3.shape
    nb = seq // dil // BLK
    scale = HEAD_DIM ** -0.5

    def body(q_ref, k_ref, v_ref, bias_ref, lse_ref, do_ref, dd_ref, dqkv_ref, dbias_ref, dq_acc, dk_acc, dv_acc):
        @pl.when(pl.program_id(1) == 0)
        def _():
            dbias_ref[...] = jnp.zeros_like(dbias_ref)

        dk_acc[...] = jnp.zeros_like(dk_acc)
        dv_acc[...] = jnp.zeros_like(dv_acc)
        masks = {True: _band_mask(True), False: _band_mask(False)}

        def unit(r, n, first):
            rows, keys = _attn_rows(dil, r, n, first)
            q2 = q_ref[0, rows, :].astype(BF16)
            k2 = k_ref[0, keys, :].astype(BF16)
            v2 = v_ref[0, keys, :].astype(BF16)
            do2 = do_ref[0, rows, :].astype(BF16)
            lse2 = lse_ref[0, rows, :]
            dd2 = dd_ref[0, rows, :]
            dqs, dks, dvs = [], [], []
            for j in range(2):
                lanes = slice(j * HEAD_DIM, (j + 1) * HEAD_DIM)
                q, kb, vb, do = q2[:, lanes], k2[:, lanes], v2[:, lanes], do2[:, lanes]
                delta = jnp.sum(dd2[:, lanes], axis=-1, keepdims=True)
                bias_j = bias_ref[j, :, BLK:] if first else bias_ref[j]
                s = _nt(q, kb) * scale + bias_j
                p = jnp.where(masks[first], jnp.exp(s - lse2[:, j * HEAD_DIM:j * HEAD_DIM + 1]), 0.0)
                ds = p * (_nt(do, vb) - delta)
                ds16 = ds.astype(BF16)
                dqs.append(jnp.dot(ds16, kb, preferred_element_type=F32) * scale)
                dks.append(_tn(ds16, q) * scale)
                dvs.append(_tn(p.astype(BF16), do))
                if first:
                    dbias_ref[j, :, BLK:] += ds
                else:
                    dbias_ref[j] += ds
            dq_acc[rows, :] = jnp.concatenate(dqs, axis=1)
            dk_acc[keys, :] += jnp.concatenate(dks, axis=1)
            dv_acc[keys, :] += jnp.concatenate(dvs, axis=1)

        _attn_schedule(dil, nb, unit)
        dqkv_ref[0, :, 0:BLK] = dq_acc[...].astype(BF16)
        dqkv_ref[0, :, BLK:2 * BLK] = dk_acc[...].astype(BF16)
        dqkv_ref[0, :, 2 * BLK:3 * BLK] = dv_acc[...].astype(BF16)

    qkv_specs, one, bias_spec = _attn_specs(seq, g)
    return pl.pallas_call(
        body, name=name,
        grid=(HEAD_PAIRS, bsz),
        in_specs=qkv_specs + [bias_spec, one, one, one],
        out_specs=[pl.BlockSpec((1, seq, 3 * BLK), lambda hp, b: (b, 0, hp)), bias_spec],
        out_shape=[jax.ShapeDtypeStruct((bsz, seq, QKV_G), BF16), jax.ShapeDtypeStruct((HEADS_PER_GROUP, BLK, 2 * BLK), F32)],
        scratch_shapes=[pltpu.VMEM((seq, BLK), F32)] * 3,
        compiler_params=_params("arbitrary", "arbitrary"),
    )(hcat3, hcat3, hcat3, bias, lse, d_out, dd)


def _t5_buckets(dil):
    import numpy as np
    qi = np.arange(BLK)[:, None]
    kj = np.arange(2 * BLK)[None, :]
    dist = np.maximum(qi + BLK - kj, 0) * dil
    max_exact = NUM_BUCKETS // 2
    d_f = np.maximum(dist, 1).astype(np.float32)
    large = max_exact + (np.log(d_f / np.float32(max_exact)) / np.float32(math.log(MAX_DISTANCE / max_exact))
                         * np.float32(NUM_BUCKETS - max_exact)).astype(np.int32)
    large = np.minimum(large, NUM_BUCKETS - 1)
    return np.where(dist < max_exact, dist, large).astype(np.int32).reshape(-1)


def _bias_matrix(table_g, dil):
    buckets = jnp.asarray(_t5_buckets(dil))
    onehot = (buckets[None, :] == lax.broadcasted_iota(jnp.int32, (NUM_BUCKETS, 1), 0)).astype(F32)
    tk = 4096

    def body(t_ref, oh_ref, o_ref):
        o_ref[...] = _dot_hi(t_ref[...], oh_ref[...])

    bias = pl.pallas_call(
        body, name=f"bias_matrix{dil}",
        grid=(onehot.shape[1] // tk,),
        in_specs=[pl.BlockSpec((HEADS_PER_GROUP, NUM_BUCKETS), lambda kk: (0, 0)), pl.BlockSpec((NUM_BUCKETS, tk), lambda kk: (0, kk))],
        out_specs=pl.BlockSpec((HEADS_PER_GROUP, tk), lambda kk: (0, kk)),
        out_shape=jax.ShapeDtypeStruct((HEADS_PER_GROUP, onehot.shape[1]), F32),
        compiler_params=_params("parallel"),
    )(table_g.T, onehot)
    return bias.reshape(HEADS_PER_GROUP, BLK, 2 * BLK), onehot


def _bias_grad(dbias, onehot, name):
    flat = dbias.reshape(HEADS_PER_GROUP, 2 * BLK * BLK)
    tk = 4096

    def body(oh_ref, g_ref, o_ref):
        @pl.when(pl.program_id(0) == 0)
        def _():
            o_ref[...] = jnp.zeros_like(o_ref)
        o_ref[...] += lax.dot_general(oh_ref[...], g_ref[...], (((1,), (1,)), ((), ())),
                                      preferred_element_type=F32, precision=lax.Precision.HIGHEST)

    return pl.pallas_call(
        body, name=name,
        grid=(flat.shape[1] // tk,),
        in_specs=[pl.BlockSpec((NUM_BUCKETS, tk), lambda kk: (0, kk)), pl.BlockSpec((HEADS_PER_GROUP, tk), lambda kk: (0, kk))],
        out_specs=pl.BlockSpec((NUM_BUCKETS, HEADS_PER_GROUP), lambda kk: (0, 0)),
        out_shape=jax.ShapeDtypeStruct((NUM_BUCKETS, HEADS_PER_GROUP), F32),
        compiler_params=_params("arbitrary"),
    )(onehot, flat)


def _shift_rows(u, s, down):
    if s == 0:
        return u
    rows = lax.broadcasted_iota(jnp.int32, u.shape, 0)
    n = u.shape[0]
    if down:
        return jnp.where(rows >= s, pltpu.roll(u, s, 0), 0.0)
    return jnp.where(rows < n - s, pltpu.roll(u, n - s, 0), 0.0)


def _conv_pre(u, w_ref, b_ref):
    acc = b_ref[0:1, :] + w_ref[CONV_WIDTH - 1:CONV_WIDTH, :] * u
    for s in range(1, CONV_WIDTH):
        acc = acc + w_ref[CONV_WIDTH - 1 - s:CONV_WIDTH - s, :] * _shift_rows(u, s, True)
    return acc


def _conv_fwd(rest3, conv_w, conv_b):
    bsz, seq, _ = rest3.shape

    def body(u_ref, w_ref, b_ref, o_ref):
        pre = _conv_pre(u_ref[0], w_ref, b_ref)
        o_ref[0] = pre * _sigmoid(pre)

    return pl.pallas_call(
        body, name="conv_fwd",
        grid=(bsz, CONV_DIM // BLK),
        in_specs=[pl.BlockSpec((1, seq, BLK), lambda b, c: (b, 0, R_XBC // BLK + c)),
                  pl.BlockSpec((CONV_WIDTH, BLK), lambda b, c: (0, c)),
                  pl.BlockSpec((1, BLK), lambda b, c: (0, c))],
        out_specs=pl.BlockSpec((1, seq, BLK), lambda b, c: (b, 0, c)),
        out_shape=jax.ShapeDtypeStruct((bsz, seq, CONV_DIM), F32),
        compiler_params=_params("parallel", "parallel"),
    )(rest3, conv_w, conv_b)


def _conv_bwd(rest3, conv_w, conv_b, d_act):
    bsz, seq, _ = rest3.shape

    def body(u_ref, w_ref, b_ref, g_ref, du_ref, dw_ref):
        @pl.when(pl.program_id(1) == 0)
        def _():
            dw_ref[...] = jnp.zeros_like(dw_ref)

        u = u_ref[0]
        pre = _conv_pre(u, w_ref, b_ref)
        d_pre = g_ref[0] * _silu_and_grad(pre)[1]
        du = w_ref[CONV_WIDTH - 1:CONV_WIDTH, :] * d_pre
        dw_ref[CONV_WIDTH - 1:CONV_WIDTH, :] += jnp.sum(d_pre * u, axis=0, keepdims=True)
        for s in range(1, CONV_WIDTH):
            du = du + w_ref[CONV_WIDTH - 1 - s:CONV_WIDTH - s, :] * _shift_rows(d_pre, s, False)
            dw_ref[CONV_WIDTH - 1 - s:CONV_WIDTH - s, :] += jnp.sum(d_pre * _shift_rows(u, s, True), axis=0, keepdims=True)
        dw_ref[CONV_WIDTH:CONV_WIDTH + 1, :] += jnp.sum(d_pre, axis=0, keepdims=True)
        du_ref[0] = du.astype(BF16)

    return pl.pallas_call(
        body, name="conv_bwd",
        grid=(CONV_DIM // BLK, bsz),
        in_specs=[pl.BlockSpec((1, seq, BLK), lambda c, b: (b, 0, R_XBC // BLK + c)),
                  pl.BlockSpec((CONV_WIDTH, BLK), lambda c, b: (0, c)),
                  pl.BlockSpec((1, BLK), lambda c, b: (0, c)),
                  pl.BlockSpec((1, seq, BLK), lambda c, b: (b, 0, c))],
        out_specs=[pl.BlockSpec((1, seq, BLK), lambda c, b: (b, 0, c)),
                   pl.BlockSpec((8, BLK), lambda c, b: (0, c))],
        out_shape=[jax.ShapeDtypeStruct((bsz, seq, CONV_DIM), BF16), jax.ShapeDtypeStruct((8, CONV_DIM), F32)],
        compiler_params=_params("parallel", "arbitrary"),
    )(rest3, conv_w, conv_b, d_act)


P_DTB, P_ALOG, P_DSKIP = 0, 1, 2


def _softplus(v):
    return jnp.maximum(v, 0.0) + jnp.log(1.0 + jnp.exp(-jnp.abs(v)))


def _dot_hi(a, b):
    return jnp.dot(a, b, preferred_element_type=F32, precision=lax.Precision.HIGHEST)


def _nt(a, b):
    return lax.dot_general(a, b, (((1,), (1,)), ((), ())), preferred_element_type=F32)


def _tn(a, b):
    return lax.dot_general(a, b, (((0,), (0,)), ((), ())), preferred_element_type=F32)


def _sum_all(v):
    return jnp.sum(jnp.sum(v, axis=0, keepdims=True), axis=1, keepdims=True)


def _ssd_decays(dtr, dtr_t, prow_ref, pcol_ref):
    ri = lax.broadcasted_iota(jnp.int32, (BLK, BLK), 0)
    ci = lax.broadcasted_iota(jnp.int32, (BLK, BLK), 1)
    tri = (ri >= ci).astype(F32)
    tri_u = (ri <= ci).astype(F32)
    pre = dtr + prow_ref[P_DTB:P_DTB + 1, :]
    dt = _softplus(pre)
    ah_row = -jnp.exp(prow_ref[P_ALOG:P_ALOG + 1, :])
    acs = _dot_hi(tri, dt * ah_row)
    pre_t = dtr_t + pcol_ref[:, P_DTB:P_DTB + 1]
    dt_t = _softplus(pre_t)
    ah_col = -jnp.exp(pcol_ref[:, P_ALOG:P_ALOG + 1])
    acs_t = _dot_hi(dt_t * ah_col, tri_u)
    return dict(tri=tri, tri_u=tri_u, pre=pre, dt=dt, ah_row=ah_row, acs=acs, pre_t=pre_t, dt_t=dt_t, ah_col=ah_col,
                acs_t=acs_t, causal=ri >= ci, last_row=ri[:, 0:1] == BLK - 1)


def _ssd_head(h, d, x_ref, s_in, g_mat):
    col = d["acs"][:, h:h + 1]
    row = d["acs_t"][h:h + 1, :]
    lm = jnp.exp(jnp.where(d["causal"], col - row, -jnp.inf))
    m = g_mat * lm
    xh = x_ref[0, :, pl.ds(h * HEAD_DIM, HEAD_DIM)]
    dtc = d["dt"][:, h:h + 1]
    xd = xh * dtc
    e = jnp.exp(col)
    clast = d["acs"][BLK - 1:BLK, h:h + 1]
    f = jnp.exp(clast - col)
    return dict(col=col, lm=lm, m=m, xh=xh, dtc=dtc, xd=xd, e=e, ecl=jnp.exp(clast), f=f, xf=xd * f)


def _ssd_specs(nc, rev):
    cidx = (lambda c: nc - 1 - c) if rev else (lambda c: c)
    x_spec = pl.BlockSpec((1, BLK, D_INNER), lambda b, c: (b, cidx(c), 0))
    bm_spec = pl.BlockSpec((1, BLK, SSM_GROUPS * D_STATE), lambda b, c: (b, cidx(c), D_INNER // (SSM_GROUPS * D_STATE)))
    cm_spec = pl.BlockSpec((1, BLK, SSM_GROUPS * D_STATE), lambda b, c: (b, cidx(c), D_INNER // (SSM_GROUPS * D_STATE) + 1))
    dt_spec = pl.BlockSpec((1, BLK, BLK), lambda b, c: (b, cidx(c), R_DT // BLK))
    dtt_spec = pl.BlockSpec((1, BLK, BLK), lambda b, c: (b, 0, cidx(c)))
    prow_spec = pl.BlockSpec((8, BLK), lambda b, c: (0, 0))
    pcol_spec = pl.BlockSpec((BLK, 8), lambda b, c: (0, 0))
    st_spec = pl.BlockSpec((1, 1, SSM_HEADS, HEAD_DIM, D_STATE), lambda b, c: (b, cidx(c), 0, 0, 0))
    y_spec = pl.BlockSpec((1, BLK, D_INNER), lambda b, c: (b, cidx(c), 0))
    return x_spec, bm_spec, cm_spec, dt_spec, dtt_spec, prow_spec, pcol_spec, st_spec, y_spec


def _ssd_fwd(xa, rest3, dtr_t, prow, pcol):
    bsz, seq, _ = xa.shape
    nc = seq // BLK

    def body(x_ref, bm_ref, cm_ref, dt_ref, dtt_ref, prow_ref, pcol_ref, y_ref, sin_ref, s_ref):
        @pl.when(pl.program_id(1) == 0)
        def _():
            s_ref[...] = jnp.zeros_like(s_ref)

        d = _ssd_decays(dt_ref[0], dtt_ref[0], prow_ref, pcol_ref)
        for g in range(SSM_GROUPS):
            lanes = pl.ds(g * D_STATE, D_STATE)
            bg = bm_ref[0, :, lanes].astype(BF16)
            cg = cm_ref[0, :, lanes].astype(BF16)
            g_mat = _nt(cg, bg)
            for hh in range(HEADS_PER_SSM_GROUP):
                h = g * HEADS_PER_SSM_GROUP + hh
                s_in = s_ref[h]
                sin_ref[0, 0, h] = s_in
                q = _ssd_head(h, d, x_ref, s_in, g_mat)
                y_diag = jnp.dot(q["m"].astype(BF16), q["xd"].astype(BF16), preferred_element_type=F32)
                y_off = _nt(cg, s_in.astype(BF16)) * q["e"]
                s_ref[h] = s_in * q["ecl"] + _tn(q["xf"].astype(BF16), bg)
                y_ref[0, :, pl.ds(h * HEAD_DIM, HEAD_DIM)] = y_diag + y_off + prow_ref[P_DSKIP:P_DSKIP + 1, h:h + 1] * q["xh"]

    x_spec, bm_spec, cm_spec, dt_spec, dtt_spec, prow_spec, pcol_spec, st_spec, y_spec = _ssd_specs(nc, False)
    return pl.pallas_call(
        body, name="ssd_fwd",
        grid=(bsz, nc),
        in_specs=[x_spec, bm_spec, cm_spec, dt_spec, dtt_spec, prow_spec, pcol_spec],
        out_specs=[y_spec, st_spec],
        out_shape=[jax.ShapeDtypeStruct((bsz, seq, D_INNER), F32),
                   jax.ShapeDtypeStruct((bsz, nc, SSM_HEADS, HEAD_DIM, D_STATE), F32)],
        scratch_shapes=[pltpu.VMEM((SSM_HEADS, HEAD_DIM, D_STATE), F32)],
        compiler_params=_params("parallel", "arbitrary"),
    )(xa, xa, xa, rest3, dtr_t, prow, pcol)


def _ssd_bwd(xa, rest3, dtr_t, prow, pcol, s_in_all, dy):
    bsz, seq, _ = xa.shape
    nc = seq // BLK

    def body(x_ref, bm_ref, cm_ref, dt_ref, dtt_ref, prow_ref, pcol_ref, sin_ref, dy_ref,
             dxa_ref, ddt_ref, ddtt_ref, gprow_ref, gpcol_ref, ds_ref, dc_ref, ddtc_ref, drt_ref):
        first = jnp.logical_and(pl.program_id(0) == 0, pl.program_id(1) == 0)

        @pl.when(first)
        def _():
            gprow_ref[...] = jnp.zeros_like(gprow_ref)
            gpcol_ref[...] = jnp.zeros_like(gpcol_ref)

        @pl.when(pl.program_id(1) == 0)
        def _():
            ds_ref[...] = jnp.zeros_like(ds_ref)

        dc_ref[...] = jnp.zeros_like(dc_ref)
        ddtc_ref[...] = jnp.zeros_like(ddtc_ref)
        drt_ref[...] = jnp.zeros_like(drt_ref)
        d = _ssd_decays(dt_ref[0], dtt_ref[0], prow_ref, pcol_ref)
        for g in range(SSM_GROUPS):
            lanes = pl.ds(g * D_STATE, D_STATE)
            bg = bm_ref[0, :, lanes].astype(BF16)
            cg = cm_ref[0, :, lanes].astype(BF16)
            g_mat = _nt(cg, bg)
            d_g = jnp.zeros((BLK, BLK), F32)
            d_bg = jnp.zeros((BLK, D_STATE), F32)
            d_cg = jnp.zeros((BLK, D_STATE), F32)
            for hh in range(HEADS_PER_SSM_GROUP):
                h = g * HEADS_PER_SSM_GROUP + hh
                head_lanes = pl.ds(h * HEAD_DIM, HEAD_DIM)
                s_in = sin_ref[0, 0, h]
                s_in16 = s_in.astype(BF16)
                q = _ssd_head(h, d, x_ref, s_in, g_mat)
                m16, xd16 = q["m"].astype(BF16), q["xd"].astype(BF16)
                d_y = dy_ref[0, :, head_lanes]
                d_y16 = d_y.astype(BF16)
                d_so = ds_ref[h]
                d_so16 = d_so.astype(BF16)
                d_x = prow_ref[P_DSKIP:P_DSKIP + 1, h:h + 1] * d_y
                gprow_ref[P_DSKIP:P_DSKIP + 1, h:h + 1] += _sum_all(d_y * q["xh"])
                d_m = _nt(d_y16, xd16)
                d_xd = _tn(m16, d_y16)
                w = d_m * q["m"]
                d_g = d_g + d_m * q["lm"]
                d_col = jnp.sum(w, axis=1, keepdims=True)
                drt_ref[h:h + 1, :] = -jnp.sum(w, axis=0, keepdims=True)
                qmat = _nt(cg, s_in16)
                d_q16 = (d_y * q["e"]).astype(BF16)
                d_col = d_col + jnp.sum(d_y * qmat, axis=1, keepdims=True) * q["e"]
                d_cg = d_cg + jnp.dot(d_q16, s_in16, preferred_element_type=F32)
                d_sin = _tn(d_q16, cg) + d_so * q["ecl"]
                d_clast = _sum_all(d_so * s_in) * q["ecl"]
                d_xf = _nt(bg, d_so16)
                d_bg = d_bg + jnp.dot(q["xf"].astype(BF16), d_so16, preferred_element_type=F32)
                d_xd = d_xd + d_xf * q["f"]
                d_f = jnp.sum(d_xf * q["xd"], axis=1, keepdims=True) * q["f"]
                d_clast = d_clast + jnp.sum(d_f, axis=0, keepdims=True)
                d_col = d_col - d_f + jnp.where(d["last_row"], d_clast, 0.0)
                dxa_ref[0, :, head_lanes] = d_x + d_xd * q["dtc"]
                dc_ref[:, h:h + 1] = d_col
                ddtc_ref[:, h:h + 1] = jnp.sum(d_xd * q["xh"], axis=1, keepdims=True)
                ds_ref[h] = d_sin
            d_g16 = d_g.astype(BF16)
            dxa_ref[0, :, pl.ds(D_INNER + g * D_STATE, D_STATE)] = d_bg + _tn(d_g16, cg)
            dxa_ref[0, :, pl.ds(D_INNER + (SSM_GROUPS + g) * D_STATE, D_STATE)] = d_cg + jnp.dot(d_g16, bg, preferred_element_type=F32)
        d_a = _dot_hi(d["tri_u"], dc_ref[...])
        d_pre = (ddtc_ref[...] + d_a * d["ah_row"]) * _sigmoid(d["pre"])
        ddt_ref[0] = d_pre
        gprow_ref[P_DTB:P_DTB + 1, :] += jnp.sum(d_pre, axis=0, keepdims=True)
        gprow_ref[P_ALOG:P_ALOG + 1, :] += jnp.sum(d_a * d["dt"], axis=0, keepdims=True) * d["ah_row"]
        d_at = _dot_hi(drt_ref[...], d["tri"])
        d_pre_t = d_at * d["ah_col"] * _sigmoid(d["pre_t"])
        ddtt_ref[0] = d_pre_t
        gpcol_ref[:, P_DTB:P_DTB + 1] += jnp.sum(d_pre_t, axis=1, keepdims=True)
        gpcol_ref[:, P_ALOG:P_ALOG + 1] += jnp.sum(d_at * d["dt_t"], axis=1, keepdims=True) * d["ah_col"]

    x_spec, bm_spec, cm_spec, dt_spec, dtt_spec, prow_spec, pcol_spec, st_spec, y_spec = _ssd_specs(nc, True)
    return pl.pallas_call(
        body, name="ssd_bwd",
        grid=(bsz, nc),
        in_specs=[x_spec, bm_spec, cm_spec, dt_spec, dtt_spec, prow_spec, pcol_spec, st_spec, y_spec],
        out_specs=[pl.BlockSpec((1, BLK, CONV_DIM), lambda b, c: (b, nc - 1 - c, 0)),
                   pl.BlockSpec((1, BLK, BLK), lambda b, c: (b, nc - 1 - c, 0)),
                   pl.BlockSpec((1, BLK, BLK), lambda b, c: (b, 0, nc - 1 - c)),
                   prow_spec, pcol_spec],
        out_shape=[jax.ShapeDtypeStruct((bsz, seq, CONV_DIM), F32),
                   jax.ShapeDtypeStruct((bsz, seq, BLK), F32),
                   jax.ShapeDtypeStruct((bsz, BLK, seq), F32),
                   jax.ShapeDtypeStruct((8, BLK), F32),
                   jax.ShapeDtypeStruct((BLK, 8), F32)],
        scratch_shapes=[pltpu.VMEM((SSM_HEADS, HEAD_DIM, D_STATE), F32), pltpu.VMEM((BLK, BLK), F32),
                        pltpu.VMEM((BLK, BLK), F32), pltpu.VMEM((BLK, BLK), F32)],
        compiler_params=_params("arbitrary", "arbitrary"),
    )(xa, xa, xa, rest3, dtr_t, prow, pcol, s_in_all, dy)


GROUP_W = HEADS_PER_SSM_GROUP * HEAD_DIM


def _select_matrix(shape, g, head_axis, per_head):
    h = lax.broadcasted_iota(jnp.int32, shape, head_axis)
    j = lax.broadcasted_iota(jnp.int32, shape, 1 - head_axis)
    return (h == g * HEADS_PER_SSM_GROUP + lax.shift_right_logical(j, per_head.bit_length() - 1)).astype(BF16)


def _split16(v, terms):
    parts, rem = [], v
    for _ in range(terms):
        p = rem.astype(BF16)
        parts.append(p)
        rem = rem - p.astype(F32)
    return parts


def _sel_dot(a, b, terms=2):
    if a.dtype == BF16:
        return sum(jnp.dot(a, p, preferred_element_type=F32) for p in _split16(b, terms))
    return sum(jnp.dot(p, b, preferred_element_type=F32) for p in _split16(a, terms))


def _ssd_group(g, d, e_all, f_all, ecl_b, x_ref, prow_ref):
    spread = _select_matrix((BLK, GROUP_W), g, 0, HEAD_DIM)
    gather = _select_matrix((GROUP_W, BLK), g, 1, HEAD_DIM)
    xg = x_ref[0, :, pl.ds(g * GROUP_W, GROUP_W)]
    dt_g = _sel_dot(d["dt"], spread)
    e_g = _sel_dot(e_all, spread)
    f_g = _sel_dot(f_all, spread)
    dsk_g = _sel_dot(prow_ref[...], spread, 3)[P_DSKIP:P_DSKIP + 1, :]
    sc_g = _sel_dot(gather, ecl_b, 3)
    xd = xg * dt_g
    return dict(spread=spread, gather=gather, xg=xg, dt_g=dt_g, e_g=e_g, f_g=f_g, dsk_g=dsk_g, sc_g=sc_g, xd=xd,
                xd16=xd.astype(BF16), xf16=(xd * f_g).astype(BF16))


def _ssd_common(d):
    e_all = jnp.exp(d["acs"])
    f_all = jnp.exp(d["acs"][BLK - 1:BLK, :] - d["acs"])
    ecl_b = jnp.broadcast_to(jnp.exp(d["acs_t"][:, BLK - 1:BLK]), (BLK, BLK))
    return e_all, f_all, ecl_b


def _ssd_mask_decay(d, h, g_mat):
    col = d["acs"][:, h:h + 1]
    row = d["acs_t"][h:h + 1, :]
    lm = jnp.exp(jnp.where(d["causal"], col - row, -jnp.inf))
    return lm, g_mat * lm


def _ssd_state_spec(nc, rev):
    cidx = (lambda c: nc - 1 - c) if rev else (lambda c: c)
    return pl.BlockSpec((1, 1, SSM_GROUPS, GROUP_W, D_STATE), lambda b, c: (b, cidx(c), 0, 0, 0))


def _ssd_fwd(xa, rest3, dtr_t, prow, pcol):
    bsz, seq, _ = xa.shape
    nc = seq // BLK

    def body(x_ref, bm_ref, cm_ref, dt_ref, dtt_ref, prow_ref, pcol_ref, y_ref, sin_ref, s_ref):
        @pl.when(pl.program_id(1) == 0)
        def _():
            s_ref[...] = jnp.zeros_like(s_ref)

        d = _ssd_decays(dt_ref[0], dtt_ref[0], prow_ref, pcol_ref)
        e_all, f_all, ecl_b = _ssd_common(d)
        for g in range(SSM_GROUPS):
            lanes = pl.ds(g * D_STATE, D_STATE)
            bg = bm_ref[0, :, lanes].astype(BF16)
            cg = cm_ref[0, :, lanes].astype(BF16)
            g_mat = _nt(cg, bg)
            q = _ssd_group(g, d, e_all, f_all, ecl_b, x_ref, prow_ref)
            s_in = s_ref[g]
            sin_ref[0, 0, g] = s_in
            y_diag = []
            for j in range(HEADS_PER_SSM_GROUP):
                _, m = _ssd_mask_decay(d, g * HEADS_PER_SSM_GROUP + j, g_mat)
                y_diag.append(jnp.dot(m.astype(BF16), q["xd16"][:, j * HEAD_DIM:(j + 1) * HEAD_DIM], preferred_element_type=F32))
            y_off = _nt(cg, s_in.astype(BF16)) * q["e_g"]
            y_ref[0, :, pl.ds(g * GROUP_W, GROUP_W)] = jnp.concatenate(y_diag, axis=1) + y_off + q["dsk_g"] * q["xg"]
            s_ref[g] = s_in * q["sc_g"] + _tn(q["xf16"], bg)

    x_spec, bm_spec, cm_spec, dt_spec, dtt_spec, prow_spec, pcol_spec, _, y_spec = _ssd_specs(nc, False)
    return pl.pallas_call(
        body, name="ssd_fwd",
        grid=(bsz, nc),
        in_specs=[x_spec, bm_spec, cm_spec, dt_spec, dtt_spec, prow_spec, pcol_spec],
        out_specs=[y_spec, _ssd_state_spec(nc, False)],
        out_shape=[jax.ShapeDtypeStruct((bsz, seq, D_INNER), F32),
                   jax.ShapeDtypeStruct((bsz, nc, SSM_GROUPS, GROUP_W, D_STATE), F32)],
        scratch_shapes=[pltpu.VMEM((SSM_GROUPS, GROUP_W, D_STATE), F32)],
        compiler_params=_params("parallel", "arbitrary"),
    )(xa, xa, xa, rest3, dtr_t, prow, pcol)


def _ssd_bwd(xa, rest3, dtr_t, prow, pcol, s_in_all, dy):
    bsz, seq, _ = xa.shape
    nc = seq // BLK

    def body(x_ref, bm_ref, cm_ref, dt_ref, dtt_ref, prow_ref, pcol_ref, sin_ref, dy_ref,
             dxa_ref, ddt_ref, ddtt_ref, gprow_ref, gpcol_ref, ds_ref, drt_ref):
        first = jnp.logical_and(pl.program_id(0) == 0, pl.program_id(1) == 0)

        @pl.when(first)
        def _():
            gprow_ref[...] = jnp.zeros_like(gprow_ref)
            gpcol_ref[...] = jnp.zeros_like(gpcol_ref)

        @pl.when(pl.program_id(1) == 0)
        def _():
            ds_ref[...] = jnp.zeros_like(ds_ref)

        drt_ref[...] = jnp.zeros_like(drt_ref)
        d = _ssd_decays(dt_ref[0], dtt_ref[0], prow_ref, pcol_ref)
        e_all, f_all, ecl_b = _ssd_common(d)
        d_c = jnp.zeros((BLK, BLK), F32)
        d_dtc = jnp.zeros((BLK, BLK), F32)
        d_clast_col = jnp.zeros((BLK, 1), F32)
        skip_rows = lax.broadcasted_iota(jnp.int32, (8, GROUP_W), 0) == P_DSKIP
        for g in range(SSM_GROUPS):
            lanes = pl.ds(g * D_STATE, D_STATE)
            bg = bm_ref[0, :, lanes].astype(BF16)
            cg = cm_ref[0, :, lanes].astype(BF16)
            g_mat = _nt(cg, bg)
            q = _ssd_group(g, d, e_all, f_all, ecl_b, x_ref, prow_ref)
            s_in = sin_ref[0, 0, g]
            s_in16 = s_in.astype(BF16)
            d_y = dy_ref[0, :, pl.ds(g * GROUP_W, GROUP_W)]
            d_y16 = d_y.astype(BF16)
            d_so = ds_ref[g]
            d_so16 = d_so.astype(BF16)
            d_g = jnp.zeros((BLK, BLK), F32)
            ws, d_xds = [], []
            for j in range(HEADS_PER_SSM_GROUP):
                h = g * HEADS_PER_SSM_GROUP + j
                head = slice(j * HEAD_DIM, (j + 1) * HEAD_DIM)
                lm, m = _ssd_mask_decay(d, h, g_mat)
                d_m = _nt(d_y16[:, head], q["xd16"][:, head])
                d_xds.append(_tn(m.astype(BF16), d_y16[:, head]))
                w = d_m * m
                d_g = d_g + d_m * lm
                drt_ref[h:h + 1, :] = -jnp.sum(w, axis=0, keepdims=True)
                ws.append(w)
            d_c = d_c + _sel_dot(jnp.concatenate(ws, axis=1), _select_matrix((HEADS_PER_SSM_GROUP * BLK, BLK), g, 1, BLK))
            d_xd = jnp.concatenate(d_xds, axis=1)
            d_g16 = d_g.astype(BF16)
            qmat = _nt(cg, s_in16)
            d_q16 = (d_y * q["e_g"]).astype(BF16)
            d_cg = jnp.dot(d_q16, s_in16, preferred_element_type=F32) + jnp.dot(d_g16, bg, preferred_element_type=F32)
            d_sin = _tn(d_q16, cg) + d_so * q["sc_g"]
            d_clast_col = d_clast_col + jnp.sum(_sel_dot(q["spread"], d_so * s_in * q["sc_g"]), axis=1, keepdims=True)
            d_xf = _nt(bg, d_so16)
            d_bg = jnp.dot(q["xf16"], d_so16, preferred_element_type=F32) + _tn(d_g16, cg)
            d_xd = d_xd + d_xf * q["f_g"]
            r_e = _sel_dot(d_y * qmat * q["e_g"], q["gather"])
            r_f = _sel_dot(d_xf * q["xd"] * q["f_g"], q["gather"])
            d_c = d_c + r_e - r_f + jnp.where(d["last_row"], jnp.sum(r_f, axis=0, keepdims=True), 0.0)
            d_dtc = d_dtc + _sel_dot(d_xd * q["xg"], q["gather"])
            skip_sum = jnp.where(skip_rows, jnp.sum(d_y * q["xg"], axis=0, keepdims=True), 0.0)
            gprow_ref[...] += _sel_dot(skip_sum, q["gather"])
            dxa_ref[0, :, pl.ds(g * GROUP_W, GROUP_W)] = q["dsk_g"] * d_y + d_xd * q["dt_g"]
            dxa_ref[0, :, pl.ds(D_INNER + g * D_STATE, D_STATE)] = d_bg
            dxa_ref[0, :, pl.ds(D_INNER + (SSM_GROUPS + g) * D_STATE, D_STATE)] = d_cg
            ds_ref[g] = d_sin
        drt_ref[:, BLK - 1:BLK] += d_clast_col
        d_a = _dot_hi(d["tri_u"], d_c)
        d_pre = (d_dtc + d_a * d["ah_row"]) * _sigmoid(d["pre"])
        ddt_ref[0] = d_pre
        gprow_ref[P_DTB:P_DTB + 1, :] += jnp.sum(d_pre, axis=0, keepdims=True)
        gprow_ref[P_ALOG:P_ALOG + 1, :] += jnp.sum(d_a * d["dt"], axis=0, keepdims=True) * d["ah_row"]
        d_at = _dot_hi(drt_ref[...], d["tri"])
        d_pre_t = d_at * d["ah_col"] * _sigmoid(d["pre_t"])
        ddtt_ref[0] = d_pre_t
        gpcol_ref[:, P_DTB:P_DTB + 1] += jnp.sum(d_pre_t, axis=1, keepdims=True)
        gpcol_ref[:, P_ALOG:P_ALOG + 1] += jnp.sum(d_at * d["dt_t"], axis=1, keepdims=True) * d["ah_col"]

    x_spec, bm_spec, cm_spec, dt_spec, dtt_spec, prow_spec, pcol_spec, _, y_spec = _ssd_specs(nc, True)
    return pl.pallas_call(
        body, name="ssd_bwd",
        grid=(bsz, nc),
        in_specs=[x_spec, bm_spec, cm_spec, dt_spec, dtt_spec, prow_spec, pcol_spec, _ssd_state_spec(nc, True), y_spec],
        out_specs=[pl.BlockSpec((1, BLK, CONV_DIM), lambda b, c: (b, nc - 1 - c, 0)),
                   pl.BlockSpec((1, BLK, BLK), lambda b, c: (b, nc - 1 - c, 0)),
                   pl.BlockSpec((1, BLK, BLK), lambda b, c: (b, 0, nc - 1 - c)),
                   prow_spec, pcol_spec],
        out_shape=[jax.ShapeDtypeStruct((bsz, seq, CONV_DIM), F32),
                   jax.ShapeDtypeStruct((bsz, seq, BLK), F32),
                   jax.ShapeDtypeStruct((bsz, BLK, seq), F32),
                   jax.ShapeDtypeStruct((8, BLK), F32),
                   jax.ShapeDtypeStruct((BLK, 8), F32)],
        scratch_shapes=[pltpu.VMEM((SSM_GROUPS, GROUP_W, D_STATE), F32), pltpu.VMEM((BLK, BLK), F32)],
        compiler_params=_params("arbitrary", "arbitrary"),
    )(xa, xa, xa, rest3, dtr_t, prow, pcol, s_in_all, dy)


ROW_TILE = 256
CMB_COLS = 256
RMS_COLS = D_INNER // SSM_GROUPS


def _combine_weights(l_refs):
    ls = [r[...] for r in l_refs]
    mx = jnp.maximum(jnp.maximum(ls[0], ls[1]), ls[2])
    es = [jnp.exp(l - mx) for l in ls]
    inv = 1.0 / (es[0] + es[1] + es[2])
    return [e * inv for e in es]


def _combine_specs():
    a = pl.BlockSpec((ROW_TILE, CMB_COLS), lambda i, j: (i, j))
    gatt = pl.BlockSpec((ROW_TILE, CMB_COLS), lambda i, j: (i, R_GATT // CMB_COLS + j))
    return a, gatt


def _combine_fwd(outs, lses, rest):
    t = rest.shape[0]

    def body(o0, o1, o2, l0, l1, l2, ga_ref, oa_ref):
        ws = _combine_weights((l0, l1, l2))
        o = ws[0] * o0[...] + ws[1] * o1[...] + ws[2] * o2[...]
        oa_ref[...] = (o * _silu_and_grad(ga_ref[...])[0]).astype(BF16)

    a, gatt = _combine_specs()
    return pl.pallas_call(
        body, name="combine_fwd",
        grid=(t // ROW_TILE, GROUP_COLS // CMB_COLS),
        in_specs=[a] * 6 + [gatt],
        out_specs=a,
        out_shape=jax.ShapeDtypeStruct((t, GROUP_COLS), BF16),
        compiler_params=_params("parallel", "parallel"),
    )(*outs, *lses, rest)


def _combine_bwd(outs, lses, rest, d_oa):
    t = rest.shape[0]

    def body(o0, o1, o2, l0, l1, l2, ga_ref, doa_ref, do0, do1, do2, dd0, dd1, dd2, dga_ref):
        ws = _combine_weights((l0, l1, l2))
        o = ws[0] * o0[...] + ws[1] * o1[...] + ws[2] * o2[...]
        sg, dsg = _silu_and_grad(ga_ref[...])
        d_oa_v = doa_ref[...]
        d_o = d_oa_v * sg
        dga_ref[...] = (d_oa_v * o * dsg).astype(BF16)
        for w, do_ref, dd_ref in zip(ws, (do0, do1, do2), (dd0, dd1, dd2)):
            d_out = w * d_o
            do_ref[...] = d_out
            dd_ref[...] = d_out * o

    a, gatt = _combine_specs()
    s16 = jax.ShapeDtypeStruct((t, GROUP_COLS), BF16)
    s32 = jax.ShapeDtypeStruct((t, GROUP_COLS), F32)
    return pl.pallas_call(
        body, name="combine_bwd",
        grid=(t // ROW_TILE, GROUP_COLS // CMB_COLS),
        in_specs=[a] * 6 + [gatt, a],
        out_specs=[a] * 7,
        out_shape=[s32, s32, s32, s32, s32, s32, s16],
        compiler_params=_params("parallel", "parallel"),
    )(*outs, *lses, rest, d_oa)


def _gatenorm_fwd(y, rest, norm_w):
    t = rest.shape[0]

    def body(y_ref, z_ref, w_ref, o_ref):
        u = y_ref[...] * _silu_and_grad(z_ref[...])[0]
        rs = lax.rsqrt(jnp.mean(u * u, axis=-1, keepdims=True) + RMS_EPS)
        o_ref[...] = (u * rs * w_ref[...]).astype(BF16)

    return pl.pallas_call(
        body, name="gatenorm_fwd",
        grid=(t // ROW_TILE, SSM_GROUPS),
        in_specs=[pl.BlockSpec((ROW_TILE, RMS_COLS), lambda i, j: (i, j)),
                  pl.BlockSpec((ROW_TILE, RMS_COLS), lambda i, j: (i, R_Z // RMS_COLS + j)),
                  pl.BlockSpec((1, RMS_COLS), lambda i, j: (0, j))],
        out_specs=pl.BlockSpec((ROW_TILE, RMS_COLS), lambda i, j: (i, j)),
        out_shape=jax.ShapeDtypeStruct((t, D_INNER), BF16),
        compiler_params=_params("parallel", "parallel"),
    )(y, rest, norm_w)


def _gatenorm_bwd(y, rest, norm_w, d_ys):
    t = rest.shape[0]

    def body(y_ref, z_ref, w_ref, g_ref, dy_ref, dz_ref, dw_ref):
        @pl.when(pl.program_id(1) == 0)
        def _():
            dw_ref[...] = jnp.zeros_like(dw_ref)

        yv = y_ref[...]
        sz, dsz = _silu_and_grad(z_ref[...])
        u = yv * sz
        rs = lax.rsqrt(jnp.mean(u * u, axis=-1, keepdims=True) + RMS_EPS)
        un = u * rs
        g = g_ref[...]
        dw_ref[0:1, :] += jnp.sum(g * un, axis=0, keepdims=True)
        d_un = g * w_ref[...]
        d_u = rs * (d_un - un * jnp.mean(d_un * un, axis=-1, keepdims=True))
        dy_ref[...] = d_u * sz
        dz_ref[...] = (d_u * yv * dsz).astype(BF16)

    blk = pl.BlockSpec((ROW_TILE, RMS_COLS), lambda j, i: (i, j))
    return pl.pallas_call(
        body, name="gatenorm_bwd",
        grid=(SSM_GROUPS, t // ROW_TILE),
        in_specs=[blk, pl.BlockSpec((ROW_TILE, RMS_COLS), lambda j, i: (i, R_Z // RMS_COLS + j)),
                  pl.BlockSpec((1, RMS_COLS), lambda j, i: (0, j)), blk],
        out_specs=[blk, blk, pl.BlockSpec((8, RMS_COLS), lambda j, i: (0, j))],
        out_shape=[jax.ShapeDtypeStruct((t, D_INNER), F32), jax.ShapeDtypeStruct((t, D_INNER), BF16),
                   jax.ShapeDtypeStruct((8, D_INNER), F32)],
        compiler_params=_params("parallel", "arbitrary"),
    )(y, rest, norm_w, d_ys)


def _row_specs():
    full = pl.BlockSpec((ROW_TILE, D_MODEL), lambda i: (i, 0))
    vec = pl.BlockSpec((8, D_MODEL), lambda i: (0, 0))
    at = lambda off: pl.BlockSpec((ROW_TILE, D_MODEL), lambda i: (i, off // D_MODEL))
    return full, vec, at


def _merge_fwd(y_a, y_b, rest, b_gate):
    t = rest.shape[0]

    def body(ya_ref, yb_ref, ga_ref, gb_ref, bg_ref, o_ref):
        sa = _sigmoid(ga_ref[...] + bg_ref[0:1, :])
        sb = _sigmoid(gb_ref[...] + bg_ref[1:2, :])
        o_ref[...] = (sa * ya_ref[...] + sb * yb_ref[...]).astype(BF16)

    full, vec, at = _row_specs()
    return pl.pallas_call(
        body, name="merge_fwd",
        grid=(t // ROW_TILE,),
        in_specs=[full, full, at(R_GM), at(R_GM + D_MODEL), vec],
        out_specs=full,
        out_shape=jax.ShapeDtypeStruct((t, D_MODEL), BF16),
        compiler_params=_params("parallel"),
    )(y_a, y_b, rest, rest, b_gate)


def _merge_bwd(y_a, y_b, rest, b_gate, d_merged):
    t = rest.shape[0]

    def body(ya_ref, yb_ref, ga_ref, gb_ref, bg_ref, dm_ref, dya_ref, dyb_ref, dga_ref, dgb_ref, dbg_ref):
        @pl.when(pl.program_id(0) == 0)
        def _():
            dbg_ref[...] = jnp.zeros_like(dbg_ref)

        dm = dm_ref[...]
        for row, y_ref, g_ref, dy_ref, dg_ref in ((0, ya_ref, ga_ref, dya_ref, dga_ref), (1, yb_ref, gb_ref, dyb_ref, dgb_ref)):
            s = _sigmoid(g_ref[...] + bg_ref[row:row + 1, :])
            dy_ref[...] = (dm * s).astype(BF16)
            dg = dm * y_ref[...] * s * (1.0 - s)
            dg_ref[...] = dg.astype(BF16)
            dbg_ref[row:row + 1, :] += jnp.sum(dg, axis=0, keepdims=True)

    full, vec, at = _row_specs()
    s16 = jax.ShapeDtypeStruct((t, D_MODEL), BF16)
    return pl.pallas_call(
        body, name="merge_bwd",
        grid=(t // ROW_TILE,),
        in_specs=[full, full, at(R_GM), at(R_GM + D_MODEL), vec, full],
        out_specs=[full, full, full, full, vec],
        out_shape=[s16, s16, s16, s16, jax.ShapeDtypeStruct((8, D_MODEL), F32)],
        compiler_params=_params("arbitrary"),
    )(y_a, y_b, rest, rest, b_gate, d_merged)


ST_LNG, ST_LNB, ST_BG2, ST_LOSS = 0, 1, 2, 3


def _final(x, mix, pw, rest, b_gate, ln_gb, target):
    t = rest.shape[0]

    def body(x_ref, mix_ref, pw_ref, gp_ref, bg_ref, ln_ref, tgt_ref, dpre_ref, dpre16_ref, dgp_ref, dpw_ref, st_ref):
        @pl.when(pl.program_id(0) == 0)
        def _():
            st_ref[...] = jnp.zeros_like(st_ref)

        sp = _sigmoid(gp_ref[...] + bg_ref[2:3, :])
        pw = pw_ref[...]
        pre = ALPHA * x_ref[...] + mix_ref[...] + sp * pw
        xc = pre - jnp.mean(pre, axis=-1, keepdims=True)
        rstd = lax.rsqrt(jnp.mean(xc * xc, axis=-1, keepdims=True) + LN_EPS)
        xhat = xc * rstd
        gain = ln_ref[0:1, :]
        err = xhat * gain + ln_ref[1:2, :] - tgt_ref[...]
        d_yo = err * (1.0 / D_MODEL)
        d_xhat = d_yo * gain
        d_pre = rstd * (d_xhat - jnp.mean(d_xhat, axis=-1, keepdims=True)
                        - xhat * jnp.mean(d_xhat * xhat, axis=-1, keepdims=True))
        dpre_ref[...] = d_pre
        dpre16_ref[...] = d_pre.astype(BF16)
        dgp = d_pre * pw * sp * (1.0 - sp)
        dgp_ref[...] = dgp.astype(BF16)
        dpw_ref[...] = (d_pre * sp).astype(BF16)
        st_ref[ST_LNG:ST_LNG + 1, :] += jnp.sum(d_yo * xhat, axis=0, keepdims=True)
        st_ref[ST_LNB:ST_LNB + 1, :] += jnp.sum(d_yo, axis=0, keepdims=True)
        st_ref[ST_BG2:ST_BG2 + 1, :] += jnp.sum(dgp, axis=0, keepdims=True)
        st_ref[ST_LOSS:ST_LOSS + 1, :] += jnp.sum(err * err, axis=0, keepdims=True) * (0.5 / D_MODEL)

    full, vec, at = _row_specs()
    s16 = jax.ShapeDtypeStruct((t, D_MODEL), BF16)
    return pl.pallas_call(
        body, name="final",
        grid=(t // ROW_TILE,),
        in_specs=[full, full, full, at(R_GPLE), vec, vec, full],
        out_specs=[full, full, full, full, vec],
        out_shape=[jax.ShapeDtypeStruct((t, D_MODEL), F32), s16, s16, s16, jax.ShapeDtypeStruct((8, D_MODEL), F32)],
        compiler_params=_params("arbitrary"),
    )(x, mix, pw, rest, b_gate, ln_gb, target)


def _mesh_position():
    return lax.axis_index("x"), lax.axis_index("y"), lax.axis_index("c")


def _flip(pos, k):
    x, y, c = pos
    return ((1 - x) if k & 4 else x, (1 - y) if k & 2 else y, (1 - c) if k & 1 else c)


def _linear(pos):
    return 4 * pos[0] + 2 * pos[1] + pos[2]


def _exchange(arrays, scatter, name):
    n = len(arrays)

    def body(*refs):
        ins, outs = refs[:n], refs[n:2 * n]
        send_sems, recv_sems, local_sems = refs[2 * n:]
        me = _mesh_position()
        me_i = _linear(me)

        def src_for(i, dest_i):
            return ins[i].at[dest_i] if scatter[i] else ins[i]

        local = [pltpu.make_async_copy(src_for(i, me_i), outs[i].at[me_i], local_sems.at[i]) for i in range(n)]
        for cp in local:
            cp.start()
        started = []
        for k in range(1, N_DEV):
            peer = _flip(me, k)
            peer_i = _linear(peer)
            for i in range(n):
                sem = i * (N_DEV - 1) + k - 1
                cp = pltpu.make_async_remote_copy(
                    src_ref=src_for(i, peer_i), dst_ref=outs[i].at[me_i], send_sem=send_sems.at[sem],
                    recv_sem=recv_sems.at[sem], device_id=peer, device_id_type=pl.DeviceIdType.MESH)
                cp.start()
                started.append(cp)
        for k in range(1, N_DEV):
            peer = _flip(me, k)
            peer_i = _linear(peer)
            for i in range(n):
                sem = i * (N_DEV - 1) + k - 1
                pltpu.make_async_remote_copy(
                    src_ref=src_for(i, peer_i), dst_ref=outs[i].at[peer_i], send_sem=send_sems.at[sem],
                    recv_sem=recv_sems.at[sem], device_id=peer, device_id_type=pl.DeviceIdType.MESH).wait_recv()
        for cp in started:
            cp.wait_send()
        for cp in local:
            cp.wait()

    any_spec = pl.BlockSpec(memory_space=pl.ANY)
    out_shape = [jax.ShapeDtypeStruct(a.shape if s else (N_DEV,) + a.shape, a.dtype) for a, s in zip(arrays, scatter)]
    return pl.pallas_call(
        body, name=name,
        in_specs=[any_spec] * n,
        out_specs=[any_spec] * n,
        out_shape=out_shape,
        scratch_shapes=[pltpu.SemaphoreType.DMA((n * (N_DEV - 1),)), pltpu.SemaphoreType.DMA((n * (N_DEV - 1),)),
                        pltpu.SemaphoreType.DMA((n,))],
        compiler_params=pltpu.CompilerParams(has_side_effects=True),
    )(*arrays)


N_CHIPS = N_DEV // 2


def _other_chips(x, y):
    return [(1 - x, y), (x, 1 - y), (1 - x, 1 - y)]


def _gather_two_level(arrays, name):
    n = len(arrays)
    per = N_DEV - 1

    def body(*refs):
        ins, outs = refs[:n], refs[n:2 * n]
        send_sems, recv_sems, local_sems = refs[2 * n:]
        x, y, c = _mesh_position()
        me, sibling = (x, y, c), (x, y, 1 - c)
        chips = _other_chips(x, y)

        def copy(i, k, block, to, src=None):
            slot = outs[i].at[_linear(block)]
            return pltpu.make_async_remote_copy(
                src_ref=slot if src is None else src, dst_ref=slot, send_sem=send_sems.at[i * per + k],
                recv_sem=recv_sems.at[i * per + k], device_id=to, device_id_type=pl.DeviceIdType.MESH)

        local = [pltpu.make_async_copy(ins[i], outs[i].at[_linear(me)], local_sems.at[i]) for i in range(n)]
        for cp in local:
            cp.start()
        started = []
        for i in range(n):
            first = [copy(i, 0, me, sibling, src=ins[i])]
            first += [copy(i, 1 + j, me, (*chip, c), src=ins[i]) for j, chip in enumerate(chips)]
            for cp in first:
                cp.start()
            started += first
        for j, chip in enumerate(chips):
            for i in range(n):
                copy(i, 1 + j, (*chip, c), me).wait_recv()
                passed = copy(i, 4 + j, (*chip, c), sibling)
                passed.start()
                started.append(passed)
        for i in range(n):
            copy(i, 0, sibling, me).wait_recv()
            for j, chip in enumerate(chips):
                copy(i, 4 + j, (*chip, 1 - c), me).wait_recv()
        for cp in started:
            cp.wait_send()
        for cp in local:
            cp.wait()

    any_spec = pl.BlockSpec(memory_space=pl.ANY)
    return pl.pallas_call(
        body, name=name,
        in_specs=[any_spec] * n,
        out_specs=[any_spec] * n,
        out_shape=[jax.ShapeDtypeStruct((N_DEV,) + a.shape, a.dtype) for a in arrays],
        scratch_shapes=[pltpu.SemaphoreType.DMA((n * per,)), pltpu.SemaphoreType.DMA((n * per,)), pltpu.SemaphoreType.DMA((n,))],
        compiler_params=pltpu.CompilerParams(has_side_effects=True),
    )(*arrays)


def _pair_reduce(a, name, rows):
    _, r, c = a.shape
    assert r % rows == 0
    n_steps = r // rows
    a5 = a.reshape(N_CHIPS, 2, r, c)
    core = lax.axis_index("c").astype(jnp.int32).reshape(1)

    def body(core_ref, keep_ref, send_ref, o_ref, land, send_sems, recv_sems, credits):
        i = pl.program_id(0)
        slot = i % 2
        x, y, cc = _mesh_position()
        sibling = (x, y, 1 - cc)

        @pl.when(i >= 2)
        def _():
            pl.semaphore_wait(credits.at[slot], 1)

        rdma = pltpu.make_async_remote_copy(
            src_ref=send_ref, dst_ref=land.at[slot], send_sem=send_sems.at[slot], recv_sem=recv_sems.at[slot],
            device_id=sibling, device_id_type=pl.DeviceIdType.MESH)
        rdma.start()
        rdma.wait_recv()
        o_ref[...] = (keep_ref[:, 0].astype(F32) + land[slot, :, 0].astype(F32)).astype(o_ref.dtype)
        rdma.wait_send()

        @pl.when(i + 2 < n_steps)
        def _():
            pl.semaphore_signal(credits.at[slot], inc=1, device_id=sibling, device_id_type=pl.DeviceIdType.MESH)

    grid_spec = pltpu.PrefetchScalarGridSpec(
        num_scalar_prefetch=1,
        grid=(n_steps,),
        in_specs=[pl.BlockSpec((N_CHIPS, 1, rows, c), lambda i, core_ref: (0, core_ref[0], i, 0)),
                  pl.BlockSpec((N_CHIPS, 1, rows, c), lambda i, core_ref: (0, 1 - core_ref[0], i, 0))],
        out_specs=pl.BlockSpec((N_CHIPS, rows, c), lambda i, core_ref: (0, i, 0)),
        scratch_shapes=[pltpu.VMEM((2, N_CHIPS, 1, rows, c), a.dtype), pltpu.SemaphoreType.DMA((2,)),
                        pltpu.SemaphoreType.DMA((2,)), pltpu.SemaphoreType.REGULAR((2,))],
    )
    return pl.pallas_call(
        body, name=name, grid_spec=grid_spec,
        out_shape=jax.ShapeDtypeStruct((N_CHIPS, r, c), a.dtype),
        compiler_params=pltpu.CompilerParams(dimension_semantics=("arbitrary",), vmem_limit_bytes=VMEM_LIMIT_BYTES,
                                             has_side_effects=True),
    )(core, a5, a5)


def _chip_exchange(arrays, name):
    n = len(arrays)
    per = N_CHIPS - 1

    def body(*refs):
        ins, outs = refs[:n], refs[n:2 * n]
        send_sems, recv_sems, local_sems = refs[2 * n:]
        x, y, c = _mesh_position()
        me_q = 2 * x + y
        local = [pltpu.make_async_copy(ins[i].at[me_q], outs[i].at[me_q], local_sems.at[i]) for i in range(n)]
        for cp in local:
            cp.start()
        started = []
        for j, (px, py) in enumerate(_other_chips(x, y)):
            for i in range(n):
                cp = pltpu.make_async_remote_copy(
                    src_ref=ins[i].at[2 * px + py], dst_ref=outs[i].at[me_q], send_sem=send_sems.at[i * per + j],
                    recv_sem=recv_sems.at[i * per + j], device_id=(px, py, c), device_id_type=pl.DeviceIdType.MESH)
                cp.start()
                started.append(cp)
        for j, (px, py) in enumerate(_other_chips(x, y)):
            for i in range(n):
                pltpu.make_async_remote_copy(
                    src_ref=ins[i].at[2 * px + py], dst_ref=outs[i].at[2 * px + py], send_sem=send_sems.at[i * per + j],
                    recv_sem=recv_sems.at[i * per + j], device_id=(px, py, c), device_id_type=pl.DeviceIdType.MESH).wait_recv()
        for cp in started:
            cp.wait_send()
        for cp in local:
            cp.wait()

    any_spec = pl.BlockSpec(memory_space=pl.ANY)
    return pl.pallas_call(
        body, name=name,
        in_specs=[any_spec] * n,
        out_specs=[any_spec] * n,
        out_shape=[jax.ShapeDtypeStruct(a.shape, a.dtype) for a in arrays],
        scratch_shapes=[pltpu.SemaphoreType.DMA((n * per,)), pltpu.SemaphoreType.DMA((n * per,)), pltpu.SemaphoreType.DMA((n,))],
        compiler_params=pltpu.CompilerParams(has_side_effects=True),
    )(*arrays)


def _chip_exchange_start(arrays, name):
    n = len(arrays)
    per = N_CHIPS - 1
    hbm = pl.BlockSpec(memory_space=pltpu.HBM)
    sem = pl.BlockSpec(memory_space=pltpu.SEMAPHORE)

    def body(*refs):
        ins, lands = refs[:n], refs[n:2 * n]
        send_sems, recv_sems = refs[2 * n], refs[2 * n + 1]
        token = refs[-1]
        x, y, c = _mesh_position()
        me_q = 2 * x + y
        for j, (px, py) in enumerate(_other_chips(x, y)):
            for i in range(n):
                pltpu.make_async_remote_copy(
                    src_ref=ins[i].at[2 * px + py], dst_ref=lands[i].at[me_q], send_sem=send_sems.at[i * per + j],
                    recv_sem=recv_sems.at[i * per + j], device_id=(px, py, c), device_id_type=pl.DeviceIdType.MESH).start()
        token[...] = jnp.zeros_like(token)

    buffers = [pltpu.HBM(a.shape, a.dtype) for a in arrays]
    res = pl.pallas_call(
        body, name=name,
        out_shape=(pltpu.SemaphoreType.DMA((n * per,)), pltpu.SemaphoreType.DMA((n * per,)), *buffers, *buffers,
                   jax.ShapeDtypeStruct((8, BLK), F32)),
        in_specs=[hbm] * (2 * n),
        out_specs=(sem, sem, *([hbm] * (2 * n)), pl.BlockSpec(memory_space=pltpu.VMEM)),
        input_output_aliases={i: 2 + i for i in range(2 * n)},
        compiler_params=pltpu.CompilerParams(has_side_effects=pltpu.SideEffectType.DATAFLOW_SIDE_EFFECTING),
    )(*[pltpu.with_memory_space_constraint(a, pltpu.HBM) for a in arrays],
      *[pltpu.with_memory_space_constraint(lax.empty(a.shape, a.dtype), pltpu.HBM) for a in arrays])
    return res[0], res[1], res[2:2 + n], res[2 + n:2 + 2 * n], res[-1]


def _chip_exchange_wait(send_sems, recv_sems, sources, lands, after, name):
    n = len(sources)
    per = N_CHIPS - 1
    hbm = pl.BlockSpec(memory_space=pltpu.HBM)
    sem = pl.BlockSpec(memory_space=pltpu.SEMAPHORE)

    def body(*refs):
        ins, zones = refs[:n], refs[n:2 * n]
        send, recv = refs[2 * n], refs[2 * n + 1]
        x, y, c = _mesh_position()
        for j, (px, py) in enumerate(_other_chips(x, y)):
            for i in range(n):
                cp = pltpu.make_async_remote_copy(
                    src_ref=ins[i].at[2 * px + py], dst_ref=zones[i].at[2 * px + py], send_sem=send.at[i * per + j],
                    recv_sem=recv.at[i * per + j], device_id=(px, py, c), device_id_type=pl.DeviceIdType.MESH)
                cp.wait_send()
                cp.wait_recv()

    buffers = [pltpu.HBM(a.shape, a.dtype) for a in sources]
    res = pl.pallas_call(
        body, name=name,
        out_shape=(*buffers, *buffers),
        in_specs=[hbm] * (2 * n) + [sem, sem, pl.BlockSpec(memory_space=pl.ANY)],
        out_specs=[hbm] * (2 * n),
        input_output_aliases={i: i for i in range(2 * n)},
        compiler_params=pltpu.CompilerParams(has_side_effects=pltpu.SideEffectType.DATAFLOW_SIDE_EFFECTING),
    )(*sources, *lands, send_sems, recv_sems, after)
    return res[:n], res[n:]


def _adam_reduce(parts, w, m, v, name, rows):
    r, c = w.shape
    n_parts = parts.shape[0]
    assert r % rows == 0
    c1 = 1.0 - ADAM_B1 ** ADAM_STEP
    c2 = 1.0 - ADAM_B2 ** ADAM_STEP

    def body(p_ref, w_ref, m_ref, v_ref, g_ref, d_ref, nm_ref, nv_ref):
        g = p_ref[0].astype(F32)
        for s in range(1, n_parts):
            g = g + p_ref[s].astype(F32)
        g_ref[...] = g
        nm = ADAM_B1 * m_ref[...] + (1.0 - ADAM_B1) * g
        nv = ADAM_B2 * v_ref[...] + (1.0 - ADAM_B2) * (g * g)
        nm_ref[...] = nm
        nv_ref[...] = nv
        d_ref[...] = -ADAM_LR * ((nm / c1) / (jnp.sqrt(nv / c2) + ADAM_EPS) + ADAM_WD * w_ref[...])

    blk = pl.BlockSpec((rows, c), lambda i: (i, 0))
    shape = jax.ShapeDtypeStruct((r, c), F32)
    return pl.pallas_call(
        body, name=name,
        grid=(r // rows,),
        in_specs=[pl.BlockSpec((n_parts, rows, c), lambda i: (0, i, 0)), blk, blk, blk],
        out_specs=[blk] * 4,
        out_shape=[shape] * 4,
        compiler_params=_params("parallel"),
    )(parts, w, m, v)


def _lane_total(rows8):
    def body(a_ref, o_ref):
        o_ref[...] = _sum_all(a_ref[...])

    return pl.pallas_call(body, name="loss_total", out_shape=jax.ShapeDtypeStruct((1, 1), F32))(rows8)


def _permute_w_in(w):
    rows = w.shape[0]
    n_pairs = N_GROUPS * HEAD_PAIRS
    qkv = w[:, :V_END].reshape(rows, 3, n_pairs, BLK).transpose(0, 2, 1, 3).reshape(rows, V_END)
    return jnp.concatenate(
        [w[:, DT_END:GMERGE_END], w[:, GATT_END:Z_END], w[:, Z_END:XBC_END], w[:, GMERGE_END:], w[:, V_END:GATT_END],
         w[:, XBC_END:DT_END], jnp.zeros((rows, QKV_OFF - R_DT - SSM_HEADS), w.dtype), qkv], axis=1)


def _unpermute_w_in(g):
    rows = g.shape[0]
    n_pairs = N_GROUPS * HEAD_PAIRS
    qkv = g[:, QKV_OFF:].reshape(rows, n_pairs, 3, BLK).transpose(0, 2, 1, 3).reshape(rows, V_END)
    return jnp.concatenate(
        [qkv, g[:, R_GATT:R_GATT + GROUP_COLS], g[:, R_Z:R_Z + D_INNER], g[:, R_XBC:R_XBC + CONV_DIM],
         g[:, R_DT:R_DT + SSM_HEADS], g[:, R_GM:R_GM + 2 * D_MODEL], g[:, R_GPLE:R_GPLE + D_MODEL]], axis=1)


SMALL_ROWS = 80
_SMALL_LAYOUT = (("conv_b", CONV_DIM), ("dt_bias", BLK), ("a_log", BLK), ("d_skip", BLK), ("ssm_norm_w", D_INNER),
                 ("ln_g", D_MODEL), ("ln_b", D_MODEL), ("rel_bias", NUM_BUCKETS * ATT_HEADS), ("loss", D_MODEL))


def _pack_small(vals):
    flat = []
    for name, width in _SMALL_LAYOUT:
        v = vals.get(name)
        v = jnp.zeros((width,), F32) if v is None else v.reshape(-1).astype(F32)
        flat.append(jnp.pad(v, (0, width - v.shape[0])))
    flat = jnp.concatenate(flat)
    return jnp.pad(flat, (0, SMALL_ROWS * BLK - flat.shape[0])).reshape(SMALL_ROWS, BLK)


def _unpack_small(packed):
    flat = packed.reshape(-1)
    out, pos = {}, 0
    for name, width in _SMALL_LAYOUT:
        out[name] = flat[pos:pos + width]
        pos += width
    for name in ("dt_bias", "a_log", "d_skip"):
        out[name] = out[name][:SSM_HEADS]
    out["rel_bias"] = out["rel_bias"].reshape(NUM_BUCKETS, ATT_HEADS)
    return out


def _pack_cols(b_gate_part, conv_w_part):
    return jnp.concatenate([jnp.pad(b_gate_part, ((0, 5), (0, 0))), jnp.pad(conv_w_part, ((0, 4), (0, 0)))], axis=1)


def _pack_cols_all(b_gate_full, conv_w_full):
    bg = b_gate_full.reshape(3, N_DEV, BLK).transpose(1, 0, 2)
    cw = conv_w_full.reshape(CONV_WIDTH, N_DEV, CONV_DIM // N_DEV).transpose(1, 0, 2)
    return jnp.concatenate([jnp.pad(bg, ((0, 0), (0, 5), (0, 0))), jnp.pad(cw, ((0, 0), (0, 4), (0, 0)))], axis=2)


def _unpack_cols_all(packed):
    bg = packed[:, :3, :BLK].transpose(1, 0, 2).reshape(3, D_MODEL)
    cw = packed[:, :CONV_WIDTH, BLK:].transpose(1, 0, 2).reshape(CONV_WIDTH, CONV_DIM)
    return bg, cw


def _local_step(x, p, target, wp16, wb16, wo16, wple16, b_gate, conv_w, small, start_exchange):
    bsz, seq, _ = x.shape
    t = bsz * seq
    x2 = x.reshape(t, D_MODEL)
    x16 = x2.astype(BF16)
    p16 = p.reshape(t, PLE_DIM).astype(BF16)
    tgt2 = target.reshape(t, D_MODEL)
    b_gate8 = jnp.pad(b_gate, ((0, 5), (0, 0)))
    ln_gb = jnp.pad(jnp.stack([small["ln_g"], small["ln_b"]]), ((0, 6), (0, 0)))
    conv_b = small["conv_b"].reshape(1, CONV_DIM)
    norm_w = small["ssm_norm_w"].reshape(1, D_INNER)
    pad_heads = lambda v: jnp.pad(v, (0, BLK - SSM_HEADS))
    prow = jnp.pad(jnp.stack([pad_heads(small["dt_bias"]), pad_heads(small["a_log"]), pad_heads(small["d_skip"])]), ((0, 5), (0, 0)))
    pcol = prow.T
    wa16, wbb16 = wb16[:GROUP_COLS], wb16[GROUP_COLS:]

    rest = _matmul(x16, wp16, mode="nn", out_dtype=F32, name="inproj", tm=512, tn=2304, tk=D_MODEL, n_outer=True)
    rest3 = rest.reshape(bsz, seq, HCAT_COLS)
    biases, onehots, outs, lses = [], [], [], []
    for g, (_, dil) in enumerate(DILATED_PATTERNS):
        bias, onehot = _bias_matrix(small["rel_bias"][:, g * HEADS_PER_GROUP:(g + 1) * HEADS_PER_GROUP], dil)
        out3, lse3 = _attn_fwd(rest3, bias, g, dil, f"attn_fwd{g}")
        biases.append(bias), onehots.append(onehot)
        outs.append(out3.reshape(t, GROUP_COLS)), lses.append(lse3.reshape(t, GROUP_COLS))
    oa = _combine_fwd(outs, lses, rest)
    xa = _conv_fwd(rest3, conv_w, conv_b)
    dtr_t = jnp.swapaxes(rest3[:, :, R_DT:R_DT + BLK], 1, 2)
    y, s_in = _ssd_fwd(xa, rest3, dtr_t, prow, pcol)
    y2 = y.reshape(t, D_INNER)
    ys = _gatenorm_fwd(y2, rest, norm_w)
    y_a = _matmul(oa, wa16, mode="nn", out_dtype=F32, name="branch_a", tm=512, tn=D_MODEL, tk=GROUP_COLS)
    y_b = _matmul(ys, wbb16, mode="nn", out_dtype=F32, name="branch_b", tm=512, tn=D_MODEL, tk=D_INNER)
    merged = _merge_fwd(y_a, y_b, rest, b_gate8)
    mix = _matmul(merged, wo16, mode="nn", out_dtype=F32, name="out_proj", tm=512, tn=D_MODEL, tk=D_MODEL)
    pw = _matmul(p16, wple16, mode="nn", out_dtype=F32, name="ple_proj", tm=512, tn=D_MODEL, tk=PLE_DIM)
    d_pre, d_pre16, d_gple, d_pw, stats = _final(x2, mix, pw, rest, b_gate8, ln_gb, tgt2)

    d_merged = _matmul(d_pre16, wo16, mode="nt", out_dtype=F32, name="d_merged", tm=512, tn=D_MODEL, tk=D_MODEL)
    g_w_out = _matmul(merged, d_pre16, mode="tn", out_dtype=BF16, name="g_w_out", tm=512, tn=D_MODEL, tk=1024)
    g_w_ple = _matmul(p16, d_pw, mode="tn", out_dtype=BF16, name="g_w_ple", tm=PLE_DIM, tn=D_MODEL, tk=1024)
    d_ya, d_yb, d_ga, d_gb, dbg01 = _merge_bwd(y_a, y_b, rest, b_gate8, d_merged)
    d_oa = _matmul(d_ya, wa16, mode="nt", out_dtype=F32, name="d_oa", tm=512, tn=GROUP_COLS, tk=D_MODEL)
    d_ys = _matmul(d_yb, wbb16, mode="nt", out_dtype=F32, name="d_ys", tm=512, tn=1024, tk=D_MODEL)
    g_wa = _matmul(oa, d_ya, mode="tn", out_dtype=BF16, name="g_w_branch_a", tm=GROUP_COLS, tn=D_MODEL, tk=1024)
    g_wb = _matmul(ys, d_yb, mode="tn", out_dtype=BF16, name="g_w_branch_b", tm=512, tn=D_MODEL, tk=1024)
    d_outs_dd_ga = _combine_bwd(outs, lses, rest, d_oa)
    d_outs, dds, d_gatt = d_outs_dd_ga[:3], d_outs_dd_ga[3:6], d_outs_dd_ga[6]
    d_y, d_z, d_nw = _gatenorm_bwd(y2, rest, norm_w, d_ys)
    d_xa, ddt, ddt_t, gprow, gpcol = _ssd_bwd(xa, rest3, dtr_t, prow, pcol, s_in, d_y.reshape(bsz, seq, D_INNER))
    d_xbc, d_conv = _conv_bwd(rest3, conv_w, conv_b, d_xa)
    d_dt = (ddt + jnp.swapaxes(ddt_t, 1, 2)).reshape(t, BLK).astype(BF16)
    pieces = [d_ga, d_gb, d_z, d_xbc.reshape(t, CONV_DIM), d_gple, d_gatt, d_dt, jnp.zeros((t, QKV_OFF - R_DT - BLK), BF16)]
    g_tables = []
    shape3 = (bsz, seq, GROUP_COLS)
    for g, (_, dil) in enumerate(DILATED_PATTERNS):
        d_qkv, dbias = _attn_bwd(rest3, biases[g], lses[g].reshape(shape3), d_outs[g].reshape(shape3), dds[g].reshape(shape3),
                                 g, dil, f"attn_bwd{g}")
        pieces.append(d_qkv.reshape(t, QKV_G))
        g_tables.append(_bias_grad(dbias, onehots[g], f"bias_grad{g}"))
    d_hcat = jnp.concatenate(pieces, axis=1)
    g_wp = _matmul(x16, d_hcat, mode="tn", out_dtype=BF16, name="g_w_in", tm=D_MODEL, tn=2304, tk=1024, n_outer=True)

    grads = dict(
        w_in=_unpermute_w_in(g_wp),
        b_gate=jnp.stack([dbg01[0], dbg01[1], stats[ST_BG2]]),
        conv_w=d_conv[:CONV_WIDTH],
        w_branch=jnp.concatenate([g_wa, g_wb], axis=0),
        w_out=g_w_out,
        w_ple=g_w_ple,
    )
    small_grads = dict(
        conv_b=d_conv[CONV_WIDTH],
        dt_bias=gprow[P_DTB, :SSM_HEADS] + gpcol[:SSM_HEADS, P_DTB],
        a_log=gprow[P_ALOG, :SSM_HEADS] + gpcol[:SSM_HEADS, P_ALOG],
        d_skip=gprow[P_DSKIP, :SSM_HEADS],
        ssm_norm_w=d_nw[0],
        ln_g=stats[ST_LNG],
        ln_b=stats[ST_LNB],
        rel_bias=jnp.concatenate(g_tables, axis=1),
        loss=stats[ST_LOSS],
    )
    in_flight, token = start_exchange(grads)
    grad_x = _matmul(d_hcat, wp16, mode="nt", out_dtype=F32, name="grad_x", tm=1024, tn=D_MODEL, tk=2304,
                     add=d_pre, add_scale=ALPHA, after=token)
    return grad_x.reshape(bsz, seq, D_MODEL), in_flight, small_grads


WEIGHT_ORDER = ("w_in", "b_gate", "conv_w", "conv_b", "dt_bias", "a_log", "d_skip", "ssm_norm_w", "w_branch", "w_out",
                "w_ple", "ln_g", "ln_b", "rel_bias")
SMALL_NAMES = ("conv_b", "dt_bias", "a_log", "d_skip", "ssm_norm_w", "ln_g", "ln_b", "rel_bias")


def kernel(x, p, w_in, b_gate, conv_w, conv_b, dt_bias, a_log, d_skip, ssm_norm_w, w_branch, w_out, w_ple, ln_g, ln_b, rel_bias, loss_target, m_w_in, m_b_gate, m_conv_w, m_conv_b, m_dt_bias, m_a_log, m_d_skip, m_ssm_norm_w, m_w_branch, m_w_out, m_w_ple, m_ln_g, m_ln_b, m_rel_bias, v_w_in, v_b_gate, v_conv_w, v_conv_b, v_dt_bias, v_a_log, v_d_skip, v_ssm_norm_w, v_w_branch, v_w_out, v_w_ple, v_ln_g, v_ln_b, v_rel_bias):
    given = dict(w_in=w_in, b_gate=b_gate, conv_w=conv_w, conv_b=conv_b, dt_bias=dt_bias, a_log=a_log, d_skip=d_skip,
                 ssm_norm_w=ssm_norm_w, w_branch=w_branch, w_out=w_out, w_ple=w_ple, ln_g=ln_g, ln_b=ln_b)
    moments_m = dict(w_in=m_w_in, b_gate=m_b_gate, conv_w=m_conv_w, conv_b=m_conv_b, dt_bias=m_dt_bias, a_log=m_a_log,
                     d_skip=m_d_skip, ssm_norm_w=m_ssm_norm_w, w_branch=m_w_branch, w_out=m_w_out, w_ple=m_w_ple,
                     ln_g=m_ln_g, ln_b=m_ln_b)
    moments_v = dict(w_in=v_w_in, b_gate=v_b_gate, conv_w=v_conv_w, conv_b=v_conv_b, dt_bias=v_dt_bias, a_log=v_a_log,
                     d_skip=v_d_skip, ssm_norm_w=v_ssm_norm_w, w_branch=v_w_branch, w_out=v_w_out, w_ple=v_w_ple,
                     ln_g=v_ln_g, ln_b=v_ln_b)
    w = {k: a[0] for k, a in given.items()} | {"rel_bias": rel_bias}
    mm = {k: a[0] for k, a in moments_m.items()} | {"rel_bias": m_rel_bias}
    vv = {k: a[0] for k, a in moments_v.items()} | {"rel_bias": v_rel_bias}

    gathered = _gather_two_level(
        [w["w_in"].astype(BF16), w["w_branch"].astype(BF16), w["w_out"].astype(BF16), w["w_ple"].astype(BF16),
         _pack_cols(w["b_gate"], w["conv_w"])], "gather_weights")
    wp16 = _permute_w_in(gathered[0].transpose(1, 0, 2).reshape(D_MODEL, IN_COLS))
    wb16 = gathered[1].reshape(BRANCH_ROWS, D_MODEL)
    wo16 = gathered[2].reshape(D_MODEL, D_MODEL)
    wple16 = gathered[3].transpose(1, 0, 2).reshape(PLE_DIM, D_MODEL)
    b_gate_full, conv_w_full = _unpack_cols_all(gathered[4])
    small = {k: w[k] for k in SMALL_NAMES}

    def start_exchange(grads):
        big = [grads["w_in"].astype(BF16).reshape(D_MODEL, N_DEV, IN_SHARD).transpose(1, 0, 2),
               grads["w_branch"].astype(BF16).reshape(N_DEV, BRANCH_ROWS // N_DEV, D_MODEL),
               grads["w_out"].astype(BF16).reshape(N_DEV, D_MODEL // N_DEV, D_MODEL),
               grads["w_ple"].astype(BF16).reshape(PLE_DIM, N_DEV, BLK).transpose(1, 0, 2)]
        sums = [_pair_reduce(a, f"pair_reduce{i}", rows) for i, (a, rows) in enumerate(zip(big, (128, 176, 128, 256)))]
        send_sems, recv_sems, sources, lands, token = _chip_exchange_start(sums, "chip_exchange_start")
        cols_all = _pack_cols_all(grads["b_gate"], grads["conv_w"])
        return (send_sems, recv_sems, sources, lands, cols_all), token

    grad_x, in_flight, small_grads = _local_step(x, p[0], loss_target, wp16, wb16, wo16, wple16, b_gate_full, conv_w_full,
                                                 small, start_exchange)
    send_sems, recv_sems, sources, lands, cols_all = in_flight
    sources, zones = _chip_exchange_wait(send_sems, recv_sems, sources, lands, grad_x, "chip_exchange_wait")
    me_q = 2 * lax.axis_index("x") + lax.axis_index("y")
    parts = [lax.dynamic_update_slice_in_dim(z, lax.dynamic_slice_in_dim(s, me_q, 1, axis=0), me_q, axis=0)
             for z, s in zip(zones, sources)]
    small_parts = _exchange([cols_all, _pack_small(small_grads)], [True, False], "exchange_small")

    out = {}
    out["w_in"] = _adam_reduce(parts[0], w["w_in"], mm["w_in"], vv["w_in"], "adam_w_in", 128)
    out["w_branch"] = _adam_reduce(parts[1], w["w_branch"], mm["w_branch"], vv["w_branch"], "adam_w_branch", 176)
    out["w_out"] = _adam_reduce(parts[2], w["w_out"], mm["w_out"], vv["w_out"], "adam_w_out", 128)
    out["w_ple"] = _adam_reduce(parts[3], w["w_ple"], mm["w_ple"], vv["w_ple"], "adam_w_ple", 256)
    cols = _adam_reduce(small_parts[0], _pack_cols(w["b_gate"], w["conv_w"]), _pack_cols(mm["b_gate"], mm["conv_w"]),
                        _pack_cols(vv["b_gate"], vv["conv_w"]), "adam_cols", 8)
    out["b_gate"] = [a[:3, :BLK] for a in cols]
    out["conv_w"] = [a[:CONV_WIDTH, BLK:] for a in cols]
    packed = _adam_reduce(small_parts[1], _pack_small({k: w[k] for k in SMALL_NAMES}), _pack_small({k: mm[k] for k in SMALL_NAMES}),
                          _pack_small({k: vv[k] for k in SMALL_NAMES}), "adam_small", SMALL_ROWS)
    unpacked = [_unpack_small(a) for a in packed]
    for k in SMALL_NAMES:
        out[k] = [u[k] for u in unpacked]
    loss_rows = unpacked[0]["loss"].reshape(D_MODEL // BLK, BLK)
    loss = _lane_total(loss_rows).reshape(())

    def shaped(k, a):
        return a if k == "rel_bias" else a[None]

    results = [loss, grad_x]
    for i in range(4):
        results += [shaped(k, out[k][i]) for k in WEIGHT_ORDER]
    return tuple(results)
```

```python
import functools
import math

import jax
import jax.numpy as jnp
from jax import lax
from jax.experimental import pallas as pl
from jax.experimental.pallas import tpu as pltpu

F32 = jnp.float32
BF16 = jnp.bfloat16

N_DEV = 8
D_MODEL = 1024
SEQ = 2048
HEAD_DIM = 64
HEADS_PER_GROUP = 12
DILATED_PATTERNS = ((128, 1), (512, 4), (2048, 16))
N_GROUPS = 3
ATT_HEADS = N_GROUPS * HEADS_PER_GROUP
GROUP_COLS = HEADS_PER_GROUP * HEAD_DIM
ATT_QKV = ATT_HEADS * HEAD_DIM
BLK = 128
NUM_BUCKETS = 32
MAX_DISTANCE = 2048
D_INNER = 2048
SSM_HEADS = 32
SSM_GROUPS = 4
HEADS_PER_SSM_GROUP = SSM_HEADS // SSM_GROUPS
D_STATE = 128
CONV_WIDTH = 4
CONV_DIM = D_INNER + 2 * SSM_GROUPS * D_STATE
PLE_DIM = 256
ALPHA = 2.0 ** 0.25
LN_EPS = 1e-5
RMS_EPS = 1e-5
IN_COLS = 15904
IN_SHARD = IN_COLS // N_DEV
BRANCH_ROWS = GROUP_COLS + D_INNER

Q_END = ATT_QKV
K_END = 2 * ATT_QKV
V_END = 3 * ATT_QKV
GATT_END = V_END + GROUP_COLS
Z_END = GATT_END + D_INNER
XBC_END = Z_END + CONV_DIM
DT_END = XBC_END + SSM_HEADS
GMERGE_END = DT_END + 2 * D_MODEL

R_GM, R_Z, R_XBC, R_GPLE, R_GATT, R_DT = 0, 2048, 4096, 7168, 8192, 8960
QKV_OFF = 9216
HEAD_PAIRS = GROUP_COLS // BLK
QKV_G = 3 * GROUP_COLS
HCAT_COLS = QKV_OFF + N_GROUPS * QKV_G

ADAM_LR, ADAM_B1, ADAM_B2, ADAM_EPS, ADAM_WD, ADAM_STEP = 0.001, 0.9, 0.999, 1e-08, 0.01, 10

VMEM_LIMIT_BYTES = 56 * 1024 * 1024


def _params(*semantics):
    return pltpu.CompilerParams(dimension_semantics=semantics, vmem_limit_bytes=VMEM_LIMIT_BYTES)


def _sigmoid(v):
    return 1.0 / (1.0 + jnp.exp(-v))


def _silu_and_grad(v):
    s = _sigmoid(v)
    return v * s, s * (1.0 + v * (1.0 - s))


def _matmul(a, b, *, mode, out_dtype, name, tm, tn, tk, n_off=0, n=None, add=None, add_scale=1.0, n_outer=False, after=None):
    if mode == "nn":
        (m, k), n_full = a.shape, b.shape[1]
        assert b.shape[0] == k
    elif mode == "nt":
        (m, k), n_full = a.shape, b.shape[0]
        assert b.shape[1] == k
    else:
        (k, m), n_full = a.shape, b.shape[1]
        assert b.shape[0] == k
    n = n_full if n is None else n
    assert m % tm == 0 and n % tn == 0 and k % tk == 0 and n_off % tn == 0, (name, m, n, k)
    nk = k // tk
    jo = n_off // tn
    dims = {"nn": (((1,), (0,)), ((), ())), "nt": (((1,), (1,)), ((), ())), "tn": (((0,), (0,)), ((), ()))}[mode]

    def body(*refs):
        a_ref, b_ref = refs[:2]
        add_ref = refs[2] if add is not None else None
        o_ref = refs[2 + (add is not None) + (after is not None)]
        acc_ref = refs[-1] if nk > 1 else None
        prod = lax.dot_general(a_ref[...].astype(BF16), b_ref[...].astype(BF16), dims, preferred_element_type=F32)

        def finish(total):
            if add_ref is not None:
                total = total + add_scale * add_ref[...]
            o_ref[...] = total.astype(out_dtype)

        if nk == 1:
            finish(prod)
        else:
            kk = pl.program_id(2)

            @pl.when(kk == 0)
            def _():
                acc_ref[...] = prod

            @pl.when(jnp.logical_and(kk > 0, kk < nk - 1))
            def _():
                acc_ref[...] += prod

            @pl.when(kk == nk - 1)
            def _():
                finish(acc_ref[...] + prod)

    def ij(f):
        return (lambda g0, g1, kk: f(g1, g0, kk)) if n_outer else f

    if mode == "nn":
        a_spec = pl.BlockSpec((tm, tk), ij(lambda i, j, kk: (i, kk)))
        b_spec = pl.BlockSpec((tk, tn), ij(lambda i, j, kk: (kk, j + jo)))
    elif mode == "nt":
        a_spec = pl.BlockSpec((tm, tk), ij(lambda i, j, kk: (i, kk)))
        b_spec = pl.BlockSpec((tn, tk), ij(lambda i, j, kk: (j, kk)))
    else:
        a_spec = pl.BlockSpec((tk, tm), ij(lambda i, j, kk: (kk, i)))
        b_spec = pl.BlockSpec((tk, tn), ij(lambda i, j, kk: (kk, j)))
    in_specs = [a_spec, b_spec]
    args = [a, b]
    if add is not None:
        in_specs.append(pl.BlockSpec((tm, tn), ij(lambda i, j, kk: (i, j))))
        args.append(add)
    if after is not None:
        in_specs.append(pl.BlockSpec((8, BLK), lambda g0, g1, kk: (0, 0)))
        args.append(after)
    return pl.pallas_call(
        body, name=name,
        grid=(n // tn, m // tm, nk) if n_outer else (m // tm, n // tn, nk),
        in_specs=in_specs,
        out_specs=pl.BlockSpec((tm, tn), ij(lambda i, j, kk: (i, j))),
        out_shape=jax.ShapeDtypeStruct((m, n), out_dtype),
        scratch_shapes=[pltpu.VMEM((tm, tn), F32)] if nk > 1 else [],
        compiler_params=_params("parallel", "parallel", "arbitrary"),
    )(*args)


UNITS_PER_ITER = 4


def _band_mask(first):
    qi = lax.broadcasted_iota(jnp.int32, (BLK, BLK if first else 2 * BLK), 0)
    kj = lax.broadcasted_iota(jnp.int32, (BLK, BLK if first else 2 * BLK), 1)
    delta = qi - kj if first else qi + BLK - kj
    return jnp.logical_and(delta >= 0, delta <= BLK)


def _attn_specs(seq, g):
    qkv = [pl.BlockSpec((1, seq, BLK), functools.partial(
        lambda hp, b, part: (b, 0, QKV_OFF // BLK + (g * HEAD_PAIRS + hp) * 3 + part), part=part)) for part in range(3)]
    one = pl.BlockSpec((1, seq, BLK), lambda hp, b: (b, 0, hp))
    bias = pl.BlockSpec((2, BLK, 2 * BLK), lambda hp, b: (hp, 0, 0))
    return qkv, one, bias


def _attn_rows(dil, r, n, first):
    start = r + (dil * BLK) * n
    if dil == 1:
        start = pl.multiple_of(start, BLK)
        rows = pl.ds(start, BLK)
        keys = rows if first else pl.ds(pl.multiple_of(start - BLK, BLK), 2 * BLK)
    else:
        rows = pl.ds(start, BLK, stride=dil)
        keys = rows if first else pl.ds(start - dil * BLK, 2 * BLK, stride=dil)
    return rows, keys


def _attn_schedule(dil, nb, unit):
    def blocks_of(r):
        unit(r, 0, True)
        for n in range(1, UNITS_PER_ITER):
            unit(r, n, False)
        if nb > UNITS_PER_ITER:
            def more(i, carry):
                for jj in range(UNITS_PER_ITER):
                    unit(r, i * UNITS_PER_ITER + jj, False)
                return carry
            lax.fori_loop(1, nb // UNITS_PER_ITER, more, 0)

    if nb >= UNITS_PER_ITER:
        assert nb % UNITS_PER_ITER == 0
        if dil == 1:
            blocks_of(0)
        else:
            def per_residue(r, carry):
                blocks_of(r)
                return carry
            lax.fori_loop(0, dil, per_residue, 0)
    else:
        per_iter = UNITS_PER_ITER // nb
        assert UNITS_PER_ITER % nb == 0 and dil % per_iter == 0

        def residues(i, carry):
            for jj in range(per_iter):
                for n in range(nb):
                    unit(i * per_iter + jj, n, n == 0)
            return carry
        lax.fori_loop(0, dil // per_iter, residues, 0)


def _attn_fwd(hcat3, bias, g, dil, name):
    bsz, seq, _ = hcat3.shape
    nb = seq // dil // BLK
    scale = HEAD_DIM ** -0.5

    def body(q_ref, k_ref, v_ref, bias_ref, o_ref, lse_ref):
        masks = {True: _band_mask(True), False: _band_mask(False)}

        def unit(r, n, first):
            rows, keys = _attn_rows(dil, r, n, first)
            q2 = q_ref[0, rows, :].astype(BF16)
            k2 = k_ref[0, keys, :].astype(BF16)
            v2 = v_ref[0, keys, :].astype(BF16)
            outs, lses = [], []
            for j in range(2):
                lanes = slice(j * HEAD_DIM, (j + 1) * HEAD_DIM)
                bias_j = bias_ref[j, :, BLK:] if first else bias_ref[j]
                s = _nt(q2[:, lanes], k2[:, lanes]) * scale + bias_j
                s = jnp.where(masks[first], s, -jnp.inf)
                mx = jnp.max(s, axis=-1, keepdims=True)
                e = jnp.exp(s - mx)
                den = jnp.sum(e, axis=-1, keepdims=True)
                outs.append(jnp.dot(e.astype(BF16), v2[:, lanes], preferred_element_type=F32) / den)
                lses.append(jnp.broadcast_to(mx + jnp.log(den), (BLK, HEAD_DIM)))
            o_ref[0, rows, :] = jnp.concatenate(outs, axis=1)
            lse_ref[0, rows, :] = jnp.concatenate(lses, axis=1)

        _attn_schedule(dil, nb, unit)

    qkv_specs, one, bias_spec = _attn_specs(seq, g)
    shape = jax.ShapeDtypeStruct((bsz, seq, GROUP_COLS), F32)
    return pl.pallas_call(
        body, name=name,
        grid=(HEAD_PAIRS, bsz),
        in_specs=qkv_specs + [bias_spec],
        out_specs=[one, one],
        out_shape=[shape, shape],
        compiler_params=_params("parallel", "parallel"),
    )(hcat3, hcat3, hcat3, bias)


def _attn_bwd(hcat3, bias, lse, d_out, dd, d_hcat3, g, dil, name):
    bsz, seq, _ = hcat3.shape
    nb = seq // dil // BLK
    scale = HEAD_DIM ** -0.5

    def body(q_ref, k_ref, v_ref, bias_ref, lse_ref, do_ref, dd_ref, _, dqkv_ref, dbias_ref, dq_acc, dk_acc, dv_acc):
        @pl.when(pl.program_id(1) == 0)
        def _():
            dbias_ref[...] = jnp.zeros_like(dbias_ref)

        dk_acc[...] = jnp.zeros_like(dk_acc)
        dv_acc[...] = jnp.zeros_like(dv_acc)
        masks = {True: _band_mask(True), False: _band_mask(False)}

        def unit(r, n, first):
            rows, keys = _attn_rows(dil, r, n, first)
            q2 = q_ref[0, rows, :].astype(BF16)
            k2 = k_ref[0, keys, :].astype(BF16)
            v2 = v_ref[0, keys, :].astype(BF16)
            do2 = do_ref[0, rows, :].astype(BF16)
            lse2 = lse_ref[0, rows, :]
            dd2 = dd_ref[0, rows, :]
            dqs, dks, dvs = [], [], []
            for j in range(2):
                lanes = slice(j * HEAD_DIM, (j + 1) * HEAD_DIM)
                q, kb, vb, do = q2[:, lanes], k2[:, lanes], v2[:, lanes], do2[:, lanes]
                delta = jnp.sum(dd2[:, lanes], axis=-1, keepdims=True)
                bias_j = bias_ref[j, :, BLK:] if first else bias_ref[j]
                s = _nt(q, kb) * scale + bias_j
                p = jnp.where(masks[first], jnp.exp(s - lse2[:, j * HEAD_DIM:j * HEAD_DIM + 1]), 0.0)
                ds = p * (_nt(do, vb) - delta)
                ds16 = ds.astype(BF16)
                dqs.append(jnp.dot(ds16, kb, preferred_element_type=F32) * scale)
                dks.append(_tn(ds16, q) * scale)
                dvs.append(_tn(p.astype(BF16), do))
                if first:
                    dbias_ref[j, :, BLK:] += ds
                else:
                    dbias_ref[j] += ds
            dq_acc[rows, :] = jnp.concatenate(dqs, axis=1)
            dk_acc[keys, :] += jnp.concatenate(dks, axis=1)
            dv_acc[keys, :] += jnp.concatenate(dvs, axis=1)

        _attn_schedule(dil, nb, unit)
        dqkv_ref[0, :, 0:BLK] = dq_acc[...].astype(BF16)
        dqkv_ref[0, :, BLK:2 * BLK] = dk_acc[...].astype(BF16)
        dqkv_ref[0, :, 2 * BLK:3 * BLK] = dv_acc[...].astype(BF16)

    qkv_specs, one, bias_spec = _attn_specs(seq, g)
    return pl.pallas_call(
        body, name=name,
        grid=(HEAD_PAIRS, bsz),
        in_specs=qkv_specs + [bias_spec, one, one, one, pl.BlockSpec(memory_space=pl.ANY)],
        out_specs=[pl.BlockSpec((1, seq, 3 * BLK), lambda hp, b: (b, 0, QKV_OFF // (3 * BLK) + g * HEAD_PAIRS + hp)), bias_spec],
        out_shape=[jax.ShapeDtypeStruct(d_hcat3.shape, d_hcat3.dtype), jax.ShapeDtypeStruct((HEADS_PER_GROUP, BLK, 2 * BLK), F32)],
        input_output_aliases={7: 0},
        scratch_shapes=[pltpu.VMEM((seq, BLK), F32)] * 3,
        compiler_params=_params("arbitrary", "arbitrary"),
    )(hcat3, hcat3, hcat3, bias, lse, d_out, dd, d_hcat3)


def _t5_buckets(dil):
    import numpy as np
    qi = np.arange(BLK)[:, None]
    kj = np.arange(2 * BLK)[None, :]
    dist = np.maximum(qi + BLK - kj, 0) * dil
    max_exact = NUM_BUCKETS // 2
    d_f = np.maximum(dist, 1).astype(np.float32)
    large = max_exact + (np.log(d_f / np.float32(max_exact)) / np.float32(math.log(MAX_DISTANCE / max_exact))
                         * np.float32(NUM_BUCKETS - max_exact)).astype(np.int32)
    large = np.minimum(large, NUM_BUCKETS - 1)
    return np.where(dist < max_exact, dist, large).astype(np.int32).reshape(-1)


def _bias_matrix(table_g, dil):
    buckets = jnp.asarray(_t5_buckets(dil))
    onehot = (buckets[None, :] == lax.broadcasted_iota(jnp.int32, (NUM_BUCKETS, 1), 0)).astype(F32)
    tk = 4096

    def body(t_ref, oh_ref, o_ref):
        o_ref[...] = _dot_hi(t_ref[...], oh_ref[...])

    bias = pl.pallas_call(
        body, name=f"bias_matrix{dil}",
        grid=(onehot.shape[1] // tk,),
        in_specs=[pl.BlockSpec((HEADS_PER_GROUP, NUM_BUCKETS), lambda kk: (0, 0)), pl.BlockSpec((NUM_BUCKETS, tk), lambda kk: (0, kk))],
        out_specs=pl.BlockSpec((HEADS_PER_GROUP, tk), lambda kk: (0, kk)),
        out_shape=jax.ShapeDtypeStruct((HEADS_PER_GROUP, onehot.shape[1]), F32),
        compiler_params=_params("parallel"),
    )(table_g.T, onehot)
    return bias.reshape(HEADS_PER_GROUP, BLK, 2 * BLK), onehot


def _bias_grad(dbias, onehot, name):
    flat = dbias.reshape(HEADS_PER_GROUP, 2 * BLK * BLK)
    tk = 4096

    def body(oh_ref, g_ref, o_ref):
        @pl.when(pl.program_id(0) == 0)
        def _():
            o_ref[...] = jnp.zeros_like(o_ref)
        o_ref[...] += lax.dot_general(oh_ref[...], g_ref[...], (((1,), (1,)), ((), ())),
                                      preferred_element_type=F32, precision=lax.Precision.HIGHEST)

    return pl.pallas_call(
        body, name=name,
        grid=(flat.shape[1] // tk,),
        in_specs=[pl.BlockSpec((NUM_BUCKETS, tk), lambda kk: (0, kk)), pl.BlockSpec((HEADS_PER_GROUP, tk), lambda kk: (0, kk))],
        out_specs=pl.BlockSpec((NUM_BUCKETS, HEADS_PER_GROUP), lambda kk: (0, 0)),
        out_shape=jax.ShapeDtypeStruct((NUM_BUCKETS, HEADS_PER_GROUP), F32),
        compiler_params=_params("arbitrary"),
    )(onehot, flat)


def _shift_rows(u, s, down):
    if s == 0:
        return u
    rows = lax.broadcasted_iota(jnp.int32, u.shape, 0)
    n = u.shape[0]
    if down:
        return jnp.where(rows >= s, pltpu.roll(u, s, 0), 0.0)
    return jnp.where(rows < n - s, pltpu.roll(u, n - s, 0), 0.0)


def _conv_pre(u, w_ref, b_ref):
    acc = b_ref[0:1, :] + w_ref[CONV_WIDTH - 1:CONV_WIDTH, :] * u
    for s in range(1, CONV_WIDTH):
        acc = acc + w_ref[CONV_WIDTH - 1 - s:CONV_WIDTH - s, :] * _shift_rows(u, s, True)
    return acc


def _conv_fwd(rest3, conv_w, conv_b):
    bsz, seq, _ = rest3.shape

    def body(u_ref, w_ref, b_ref, o_ref):
        pre = _conv_pre(u_ref[0], w_ref, b_ref)
        o_ref[0] = pre * _sigmoid(pre)

    return pl.pallas_call(
        body, name="conv_fwd",
        grid=(bsz, CONV_DIM // BLK),
        in_specs=[pl.BlockSpec((1, seq, BLK), lambda b, c: (b, 0, R_XBC // BLK + c)),
                  pl.BlockSpec((CONV_WIDTH, BLK), lambda b, c: (0, c)),
                  pl.BlockSpec((1, BLK), lambda b, c: (0, c))],
        out_specs=pl.BlockSpec((1, seq, BLK), lambda b, c: (b, 0, c)),
        out_shape=jax.ShapeDtypeStruct((bsz, seq, CONV_DIM), F32),
        compiler_params=_params("parallel", "parallel"),
    )(rest3, conv_w, conv_b)


def _conv_bwd(rest3, conv_w, conv_b, d_act, d_hcat3):
    bsz, seq, _ = rest3.shape

    def body(u_ref, w_ref, b_ref, g_ref, _, du_ref, dw_ref):
        @pl.when(pl.program_id(1) == 0)
        def _():
            dw_ref[...] = jnp.zeros_like(dw_ref)

        u = u_ref[0]
        pre = _conv_pre(u, w_ref, b_ref)
        d_pre = g_ref[0] * _silu_and_grad(pre)[1]
        du = w_ref[CONV_WIDTH - 1:CONV_WIDTH, :] * d_pre
        dw_ref[CONV_WIDTH - 1:CONV_WIDTH, :] += jnp.sum(d_pre * u, axis=0, keepdims=True)
        for s in range(1, CONV_WIDTH):
            du = du + w_ref[CONV_WIDTH - 1 - s:CONV_WIDTH - s, :] * _shift_rows(d_pre, s, False)
            dw_ref[CONV_WIDTH - 1 - s:CONV_WIDTH - s, :] += jnp.sum(d_pre * _shift_rows(u, s, True), axis=0, keepdims=True)
        dw_ref[CONV_WIDTH:CONV_WIDTH + 1, :] += jnp.sum(d_pre, axis=0, keepdims=True)
        du_ref[0] = du.astype(BF16)

    return pl.pallas_call(
        body, name="conv_bwd",
        grid=(CONV_DIM // BLK, bsz),
        in_specs=[pl.BlockSpec((1, seq, BLK), lambda c, b: (b, 0, R_XBC // BLK + c)),
                  pl.BlockSpec((CONV_WIDTH, BLK), lambda c, b: (0, c)),
                  pl.BlockSpec((1, BLK), lambda c, b: (0, c)),
                  pl.BlockSpec((1, seq, BLK), lambda c, b: (b, 0, c)),
                  pl.BlockSpec(memory_space=pl.ANY)],
        out_specs=[pl.BlockSpec((1, seq, BLK), lambda c, b: (b, 0, R_XBC // BLK + c)),
                   pl.BlockSpec((8, BLK), lambda c, b: (0, c))],
        out_shape=[jax.ShapeDtypeStruct(d_hcat3.shape, d_hcat3.dtype), jax.ShapeDtypeStruct((8, CONV_DIM), F32)],
        input_output_aliases={4: 0},
        compiler_params=_params("parallel", "arbitrary"),
    )(rest3, conv_w, conv_b, d_act, d_hcat3)


P_DTB, P_ALOG, P_DSKIP = 0, 1, 2


def _softplus(v):
    return jnp.maximum(v, 0.0) + jnp.log(1.0 + jnp.exp(-jnp.abs(v)))


def _dot_hi(a, b):
    return jnp.dot(a, b, preferred_element_type=F32, precision=lax.Precision.HIGHEST)


def _nt(a, b):
    return lax.dot_general(a, b, (((1,), (1,)), ((), ())), preferred_element_type=F32)


def _tn(a, b):
    return lax.dot_general(a, b, (((0,), (0,)), ((), ())), preferred_element_type=F32)


def _sum_all(v):
    return jnp.sum(jnp.sum(v, axis=0, keepdims=True), axis=1, keepdims=True)


def _ssd_decays(dtr, dtr_t, prow_ref, pcol_ref):
    ri = lax.broadcasted_iota(jnp.int32, (BLK, BLK), 0)
    ci = lax.broadcasted_iota(jnp.int32, (BLK, BLK), 1)
    tri = (ri >= ci).astype(F32)
    tri_u = (ri <= ci).astype(F32)
    pre = dtr + prow_ref[P_DTB:P_DTB + 1, :]
    dt = _softplus(pre)
    ah_row = -jnp.exp(prow_ref[P_ALOG:P_ALOG + 1, :])
    acs = _dot_hi(tri, dt * ah_row)
    pre_t = dtr_t + pcol_ref[:, P_DTB:P_DTB + 1]
    dt_t = _softplus(pre_t)
    ah_col = -jnp.exp(pcol_ref[:, P_ALOG:P_ALOG + 1])
    acs_t = _dot_hi(dt_t * ah_col, tri_u)
    return dict(tri=tri, tri_u=tri_u, pre=pre, dt=dt, ah_row=ah_row, acs=acs, pre_t=pre_t, dt_t=dt_t, ah_col=ah_col,
                acs_t=acs_t, causal=ri >= ci, last_row=ri[:, 0:1] == BLK - 1)


def _ssd_head(h, d, x_ref, s_in, g_mat):
    col = d["acs"][:, h:h + 1]
    row = d["acs_t"][h:h + 1, :]
    lm = jnp.exp(jnp.where(d["causal"], col - row, -jnp.inf))
    m = g_mat * lm
    xh = x_ref[0, :, pl.ds(h * HEAD_DIM, HEAD_DIM)]
    dtc = d["dt"][:, h:h + 1]
    xd = xh * dtc
    e = jnp.exp(col)
    clast = d["acs"][BLK - 1:BLK, h:h + 1]
    f = jnp.exp(clast - col)
    return dict(col=col, lm=lm, m=m, xh=xh, dtc=dtc, xd=xd, e=e, ecl=jnp.exp(clast), f=f, xf=xd * f)


def _ssd_specs(nc, rev):
    cidx = (lambda c: nc - 1 - c) if rev else (lambda c: c)
    x_spec = pl.BlockSpec((1, BLK, D_INNER), lambda b, c: (b, cidx(c), 0))
    bm_spec = pl.BlockSpec((1, BLK, SSM_GROUPS * D_STATE), lambda b, c: (b, cidx(c), D_INNER // (SSM_GROUPS * D_STATE)))
    cm_spec = pl.BlockSpec((1, BLK, SSM_GROUPS * D_STATE), lambda b, c: (b, cidx(c), D_INNER // (SSM_GROUPS * D_STATE) + 1))
    dt_spec = pl.BlockSpec((1, BLK, BLK), lambda b, c: (b, cidx(c), R_DT // BLK))
    dtt_spec = pl.BlockSpec((1, BLK, BLK), lambda b, c: (b, 0, cidx(c)))
    prow_spec = pl.BlockSpec((8, BLK), lambda b, c: (0, 0))
    pcol_spec = pl.BlockSpec((BLK, 8), lambda b, c: (0, 0))
    st_spec = pl.BlockSpec((1, 1, SSM_HEADS, HEAD_DIM, D_STATE), lambda b, c: (b, cidx(c), 0, 0, 0))
    y_spec = pl.BlockSpec((1, BLK, D_INNER), lambda b, c: (b, cidx(c), 0))
    return x_spec, bm_spec, cm_spec, dt_spec, dtt_spec, prow_spec, pcol_spec, st_spec, y_spec


def _ssd_fwd(xa, rest3, dtr_t, prow, pcol):
    bsz, seq, _ = xa.shape
    nc = seq // BLK

    def body(x_ref, bm_ref, cm_ref, dt_ref, dtt_ref, prow_ref, pcol_ref, y_ref, sin_ref, s_ref):
        @pl.when(pl.program_id(1) == 0)
        def _():
            s_ref[...] = jnp.zeros_like(s_ref)

        d = _ssd_decays(dt_ref[0], dtt_ref[0], prow_ref, pcol_ref)
        for g in range(SSM_GROUPS):
            lanes = pl.ds(g * D_STATE, D_STATE)
            bg = bm_ref[0, :, lanes].astype(BF16)
            cg = cm_ref[0, :, lanes].astype(BF16)
            g_mat = _nt(cg, bg)
            for hh in range(HEADS_PER_SSM_GROUP):
                h = g * HEADS_PER_SSM_GROUP + hh
                s_in = s_ref[h]
                sin_ref[0, 0, h] = s_in
                q = _ssd_head(h, d, x_ref, s_in, g_mat)
                y_diag = jnp.dot(q["m"].astype(BF16), q["xd"].astype(BF16), preferred_element_type=F32)
                y_off = _nt(cg, s_in.astype(BF16)) * q["e"]
                s_ref[h] = s_in * q["ecl"] + _tn(q["xf"].astype(BF16), bg)
                y_ref[0, :, pl.ds(h * HEAD_DIM, HEAD_DIM)] = y_diag + y_off + prow_ref[P_DSKIP:P_DSKIP + 1, h:h + 1] * q["xh"]

    x_spec, bm_spec, cm_spec, dt_spec, dtt_spec, prow_spec, pcol_spec, st_spec, y_spec = _ssd_specs(nc, False)
    return pl.pallas_call(
        body, name="ssd_fwd",
        grid=(bsz, nc),
        in_specs=[x_spec, bm_spec, cm_spec, dt_spec, dtt_spec, prow_spec, pcol_spec],
        out_specs=[y_spec, st_spec],
        out_shape=[jax.ShapeDtypeStruct((bsz, seq, D_INNER), F32),
                   jax.ShapeDtypeStruct((bsz, nc, SSM_HEADS, HEAD_DIM, D_STATE), F32)],
        scratch_shapes=[pltpu.VMEM((SSM_HEADS, HEAD_DIM, D_STATE), F32)],
        compiler_params=_params("parallel", "arbitrary"),
    )(xa, xa, xa, rest3, dtr_t, prow, pcol)


def _ssd_bwd(xa, rest3, dtr_t, prow, pcol, s_in_all, dy):
    bsz, seq, _ = xa.shape
    nc = seq // BLK

    def body(x_ref, bm_ref, cm_ref, dt_ref, dtt_ref, prow_ref, pcol_ref, sin_ref, dy_ref,
             dxa_ref, ddt_ref, ddtt_ref, gprow_ref, gpcol_ref, ds_ref, dc_ref, ddtc_ref, drt_ref):
        first = jnp.logical_and(pl.program_id(0) == 0, pl.program_id(1) == 0)

        @pl.when(first)
        def _():
            gprow_ref[...] = jnp.zeros_like(gprow_ref)
            gpcol_ref[...] = jnp.zeros_like(gpcol_ref)

        @pl.when(pl.program_id(1) == 0)
        def _():
            ds_ref[...] = jnp.zeros_like(ds_ref)

        dc_ref[...] = jnp.zeros_like(dc_ref)
        ddtc_ref[...] = jnp.zeros_like(ddtc_ref)
        drt_ref[...] = jnp.zeros_like(drt_ref)
        d = _ssd_decays(dt_ref[0], dtt_ref[0], prow_ref, pcol_ref)
        for g in range(SSM_GROUPS):
            lanes = pl.ds(g * D_STATE, D_STATE)
            bg = bm_ref[0, :, lanes].astype(BF16)
            cg = cm_ref[0, :, lanes].astype(BF16)
            g_mat = _nt(cg, bg)
            d_g = jnp.zeros((BLK, BLK), F32)
            d_bg = jnp.zeros((BLK, D_STATE), F32)
            d_cg = jnp.zeros((BLK, D_STATE), F32)
            for hh in range(HEADS_PER_SSM_GROUP):
                h = g * HEADS_PER_SSM_GROUP + hh
                head_lanes = pl.ds(h * HEAD_DIM, HEAD_DIM)
                s_in = sin_ref[0, 0, h]
                s_in16 = s_in.astype(BF16)
                q = _ssd_head(h, d, x_ref, s_in, g_mat)
                m16, xd16 = q["m"].astype(BF16), q["xd"].astype(BF16)
                d_y = dy_ref[0, :, head_lanes]
                d_y16 = d_y.astype(BF16)
                d_so = ds_ref[h]
                d_so16 = d_so.astype(BF16)
                d_x = prow_ref[P_DSKIP:P_DSKIP + 1, h:h + 1] * d_y
                gprow_ref[P_DSKIP:P_DSKIP + 1, h:h + 1] += _sum_all(d_y * q["xh"])
                d_m = _nt(d_y16, xd16)
                d_xd = _tn(m16, d_y16)
                w = d_m * q["m"]
                d_g = d_g + d_m * q["lm"]
                d_col = jnp.sum(w, axis=1, keepdims=True)
                drt_ref[h:h + 1, :] = -jnp.sum(w, axis=0, keepdims=True)
                qmat = _nt(cg, s_in16)
                d_q16 = (d_y * q["e"]).astype(BF16)
                d_col = d_col + jnp.sum(d_y * qmat, axis=1, keepdims=True) * q["e"]
                d_cg = d_cg + jnp.dot(d_q16, s_in16, preferred_element_type=F32)
                d_sin = _tn(d_q16, cg) + d_so * q["ecl"]
                d_clast = _sum_all(d_so * s_in) * q["ecl"]
                d_xf = _nt(bg, d_so16)
                d_bg = d_bg + jnp.dot(q["xf"].astype(BF16), d_so16, preferred_element_type=F32)
                d_xd = d_xd + d_xf * q["f"]
                d_f = jnp.sum(d_xf * q["xd"], axis=1, keepdims=True) * q["f"]
                d_clast = d_clast + jnp.sum(d_f, axis=0, keepdims=True)
                d_col = d_col - d_f + jnp.where(d["last_row"], d_clast, 0.0)
                dxa_ref[0, :, head_lanes] = d_x + d_xd * q["dtc"]
                dc_ref[:, h:h + 1] = d_col
                ddtc_ref[:, h:h + 1] = jnp.sum(d_xd * q["xh"], axis=1, keepdims=True)
                ds_ref[h] = d_sin
            d_g16 = d_g.astype(BF16)
            dxa_ref[0, :, pl.ds(D_INNER + g * D_STATE, D_STATE)] = d_bg + _tn(d_g16, cg)
            dxa_ref[0, :, pl.ds(D_INNER + (SSM_GROUPS + g) * D_STATE, D_STATE)] = d_cg + jnp.dot(d_g16, bg, preferred_element_type=F32)
        d_a = _dot_hi(d["tri_u"], dc_ref[...])
        d_pre = (ddtc_ref[...] + d_a * d["ah_row"]) * _sigmoid(d["pre"])
        ddt_ref[0] = d_pre
        gprow_ref[P_DTB:P_DTB + 1, :] += jnp.sum(d_pre, axis=0, keepdims=True)
        gprow_ref[P_ALOG:P_ALOG + 1, :] += jnp.sum(d_a * d["dt"], axis=0, keepdims=True) * d["ah_row"]
        d_at = _dot_hi(drt_ref[...], d["tri"])
        d_pre_t = d_at * d["ah_col"] * _sigmoid(d["pre_t"])
        ddtt_ref[0] = d_pre_t
        gpcol_ref[:, P_DTB:P_DTB + 1] += jnp.sum(d_pre_t, axis=1, keepdims=True)
        gpcol_ref[:, P_ALOG:P_ALOG + 1] += jnp.sum(d_at * d["dt_t"], axis=1, keepdims=True) * d["ah_col"]

    x_spec, bm_spec, cm_spec, dt_spec, dtt_spec, prow_spec, pcol_spec, st_spec, y_spec = _ssd_specs(nc, True)
    return pl.pallas_call(
        body, name="ssd_bwd",
        grid=(bsz, nc),
        in_specs=[x_spec, bm_spec, cm_spec, dt_spec, dtt_spec, prow_spec, pcol_spec, st_spec, y_spec],
        out_specs=[pl.BlockSpec((1, BLK, CONV_DIM), lambda b, c: (b, nc - 1 - c, 0)),
                   pl.BlockSpec((1, BLK, BLK), lambda b, c: (b, nc - 1 - c, 0)),
                   pl.BlockSpec((1, BLK, BLK), lambda b, c: (b, 0, nc - 1 - c)),
                   prow_spec, pcol_spec],
        out_shape=[jax.ShapeDtypeStruct((bsz, seq, CONV_DIM), F32),
                   jax.ShapeDtypeStruct((bsz, seq, BLK), F32),
                   jax.ShapeDtypeStruct((bsz, BLK, seq), F32),
                   jax.ShapeDtypeStruct((8, BLK), F32),
                   jax.ShapeDtypeStruct((BLK, 8), F32)],
        scratch_shapes=[pltpu.VMEM((SSM_HEADS, HEAD_DIM, D_STATE), F32), pltpu.VMEM((BLK, BLK), F32),
                        pltpu.VMEM((BLK, BLK), F32), pltpu.VMEM((BLK, BLK), F32)],
        compiler_params=_params("arbitrary", "arbitrary"),
    )(xa, xa, xa, rest3, dtr_t, prow, pcol, s_in_all, dy)


GROUP_W = HEADS_PER_SSM_GROUP * HEAD_DIM


def _select_matrix(shape, g, head_axis, per_head):
    h = lax.broadcasted_iota(jnp.int32, shape, head_axis)
    j = lax.broadcasted_iota(jnp.int32, shape, 1 - head_axis)
    return (h == g * HEADS_PER_SSM_GROUP + lax.shift_right_logical(j, per_head.bit_length() - 1)).astype(BF16)


def _split16(v, terms):
    parts, rem = [], v
    for _ in range(terms):
        p = rem.astype(BF16)
        parts.append(p)
        rem = rem - p.astype(F32)
    return parts


def _sel_dot(a, b, terms=2):
    if a.dtype == BF16:
        return sum(jnp.dot(a, p, preferred_element_type=F32) for p in _split16(b, terms))
    return sum(jnp.dot(p, b, preferred_element_type=F32) for p in _split16(a, terms))


def _ssd_group(g, d, e_all, f_all, ecl_b, x_ref, prow_ref):
    spread = _select_matrix((BLK, GROUP_W), g, 0, HEAD_DIM)
    gather = _select_matrix((GROUP_W, BLK), g, 1, HEAD_DIM)
    xg = x_ref[0, :, pl.ds(g * GROUP_W, GROUP_W)]
    dt_g = _sel_dot(d["dt"], spread)
    e_g = _sel_dot(e_all, spread)
    f_g = _sel_dot(f_all, spread)
    dsk_g = _sel_dot(prow_ref[...], spread, 3)[P_DSKIP:P_DSKIP + 1, :]
    sc_g = _sel_dot(gather, ecl_b, 3)
    xd = xg * dt_g
    return dict(spread=spread, gather=gather, xg=xg, dt_g=dt_g, e_g=e_g, f_g=f_g, dsk_g=dsk_g, sc_g=sc_g, xd=xd,
                xd16=xd.astype(BF16), xf16=(xd * f_g).astype(BF16))


def _ssd_common(d):
    e_all = jnp.exp(d["acs"])
    f_all = jnp.exp(d["acs"][BLK - 1:BLK, :] - d["acs"])
    ecl_b = jnp.broadcast_to(jnp.exp(d["acs_t"][:, BLK - 1:BLK]), (BLK, BLK))
    return e_all, f_all, ecl_b


def _ssd_mask_decay(d, h, g_mat):
    col = d["acs"][:, h:h + 1]
    row = d["acs_t"][h:h + 1, :]
    lm = jnp.exp(jnp.where(d["causal"], col - row, -jnp.inf))
    return lm, g_mat * lm


def _ssd_state_spec(nc, rev):
    cidx = (lambda c: nc - 1 - c) if rev else (lambda c: c)
    return pl.BlockSpec((1, 1, SSM_GROUPS, GROUP_W, D_STATE), lambda b, c: (b, cidx(c), 0, 0, 0))


def _ssd_fwd(xa, rest3, dtr_t, prow, pcol):
    bsz, seq, _ = xa.shape
    nc = seq // BLK

    def body(x_ref, bm_ref, cm_ref, dt_ref, dtt_ref, prow_ref, pcol_ref, y_ref, sin_ref, s_ref):
        @pl.when(pl.program_id(1) == 0)
        def _():
            s_ref[...] = jnp.zeros_like(s_ref)

        d = _ssd_decays(dt_ref[0], dtt_ref[0], prow_ref, pcol_ref)
        e_all, f_all, ecl_b = _ssd_common(d)
        for g in range(SSM_GROUPS):
            lanes = pl.ds(g * D_STATE, D_STATE)
            bg = bm_ref[0, :, lanes].astype(BF16)
            cg = cm_ref[0, :, lanes].astype(BF16)
            g_mat = _nt(cg, bg)
            q = _ssd_group(g, d, e_all, f_all, ecl_b, x_ref, prow_ref)
            s_in = s_ref[g]
            sin_ref[0, 0, g] = s_in
            y_diag = []
            for j in range(HEADS_PER_SSM_GROUP):
                _, m = _ssd_mask_decay(d, g * HEADS_PER_SSM_GROUP + j, g_mat)
                y_diag.append(jnp.dot(m.astype(BF16), q["xd16"][:, j * HEAD_DIM:(j + 1) * HEAD_DIM], preferred_element_type=F32))
            y_off = _nt(cg, s_in.astype(BF16)) * q["e_g"]
            y_ref[0, :, pl.ds(g * GROUP_W, GROUP_W)] = jnp.concatenate(y_diag, axis=1) + y_off + q["dsk_g"] * q["xg"]
            s_ref[g] = s_in * q["sc_g"] + _tn(q["xf16"], bg)

    x_spec, bm_spec, cm_spec, dt_spec, dtt_spec, prow_spec, pcol_spec, _, y_spec = _ssd_specs(nc, False)
    return pl.pallas_call(
        body, name="ssd_fwd",
        grid=(bsz, nc),
        in_specs=[x_spec, bm_spec, cm_spec, dt_spec, dtt_spec, prow_spec, pcol_spec],
        out_specs=[y_spec, _ssd_state_spec(nc, False)],
        out_shape=[jax.ShapeDtypeStruct((bsz, seq, D_INNER), F32),
                   jax.ShapeDtypeStruct((bsz, nc, SSM_GROUPS, GROUP_W, D_STATE), F32)],
        scratch_shapes=[pltpu.VMEM((SSM_GROUPS, GROUP_W, D_STATE), F32)],
        compiler_params=_params("parallel", "arbitrary"),
    )(xa, xa, xa, rest3, dtr_t, prow, pcol)


def _ssd_bwd(xa, rest3, dtr_t, prow, pcol, s_in_all, dy):
    bsz, seq, _ = xa.shape
    nc = seq // BLK

    def body(x_ref, bm_ref, cm_ref, dt_ref, dtt_ref, prow_ref, pcol_ref, sin_ref, dy_ref,
             dxa_ref, ddt_ref, ddtt_ref, gprow_ref, gpcol_ref, ds_ref, drt_ref):
        first = jnp.logical_and(pl.program_id(0) == 0, pl.program_id(1) == 0)

        @pl.when(first)
        def _():
            gprow_ref[...] = jnp.zeros_like(gprow_ref)
            gpcol_ref[...] = jnp.zeros_like(gpcol_ref)

        @pl.when(pl.program_id(1) == 0)
        def _():
            ds_ref[...] = jnp.zeros_like(ds_ref)

        drt_ref[...] = jnp.zeros_like(drt_ref)
        d = _ssd_decays(dt_ref[0], dtt_ref[0], prow_ref, pcol_ref)
        e_all, f_all, ecl_b = _ssd_common(d)
        d_c = jnp.zeros((BLK, BLK), F32)
        d_dtc = jnp.zeros((BLK, BLK), F32)
        d_clast_col = jnp.zeros((BLK, 1), F32)
        skip_rows = lax.broadcasted_iota(jnp.int32, (8, GROUP_W), 0) == P_DSKIP
        for g in range(SSM_GROUPS):
            lanes = pl.ds(g * D_STATE, D_STATE)
            bg = bm_ref[0, :, lanes].astype(BF16)
            cg = cm_ref[0, :, lanes].astype(BF16)
            g_mat = _nt(cg, bg)
            q = _ssd_group(g, d, e_all, f_all, ecl_b, x_ref, prow_ref)
            s_in = sin_ref[0, 0, g]
            s_in16 = s_in.astype(BF16)
            d_y = dy_ref[0, :, pl.ds(g * GROUP_W, GROUP_W)]
            d_y16 = d_y.astype(BF16)
            d_so = ds_ref[g]
            d_so16 = d_so.astype(BF16)
            d_g = jnp.zeros((BLK, BLK), F32)
            ws, d_xds = [], []
            for j in range(HEADS_PER_SSM_GROUP):
                h = g * HEADS_PER_SSM_GROUP + j
                head = slice(j * HEAD_DIM, (j + 1) * HEAD_DIM)
                lm, m = _ssd_mask_decay(d, h, g_mat)
                d_m = _nt(d_y16[:, head], q["xd16"][:, head])
                d_xds.append(_tn(m.astype(BF16), d_y16[:, head]))
                w = d_m * m
                d_g = d_g + d_m * lm
                drt_ref[h:h + 1, :] = -jnp.sum(w, axis=0, keepdims=True)
                ws.append(w)
            d_c = d_c + _sel_dot(jnp.concatenate(ws, axis=1), _select_matrix((HEADS_PER_SSM_GROUP * BLK, BLK), g, 1, BLK))
            d_xd = jnp.concatenate(d_xds, axis=1)
            d_g16 = d_g.astype(BF16)
            qmat = _nt(cg, s_in16)
            d_q16 = (d_y * q["e_g"]).astype(BF16)
            d_cg = jnp.dot(d_q16, s_in16, preferred_element_type=F32) + jnp.dot(d_g16, bg, preferred_element_type=F32)
            d_sin = _tn(d_q16, cg) + d_so * q["sc_g"]
            d_clast_col = d_clast_col + jnp.sum(_sel_dot(q["spread"], d_so * s_in * q["sc_g"]), axis=1, keepdims=True)
            d_xf = _nt(bg, d_so16)
            d_bg = jnp.dot(q["xf16"], d_so16, preferred_element_type=F32) + _tn(d_g16, cg)
            d_xd = d_xd + d_xf * q["f_g"]
            r_e = _sel_dot(d_y * qmat * q["e_g"], q["gather"])
            r_f = _sel_dot(d_xf * q["xd"] * q["f_g"], q["gather"])
            d_c = d_c + r_e - r_f + jnp.where(d["last_row"], jnp.sum(r_f, axis=0, keepdims=True), 0.0)
            d_dtc = d_dtc + _sel_dot(d_xd * q["xg"], q["gather"])
            skip_sum = jnp.where(skip_rows, jnp.sum(d_y * q["xg"], axis=0, keepdims=True), 0.0)
            gprow_ref[...] += _sel_dot(skip_sum, q["gather"])
            dxa_ref[0, :, pl.ds(g * GROUP_W, GROUP_W)] = q["dsk_g"] * d_y + d_xd * q["dt_g"]
            dxa_ref[0, :, pl.ds(D_INNER + g * D_STATE, D_STATE)] = d_bg
            dxa_ref[0, :, pl.ds(D_INNER + (SSM_GROUPS + g) * D_STATE, D_STATE)] = d_cg
            ds_ref[g] = d_sin
        drt_ref[:, BLK - 1:BLK] += d_clast_col
        d_a = _dot_hi(d["tri_u"], d_c)
        d_pre = (d_dtc + d_a * d["ah_row"]) * _sigmoid(d["pre"])
        ddt_ref[0] = d_pre
        gprow_ref[P_DTB:P_DTB + 1, :] += jnp.sum(d_pre, axis=0, keepdims=True)
        gprow_ref[P_ALOG:P_ALOG + 1, :] += jnp.sum(d_a * d["dt"], axis=0, keepdims=True) * d["ah_row"]
        d_at = _dot_hi(drt_ref[...], d["tri"])
        d_pre_t = d_at * d["ah_col"] * _sigmoid(d["pre_t"])
        ddtt_ref[0] = d_pre_t
        gpcol_ref[:, P_DTB:P_DTB + 1] += jnp.sum(d_pre_t, axis=1, keepdims=True)
        gpcol_ref[:, P_ALOG:P_ALOG + 1] += jnp.sum(d_at * d["dt_t"], axis=1, keepdims=True) * d["ah_col"]

    x_spec, bm_spec, cm_spec, dt_spec, dtt_spec, prow_spec, pcol_spec, _, y_spec = _ssd_specs(nc, True)
    return pl.pallas_call(
        body, name="ssd_bwd",
        grid=(bsz, nc),
        in_specs=[x_spec, bm_spec, cm_spec, dt_spec, dtt_spec, prow_spec, pcol_spec, _ssd_state_spec(nc, True), y_spec],
        out_specs=[pl.BlockSpec((1, BLK, CONV_DIM), lambda b, c: (b, nc - 1 - c, 0)),
                   pl.BlockSpec((1, BLK, BLK), lambda b, c: (b, nc - 1 - c, 0)),
                   pl.BlockSpec((1, BLK, BLK), lambda b, c: (b, 0, nc - 1 - c)),
                   prow_spec, pcol_spec],
        out_shape=[jax.ShapeDtypeStruct((bsz, seq, CONV_DIM), F32),
                   jax.ShapeDtypeStruct((bsz, seq, BLK), F32),
                   jax.ShapeDtypeStruct((bsz, BLK, seq), F32),
                   jax.ShapeDtypeStruct((8, BLK), F32),
                   jax.ShapeDtypeStruct((BLK, 8), F32)],
        scratch_shapes=[pltpu.VMEM((SSM_GROUPS, GROUP_W, D_STATE), F32), pltpu.VMEM((BLK, BLK), F32)],
        compiler_params=_params("arbitrary", "arbitrary"),
    )(xa, xa, xa, rest3, dtr_t, prow, pcol, s_in_all, dy)


ROW_TILE = 256
CMB_COLS = 256
RMS_COLS = D_INNER // SSM_GROUPS


def _combine_weights(l_refs):
    ls = [r[...] for r in l_refs]
    mx = jnp.maximum(jnp.maximum(ls[0], ls[1]), ls[2])
    es = [jnp.exp(l - mx) for l in ls]
    inv = 1.0 / (es[0] + es[1] + es[2])
    return [e * inv for e in es]


def _combine_specs():
    a = pl.BlockSpec((ROW_TILE, CMB_COLS), lambda i, j: (i, j))
    gatt = pl.BlockSpec((ROW_TILE, CMB_COLS), lambda i, j: (i, R_GATT // CMB_COLS + j))
    return a, gatt


def _combine_fwd(outs, lses, rest):
    t = rest.shape[0]

    def body(o0, o1, o2, l0, l1, l2, ga_ref, oa_ref):
        ws = _combine_weights((l0, l1, l2))
        o = ws[0] * o0[...] + ws[1] * o1[...] + ws[2] * o2[...]
        oa_ref[...] = (o * _silu_and_grad(ga_ref[...])[0]).astype(BF16)

    a, gatt = _combine_specs()
    return pl.pallas_call(
        body, name="combine_fwd",
        grid=(t // ROW_TILE, GROUP_COLS // CMB_COLS),
        in_specs=[a] * 6 + [gatt],
        out_specs=a,
        out_shape=jax.ShapeDtypeStruct((t, GROUP_COLS), BF16),
        compiler_params=_params("parallel", "parallel"),
    )(*outs, *lses, rest)


def _combine_bwd(outs, lses, rest, d_oa, d_hcat):
    t = rest.shape[0]

    def body(o0, o1, o2, l0, l1, l2, ga_ref, doa_ref, _, do0, do1, do2, dd0, dd1, dd2, dga_ref):
        ws = _combine_weights((l0, l1, l2))
        o = ws[0] * o0[...] + ws[1] * o1[...] + ws[2] * o2[...]
        sg, dsg = _silu_and_grad(ga_ref[...])
        d_oa_v = doa_ref[...]
        d_o = d_oa_v * sg
        dga_ref[...] = (d_oa_v * o * dsg).astype(BF16)
        for w, do_ref, dd_ref in zip(ws, (do0, do1, do2), (dd0, dd1, dd2)):
            d_out = w * d_o
            do_ref[...] = d_out
            dd_ref[...] = d_out * o

    a, gatt = _combine_specs()
    s32 = jax.ShapeDtypeStruct((t, GROUP_COLS), F32)
    return pl.pallas_call(
        body, name="combine_bwd",
        grid=(t // ROW_TILE, GROUP_COLS // CMB_COLS),
        in_specs=[a] * 6 + [gatt, a, pl.BlockSpec(memory_space=pl.ANY)],
        out_specs=[a] * 6 + [gatt],
        out_shape=[s32, s32, s32, s32, s32, s32, jax.ShapeDtypeStruct(d_hcat.shape, d_hcat.dtype)],
        input_output_aliases={8: 6},
        compiler_params=_params("parallel", "parallel"),
    )(*outs, *lses, rest, d_oa, d_hcat)


def _gatenorm_fwd(y, rest, norm_w):
    t = rest.shape[0]

    def body(y_ref, z_ref, w_ref, o_ref):
        u = y_ref[...] * _silu_and_grad(z_ref[...])[0]
        rs = lax.rsqrt(jnp.mean(u * u, axis=-1, keepdims=True) + RMS_EPS)
        o_ref[...] = (u * rs * w_ref[...]).astype(BF16)

    return pl.pallas_call(
        body, name="gatenorm_fwd",
        grid=(t // ROW_TILE, SSM_GROUPS),
        in_specs=[pl.BlockSpec((ROW_TILE, RMS_COLS), lambda i, j: (i, j)),
                  pl.BlockSpec((ROW_TILE, RMS_COLS), lambda i, j: (i, R_Z // RMS_COLS + j)),
                  pl.BlockSpec((1, RMS_COLS), lambda i, j: (0, j))],
        out_specs=pl.BlockSpec((ROW_TILE, RMS_COLS), lambda i, j: (i, j)),
        out_shape=jax.ShapeDtypeStruct((t, D_INNER), BF16),
        compiler_params=_params("parallel", "parallel"),
    )(y, rest, norm_w)


def _gatenorm_bwd(y, rest, norm_w, d_ys, d_hcat):
    t = rest.shape[0]

    def body(y_ref, z_ref, w_ref, g_ref, _, dy_ref, dz_ref, dw_ref):
        @pl.when(pl.program_id(1) == 0)
        def _():
            dw_ref[...] = jnp.zeros_like(dw_ref)

        yv = y_ref[...]
        sz, dsz = _silu_and_grad(z_ref[...])
        u = yv * sz
        rs = lax.rsqrt(jnp.mean(u * u, axis=-1, keepdims=True) + RMS_EPS)
        un = u * rs
        g = g_ref[...]
        dw_ref[0:1, :] += jnp.sum(g * un, axis=0, keepdims=True)
        d_un = g * w_ref[...]
        d_u = rs * (d_un - un * jnp.mean(d_un * un, axis=-1, keepdims=True))
        dy_ref[...] = d_u * sz
        dz_ref[...] = (d_u * yv * dsz).astype(BF16)

    blk = pl.BlockSpec((ROW_TILE, RMS_COLS), lambda j, i: (i, j))
    z_blk = pl.BlockSpec((ROW_TILE, RMS_COLS), lambda j, i: (i, R_Z // RMS_COLS + j))
    return pl.pallas_call(
        body, name="gatenorm_bwd",
        grid=(SSM_GROUPS, t // ROW_TILE),
        in_specs=[blk, z_blk, pl.BlockSpec((1, RMS_COLS), lambda j, i: (0, j)), blk, pl.BlockSpec(memory_space=pl.ANY)],
        out_specs=[blk, z_blk, pl.BlockSpec((8, RMS_COLS), lambda j, i: (0, j))],
        out_shape=[jax.ShapeDtypeStruct((t, D_INNER), F32), jax.ShapeDtypeStruct(d_hcat.shape, d_hcat.dtype),
                   jax.ShapeDtypeStruct((8, D_INNER), F32)],
        input_output_aliases={4: 1},
        compiler_params=_params("parallel", "arbitrary"),
    )(y, rest, norm_w, d_ys, d_hcat)


def _row_specs():
    full = pl.BlockSpec((ROW_TILE, D_MODEL), lambda i: (i, 0))
    vec = pl.BlockSpec((8, D_MODEL), lambda i: (0, 0))
    at = lambda off: pl.BlockSpec((ROW_TILE, D_MODEL), lambda i: (i, off // D_MODEL))
    return full, vec, at


def _merge_fwd(y_a, y_b, rest, b_gate):
    t = rest.shape[0]

    def body(ya_ref, yb_ref, ga_ref, gb_ref, bg_ref, o_ref):
        sa = _sigmoid(ga_ref[...] + bg_ref[0:1, :])
        sb = _sigmoid(gb_ref[...] + bg_ref[1:2, :])
        o_ref[...] = (sa * ya_ref[...] + sb * yb_ref[...]).astype(BF16)

    full, vec, at = _row_specs()
    return pl.pallas_call(
        body, name="merge_fwd",
        grid=(t // ROW_TILE,),
        in_specs=[full, full, at(R_GM), at(R_GM + D_MODEL), vec],
        out_specs=full,
        out_shape=jax.ShapeDtypeStruct((t, D_MODEL), BF16),
        compiler_params=_params("parallel"),
    )(y_a, y_b, rest, rest, b_gate)


def _merge_bwd(y_a, y_b, rest, b_gate, d_merged, d_hcat):
    t = rest.shape[0]

    def body(ya_ref, yb_ref, ga_ref, gb_ref, bg_ref, dm_ref, _, dya_ref, dyb_ref, dg_ref, dbg_ref):
        @pl.when(pl.program_id(0) == 0)
        def _():
            dbg_ref[...] = jnp.zeros_like(dbg_ref)

        dm = dm_ref[...]
        for row, y_ref, g_ref, dy_ref in ((0, ya_ref, ga_ref, dya_ref), (1, yb_ref, gb_ref, dyb_ref)):
            s = _sigmoid(g_ref[...] + bg_ref[row:row + 1, :])
            dy_ref[...] = (dm * s).astype(BF16)
            dg = dm * y_ref[...] * s * (1.0 - s)
            dg_ref[:, row * D_MODEL:(row + 1) * D_MODEL] = dg.astype(BF16)
            dbg_ref[row:row + 1, :] += jnp.sum(dg, axis=0, keepdims=True)

    full, vec, at = _row_specs()
    s16 = jax.ShapeDtypeStruct((t, D_MODEL), BF16)
    return pl.pallas_call(
        body, name="merge_bwd",
        grid=(t // ROW_TILE,),
        in_specs=[full, full, at(R_GM), at(R_GM + D_MODEL), vec, full, pl.BlockSpec(memory_space=pl.ANY)],
        out_specs=[full, full, pl.BlockSpec((ROW_TILE, 2 * D_MODEL), lambda i: (i, R_GM // (2 * D_MODEL))), vec],
        out_shape=[s16, s16, jax.ShapeDtypeStruct(d_hcat.shape, d_hcat.dtype), jax.ShapeDtypeStruct((8, D_MODEL), F32)],
        input_output_aliases={6: 2},
        compiler_params=_params("arbitrary"),
    )(y_a, y_b, rest, rest, b_gate, d_merged, d_hcat)


ST_LNG, ST_LNB, ST_BG2, ST_LOSS = 0, 1, 2, 3


def _final(x, mix, pw, rest, b_gate, ln_gb, target):
    t = rest.shape[0]

    def body(x_ref, mix_ref, pw_ref, gp_ref, bg_ref, ln_ref, tgt_ref, dpre_ref, dpre16_ref, dgp_ref, dpw_ref, st_ref):
        @pl.when(pl.program_id(0) == 0)
        def _():
            st_ref[...] = jnp.zeros_like(st_ref)

        sp = _sigmoid(gp_ref[...] + bg_ref[2:3, :])
        pw = pw_ref[...]
        pre = ALPHA * x_ref[...] + mix_ref[...] + sp * pw
        xc = pre - jnp.mean(pre, axis=-1, keepdims=True)
        rstd = lax.rsqrt(jnp.mean(xc * xc, axis=-1, keepdims=True) + LN_EPS)
        xhat = xc * rstd
        gain = ln_ref[0:1, :]
        err = xhat * gain + ln_ref[1:2, :] - tgt_ref[...]
        d_yo = err * (1.0 / D_MODEL)
        d_xhat = d_yo * gain
        d_pre = rstd * (d_xhat - jnp.mean(d_xhat, axis=-1, keepdims=True)
                        - xhat * jnp.mean(d_xhat * xhat, axis=-1, keepdims=True))
        dpre_ref[...] = d_pre
        dpre16_ref[...] = d_pre.astype(BF16)
        dgp = d_pre * pw * sp * (1.0 - sp)
        dgp_ref[...] = dgp.astype(BF16)
        dpw_ref[...] = (d_pre * sp).astype(BF16)
        st_ref[ST_LNG:ST_LNG + 1, :] += jnp.sum(d_yo * xhat, axis=0, keepdims=True)
        st_ref[ST_LNB:ST_LNB + 1, :] += jnp.sum(d_yo, axis=0, keepdims=True)
        st_ref[ST_BG2:ST_BG2 + 1, :] += jnp.sum(dgp, axis=0, keepdims=True)
        st_ref[ST_LOSS:ST_LOSS + 1, :] += jnp.sum(err * err, axis=0, keepdims=True) * (0.5 / D_MODEL)

    full, vec, at = _row_specs()
    s16 = jax.ShapeDtypeStruct((t, D_MODEL), BF16)
    return pl.pallas_call(
        body, name="final",
        grid=(t // ROW_TILE,),
        in_specs=[full, full, full, at(R_GPLE), vec, vec, full],
        out_specs=[full, full, at(R_GPLE), full, vec],
        out_shape=[jax.ShapeDtypeStruct((t, D_MODEL), F32), s16, jax.ShapeDtypeStruct((t, HCAT_COLS), BF16), s16,
                   jax.ShapeDtypeStruct((8, D_MODEL), F32)],
        compiler_params=_params("arbitrary"),
    )(x, mix, pw, rest, b_gate, ln_gb, target)


def _mesh_position():
    return lax.axis_index("x"), lax.axis_index("y"), lax.axis_index("c")


def _flip(pos, k):
    x, y, c = pos
    return ((1 - x) if k & 4 else x, (1 - y) if k & 2 else y, (1 - c) if k & 1 else c)


def _linear(pos):
    return 4 * pos[0] + 2 * pos[1] + pos[2]


def _exchange(arrays, scatter, name):
    n = len(arrays)

    def body(*refs):
        ins, outs = refs[:n], refs[n:2 * n]
        send_sems, recv_sems, local_sems = refs[2 * n:]
        me = _mesh_position()
        me_i = _linear(me)

        def src_for(i, dest_i):
            return ins[i].at[dest_i] if scatter[i] else ins[i]

        local = [pltpu.make_async_copy(src_for(i, me_i), outs[i].at[me_i], local_sems.at[i]) for i in range(n)]
        for cp in local:
            cp.start()
        started = []
        for k in range(1, N_DEV):
            peer = _flip(me, k)
            peer_i = _linear(peer)
            for i in range(n):
                sem = i * (N_DEV - 1) + k - 1
                cp = pltpu.make_async_remote_copy(
                    src_ref=src_for(i, peer_i), dst_ref=outs[i].at[me_i], send_sem=send_sems.at[sem],
                    recv_sem=recv_sems.at[sem], device_id=peer, device_id_type=pl.DeviceIdType.MESH)
                cp.start()
                started.append(cp)
        for k in range(1, N_DEV):
            peer = _flip(me, k)
            peer_i = _linear(peer)
            for i in range(n):
                sem = i * (N_DEV - 1) + k - 1
                pltpu.make_async_remote_copy(
                    src_ref=src_for(i, peer_i), dst_ref=outs[i].at[peer_i], send_sem=send_sems.at[sem],
                    recv_sem=recv_sems.at[sem], device_id=peer, device_id_type=pl.DeviceIdType.MESH).wait_recv()
        for cp in started:
            cp.wait_send()
        for cp in local:
            cp.wait()

    any_spec = pl.BlockSpec(memory_space=pl.ANY)
    out_shape = [jax.ShapeDtypeStruct(a.shape if s else (N_DEV,) + a.shape, a.dtype) for a, s in zip(arrays, scatter)]
    return pl.pallas_call(
        body, name=name,
        in_specs=[any_spec] * n,
        out_specs=[any_spec] * n,
        out_shape=out_shape,
        scratch_shapes=[pltpu.SemaphoreType.DMA((n * (N_DEV - 1),)), pltpu.SemaphoreType.DMA((n * (N_DEV - 1),)),
                        pltpu.SemaphoreType.DMA((n,))],
        compiler_params=pltpu.CompilerParams(has_side_effects=True),
    )(*arrays)


N_CHIPS = N_DEV // 2


def _other_chips(x, y):
    return [(1 - x, y), (x, 1 - y), (1 - x, 1 - y)]


def _gather_two_level(arrays, name):
    n = len(arrays)
    per = N_DEV - 1

    def body(*refs):
        ins, outs = refs[:n], refs[n:2 * n]
        send_sems, recv_sems, local_sems = refs[2 * n:]
        x, y, c = _mesh_position()
        me, sibling = (x, y, c), (x, y, 1 - c)
        chips = _other_chips(x, y)

        def copy(i, k, block, to, src=None):
            slot = outs[i].at[_linear(block)]
            return pltpu.make_async_remote_copy(
                src_ref=slot if src is None else src, dst_ref=slot, send_sem=send_sems.at[i * per + k],
                recv_sem=recv_sems.at[i * per + k], device_id=to, device_id_type=pl.DeviceIdType.MESH)

        local = [pltpu.make_async_copy(ins[i], outs[i].at[_linear(me)], local_sems.at[i]) for i in range(n)]
        for cp in local:
            cp.start()
        started = []
        for i in range(n):
            first = [copy(i, 0, me, sibling, src=ins[i])]
            first += [copy(i, 1 + j, me, (*chip, c), src=ins[i]) for j, chip in enumerate(chips)]
            for cp in first:
                cp.start()
            started += first
        for j, chip in enumerate(chips):
            for i in range(n):
                copy(i, 1 + j, (*chip, c), me).wait_recv()
                passed = copy(i, 4 + j, (*chip, c), sibling)
                passed.start()
                started.append(passed)
        for i in range(n):
            copy(i, 0, sibling, me).wait_recv()
            for j, chip in enumerate(chips):
                copy(i, 4 + j, (*chip, 1 - c), me).wait_recv()
        for cp in started:
            cp.wait_send()
        for cp in local:
            cp.wait()

    any_spec = pl.BlockSpec(memory_space=pl.ANY)
    return pl.pallas_call(
        body, name=name,
        in_specs=[any_spec] * n,
        out_specs=[any_spec] * n,
        out_shape=[jax.ShapeDtypeStruct((N_DEV,) + a.shape, a.dtype) for a in arrays],
        scratch_shapes=[pltpu.SemaphoreType.DMA((n * per,)), pltpu.SemaphoreType.DMA((n * per,)), pltpu.SemaphoreType.DMA((n,))],
        compiler_params=pltpu.CompilerParams(has_side_effects=True),
    )(*arrays)


def _pair_reduce(a, name, rows):
    _, r, c = a.shape
    assert r % rows == 0
    n_steps = r // rows
    a5 = a.reshape(N_CHIPS, 2, r, c)
    core = lax.axis_index("c").astype(jnp.int32).reshape(1)

    def body(core_ref, keep_ref, send_ref, o_ref, land, send_sems, recv_sems, credits):
        i = pl.program_id(0)
        slot = i % 2
        x, y, cc = _mesh_position()
        sibling = (x, y, 1 - cc)

        @pl.when(i >= 2)
        def _():
            pl.semaphore_wait(credits.at[slot], 1)

        rdma = pltpu.make_async_remote_copy(
            src_ref=send_ref, dst_ref=land.at[slot], send_sem=send_sems.at[slot], recv_sem=recv_sems.at[slot],
            device_id=sibling, device_id_type=pl.DeviceIdType.MESH)
        rdma.start()
        rdma.wait_recv()
        o_ref[...] = (keep_ref[:, 0].astype(F32) + land[slot, :, 0].astype(F32)).astype(o_ref.dtype)
        rdma.wait_send()

        @pl.when(i + 2 < n_steps)
        def _():
            pl.semaphore_signal(credits.at[slot], inc=1, device_id=sibling, device_id_type=pl.DeviceIdType.MESH)

    grid_spec = pltpu.PrefetchScalarGridSpec(
        num_scalar_prefetch=1,
        grid=(n_steps,),
        in_specs=[pl.BlockSpec((N_CHIPS, 1, rows, c), lambda i, core_ref: (0, core_ref[0], i, 0)),
                  pl.BlockSpec((N_CHIPS, 1, rows, c), lambda i, core_ref: (0, 1 - core_ref[0], i, 0))],
        out_specs=pl.BlockSpec((N_CHIPS, rows, c), lambda i, core_ref: (0, i, 0)),
        scratch_shapes=[pltpu.VMEM((2, N_CHIPS, 1, rows, c), a.dtype), pltpu.SemaphoreType.DMA((2,)),
                        pltpu.SemaphoreType.DMA((2,)), pltpu.SemaphoreType.REGULAR((2,))],
    )
    return pl.pallas_call(
        body, name=name, grid_spec=grid_spec,
        out_shape=jax.ShapeDtypeStruct((N_CHIPS, r, c), a.dtype),
        compiler_params=pltpu.CompilerParams(dimension_semantics=("arbitrary",), vmem_limit_bytes=VMEM_LIMIT_BYTES,
                                             has_side_effects=True),
    )(core, a5, a5)


def _chip_exchange(arrays, name):
    n = len(arrays)
    per = N_CHIPS - 1

    def body(*refs):
        ins, outs = refs[:n], refs[n:2 * n]
        send_sems, recv_sems, local_sems = refs[2 * n:]
        x, y, c = _mesh_position()
        me_q = 2 * x + y
        local = [pltpu.make_async_copy(ins[i].at[me_q], outs[i].at[me_q], local_sems.at[i]) for i in range(n)]
        for cp in local:
            cp.start()
        started = []
        for j, (px, py) in enumerate(_other_chips(x, y)):
            for i in range(n):
                cp = pltpu.make_async_remote_copy(
                    src_ref=ins[i].at[2 * px + py], dst_ref=outs[i].at[me_q], send_sem=send_sems.at[i * per + j],
                    recv_sem=recv_sems.at[i * per + j], device_id=(px, py, c), device_id_type=pl.DeviceIdType.MESH)
                cp.start()
                started.append(cp)
        for j, (px, py) in enumerate(_other_chips(x, y)):
            for i in range(n):
                pltpu.make_async_remote_copy(
                    src_ref=ins[i].at[2 * px + py], dst_ref=outs[i].at[2 * px + py], send_sem=send_sems.at[i * per + j],
                    recv_sem=recv_sems.at[i * per + j], device_id=(px, py, c), device_id_type=pl.DeviceIdType.MESH).wait_recv()
        for cp in started:
            cp.wait_send()
        for cp in local:
            cp.wait()

    any_spec = pl.BlockSpec(memory_space=pl.ANY)
    return pl.pallas_call(
        body, name=name,
        in_specs=[any_spec] * n,
        out_specs=[any_spec] * n,
        out_shape=[jax.ShapeDtypeStruct(a.shape, a.dtype) for a in arrays],
        scratch_shapes=[pltpu.SemaphoreType.DMA((n * per,)), pltpu.SemaphoreType.DMA((n * per,)), pltpu.SemaphoreType.DMA((n,))],
        compiler_params=pltpu.CompilerParams(has_side_effects=True),
    )(*arrays)


def _chip_exchange_start(arrays, name):
    n = len(arrays)
    per = N_CHIPS - 1
    hbm = pl.BlockSpec(memory_space=pltpu.HBM)
    sem = pl.BlockSpec(memory_space=pltpu.SEMAPHORE)

    def body(*refs):
        ins, lands = refs[:n], refs[n:2 * n]
        send_sems, recv_sems = refs[2 * n], refs[2 * n + 1]
        token = refs[-1]
        x, y, c = _mesh_position()
        me_q = 2 * x + y
        for j, (px, py) in enumerate(_other_chips(x, y)):
            for i in range(n):
                pltpu.make_async_remote_copy(
                    src_ref=ins[i].at[2 * px + py], dst_ref=lands[i].at[me_q], send_sem=send_sems.at[i * per + j],
                    recv_sem=recv_sems.at[i * per + j], device_id=(px, py, c), device_id_type=pl.DeviceIdType.MESH).start()
        token[...] = jnp.zeros_like(token)

    buffers = [pltpu.HBM(a.shape, a.dtype) for a in arrays]
    res = pl.pallas_call(
        body, name=name,
        out_shape=(pltpu.SemaphoreType.DMA((n * per,)), pltpu.SemaphoreType.DMA((n * per,)), *buffers, *buffers,
                   jax.ShapeDtypeStruct((8, BLK), F32)),
        in_specs=[hbm] * (2 * n),
        out_specs=(sem, sem, *([hbm] * (2 * n)), pl.BlockSpec(memory_space=pltpu.VMEM)),
        input_output_aliases={i: 2 + i for i in range(2 * n)},
        compiler_params=pltpu.CompilerParams(has_side_effects=pltpu.SideEffectType.DATAFLOW_SIDE_EFFECTING),
    )(*[pltpu.with_memory_space_constraint(a, pltpu.HBM) for a in arrays],
      *[pltpu.with_memory_space_constraint(lax.empty(a.shape, a.dtype), pltpu.HBM) for a in arrays])
    return res[0], res[1], res[2:2 + n], res[2 + n:2 + 2 * n], res[-1]


def _chip_exchange_wait(send_sems, recv_sems, sources, lands, after, name):
    n = len(sources)
    per = N_CHIPS - 1
    hbm = pl.BlockSpec(memory_space=pltpu.HBM)
    sem = pl.BlockSpec(memory_space=pltpu.SEMAPHORE)

    def body(*refs):
        ins, zones = refs[:n], refs[n:2 * n]
        send, recv = refs[2 * n], refs[2 * n + 1]
        x, y, c = _mesh_position()
        for j, (px, py) in enumerate(_other_chips(x, y)):
            for i in range(n):
                cp = pltpu.make_async_remote_copy(
                    src_ref=ins[i].at[2 * px + py], dst_ref=zones[i].at[2 * px + py], send_sem=send.at[i * per + j],
                    recv_sem=recv.at[i * per + j], device_id=(px, py, c), device_id_type=pl.DeviceIdType.MESH)
                cp.wait_send()
                cp.wait_recv()

    buffers = [pltpu.HBM(a.shape, a.dtype) for a in sources]
    res = pl.pallas_call(
        body, name=name,
        out_shape=(*buffers, *buffers),
        in_specs=[hbm] * (2 * n) + [sem, sem, pl.BlockSpec(memory_space=pl.ANY)],
        out_specs=[hbm] * (2 * n),
        input_output_aliases={i: i for i in range(2 * n)},
        compiler_params=pltpu.CompilerParams(has_side_effects=pltpu.SideEffectType.DATAFLOW_SIDE_EFFECTING),
    )(*sources, *lands, send_sems, recv_sems, after)
    return res[:n], res[n:]


def _adam_reduce(parts, w, m, v, name, rows):
    r, c = w.shape
    n_parts = parts.shape[0]
    assert r % rows == 0
    c1 = 1.0 - ADAM_B1 ** ADAM_STEP
    c2 = 1.0 - ADAM_B2 ** ADAM_STEP

    def body(p_ref, w_ref, m_ref, v_ref, g_ref, d_ref, nm_ref, nv_ref):
        g = p_ref[0].astype(F32)
        for s in range(1, n_parts):
            g = g + p_ref[s].astype(F32)
        g_ref[...] = g
        nm = ADAM_B1 * m_ref[...] + (1.0 - ADAM_B1) * g
        nv = ADAM_B2 * v_ref[...] + (1.0 - ADAM_B2) * (g * g)
        nm_ref[...] = nm
        nv_ref[...] = nv
        d_ref[...] = -ADAM_LR * ((nm / c1) / (jnp.sqrt(nv / c2) + ADAM_EPS) + ADAM_WD * w_ref[...])

    blk = pl.BlockSpec((rows, c), lambda i: (i, 0))
    shape = jax.ShapeDtypeStruct((r, c), F32)
    return pl.pallas_call(
        body, name=name,
        grid=(r // rows,),
        in_specs=[pl.BlockSpec((n_parts, rows, c), lambda i: (0, i, 0)), blk, blk, blk],
        out_specs=[blk] * 4,
        out_shape=[shape] * 4,
        compiler_params=_params("parallel"),
    )(parts, w, m, v)


def _lane_total(rows8):
    def body(a_ref, o_ref):
        o_ref[...] = _sum_all(a_ref[...])

    return pl.pallas_call(body, name="loss_total", out_shape=jax.ShapeDtypeStruct((1, 1), F32))(rows8)


def _permute_w_in(w):
    rows = w.shape[0]
    n_pairs = N_GROUPS * HEAD_PAIRS
    qkv = w[:, :V_END].reshape(rows, 3, n_pairs, BLK).transpose(0, 2, 1, 3).reshape(rows, V_END)
    return jnp.concatenate(
        [w[:, DT_END:GMERGE_END], w[:, GATT_END:Z_END], w[:, Z_END:XBC_END], w[:, GMERGE_END:], w[:, V_END:GATT_END],
         w[:, XBC_END:DT_END], jnp.zeros((rows, QKV_OFF - R_DT - SSM_HEADS), w.dtype), qkv], axis=1)


def _unpermute_w_in(g):
    rows = g.shape[0]
    n_pairs = N_GROUPS * HEAD_PAIRS
    qkv = g[:, QKV_OFF:].reshape(rows, n_pairs, 3, BLK).transpose(0, 2, 1, 3).reshape(rows, V_END)
    return jnp.concatenate(
        [qkv, g[:, R_GATT:R_GATT + GROUP_COLS], g[:, R_Z:R_Z + D_INNER], g[:, R_XBC:R_XBC + CONV_DIM],
         g[:, R_DT:R_DT + SSM_HEADS], g[:, R_GM:R_GM + 2 * D_MODEL], g[:, R_GPLE:R_GPLE + D_MODEL]], axis=1)


def _column_runs():
    runs = [(DT_END, R_GM, 2 * D_MODEL), (GATT_END, R_Z, D_INNER), (Z_END, R_XBC, CONV_DIM), (GMERGE_END, R_GPLE, D_MODEL),
            (V_END, R_GATT, GROUP_COLS), (XBC_END, R_DT, SSM_HEADS)]
    for pair in range(N_GROUPS * HEAD_PAIRS):
        for part in range(3):
            runs.append((part * ATT_QKV + pair * BLK, QKV_OFF + (pair * 3 + part) * BLK, BLK))
    return runs


RELAYOUT_ROWS = 128


def _assemble_w_in(gathered):
    rows = gathered.shape[1]

    def pieces(ref_col, width):
        out = []
        while width > 0:
            s, o = divmod(ref_col, IN_SHARD)
            w = min(width, IN_SHARD - o)
            out.append((s, o, w))
            ref_col, width = ref_col + w, width - w
        return out

    blocks = {}
    for ref_col, col, width in _column_runs():
        for off in range(0, width, BLK):
            blocks[(col + off) // BLK] = (ref_col + off, min(BLK, width - off))

    def body(g_ref, o_ref):
        for blk in range(HCAT_COLS // BLK):
            lanes = pl.ds(blk * BLK, BLK)
            if blk not in blocks:
                o_ref[:, lanes] = jnp.zeros((RELAYOUT_ROWS, BLK), o_ref.dtype)
                continue
            ref_col, width = blocks[blk]
            parts = [g_ref[s, :, pl.ds(o, w)] for s, o, w in pieces(ref_col, width)]
            if width < BLK:
                parts.append(jnp.zeros((RELAYOUT_ROWS, BLK - width), o_ref.dtype))
            o_ref[:, lanes] = parts[0] if len(parts) == 1 else jnp.concatenate(parts, axis=1)

    return pl.pallas_call(
        body, name="assemble_w_in",
        grid=(rows // RELAYOUT_ROWS,),
        in_specs=[pl.BlockSpec((N_DEV, RELAYOUT_ROWS, IN_SHARD), lambda i: (0, i, 0))],
        out_specs=pl.BlockSpec((RELAYOUT_ROWS, HCAT_COLS), lambda i: (i, 0)),
        out_shape=jax.ShapeDtypeStruct((rows, HCAT_COLS), gathered.dtype),
        compiler_params=_params("parallel"),
    )(gathered)


def _scatter_w_in_grad(g):
    rows = g.shape[0]
    runs = sorted(_column_runs())

    def pieces(shard):
        lo, hi = shard * IN_SHARD, (shard + 1) * IN_SHARD
        out = []
        for ref_col, col, width in runs:
            a, b = max(lo, ref_col), min(hi, ref_col + width)
            if a < b:
                out.append((col + a - ref_col, b - a))
        return out

    def body(g_ref, o_ref):
        for shard in range(N_DEV):
            o_ref[shard] = jnp.concatenate([g_ref[:, pl.ds(col, w)] for col, w in pieces(shard)], axis=1)

    return pl.pallas_call(
        body, name="scatter_w_in_grad",
        grid=(rows // RELAYOUT_ROWS,),
        in_specs=[pl.BlockSpec((RELAYOUT_ROWS, HCAT_COLS), lambda i: (i, 0))],
        out_specs=pl.BlockSpec((N_DEV, RELAYOUT_ROWS, IN_SHARD), lambda i: (0, i, 0)),
        out_shape=jax.ShapeDtypeStruct((N_DEV, rows, IN_SHARD), g.dtype),
        compiler_params=_params("parallel"),
    )(g)


SMALL_ROWS = 80
_SMALL_LAYOUT = (("conv_b", CONV_DIM), ("dt_bias", BLK), ("a_log", BLK), ("d_skip", BLK), ("ssm_norm_w", D_INNER),
                 ("ln_g", D_MODEL), ("ln_b", D_MODEL), ("rel_bias", NUM_BUCKETS * ATT_HEADS), ("loss", D_MODEL))


def _pack_small(vals):
    flat = []
    for name, width in _SMALL_LAYOUT:
        v = vals.get(name)
        v = jnp.zeros((width,), F32) if v is None else v.reshape(-1).astype(F32)
        flat.append(jnp.pad(v, (0, width - v.shape[0])))
    flat = jnp.concatenate(flat)
    return jnp.pad(flat, (0, SMALL_ROWS * BLK - flat.shape[0])).reshape(SMALL_ROWS, BLK)


def _unpack_small(packed):
    flat = packed.reshape(-1)
    out, pos = {}, 0
    for name, width in _SMALL_LAYOUT:
        out[name] = flat[pos:pos + width]
        pos += width
    for name in ("dt_bias", "a_log", "d_skip"):
        out[name] = out[name][:SSM_HEADS]
    out["rel_bias"] = out["rel_bias"].reshape(NUM_BUCKETS, ATT_HEADS)
    return out


def _pack_cols(b_gate_part, conv_w_part):
    return jnp.concatenate([jnp.pad(b_gate_part, ((0, 5), (0, 0))), jnp.pad(conv_w_part, ((0, 4), (0, 0)))], axis=1)


def _pack_cols_all(b_gate_full, conv_w_full):
    bg = b_gate_full.reshape(3, N_DEV, BLK).transpose(1, 0, 2)
    cw = conv_w_full.reshape(CONV_WIDTH, N_DEV, CONV_DIM // N_DEV).transpose(1, 0, 2)
    return jnp.concatenate([jnp.pad(bg, ((0, 0), (0, 5), (0, 0))), jnp.pad(cw, ((0, 0), (0, 4), (0, 0)))], axis=2)


def _unpack_cols_all(packed):
    bg = packed[:, :3, :BLK].transpose(1, 0, 2).reshape(3, D_MODEL)
    cw = packed[:, :CONV_WIDTH, BLK:].transpose(1, 0, 2).reshape(CONV_WIDTH, CONV_DIM)
    return bg, cw


def _local_step(x, p, target, wp16, wb16, wo16, wple16, b_gate, conv_w, small, start_exchange):
    bsz, seq, _ = x.shape
    t = bsz * seq
    x2 = x.reshape(t, D_MODEL)
    x16 = x2.astype(BF16)
    p16 = p.reshape(t, PLE_DIM).astype(BF16)
    tgt2 = target.reshape(t, D_MODEL)
    b_gate8 = jnp.pad(b_gate, ((0, 5), (0, 0)))
    ln_gb = jnp.pad(jnp.stack([small["ln_g"], small["ln_b"]]), ((0, 6), (0, 0)))
    conv_b = small["conv_b"].reshape(1, CONV_DIM)
    norm_w = small["ssm_norm_w"].reshape(1, D_INNER)
    pad_heads = lambda v: jnp.pad(v, (0, BLK - SSM_HEADS))
    prow = jnp.pad(jnp.stack([pad_heads(small["dt_bias"]), pad_heads(small["a_log"]), pad_heads(small["d_skip"])]), ((0, 5), (0, 0)))
    pcol = prow.T
    wa16, wbb16 = wb16[:GROUP_COLS], wb16[GROUP_COLS:]

    rest = _matmul(x16, wp16, mode="nn", out_dtype=F32, name="inproj", tm=512, tn=2304, tk=D_MODEL, n_outer=True)
    rest3 = rest.reshape(bsz, seq, HCAT_COLS)
    biases, onehots, outs, lses = [], [], [], []
    for g, (_, dil) in enumerate(DILATED_PATTERNS):
        bias, onehot = _bias_matrix(small["rel_bias"][:, g * HEADS_PER_GROUP:(g + 1) * HEADS_PER_GROUP], dil)
        out3, lse3 = _attn_fwd(rest3, bias, g, dil, f"attn_fwd{g}")
        biases.append(bias), onehots.append(onehot)
        outs.append(out3.reshape(t, GROUP_COLS)), lses.append(lse3.reshape(t, GROUP_COLS))
    oa = _combine_fwd(outs, lses, rest)
    xa = _conv_fwd(rest3, conv_w, conv_b)
    dtr_t = jnp.swapaxes(rest3[:, :, R_DT:R_DT + BLK], 1, 2)
    y, s_in = _ssd_fwd(xa, rest3, dtr_t, prow, pcol)
    y2 = y.reshape(t, D_INNER)
    ys = _gatenorm_fwd(y2, rest, norm_w)
    y_a = _matmul(oa, wa16, mode="nn", out_dtype=F32, name="branch_a", tm=512, tn=D_MODEL, tk=GROUP_COLS)
    y_b = _matmul(ys, wbb16, mode="nn", out_dtype=F32, name="branch_b", tm=512, tn=D_MODEL, tk=D_INNER)
    merged = _merge_fwd(y_a, y_b, rest, b_gate8)
    mix = _matmul(merged, wo16, mode="nn", out_dtype=F32, name="out_proj", tm=512, tn=D_MODEL, tk=D_MODEL)
    pw = _matmul(p16, wple16, mode="nn", out_dtype=F32, name="ple_proj", tm=512, tn=D_MODEL, tk=PLE_DIM)
    d_pre, d_pre16, d_hcat, d_pw, stats = _final(x2, mix, pw, rest, b_gate8, ln_gb, tgt2)

    d_merged = _matmul(d_pre16, wo16, mode="nt", out_dtype=F32, name="d_merged", tm=512, tn=D_MODEL, tk=D_MODEL)
    g_w_out = _matmul(merged, d_pre16, mode="tn", out_dtype=BF16, name="g_w_out", tm=512, tn=D_MODEL, tk=1024)
    g_w_ple = _matmul(p16, d_pw, mode="tn", out_dtype=BF16, name="g_w_ple", tm=PLE_DIM, tn=D_MODEL, tk=1024)
    d_ya, d_yb, d_hcat, dbg01 = _merge_bwd(y_a, y_b, rest, b_gate8, d_merged, d_hcat)
    d_oa = _matmul(d_ya, wa16, mode="nt", out_dtype=F32, name="d_oa", tm=512, tn=GROUP_COLS, tk=D_MODEL)
    d_ys = _matmul(d_yb, wbb16, mode="nt", out_dtype=F32, name="d_ys", tm=512, tn=1024, tk=D_MODEL)
    g_wa = _matmul(oa, d_ya, mode="tn", out_dtype=BF16, name="g_w_branch_a", tm=GROUP_COLS, tn=D_MODEL, tk=1024)
    g_wb = _matmul(ys, d_yb, mode="tn", out_dtype=BF16, name="g_w_branch_b", tm=512, tn=D_MODEL, tk=1024)
    d_outs_dd_hc = _combine_bwd(outs, lses, rest, d_oa, d_hcat)
    d_outs, dds, d_hcat = d_outs_dd_hc[:3], d_outs_dd_hc[3:6], d_outs_dd_hc[6]
    d_y, d_hcat, d_nw = _gatenorm_bwd(y2, rest, norm_w, d_ys, d_hcat)
    d_xa, ddt, ddt_t, gprow, gpcol = _ssd_bwd(xa, rest3, dtr_t, prow, pcol, s_in, d_y.reshape(bsz, seq, D_INNER))
    d_hcat3, d_conv = _conv_bwd(rest3, conv_w, conv_b, d_xa, d_hcat.reshape(bsz, seq, HCAT_COLS))
    g_tables = []
    shape3 = (bsz, seq, GROUP_COLS)
    for g, (_, dil) in enumerate(DILATED_PATTERNS):
        d_hcat3, dbias = _attn_bwd(rest3, biases[g], lses[g].reshape(shape3), d_outs[g].reshape(shape3), dds[g].reshape(shape3),
                                   d_hcat3, g, dil, f"attn_bwd{g}")
        g_tables.append(_bias_grad(dbias, onehots[g], f"bias_grad{g}"))
    d_dt = (ddt + jnp.swapaxes(ddt_t, 1, 2)).reshape(t, BLK).astype(BF16)
    d_hcat = lax.dynamic_update_slice(d_hcat3.reshape(t, HCAT_COLS),
                                      jnp.concatenate([d_dt, jnp.zeros((t, QKV_OFF - R_DT - BLK), BF16)], axis=1), (0, R_DT))
    g_wp = _matmul(x16, d_hcat, mode="tn", out_dtype=BF16, name="g_w_in", tm=D_MODEL, tn=2304, tk=1024, n_outer=True)

    grads = dict(
        w_in=_scatter_w_in_grad(g_wp),
        b_gate=jnp.stack([dbg01[0], dbg01[1], stats[ST_BG2]]),
        conv_w=d_conv[:CONV_WIDTH],
        w_branch=jnp.concatenate([g_wa, g_wb], axis=0),
        w_out=g_w_out,
        w_ple=g_w_ple,
    )
    small_grads = dict(
        conv_b=d_conv[CONV_WIDTH],
        dt_bias=gprow[P_DTB, :SSM_HEADS] + gpcol[:SSM_HEADS, P_DTB],
        a_log=gprow[P_ALOG, :SSM_HEADS] + gpcol[:SSM_HEADS, P_ALOG],
        d_skip=gprow[P_DSKIP, :SSM_HEADS],
        ssm_norm_w=d_nw[0],
        ln_g=stats[ST_LNG],
        ln_b=stats[ST_LNB],
        rel_bias=jnp.concatenate(g_tables, axis=1),
        loss=stats[ST_LOSS],
    )
    in_flight, token = start_exchange(grads)
    grad_x = _matmul(d_hcat, wp16, mode="nt", out_dtype=F32, name="grad_x", tm=1024, tn=D_MODEL, tk=2304,
                     add=d_pre, add_scale=ALPHA, after=token)
    return grad_x.reshape(bsz, seq, D_MODEL), in_flight, small_grads


WEIGHT_ORDER = ("w_in", "b_gate", "conv_w", "conv_b", "dt_bias", "a_log", "d_skip", "ssm_norm_w", "w_branch", "w_out",
                "w_ple", "ln_g", "ln_b", "rel_bias")
SMALL_NAMES = ("conv_b", "dt_bias", "a_log", "d_skip", "ssm_norm_w", "ln_g", "ln_b", "rel_bias")


def kernel(x, p, w_in, b_gate, conv_w, conv_b, dt_bias, a_log, d_skip, ssm_norm_w, w_branch, w_out, w_ple, ln_g, ln_b, rel_bias, loss_target, m_w_in, m_b_gate, m_conv_w, m_conv_b, m_dt_bias, m_a_log, m_d_skip, m_ssm_norm_w, m_w_branch, m_w_out, m_w_ple, m_ln_g, m_ln_b, m_rel_bias, v_w_in, v_b_gate, v_conv_w, v_conv_b, v_dt_bias, v_a_log, v_d_skip, v_ssm_norm_w, v_w_branch, v_w_out, v_w_ple, v_ln_g, v_ln_b, v_rel_bias):
    given = dict(w_in=w_in, b_gate=b_gate, conv_w=conv_w, conv_b=conv_b, dt_bias=dt_bias, a_log=a_log, d_skip=d_skip,
                 ssm_norm_w=ssm_norm_w, w_branch=w_branch, w_out=w_out, w_ple=w_ple, ln_g=ln_g, ln_b=ln_b)
    moments_m = dict(w_in=m_w_in, b_gate=m_b_gate, conv_w=m_conv_w, conv_b=m_conv_b, dt_bias=m_dt_bias, a_log=m_a_log,
                     d_skip=m_d_skip, ssm_norm_w=m_ssm_norm_w, w_branch=m_w_branch, w_out=m_w_out, w_ple=m_w_ple,
                     ln_g=m_ln_g, ln_b=m_ln_b)
    moments_v = dict(w_in=v_w_in, b_gate=v_b_gate, conv_w=v_conv_w, conv_b=v_conv_b, dt_bias=v_dt_bias, a_log=v_a_log,
                     d_skip=v_d_skip, ssm_norm_w=v_ssm_norm_w, w_branch=v_w_branch, w_out=v_w_out, w_ple=v_w_ple,
                     ln_g=v_ln_g, ln_b=v_ln_b)
    w = {k: a[0] for k, a in given.items()} | {"rel_bias": rel_bias}
    mm = {k: a[0] for k, a in moments_m.items()} | {"rel_bias": m_rel_bias}
    vv = {k: a[0] for k, a in moments_v.items()} | {"rel_bias": v_rel_bias}

    gathered = _gather_two_level(
        [w["w_in"].astype(BF16), w["w_branch"].astype(BF16), w["w_out"].astype(BF16), w["w_ple"].astype(BF16),
         _pack_cols(w["b_gate"], w["conv_w"])], "gather_weights")
    wp16 = _assemble_w_in(gathered[0])
    wb16 = gathered[1].reshape(BRANCH_ROWS, D_MODEL)
    wo16 = gathered[2].reshape(D_MODEL, D_MODEL)
    wple16 = gathered[3].transpose(1, 0, 2).reshape(PLE_DIM, D_MODEL)
    b_gate_full, conv_w_full = _unpack_cols_all(gathered[4])
    small = {k: w[k] for k in SMALL_NAMES}

    def start_exchange(grads):
        big = [grads["w_in"],
               grads["w_branch"].astype(BF16).reshape(N_DEV, BRANCH_ROWS // N_DEV, D_MODEL),
               grads["w_out"].astype(BF16).reshape(N_DEV, D_MODEL // N_DEV, D_MODEL),
               grads["w_ple"].astype(BF16).reshape(PLE_DIM, N_DEV, BLK).transpose(1, 0, 2)]
        sums = [_pair_reduce(a, f"pair_reduce{i}", rows) for i, (a, rows) in enumerate(zip(big, (128, 176, 128, 256)))]
        send_sems, recv_sems, sources, lands, token = _chip_exchange_start(sums, "chip_exchange_start")
        cols_all = _pack_cols_all(grads["b_gate"], grads["conv_w"])
        return (send_sems, recv_sems, sources, lands, cols_all), token

    grad_x, in_flight, small_grads = _local_step(x, p[0], loss_target, wp16, wb16, wo16, wple16, b_gate_full, conv_w_full,
                                                 small, start_exchange)
    send_sems, recv_sems, sources, lands, cols_all = in_flight
    sources, zones = _chip_exchange_wait(send_sems, recv_sems, sources, lands, grad_x, "chip_exchange_wait")
    me_q = 2 * lax.axis_index("x") + lax.axis_index("y")
    parts = [lax.dynamic_update_slice_in_dim(z, lax.dynamic_slice_in_dim(s, me_q, 1, axis=0), me_q, axis=0)
             for z, s in zip(zones, sources)]
    small_parts = _exchange([cols_all, _pack_small(small_grads)], [True, False], "exchange_small")

    out = {}
    out["w_in"] = _adam_reduce(parts[0], w["w_in"], mm["w_in"], vv["w_in"], "adam_w_in", 128)
    out["w_branch"] = _adam_reduce(parts[1], w["w_branch"], mm["w_branch"], vv["w_branch"], "adam_w_branch", 176)
    out["w_out"] = _adam_reduce(parts[2], w["w_out"], mm["w_out"], vv["w_out"], "adam_w_out", 128)
    out["w_ple"] = _adam_reduce(parts[3], w["w_ple"], mm["w_ple"], vv["w_ple"], "adam_w_ple", 256)
    cols = _adam_reduce(small_parts[0], _pack_cols(w["b_gate"], w["conv_w"]), _pack_cols(mm["b_gate"], mm["conv_w"]),
                        _pack_cols(vv["b_gate"], vv["conv_w"]), "adam_cols", 8)
    out["b_gate"] = [a[:3, :BLK] for a in cols]
    out["conv_w"] = [a[:CONV_WIDTH, BLK:] for a in cols]
    packed = _adam_reduce(small_parts[1], _pack_small({k: w[k] for k in SMALL_NAMES}), _pack_small({k: mm[k] for k in SMALL_NAMES}),
                          _pack_small({k: vv[k] for k in SMALL_NAMES}), "adam_small", SMALL_ROWS)
    unpacked = [_unpack_small(a) for a in packed]
    for k in SMALL_NAMES:
        out[k] = [u[k] for u in unpacked]
    loss_rows = unpacked[0]["loss"].reshape(D_MODEL // BLK, BLK)
    loss = _lane_total(loss_rows).reshape(())

    def shaped(k, a):
        return a if k == "rel_bias" else a[None]

    results = [loss, grad_x]
    for i in range(4):
        results += [shaped(k, out[k][i]) for k in WEIGHT_ORDER]
    return tuple(results)
```

```python
import functools
import math

import jax
import jax.numpy as jnp
from jax import lax
from jax.experimental import pallas as pl
from jax.experimental.pallas import tpu as pltpu

F32 = jnp.float32
BF16 = jnp.bfloat16

N_DEV = 8
D_MODEL = 1024
SEQ = 2048
HEAD_DIM = 64
HEADS_PER_GROUP = 12
DILATED_PATTERNS = ((128, 1), (512, 4), (2048, 16))
N_GROUPS = 3
ATT_HEADS = N_GROUPS * HEADS_PER_GROUP
GROUP_COLS = HEADS_PER_GROUP * HEAD_DIM
ATT_QKV = ATT_HEADS * HEAD_DIM
BLK = 128
NUM_BUCKETS = 32
MAX_DISTANCE = 2048
D_INNER = 2048
SSM_HEADS = 32
SSM_GROUPS = 4
HEADS_PER_SSM_GROUP = SSM_HEADS // SSM_GROUPS
D_STATE = 128
CONV_WIDTH = 4
CONV_DIM = D_INNER + 2 * SSM_GROUPS * D_STATE
PLE_DIM = 256
ALPHA = 2.0 ** 0.25
LN_EPS = 1e-5
RMS_EPS = 1e-5
IN_COLS = 15904
IN_SHARD = IN_COLS // N_DEV
BRANCH_ROWS = GROUP_COLS + D_INNER

Q_END = ATT_QKV
K_END = 2 * ATT_QKV
V_END = 3 * ATT_QKV
GATT_END = V_END + GROUP_COLS
Z_END = GATT_END + D_INNER
XBC_END = Z_END + CONV_DIM
DT_END = XBC_END + SSM_HEADS
GMERGE_END = DT_END + 2 * D_MODEL

R_GM, R_Z, R_XBC, R_GPLE, R_GATT, R_DT = 0, 2048, 4096, 7168, 8192, 8960
QKV_OFF = 9216
HEAD_PAIRS = GROUP_COLS // BLK
QKV_G = 3 * GROUP_COLS
HCAT_COLS = QKV_OFF + N_GROUPS * QKV_G

ADAM_LR, ADAM_B1, ADAM_B2, ADAM_EPS, ADAM_WD, ADAM_STEP = 0.001, 0.9, 0.999, 1e-08, 0.01, 10

VMEM_LIMIT_BYTES = 56 * 1024 * 1024


def _params(*semantics):
    return pltpu.CompilerParams(dimension_semantics=semantics, vmem_limit_bytes=VMEM_LIMIT_BYTES)


def _sigmoid(v):
    return 1.0 / (1.0 + jnp.exp(-v))


def _silu_and_grad(v):
    s = _sigmoid(v)
    return v * s, s * (1.0 + v * (1.0 - s))


def _matmul(a, b, *, mode, out_dtype, name, tm, tn, tk, n_off=0, n=None, add=None, add_scale=1.0, n_outer=False, after=None):
    if mode == "nn":
        (m, k), n_full = a.shape, b.shape[1]
        assert b.shape[0] == k
    elif mode == "nt":
        (m, k), n_full = a.shape, b.shape[0]
        assert b.shape[1] == k
    else:
        (k, m), n_full = a.shape, b.shape[1]
        assert b.shape[0] == k
    n = n_full if n is None else n
    assert m % tm == 0 and n % tn == 0 and k % tk == 0 and n_off % tn == 0, (name, m, n, k)
    nk = k // tk
    jo = n_off // tn
    dims = {"nn": (((1,), (0,)), ((), ())), "nt": (((1,), (1,)), ((), ())), "tn": (((0,), (0,)), ((), ()))}[mode]

    def body(*refs):
        a_ref, b_ref = refs[:2]
        add_ref = refs[2] if add is not None else None
        o_ref = refs[2 + (add is not None) + (after is not None)]
        acc_ref = refs[-1] if nk > 1 else None
        prod = lax.dot_general(a_ref[...].astype(BF16), b_ref[...].astype(BF16), dims, preferred_element_type=F32)

        def finish(total):
            if add_ref is not None:
                total = total + add_scale * add_ref[...]
            o_ref[...] = total.astype(out_dtype)

        if nk == 1:
            finish(prod)
        else:
            kk = pl.program_id(2)

            @pl.when(kk == 0)
            def _():
                acc_ref[...] = prod

            @pl.when(jnp.logical_and(kk > 0, kk < nk - 1))
            def _():
                acc_ref[...] += prod

            @pl.when(kk == nk - 1)
            def _():
                finish(acc_ref[...] + prod)

    def ij(f):
        return (lambda g0, g1, kk: f(g1, g0, kk)) if n_outer else f

    if mode == "nn":
        a_spec = pl.BlockSpec((tm, tk), ij(lambda i, j, kk: (i, kk)))
        b_spec = pl.BlockSpec((tk, tn), ij(lambda i, j, kk: (kk, j + jo)))
    elif mode == "nt":
        a_spec = pl.BlockSpec((tm, tk), ij(lambda i, j, kk: (i, kk)))
        b_spec = pl.BlockSpec((tn, tk), ij(lambda i, j, kk: (j, kk)))
    else:
        a_spec = pl.BlockSpec((tk, tm), ij(lambda i, j, kk: (kk, i)))
        b_spec = pl.BlockSpec((tk, tn), ij(lambda i, j, kk: (kk, j)))
    in_specs = [a_spec, b_spec]
    args = [a, b]
    if add is not None:
        in_specs.append(pl.BlockSpec((tm, tn), ij(lambda i, j, kk: (i, j))))
        args.append(add)
    if after is not None:
        in_specs.append(pl.BlockSpec((8, BLK), lambda g0, g1, kk: (0, 0)))
        args.append(after)
    return pl.pallas_call(
        body, name=name,
        grid=(n // tn, m // tm, nk) if n_outer else (m // tm, n // tn, nk),
        in_specs=in_specs,
        out_specs=pl.BlockSpec((tm, tn), ij(lambda i, j, kk: (i, j))),
        out_shape=jax.ShapeDtypeStruct((m, n), out_dtype),
        scratch_shapes=[pltpu.VMEM((tm, tn), F32)] if nk > 1 else [],
        compiler_params=_params("parallel", "parallel", "arbitrary"),
    )(*args)


UNITS_PER_ITER = 4


def _band_mask(first):
    qi = lax.broadcasted_iota(jnp.int32, (BLK, BLK if first else 2 * BLK), 0)
    kj = lax.broadcasted_iota(jnp.int32, (BLK, BLK if first else 2 * BLK), 1)
    delta = qi - kj if first else qi + BLK - kj
    return jnp.logical_and(delta >= 0, delta <= BLK)


def _attn_specs(seq, g):
    qkv = [pl.BlockSpec((1, seq, BLK), functools.partial(
        lambda hp, b, part: (b, 0, QKV_OFF // BLK + (g * HEAD_PAIRS + hp) * 3 + part), part=part)) for part in range(3)]
    one = pl.BlockSpec((1, seq, BLK), lambda hp, b: (b, 0, hp))
    bias = pl.BlockSpec((2, BLK, 2 * BLK), lambda hp, b: (hp, 0, 0))
    return qkv, one, bias


def _attn_rows(dil, r, n, first):
    start = r + (dil * BLK) * n
    if dil == 1:
        start = pl.multiple_of(start, BLK)
        rows = pl.ds(start, BLK)
        keys = rows if first else pl.ds(pl.multiple_of(start - BLK, BLK), 2 * BLK)
    else:
        rows = pl.ds(start, BLK, stride=dil)
        keys = rows if first else pl.ds(start - dil * BLK, 2 * BLK, stride=dil)
    return rows, keys


def _attn_schedule(dil, nb, unit):
    def blocks_of(r):
        unit(r, 0, True)
        for n in range(1, UNITS_PER_ITER):
            unit(r, n, False)
        if nb > UNITS_PER_ITER:
            def more(i, carry):
                for jj in range(UNITS_PER_ITER):
                    unit(r, i * UNITS_PER_ITER + jj, False)
                return carry
            lax.fori_loop(1, nb // UNITS_PER_ITER, more, 0)

    if nb >= UNITS_PER_ITER:
        assert nb % UNITS_PER_ITER == 0
        if dil == 1:
            blocks_of(0)
        else:
            def per_residue(r, carry):
                blocks_of(r)
                return carry
            lax.fori_loop(0, dil, per_residue, 0)
    else:
        per_iter = UNITS_PER_ITER // nb
        assert UNITS_PER_ITER % nb == 0 and dil % per_iter == 0

        def residues(i, carry):
            for jj in range(per_iter):
                for n in range(nb):
                    unit(i * per_iter + jj, n, n == 0)
            return carry
        lax.fori_loop(0, dil // per_iter, residues, 0)


def _attn_fwd(hcat3, bias, g, dil, name):
    bsz, seq, _ = hcat3.shape
    nb = seq // dil // BLK
    scale = HEAD_DIM ** -0.5

    def body(q_ref, k_ref, v_ref, bias_ref, o_ref, lse_ref):
        masks = {True: _band_mask(True), False: _band_mask(False)}

        def unit(r, n, first):
            rows, keys = _attn_rows(dil, r, n, first)
            q2 = q_ref[0, rows, :].astype(BF16)
            k2 = k_ref[0, keys, :].astype(BF16)
            v2 = v_ref[0, keys, :].astype(BF16)
            outs, lses = [], []
            for j in range(2):
                lanes = slice(j * HEAD_DIM, (j + 1) * HEAD_DIM)
                bias_j = bias_ref[j, :, BLK:] if first else bias_ref[j]
                s = _nt(q2[:, lanes], k2[:, lanes]) * scale + bias_j
                s = jnp.where(masks[first], s, -jnp.inf)
                mx = jnp.max(s, axis=-1, keepdims=True)
                e = jnp.exp(s - mx)
                den = jnp.sum(e, axis=-1, keepdims=True)
                outs.append(jnp.dot(e.astype(BF16), v2[:, lanes], preferred_element_type=F32) / den)
                lses.append(jnp.broadcast_to(mx + jnp.log(den), (BLK, HEAD_DIM)))
            o_ref[0, rows, :] = jnp.concatenate(outs, axis=1)
            lse_ref[0, rows, :] = jnp.concatenate(lses, axis=1)

        _attn_schedule(dil, nb, unit)

    qkv_specs, one, bias_spec = _attn_specs(seq, g)
    shape = jax.ShapeDtypeStruct((bsz, seq, GROUP_COLS), F32)
    return pl.pallas_call(
        body, name=name,
        grid=(HEAD_PAIRS, bsz),
        in_specs=qkv_specs + [bias_spec],
        out_specs=[one, one],
        out_shape=[shape, shape],
        compiler_params=_params("parallel", "parallel"),
    )(hcat3, hcat3, hcat3, bias)


def _attn_bwd(hcat3, bias, lse, d_out, dd, d_hcat3, g, dil, name):
    bsz, seq, _ = hcat3.shape
    nb = seq // dil // BLK
    scale = HEAD_DIM ** -0.5

    def body(q_ref, k_ref, v_ref, bias_ref, lse_ref, do_ref, dd_ref, _, dqkv_ref, dbias_ref, dq_acc, dk_acc, dv_acc):
        @pl.when(pl.program_id(1) == 0)
        def _():
            dbias_ref[...] = jnp.zeros_like(dbias_ref)

        dk_acc[...] = jnp.zeros_like(dk_acc)
        dv_acc[...] = jnp.zeros_like(dv_acc)
        masks = {True: _band_mask(True), False: _band_mask(False)}

        def unit(r, n, first):
            rows, keys = _attn_rows(dil, r, n, first)
            q2 = q_ref[0, rows, :].astype(BF16)
            k2 = k_ref[0, keys, :].astype(BF16)
            v2 = v_ref[0, keys, :].astype(BF16)
            do2 = do_ref[0, rows, :].astype(BF16)
            lse2 = lse_ref[0, rows, :]
            dd2 = dd_ref[0, rows, :]
            dqs, dks, dvs = [], [], []
            for j in range(2):
                lanes = slice(j * HEAD_DIM, (j + 1) * HEAD_DIM)
                q, kb, vb, do = q2[:, lanes], k2[:, lanes], v2[:, lanes], do2[:, lanes]
                delta = jnp.sum(dd2[:, lanes], axis=-1, keepdims=True)
                bias_j = bias_ref[j, :, BLK:] if first else bias_ref[j]
                s = _nt(q, kb) * scale + bias_j
                p = jnp.where(masks[first], jnp.exp(s - lse2[:, j * HEAD_DIM:j * HEAD_DIM + 1]), 0.0)
                ds = p * (_nt(do, vb) - delta)
                ds16 = ds.astype(BF16)
                dqs.append(jnp.dot(ds16, kb, preferred_element_type=F32) * scale)
                dks.append(_tn(ds16, q) * scale)
                dvs.append(_tn(p.astype(BF16), do))
                if first:
                    dbias_ref[j, :, BLK:] += ds
                else:
                    dbias_ref[j] += ds
            dq_acc[rows, :] = jnp.concatenate(dqs, axis=1)
            dk_acc[keys, :] += jnp.concatenate(dks, axis=1)
            dv_acc[keys, :] += jnp.concatenate(dvs, axis=1)

        _attn_schedule(dil, nb, unit)
        dqkv_ref[0, :, 0:BLK] = dq_acc[...].astype(BF16)
        dqkv_ref[0, :, BLK:2 * BLK] = dk_acc[...].astype(BF16)
        dqkv_ref[0, :, 2 * BLK:3 * BLK] = dv_acc[...].astype(BF16)

    qkv_specs, one, bias_spec = _attn_specs(seq, g)
    return pl.pallas_call(
        body, name=name,
        grid=(HEAD_PAIRS, bsz),
        in_specs=qkv_specs + [bias_spec, one, one, one, pl.BlockSpec(memory_space=pl.ANY)],
        out_specs=[pl.BlockSpec((1, seq, 3 * BLK), lambda hp, b: (b, 0, QKV_OFF // (3 * BLK) + g * HEAD_PAIRS + hp)), bias_spec],
        out_shape=[jax.ShapeDtypeStruct(d_hcat3.shape, d_hcat3.dtype), jax.ShapeDtypeStruct((HEADS_PER_GROUP, BLK, 2 * BLK), F32)],
        input_output_aliases={7: 0},
        scratch_shapes=[pltpu.VMEM((seq, BLK), F32)] * 3,
        compiler_params=_params("arbitrary", "arbitrary"),
    )(hcat3, hcat3, hcat3, bias, lse, d_out, dd, d_hcat3)


def _t5_buckets(dil):
    import numpy as np
    qi = np.arange(BLK)[:, None]
    kj = np.arange(2 * BLK)[None, :]
    dist = np.maximum(qi + BLK - kj, 0) * dil
    max_exact = NUM_BUCKETS // 2
    d_f = np.maximum(dist, 1).astype(np.float32)
    large = max_exact + (np.log(d_f / np.float32(max_exact)) / np.float32(math.log(MAX_DISTANCE / max_exact))
                         * np.float32(NUM_BUCKETS - max_exact)).astype(np.int32)
    large = np.minimum(large, NUM_BUCKETS - 1)
    return np.where(dist < max_exact, dist, large).astype(np.int32).reshape(-1)


def _bias_matrix(table_g, dil):
    buckets = jnp.asarray(_t5_buckets(dil))
    onehot = (buckets[None, :] == lax.broadcasted_iota(jnp.int32, (NUM_BUCKETS, 1), 0)).astype(F32)
    tk = 4096

    def body(t_ref, oh_ref, o_ref):
        o_ref[...] = _dot_hi(t_ref[...], oh_ref[...])

    bias = pl.pallas_call(
        body, name=f"bias_matrix{dil}",
        grid=(onehot.shape[1] // tk,),
        in_specs=[pl.BlockSpec((HEADS_PER_GROUP, NUM_BUCKETS), lambda kk: (0, 0)), pl.BlockSpec((NUM_BUCKETS, tk), lambda kk: (0, kk))],
        out_specs=pl.BlockSpec((HEADS_PER_GROUP, tk), lambda kk: (0, kk)),
        out_shape=jax.ShapeDtypeStruct((HEADS_PER_GROUP, onehot.shape[1]), F32),
        compiler_params=_params("parallel"),
    )(table_g.T, onehot)
    return bias.reshape(HEADS_PER_GROUP, BLK, 2 * BLK), onehot


def _bias_grad(dbias, onehot, name):
    flat = dbias.reshape(HEADS_PER_GROUP, 2 * BLK * BLK)
    tk = 4096

    def body(oh_ref, g_ref, o_ref):
        @pl.when(pl.program_id(0) == 0)
        def _():
            o_ref[...] = jnp.zeros_like(o_ref)
        o_ref[...] += lax.dot_general(oh_ref[...], g_ref[...], (((1,), (1,)), ((), ())),
                                      preferred_element_type=F32, precision=lax.Precision.HIGHEST)

    return pl.pallas_call(
        body, name=name,
        grid=(flat.shape[1] // tk,),
        in_specs=[pl.BlockSpec((NUM_BUCKETS, tk), lambda kk: (0, kk)), pl.BlockSpec((HEADS_PER_GROUP, tk), lambda kk: (0, kk))],
        out_specs=pl.BlockSpec((NUM_BUCKETS, HEADS_PER_GROUP), lambda kk: (0, 0)),
        out_shape=jax.ShapeDtypeStruct((NUM_BUCKETS, HEADS_PER_GROUP), F32),
        compiler_params=_params("arbitrary"),
    )(onehot, flat)


CONV_ROWS = 128
HALO = 8


def _conv_chunks(seq, fn):
    n = seq // CONV_ROWS
    fn(0, True, n == 1)
    if n > 2:
        def step(i, carry):
            fn(pl.multiple_of(i * CONV_ROWS, CONV_ROWS), False, False)
            return carry
        lax.fori_loop(1, n - 1, step, 0, unroll=7)
    if n > 1:
        fn((n - 1) * CONV_ROWS, False, True)


def _load_with_halo(ref, r0, first, last, after):
    lo = 0 if first else HALO
    hi = HALO if (after and not last) else 0
    v = ref[0, pl.ds(r0 - lo, CONV_ROWS + lo + hi), :]
    parts = ([jnp.zeros((HALO, v.shape[1]), v.dtype)] if first else []) + [v]
    if after and last:
        parts.append(jnp.zeros((HALO, v.shape[1]), v.dtype))
    return v if len(parts) == 1 else jnp.concatenate(parts, axis=0)


def _conv_pre(u_ext, n_rows, w_ref, b_ref):
    acc = b_ref[0:1, :] + w_ref[CONV_WIDTH - 1:CONV_WIDTH, :] * u_ext[HALO:HALO + n_rows]
    for s in range(1, CONV_WIDTH):
        acc = acc + w_ref[CONV_WIDTH - 1 - s:CONV_WIDTH - s, :] * u_ext[HALO - s:HALO - s + n_rows]
    return acc


def _conv_fwd(rest3, conv_w, conv_b):
    bsz, seq, _ = rest3.shape

    def body(u_ref, w_ref, b_ref, o_ref):
        def chunk(r0, first, last):
            pre = _conv_pre(_load_with_halo(u_ref, r0, first, last, False), CONV_ROWS, w_ref, b_ref)
            o_ref[0, pl.ds(r0, CONV_ROWS), :] = pre * _sigmoid(pre)
        _conv_chunks(seq, chunk)

    return pl.pallas_call(
        body, name="conv_fwd",
        grid=(bsz, CONV_DIM // BLK),
        in_specs=[pl.BlockSpec((1, seq, BLK), lambda b, c: (b, 0, R_XBC // BLK + c)),
                  pl.BlockSpec((CONV_WIDTH, BLK), lambda b, c: (0, c)),
                  pl.BlockSpec((1, BLK), lambda b, c: (0, c))],
        out_specs=pl.BlockSpec((1, seq, BLK), lambda b, c: (b, 0, c)),
        out_shape=jax.ShapeDtypeStruct((bsz, seq, CONV_DIM), F32),
        compiler_params=_params("parallel", "parallel"),
    )(rest3, conv_w, conv_b)


def _conv_bwd(rest3, conv_w, conv_b, d_act, d_hcat3):
    bsz, seq, _ = rest3.shape

    def body(u_ref, w_ref, b_ref, g_ref, _, du_ref, dw_ref):
        @pl.when(pl.program_id(1) == 0)
        def _():
            dw_ref[...] = jnp.zeros_like(dw_ref)

        def chunk(r0, first, last):
            ext = CONV_ROWS + HALO
            u_ext = _load_with_halo(u_ref, r0, first, last, True)
            g_ext = _load_with_halo(g_ref, r0, True, last, True)[HALO:]
            d_pre = g_ext * _silu_and_grad(_conv_pre(u_ext, ext, w_ref, b_ref))[1]
            d_now = d_pre[:CONV_ROWS]
            du = w_ref[CONV_WIDTH - 1:CONV_WIDTH, :] * d_now
            dw_ref[CONV_WIDTH - 1:CONV_WIDTH, :] += jnp.sum(d_now * u_ext[HALO:HALO + CONV_ROWS], axis=0, keepdims=True)
            for s in range(1, CONV_WIDTH):
                du = du + w_ref[CONV_WIDTH - 1 - s:CONV_WIDTH - s, :] * d_pre[s:s + CONV_ROWS]
                dw_ref[CONV_WIDTH - 1 - s:CONV_WIDTH - s, :] += jnp.sum(d_now * u_ext[HALO - s:HALO - s + CONV_ROWS],
                                                                      axis=0, keepdims=True)
            dw_ref[CONV_WIDTH:CONV_WIDTH + 1, :] += jnp.sum(d_now, axis=0, keepdims=True)
            du_ref[0, pl.ds(r0, CONV_ROWS), :] = du.astype(BF16)
        _conv_chunks(seq, chunk)

    return pl.pallas_call(
        body, name="conv_bwd",
        grid=(CONV_DIM // BLK, bsz),
        in_specs=[pl.BlockSpec((1, seq, BLK), lambda c, b: (b, 0, R_XBC // BLK + c)),
                  pl.BlockSpec((CONV_WIDTH, BLK), lambda c, b: (0, c)),
                  pl.BlockSpec((1, BLK), lambda c, b: (0, c)),
                  pl.BlockSpec((1, seq, BLK), lambda c, b: (b, 0, c)),
                  pl.BlockSpec(memory_space=pl.ANY)],
        out_specs=[pl.BlockSpec((1, seq, BLK), lambda c, b: (b, 0, R_XBC // BLK + c)),
                   pl.BlockSpec((8, BLK), lambda c, b: (0, c))],
        out_shape=[jax.ShapeDtypeStruct(d_hcat3.shape, d_hcat3.dtype), jax.ShapeDtypeStruct((8, CONV_DIM), F32)],
        input_output_aliases={4: 0},
        compiler_params=_params("parallel", "arbitrary"),
    )(rest3, conv_w, conv_b, d_act, d_hcat3)


P_DTB, P_ALOG, P_DSKIP = 0, 1, 2


def _softplus(v):
    return jnp.maximum(v, 0.0) + jnp.log(1.0 + jnp.exp(-jnp.abs(v)))


def _dot_hi(a, b):
    return jnp.dot(a, b, preferred_element_type=F32, precision=lax.Precision.HIGHEST)


def _nt(a, b):
    return lax.dot_general(a, b, (((1,), (1,)), ((), ())), preferred_element_type=F32)


def _tn(a, b):
    return lax.dot_general(a, b, (((0,), (0,)), ((), ())), preferred_element_type=F32)


def _sum_all(v):
    return jnp.sum(jnp.sum(v, axis=0, keepdims=True), axis=1, keepdims=True)


def _ssd_decays(dtr, dtr_t, prow_ref, pcol_ref):
    ri = lax.broadcasted_iota(jnp.int32, (BLK, BLK), 0)
    ci = lax.broadcasted_iota(jnp.int32, (BLK, BLK), 1)
    tri = (ri >= ci).astype(F32)
    tri_u = (ri <= ci).astype(F32)
    pre = dtr + prow_ref[P_DTB:P_DTB + 1, :]
    dt = _softplus(pre)
    ah_row = -jnp.exp(prow_ref[P_ALOG:P_ALOG + 1, :])
    acs = _dot_hi(tri, dt * ah_row)
    pre_t = dtr_t + pcol_ref[:, P_DTB:P_DTB + 1]
    dt_t = _softplus(pre_t)
    ah_col = -jnp.exp(pcol_ref[:, P_ALOG:P_ALOG + 1])
    acs_t = _dot_hi(dt_t * ah_col, tri_u)
    return dict(tri=tri, tri_u=tri_u, pre=pre, dt=dt, ah_row=ah_row, acs=acs, pre_t=pre_t, dt_t=dt_t, ah_col=ah_col,
                acs_t=acs_t, causal=ri >= ci, last_row=ri[:, 0:1] == BLK - 1)


def _ssd_head(h, d, x_ref, s_in, g_mat):
    col = d["acs"][:, h:h + 1]
    row = d["acs_t"][h:h + 1, :]
    lm = jnp.exp(jnp.where(d["causal"], col - row, -jnp.inf))
    m = g_mat * lm
    xh = x_ref[0, :, pl.ds(h * HEAD_DIM, HEAD_DIM)]
    dtc = d["dt"][:, h:h + 1]
    xd = xh * dtc
    e = jnp.exp(col)
    clast = d["acs"][BLK - 1:BLK, h:h + 1]
    f = jnp.exp(clast - col)
    return dict(col=col, lm=lm, m=m, xh=xh, dtc=dtc, xd=xd, e=e, ecl=jnp.exp(clast), f=f, xf=xd * f)


def _ssd_specs(nc, rev):
    cidx = (lambda c: nc - 1 - c) if rev else (lambda c: c)
    x_spec = pl.BlockSpec((1, BLK, D_INNER), lambda b, c: (b, cidx(c), 0))
    bm_spec = pl.BlockSpec((1, BLK, SSM_GROUPS * D_STATE), lambda b, c: (b, cidx(c), D_INNER // (SSM_GROUPS * D_STATE)))
    cm_spec = pl.BlockSpec((1, BLK, SSM_GROUPS * D_STATE), lambda b, c: (b, cidx(c), D_INNER // (SSM_GROUPS * D_STATE) + 1))
    dt_spec = pl.BlockSpec((1, BLK, BLK), lambda b, c: (b, cidx(c), R_DT // BLK))
    dtt_spec = pl.BlockSpec((1, BLK, BLK), lambda b, c: (b, 0, cidx(c)))
    prow_spec = pl.BlockSpec((8, BLK), lambda b, c: (0, 0))
    pcol_spec = pl.BlockSpec((BLK, 8), lambda b, c: (0, 0))
    st_spec = pl.BlockSpec((1, 1, SSM_HEADS, HEAD_DIM, D_STATE), lambda b, c: (b, cidx(c), 0, 0, 0))
    y_spec = pl.BlockSpec((1, BLK, D_INNER), lambda b, c: (b, cidx(c), 0))
    return x_spec, bm_spec, cm_spec, dt_spec, dtt_spec, prow_spec, pcol_spec, st_spec, y_spec


def _ssd_fwd(xa, rest3, dtr_t, prow, pcol):
    bsz, seq, _ = xa.shape
    nc = seq // BLK

    def body(x_ref, bm_ref, cm_ref, dt_ref, dtt_ref, prow_ref, pcol_ref, y_ref, sin_ref, s_ref):
        @pl.when(pl.program_id(1) == 0)
        def _():
            s_ref[...] = jnp.zeros_like(s_ref)

        d = _ssd_decays(dt_ref[0], dtt_ref[0], prow_ref, pcol_ref)
        for g in range(SSM_GROUPS):
            lanes = pl.ds(g * D_STATE, D_STATE)
            bg = bm_ref[0, :, lanes].astype(BF16)
            cg = cm_ref[0, :, lanes].astype(BF16)
            g_mat = _nt(cg, bg)
            for hh in range(HEADS_PER_SSM_GROUP):
                h = g * HEADS_PER_SSM_GROUP + hh
                s_in = s_ref[h]
                sin_ref[0, 0, h] = s_in
                q = _ssd_head(h, d, x_ref, s_in, g_mat)
                y_diag = jnp.dot(q["m"].astype(BF16), q["xd"].astype(BF16), preferred_element_type=F32)
                y_off = _nt(cg, s_in.astype(BF16)) * q["e"]
                s_ref[h] = s_in * q["ecl"] + _tn(q["xf"].astype(BF16), bg)
                y_ref[0, :, pl.ds(h * HEAD_DIM, HEAD_DIM)] = y_diag + y_off + prow_ref[P_DSKIP:P_DSKIP + 1, h:h + 1] * q["xh"]

    x_spec, bm_spec, cm_spec, dt_spec, dtt_spec, prow_spec, pcol_spec, st_spec, y_spec = _ssd_specs(nc, False)
    return pl.pallas_call(
        body, name="ssd_fwd",
        grid=(bsz, nc),
        in_specs=[x_spec, bm_spec, cm_spec, dt_spec, dtt_spec, prow_spec, pcol_spec],
        out_specs=[y_spec, st_spec],
        out_shape=[jax.ShapeDtypeStruct((bsz, seq, D_INNER), F32),
                   jax.ShapeDtypeStruct((bsz, nc, SSM_HEADS, HEAD_DIM, D_STATE), F32)],
        scratch_shapes=[pltpu.VMEM((SSM_HEADS, HEAD_DIM, D_STATE), F32)],
        compiler_params=_params("parallel", "arbitrary"),
    )(xa, xa, xa, rest3, dtr_t, prow, pcol)


def _ssd_bwd(xa, rest3, dtr_t, prow, pcol, s_in_all, dy):
    bsz, seq, _ = xa.shape
    nc = seq // BLK

    def body(x_ref, bm_ref, cm_ref, dt_ref, dtt_ref, prow_ref, pcol_ref, sin_ref, dy_ref,
             dxa_ref, ddt_ref, ddtt_ref, gprow_ref, gpcol_ref, ds_ref, dc_ref, ddtc_ref, drt_ref):
        first = jnp.logical_and(pl.program_id(0) == 0, pl.program_id(1) == 0)

        @pl.when(first)
        def _():
            gprow_ref[...] = jnp.zeros_like(gprow_ref)
            gpcol_ref[...] = jnp.zeros_like(gpcol_ref)

        @pl.when(pl.program_id(1) == 0)
        def _():
            ds_ref[...] = jnp.zeros_like(ds_ref)

        dc_ref[...] = jnp.zeros_like(dc_ref)
        ddtc_ref[...] = jnp.zeros_like(ddtc_ref)
        drt_ref[...] = jnp.zeros_like(drt_ref)
        d = _ssd_decays(dt_ref[0], dtt_ref[0], prow_ref, pcol_ref)
        for g in range(SSM_GROUPS):
            lanes = pl.ds(g * D_STATE, D_STATE)
            bg = bm_ref[0, :, lanes].astype(BF16)
            cg = cm_ref[0, :, lanes].astype(BF16)
            g_mat = _nt(cg, bg)
            d_g = jnp.zeros((BLK, BLK), F32)
            d_bg = jnp.zeros((BLK, D_STATE), F32)
            d_cg = jnp.zeros((BLK, D_STATE), F32)
            for hh in range(HEADS_PER_SSM_GROUP):
                h = g * HEADS_PER_SSM_GROUP + hh
                head_lanes = pl.ds(h * HEAD_DIM, HEAD_DIM)
                s_in = sin_ref[0, 0, h]
                s_in16 = s_in.astype(BF16)
                q = _ssd_head(h, d, x_ref, s_in, g_mat)
                m16, xd16 = q["m"].astype(BF16), q["xd"].astype(BF16)
                d_y = dy_ref[0, :, head_lanes]
                d_y16 = d_y.astype(BF16)
                d_so = ds_ref[h]
                d_so16 = d_so.astype(BF16)
                d_x = prow_ref[P_DSKIP:P_DSKIP + 1, h:h + 1] * d_y
                gprow_ref[P_DSKIP:P_DSKIP + 1, h:h + 1] += _sum_all(d_y * q["xh"])
                d_m = _nt(d_y16, xd16)
                d_xd = _tn(m16, d_y16)
                w = d_m * q["m"]
                d_g = d_g + d_m * q["lm"]
                d_col = jnp.sum(w, axis=1, keepdims=True)
                drt_ref[h:h + 1, :] = -jnp.sum(w, axis=0, keepdims=True)
                qmat = _nt(cg, s_in16)
                d_q16 = (d_y * q["e"]).astype(BF16)
                d_col = d_col + jnp.sum(d_y * qmat, axis=1, keepdims=True) * q["e"]
                d_cg = d_cg + jnp.dot(d_q16, s_in16, preferred_element_type=F32)
                d_sin = _tn(d_q16, cg) + d_so * q["ecl"]
                d_clast = _sum_all(d_so * s_in) * q["ecl"]
                d_xf = _nt(bg, d_so16)
                d_bg = d_bg + jnp.dot(q["xf"].astype(BF16), d_so16, preferred_element_type=F32)
                d_xd = d_xd + d_xf * q["f"]
                d_f = jnp.sum(d_xf * q["xd"], axis=1, keepdims=True) * q["f"]
                d_clast = d_clast + jnp.sum(d_f, axis=0, keepdims=True)
                d_col = d_col - d_f + jnp.where(d["last_row"], d_clast, 0.0)
                dxa_ref[0, :, head_lanes] = d_x + d_xd * q["dtc"]
                dc_ref[:, h:h + 1] = d_col
                ddtc_ref[:, h:h + 1] = jnp.sum(d_xd * q["xh"], axis=1, keepdims=True)
                ds_ref[h] = d_sin
            d_g16 = d_g.astype(BF16)
            dxa_ref[0, :, pl.ds(D_INNER + g * D_STATE, D_STATE)] = d_bg + _tn(d_g16, cg)
            dxa_ref[0, :, pl.ds(D_INNER + (SSM_GROUPS + g) * D_STATE, D_STATE)] = d_cg + jnp.dot(d_g16, bg, preferred_element_type=F32)
        d_a = _dot_hi(d["tri_u"], dc_ref[...])
        d_pre = (ddtc_ref[...] + d_a * d["ah_row"]) * _sigmoid(d["pre"])
        ddt_ref[0] = d_pre
        gprow_ref[P_DTB:P_DTB + 1, :] += jnp.sum(d_pre, axis=0, keepdims=True)
        gprow_ref[P_ALOG:P_ALOG + 1, :] += jnp.sum(d_a * d["dt"], axis=0, keepdims=True) * d["ah_row"]
        d_at = _dot_hi(drt_ref[...], d["tri"])
        d_pre_t = d_at * d["ah_col"] * _sigmoid(d["pre_t"])
        ddtt_ref[0] = d_pre_t
        gpcol_ref[:, P_DTB:P_DTB + 1] += jnp.sum(d_pre_t, axis=1, keepdims=True)
        gpcol_ref[:, P_ALOG:P_ALOG + 1] += jnp.sum(d_at * d["dt_t"], axis=1, keepdims=True) * d["ah_col"]

    x_spec, bm_spec, cm_spec, dt_spec, dtt_spec, prow_spec, pcol_spec, st_spec, y_spec = _ssd_specs(nc, True)
    return pl.pallas_call(
        body, name="ssd_bwd",
        grid=(bsz, nc),
        in_specs=[x_spec, bm_spec, cm_spec, dt_spec, dtt_spec, prow_spec, pcol_spec, st_spec, y_spec],
        out_specs=[pl.BlockSpec((1, BLK, CONV_DIM), lambda b, c: (b, nc - 1 - c, 0)),
                   pl.BlockSpec((1, BLK, BLK), lambda b, c: (b, nc - 1 - c, 0)),
                   pl.BlockSpec((1, BLK, BLK), lambda b, c: (b, 0, nc - 1 - c)),
                   prow_spec, pcol_spec],
        out_shape=[jax.ShapeDtypeStruct((bsz, seq, CONV_DIM), F32),
                   jax.ShapeDtypeStruct((bsz, seq, BLK), F32),
                   jax.ShapeDtypeStruct((bsz, BLK, seq), F32),
                   jax.ShapeDtypeStruct((8, BLK), F32),
                   jax.ShapeDtypeStruct((BLK, 8), F32)],
        scratch_shapes=[pltpu.VMEM((SSM_HEADS, HEAD_DIM, D_STATE), F32), pltpu.VMEM((BLK, BLK), F32),
                        pltpu.VMEM((BLK, BLK), F32), pltpu.VMEM((BLK, BLK), F32)],
        compiler_params=_params("arbitrary", "arbitrary"),
    )(xa, xa, xa, rest3, dtr_t, prow, pcol, s_in_all, dy)


GROUP_W = HEADS_PER_SSM_GROUP * HEAD_DIM


def _select_matrix(shape, g, head_axis, per_head):
    h = lax.broadcasted_iota(jnp.int32, shape, head_axis)
    j = lax.broadcasted_iota(jnp.int32, shape, 1 - head_axis)
    return (h == g * HEADS_PER_SSM_GROUP + lax.shift_right_logical(j, per_head.bit_length() - 1)).astype(BF16)


def _split16(v, terms):
    parts, rem = [], v
    for _ in range(terms):
        p = rem.astype(BF16)
        parts.append(p)
        rem = rem - p.astype(F32)
    return parts


def _sel_dot(a, b, terms=2):
    if a.dtype == BF16:
        return sum(jnp.dot(a, p, preferred_element_type=F32) for p in _split16(b, terms))
    return sum(jnp.dot(p, b, preferred_element_type=F32) for p in _split16(a, terms))


def _ssd_group(g, d, e_all, f_all, ecl_b, x_ref, prow_ref):
    spread = _select_matrix((BLK, GROUP_W), g, 0, HEAD_DIM)
    gather = _select_matrix((GROUP_W, BLK), g, 1, HEAD_DIM)
    xg = x_ref[0, :, pl.ds(g * GROUP_W, GROUP_W)]
    dt_g = _sel_dot(d["dt"], spread)
    e_g = _sel_dot(e_all, spread)
    f_g = _sel_dot(f_all, spread)
    dsk_g = _sel_dot(prow_ref[...], spread, 3)[P_DSKIP:P_DSKIP + 1, :]
    sc_g = _sel_dot(gather, ecl_b, 3)
    xd = xg * dt_g
    return dict(spread=spread, gather=gather, xg=xg, dt_g=dt_g, e_g=e_g, f_g=f_g, dsk_g=dsk_g, sc_g=sc_g, xd=xd,
                xd16=xd.astype(BF16), xf16=(xd * f_g).astype(BF16))


def _ssd_common(d):
    e_all = jnp.exp(d["acs"])
    f_all = jnp.exp(d["acs"][BLK - 1:BLK, :] - d["acs"])
    ecl_b = jnp.broadcast_to(jnp.exp(d["acs_t"][:, BLK - 1:BLK]), (BLK, BLK))
    return e_all, f_all, ecl_b


def _ssd_mask_decay(d, h, g_mat):
    col = d["acs"][:, h:h + 1]
    row = d["acs_t"][h:h + 1, :]
    lm = jnp.exp(jnp.where(d["causal"], col - row, -jnp.inf))
    return lm, g_mat * lm


def _ssd_state_spec(nc, rev):
    cidx = (lambda c: nc - 1 - c) if rev else (lambda c: c)
    return pl.BlockSpec((1, 1, SSM_GROUPS, GROUP_W, D_STATE), lambda b, c: (b, cidx(c), 0, 0, 0))


def _ssd_fwd(xa, rest3, dtr_t, prow, pcol):
    bsz, seq, _ = xa.shape
    nc = seq // BLK

    def body(x_ref, bm_ref, cm_ref, dt_ref, dtt_ref, prow_ref, pcol_ref, y_ref, sin_ref, s_ref):
        @pl.when(pl.program_id(1) == 0)
        def _():
            s_ref[...] = jnp.zeros_like(s_ref)

        d = _ssd_decays(dt_ref[0], dtt_ref[0], prow_ref, pcol_ref)
        e_all, f_all, ecl_b = _ssd_common(d)
        for g in range(SSM_GROUPS):
            lanes = pl.ds(g * D_STATE, D_STATE)
            bg = bm_ref[0, :, lanes].astype(BF16)
            cg = cm_ref[0, :, lanes].astype(BF16)
            g_mat = _nt(cg, bg)
            q = _ssd_group(g, d, e_all, f_all, ecl_b, x_ref, prow_ref)
            s_in = s_ref[g]
            sin_ref[0, 0, g] = s_in
            y_diag = []
            for j in range(HEADS_PER_SSM_GROUP):
                _, m = _ssd_mask_decay(d, g * HEADS_PER_SSM_GROUP + j, g_mat)
                y_diag.append(jnp.dot(m.astype(BF16), q["xd16"][:, j * HEAD_DIM:(j + 1) * HEAD_DIM], preferred_element_type=F32))
            y_off = _nt(cg, s_in.astype(BF16)) * q["e_g"]
            y_ref[0, :, pl.ds(g * GROUP_W, GROUP_W)] = jnp.concatenate(y_diag, axis=1) + y_off + q["dsk_g"] * q["xg"]
            s_ref[g] = s_in * q["sc_g"] + _tn(q["xf16"], bg)

    x_spec, bm_spec, cm_spec, dt_spec, dtt_spec, prow_spec, pcol_spec, _, y_spec = _ssd_specs(nc, False)
    return pl.pallas_call(
        body, name="ssd_fwd",
        grid=(bsz, nc),
        in_specs=[x_spec, bm_spec, cm_spec, dt_spec, dtt_spec, prow_spec, pcol_spec],
        out_specs=[y_spec, _ssd_state_spec(nc, False)],
        out_shape=[jax.ShapeDtypeStruct((bsz, seq, D_INNER), F32),
                   jax.ShapeDtypeStruct((bsz, nc, SSM_GROUPS, GROUP_W, D_STATE), F32)],
        scratch_shapes=[pltpu.VMEM((SSM_GROUPS, GROUP_W, D_STATE), F32)],
        compiler_params=_params("parallel", "arbitrary"),
    )(xa, xa, xa, rest3, dtr_t, prow, pcol)


def _ssd_bwd(xa, rest3, dtr_t, prow, pcol, s_in_all, dy):
    bsz, seq, _ = xa.shape
    nc = seq // BLK

    def body(x_ref, bm_ref, cm_ref, dt_ref, dtt_ref, prow_ref, pcol_ref, sin_ref, dy_ref,
             dxa_ref, ddt_ref, ddtt_ref, gprow_ref, gpcol_ref, ds_ref, drt_ref):
        first = jnp.logical_and(pl.program_id(0) == 0, pl.program_id(1) == 0)

        @pl.when(first)
        def _():
            gprow_ref[...] = jnp.zeros_like(gprow_ref)
            gpcol_ref[...] = jnp.zeros_like(gpcol_ref)

        @pl.when(pl.program_id(1) == 0)
        def _():
            ds_ref[...] = jnp.zeros_like(ds_ref)

        drt_ref[...] = jnp.zeros_like(drt_ref)
        d = _ssd_decays(dt_ref[0], dtt_ref[0], prow_ref, pcol_ref)
        e_all, f_all, ecl_b = _ssd_common(d)
        d_c = jnp.zeros((BLK, BLK), F32)
        d_dtc = jnp.zeros((BLK, BLK), F32)
        d_clast_col = jnp.zeros((BLK, 1), F32)
        skip_rows = lax.broadcasted_iota(jnp.int32, (8, GROUP_W), 0) == P_DSKIP
        for g in range(SSM_GROUPS):
            lanes = pl.ds(g * D_STATE, D_STATE)
            bg = bm_ref[0, :, lanes].astype(BF16)
            cg = cm_ref[0, :, lanes].astype(BF16)
            g_mat = _nt(cg, bg)
            q = _ssd_group(g, d, e_all, f_all, ecl_b, x_ref, prow_ref)
            s_in = sin_ref[0, 0, g]
            s_in16 = s_in.astype(BF16)
            d_y = dy_ref[0, :, pl.ds(g * GROUP_W, GROUP_W)]
            d_y16 = d_y.astype(BF16)
            d_so = ds_ref[g]
            d_so16 = d_so.astype(BF16)
            d_g = jnp.zeros((BLK, BLK), F32)
            ws, d_xds = [], []
            for j in range(HEADS_PER_SSM_GROUP):
                h = g * HEADS_PER_SSM_GROUP + j
                head = slice(j * HEAD_DIM, (j + 1) * HEAD_DIM)
                lm, m = _ssd_mask_decay(d, h, g_mat)
                d_m = _nt(d_y16[:, head], q["xd16"][:, head])
                d_xds.append(_tn(m.astype(BF16), d_y16[:, head]))
                w = d_m * m
                d_g = d_g + d_m * lm
                drt_ref[h:h + 1, :] = -jnp.sum(w, axis=0, keepdims=True)
                ws.append(w)
            d_c = d_c + _sel_dot(jnp.concatenate(ws, axis=1), _select_matrix((HEADS_PER_SSM_GROUP * BLK, BLK), g, 1, BLK))
            d_xd = jnp.concatenate(d_xds, axis=1)
            d_g16 = d_g.astype(BF16)
            qmat = _nt(cg, s_in16)
            d_q16 = (d_y * q["e_g"]).astype(BF16)
            d_cg = jnp.dot(d_q16, s_in16, preferred_element_type=F32) + jnp.dot(d_g16, bg, preferred_element_type=F32)
            d_sin = _tn(d_q16, cg) + d_so * q["sc_g"]
            d_clast_col = d_clast_col + jnp.sum(_sel_dot(q["spread"], d_so * s_in * q["sc_g"]), axis=1, keepdims=True)
            d_xf = _nt(bg, d_so16)
            d_bg = jnp.dot(q["xf16"], d_so16, preferred_element_type=F32) + _tn(d_g16, cg)
            d_xd = d_xd + d_xf * q["f_g"]
            r_e = _sel_dot(d_y * qmat * q["e_g"], q["gather"])
            r_f = _sel_dot(d_xf * q["xd"] * q["f_g"], q["gather"])
            d_c = d_c + r_e - r_f + jnp.where(d["last_row"], jnp.sum(r_f, axis=0, keepdims=True), 0.0)
            d_dtc = d_dtc + _sel_dot(d_xd * q["xg"], q["gather"])
            skip_sum = jnp.where(skip_rows, jnp.sum(d_y * q["xg"], axis=0, keepdims=True), 0.0)
            gprow_ref[...] += _sel_dot(skip_sum, q["gather"])
            dxa_ref[0, :, pl.ds(g * GROUP_W, GROUP_W)] = q["dsk_g"] * d_y + d_xd * q["dt_g"]
            dxa_ref[0, :, pl.ds(D_INNER + g * D_STATE, D_STATE)] = d_bg
            dxa_ref[0, :, pl.ds(D_INNER + (SSM_GROUPS + g) * D_STATE, D_STATE)] = d_cg
            ds_ref[g] = d_sin
        drt_ref[:, BLK - 1:BLK] += d_clast_col
        d_a = _dot_hi(d["tri_u"], d_c)
        d_pre = (d_dtc + d_a * d["ah_row"]) * _sigmoid(d["pre"])
        ddt_ref[0] = d_pre
        gprow_ref[P_DTB:P_DTB + 1, :] += jnp.sum(d_pre, axis=0, keepdims=True)
        gprow_ref[P_ALOG:P_ALOG + 1, :] += jnp.sum(d_a * d["dt"], axis=0, keepdims=True) * d["ah_row"]
        d_at = _dot_hi(drt_ref[...], d["tri"])
        d_pre_t = d_at * d["ah_col"] * _sigmoid(d["pre_t"])
        ddtt_ref[0] = d_pre_t
        gpcol_ref[:, P_DTB:P_DTB + 1] += jnp.sum(d_pre_t, axis=1, keepdims=True)
        gpcol_ref[:, P_ALOG:P_ALOG + 1] += jnp.sum(d_at * d["dt_t"], axis=1, keepdims=True) * d["ah_col"]

    x_spec, bm_spec, cm_spec, dt_spec, dtt_spec, prow_spec, pcol_spec, _, y_spec = _ssd_specs(nc, True)
    return pl.pallas_call(
        body, name="ssd_bwd",
        grid=(bsz, nc),
        in_specs=[x_spec, bm_spec, cm_spec, dt_spec, dtt_spec, prow_spec, pcol_spec, _ssd_state_spec(nc, True), y_spec],
        out_specs=[pl.BlockSpec((1, BLK, CONV_DIM), lambda b, c: (b, nc - 1 - c, 0)),
                   pl.BlockSpec((1, BLK, BLK), lambda b, c: (b, nc - 1 - c, 0)),
                   pl.BlockSpec((1, BLK, BLK), lambda b, c: (b, 0, nc - 1 - c)),
                   prow_spec, pcol_spec],
        out_shape=[jax.ShapeDtypeStruct((bsz, seq, CONV_DIM), F32),
                   jax.ShapeDtypeStruct((bsz, seq, BLK), F32),
                   jax.ShapeDtypeStruct((bsz, BLK, seq), F32),
                   jax.ShapeDtypeStruct((8, BLK), F32),
                   jax.ShapeDtypeStruct((BLK, 8), F32)],
        scratch_shapes=[pltpu.VMEM((SSM_GROUPS, GROUP_W, D_STATE), F32), pltpu.VMEM((BLK, BLK), F32)],
        compiler_params=_params("arbitrary", "arbitrary"),
    )(xa, xa, xa, rest3, dtr_t, prow, pcol, s_in_all, dy)


ROW_TILE = 256
CMB_COLS = 256
RMS_COLS = D_INNER // SSM_GROUPS


ROW_CHUNK = 32
WIDE_ROW_CHUNK = 16


def _row_chunks(fn, chunk=ROW_CHUNK):
    def step(i, carry):
        fn(pl.ds(pl.multiple_of(i * chunk, chunk), chunk))
        return carry
    lax.fori_loop(0, ROW_TILE // chunk, step, 0, unroll=8)


def _combine_weights(l_refs, rows):
    ls = [r[rows, :] for r in l_refs]
    mx = jnp.maximum(jnp.maximum(ls[0], ls[1]), ls[2])
    es = [jnp.exp(l - mx) for l in ls]
    inv = 1.0 / (es[0] + es[1] + es[2])
    return [e * inv for e in es]


def _combine_specs():
    a = pl.BlockSpec((ROW_TILE, CMB_COLS), lambda i, j: (i, j))
    gatt = pl.BlockSpec((ROW_TILE, CMB_COLS), lambda i, j: (i, R_GATT // CMB_COLS + j))
    return a, gatt


def _combine_fwd(outs, lses, rest):
    t = rest.shape[0]

    def body(o0, o1, o2, l0, l1, l2, ga_ref, oa_ref):
        def chunk(rows):
            ws = _combine_weights((l0, l1, l2), rows)
            o = ws[0] * o0[rows, :] + ws[1] * o1[rows, :] + ws[2] * o2[rows, :]
            oa_ref[rows, :] = (o * _silu_and_grad(ga_ref[rows, :])[0]).astype(BF16)
        _row_chunks(chunk)

    a, gatt = _combine_specs()
    return pl.pallas_call(
        body, name="combine_fwd",
        grid=(t // ROW_TILE, GROUP_COLS // CMB_COLS),
        in_specs=[a] * 6 + [gatt],
        out_specs=a,
        out_shape=jax.ShapeDtypeStruct((t, GROUP_COLS), BF16),
        compiler_params=_params("parallel", "parallel"),
    )(*outs, *lses, rest)


def _combine_bwd(outs, lses, rest, d_oa, d_hcat):
    t = rest.shape[0]

    def body(o0, o1, o2, l0, l1, l2, ga_ref, doa_ref, _, do0, do1, do2, dd0, dd1, dd2, dga_ref):
        def chunk(rows):
            ws = _combine_weights((l0, l1, l2), rows)
            o = ws[0] * o0[rows, :] + ws[1] * o1[rows, :] + ws[2] * o2[rows, :]
            sg, dsg = _silu_and_grad(ga_ref[rows, :])
            d_oa_v = doa_ref[rows, :]
            d_o = d_oa_v * sg
            dga_ref[rows, :] = (d_oa_v * o * dsg).astype(BF16)
            for w, do_ref, dd_ref in zip(ws, (do0, do1, do2), (dd0, dd1, dd2)):
                d_out = w * d_o
                do_ref[rows, :] = d_out
                dd_ref[rows, :] = d_out * o
        _row_chunks(chunk)

    a, gatt = _combine_specs()
    s32 = jax.ShapeDtypeStruct((t, GROUP_COLS), F32)
    return pl.pallas_call(
        body, name="combine_bwd",
        grid=(t // ROW_TILE, GROUP_COLS // CMB_COLS),
        in_specs=[a] * 6 + [gatt, a, pl.BlockSpec(memory_space=pl.ANY)],
        out_specs=[a] * 6 + [gatt],
        out_shape=[s32, s32, s32, s32, s32, s32, jax.ShapeDtypeStruct(d_hcat.shape, d_hcat.dtype)],
        input_output_aliases={8: 6},
        compiler_params=_params("parallel", "parallel"),
    )(*outs, *lses, rest, d_oa, d_hcat)


def _gatenorm_fwd(y, rest, norm_w):
    t = rest.shape[0]

    def body(y_ref, z_ref, w_ref, o_ref):
        def chunk(rows):
            u = y_ref[rows, :] * _silu_and_grad(z_ref[rows, :])[0]
            rs = lax.rsqrt(jnp.mean(u * u, axis=-1, keepdims=True) + RMS_EPS)
            o_ref[rows, :] = (u * rs * w_ref[...]).astype(BF16)
        _row_chunks(chunk, WIDE_ROW_CHUNK)

    return pl.pallas_call(
        body, name="gatenorm_fwd",
        grid=(t // ROW_TILE, SSM_GROUPS),
        in_specs=[pl.BlockSpec((ROW_TILE, RMS_COLS), lambda i, j: (i, j)),
                  pl.BlockSpec((ROW_TILE, RMS_COLS), lambda i, j: (i, R_Z // RMS_COLS + j)),
                  pl.BlockSpec((1, RMS_COLS), lambda i, j: (0, j))],
        out_specs=pl.BlockSpec((ROW_TILE, RMS_COLS), lambda i, j: (i, j)),
        out_shape=jax.ShapeDtypeStruct((t, D_INNER), BF16),
        compiler_params=_params("parallel", "parallel"),
    )(y, rest, norm_w)


def _gatenorm_bwd(y, rest, norm_w, d_ys, d_hcat):
    t = rest.shape[0]

    def body(y_ref, z_ref, w_ref, g_ref, _, dy_ref, dz_ref, dw_ref):
        @pl.when(pl.program_id(1) == 0)
        def _():
            dw_ref[...] = jnp.zeros_like(dw_ref)

        def chunk(rows):
            yv = y_ref[rows, :]
            sz, dsz = _silu_and_grad(z_ref[rows, :])
            u = yv * sz
            rs = lax.rsqrt(jnp.mean(u * u, axis=-1, keepdims=True) + RMS_EPS)
            un = u * rs
            g = g_ref[rows, :]
            dw_ref[0:1, :] += jnp.sum(g * un, axis=0, keepdims=True)
            d_un = g * w_ref[...]
            d_u = rs * (d_un - un * jnp.mean(d_un * un, axis=-1, keepdims=True))
            dy_ref[rows, :] = d_u * sz
            dz_ref[rows, :] = (d_u * yv * dsz).astype(BF16)
        _row_chunks(chunk, WIDE_ROW_CHUNK)

    blk = pl.BlockSpec((ROW_TILE, RMS_COLS), lambda j, i: (i, j))
    z_blk = pl.BlockSpec((ROW_TILE, RMS_COLS), lambda j, i: (i, R_Z // RMS_COLS + j))
    return pl.pallas_call(
        body, name="gatenorm_bwd",
        grid=(SSM_GROUPS, t // ROW_TILE),
        in_specs=[blk, z_blk, pl.BlockSpec((1, RMS_COLS), lambda j, i: (0, j)), blk, pl.BlockSpec(memory_space=pl.ANY)],
        out_specs=[blk, z_blk, pl.BlockSpec((8, RMS_COLS), lambda j, i: (0, j))],
        out_shape=[jax.ShapeDtypeStruct((t, D_INNER), F32), jax.ShapeDtypeStruct(d_hcat.shape, d_hcat.dtype),
                   jax.ShapeDtypeStruct((8, D_INNER), F32)],
        input_output_aliases={4: 1},
        compiler_params=_params("parallel", "arbitrary"),
    )(y, rest, norm_w, d_ys, d_hcat)


def _row_specs():
    full = pl.BlockSpec((ROW_TILE, D_MODEL), lambda i: (i, 0))
    vec = pl.BlockSpec((8, D_MODEL), lambda i: (0, 0))
    at = lambda off: pl.BlockSpec((ROW_TILE, D_MODEL), lambda i: (i, off // D_MODEL))
    return full, vec, at


def _merge_fwd(y_a, y_b, rest, b_gate):
    t = rest.shape[0]

    def body(ya_ref, yb_ref, ga_ref, gb_ref, bg_ref, o_ref):
        def chunk(rows):
            sa = _sigmoid(ga_ref[rows, :] + bg_ref[0:1, :])
            sb = _sigmoid(gb_ref[rows, :] + bg_ref[1:2, :])
            o_ref[rows, :] = (sa * ya_ref[rows, :] + sb * yb_ref[rows, :]).astype(BF16)
        _row_chunks(chunk, WIDE_ROW_CHUNK)

    full, vec, at = _row_specs()
    return pl.pallas_call(
        body, name="merge_fwd",
        grid=(t // ROW_TILE,),
        in_specs=[full, full, at(R_GM), at(R_GM + D_MODEL), vec],
        out_specs=full,
        out_shape=jax.ShapeDtypeStruct((t, D_MODEL), BF16),
        compiler_params=_params("parallel"),
    )(y_a, y_b, rest, rest, b_gate)


def _merge_bwd(y_a, y_b, rest, b_gate, d_merged, d_hcat):
    t = rest.shape[0]

    def body(ya_ref, yb_ref, ga_ref, gb_ref, bg_ref, dm_ref, _, dya_ref, dyb_ref, dg_ref, dbg_ref):
        @pl.when(pl.program_id(0) == 0)
        def _():
            dbg_ref[...] = jnp.zeros_like(dbg_ref)

        def chunk(rows):
            dm = dm_ref[rows, :]
            for row, y_ref, g_ref, dy_ref in ((0, ya_ref, ga_ref, dya_ref), (1, yb_ref, gb_ref, dyb_ref)):
                s = _sigmoid(g_ref[rows, :] + bg_ref[row:row + 1, :])
                dy_ref[rows, :] = (dm * s).astype(BF16)
                dg = dm * y_ref[rows, :] * s * (1.0 - s)
                dg_ref[rows, row * D_MODEL:(row + 1) * D_MODEL] = dg.astype(BF16)
                dbg_ref[row:row + 1, :] += jnp.sum(dg, axis=0, keepdims=True)
        _row_chunks(chunk, WIDE_ROW_CHUNK)

    full, vec, at = _row_specs()
    s16 = jax.ShapeDtypeStruct((t, D_MODEL), BF16)
    return pl.pallas_call(
        body, name="merge_bwd",
        grid=(t // ROW_TILE,),
        in_specs=[full, full, at(R_GM), at(R_GM + D_MODEL), vec, full, pl.BlockSpec(memory_space=pl.ANY)],
        out_specs=[full, full, pl.BlockSpec((ROW_TILE, 2 * D_MODEL), lambda i: (i, R_GM // (2 * D_MODEL))), vec],
        out_shape=[s16, s16, jax.ShapeDtypeStruct(d_hcat.shape, d_hcat.dtype), jax.ShapeDtypeStruct((8, D_MODEL), F32)],
        input_output_aliases={6: 2},
        compiler_params=_params("arbitrary"),
    )(y_a, y_b, rest, rest, b_gate, d_merged, d_hcat)


ST_LNG, ST_LNB, ST_BG2, ST_LOSS = 0, 1, 2, 3


def _final(x, mix, pw, rest, b_gate, ln_gb, target):
    t = rest.shape[0]

    def body(x_ref, mix_ref, pw_ref, gp_ref, bg_ref, ln_ref, tgt_ref, dpre_ref, dpre16_ref, dgp_ref, dpw_ref, st_ref):
        @pl.when(pl.program_id(0) == 0)
        def _():
            st_ref[...] = jnp.zeros_like(st_ref)

        def chunk(rows):
            sp = _sigmoid(gp_ref[rows, :] + bg_ref[2:3, :])
            pw = pw_ref[rows, :]
            pre = ALPHA * x_ref[rows, :] + mix_ref[rows, :] + sp * pw
            xc = pre - jnp.mean(pre, axis=-1, keepdims=True)
            rstd = lax.rsqrt(jnp.mean(xc * xc, axis=-1, keepdims=True) + LN_EPS)
            xhat = xc * rstd
            gain = ln_ref[0:1, :]
            err = xhat * gain + ln_ref[1:2, :] - tgt_ref[rows, :]
            d_yo = err * (1.0 / D_MODEL)
            d_xhat = d_yo * gain
            d_pre = rstd * (d_xhat - jnp.mean(d_xhat, axis=-1, keepdims=True)
                            - xhat * jnp.mean(d_xhat * xhat, axis=-1, keepdims=True))
            dpre_ref[rows, :] = d_pre
            dpre16_ref[rows, :] = d_pre.astype(BF16)
            dgp = d_pre * pw * sp * (1.0 - sp)
            dgp_ref[rows, :] = dgp.astype(BF16)
            dpw_ref[rows, :] = (d_pre * sp).astype(BF16)
            st_ref[ST_LNG:ST_LNG + 1, :] += jnp.sum(d_yo * xhat, axis=0, keepdims=True)
            st_ref[ST_LNB:ST_LNB + 1, :] += jnp.sum(d_yo, axis=0, keepdims=True)
            st_ref[ST_BG2:ST_BG2 + 1, :] += jnp.sum(dgp, axis=0, keepdims=True)
            st_ref[ST_LOSS:ST_LOSS + 1, :] += jnp.sum(err * err, axis=0, keepdims=True) * (0.5 / D_MODEL)
        _row_chunks(chunk, WIDE_ROW_CHUNK)

    full, vec, at = _row_specs()
    s16 = jax.ShapeDtypeStruct((t, D_MODEL), BF16)
    return pl.pallas_call(
        body, name="final",
        grid=(t // ROW_TILE,),
        in_specs=[full, full, full, at(R_GPLE), vec, vec, full],
        out_specs=[full, full, at(R_GPLE), full, vec],
        out_shape=[jax.ShapeDtypeStruct((t, D_MODEL), F32), s16, jax.ShapeDtypeStruct((t, HCAT_COLS), BF16), s16,
                   jax.ShapeDtypeStruct((8, D_MODEL), F32)],
        compiler_params=_params("arbitrary"),
    )(x, mix, pw, rest, b_gate, ln_gb, target)


def _mesh_position():
    return lax.axis_index("x"), lax.axis_index("y"), lax.axis_index("c")


def _flip(pos, k):
    x, y, c = pos
    return ((1 - x) if k & 4 else x, (1 - y) if k & 2 else y, (1 - c) if k & 1 else c)


def _linear(pos):
    return 4 * pos[0] + 2 * pos[1] + pos[2]


def _exchange(arrays, scatter, name):
    n = len(arrays)

    def body(*refs):
        ins, outs = refs[:n], refs[n:2 * n]
        send_sems, recv_sems, local_sems = refs[2 * n:]
        me = _mesh_position()
        me_i = _linear(me)

        def src_for(i, dest_i):
            return ins[i].at[dest_i] if scatter[i] else ins[i]

        local = [pltpu.make_async_copy(src_for(i, me_i), outs[i].at[me_i], local_sems.at[i]) for i in range(n)]
        for cp in local:
            cp.start()
        started = []
        for k in range(1, N_DEV):
            peer = _flip(me, k)
            peer_i = _linear(peer)
            for i in range(n):
                sem = i * (N_DEV - 1) + k - 1
                cp = pltpu.make_async_remote_copy(
                    src_ref=src_for(i, peer_i), dst_ref=outs[i].at[me_i], send_sem=send_sems.at[sem],
                    recv_sem=recv_sems.at[sem], device_id=peer, device_id_type=pl.DeviceIdType.MESH)
                cp.start()
                started.append(cp)
        for k in range(1, N_DEV):
            peer = _flip(me, k)
            peer_i = _linear(peer)
            for i in range(n):
                sem = i * (N_DEV - 1) + k - 1
                pltpu.make_async_remote_copy(
                    src_ref=src_for(i, peer_i), dst_ref=outs[i].at[peer_i], send_sem=send_sems.at[sem],
                    recv_sem=recv_sems.at[sem], device_id=peer, device_id_type=pl.DeviceIdType.MESH).wait_recv()
        for cp in started:
            cp.wait_send()
        for cp in local:
            cp.wait()

    any_spec = pl.BlockSpec(memory_space=pl.ANY)
    out_shape = [jax.ShapeDtypeStruct(a.shape if s else (N_DEV,) + a.shape, a.dtype) for a, s in zip(arrays, scatter)]
    return pl.pallas_call(
        body, name=name,
        in_specs=[any_spec] * n,
        out_specs=[any_spec] * n,
        out_shape=out_shape,
        scratch_shapes=[pltpu.SemaphoreType.DMA((n * (N_DEV - 1),)), pltpu.SemaphoreType.DMA((n * (N_DEV - 1),)),
                        pltpu.SemaphoreType.DMA((n,))],
        compiler_params=pltpu.CompilerParams(has_side_effects=True),
    )(*arrays)


N_CHIPS = N_DEV // 2


def _other_chips(x, y):
    return [(1 - x, y), (x, 1 - y), (1 - x, 1 - y)]


def _gather_two_level(arrays, name):
    n = len(arrays)
    per = N_DEV - 1

    def body(*refs):
        ins, outs = refs[:n], refs[n:2 * n]
        send_sems, recv_sems, local_sems = refs[2 * n:]
        x, y, c = _mesh_position()
        me, sibling = (x, y, c), (x, y, 1 - c)
        chips = _other_chips(x, y)

        def copy(i, k, block, to, src=None):
            slot = outs[i].at[_linear(block)]
            return pltpu.make_async_remote_copy(
                src_ref=slot if src is None else src, dst_ref=slot, send_sem=send_sems.at[i * per + k],
                recv_sem=recv_sems.at[i * per + k], device_id=to, device_id_type=pl.DeviceIdType.MESH)

        local = [pltpu.make_async_copy(ins[i], outs[i].at[_linear(me)], local_sems.at[i]) for i in range(n)]
        for cp in local:
            cp.start()
        started = []
        for i in range(n):
            first = [copy(i, 0, me, sibling, src=ins[i])]
            first += [copy(i, 1 + j, me, (*chip, c), src=ins[i]) for j, chip in enumerate(chips)]
            for cp in first:
                cp.start()
            started += first
        for j, chip in enumerate(chips):
            for i in range(n):
                copy(i, 1 + j, (*chip, c), me).wait_recv()
                passed = copy(i, 4 + j, (*chip, c), sibling)
                passed.start()
                started.append(passed)
        for i in range(n):
            copy(i, 0, sibling, me).wait_recv()
            for j, chip in enumerate(chips):
                copy(i, 4 + j, (*chip, 1 - c), me).wait_recv()
        for cp in started:
            cp.wait_send()
        for cp in local:
            cp.wait()

    any_spec = pl.BlockSpec(memory_space=pl.ANY)
    return pl.pallas_call(
        body, name=name,
        in_specs=[any_spec] * n,
        out_specs=[any_spec] * n,
        out_shape=[jax.ShapeDtypeStruct((N_DEV,) + a.shape, a.dtype) for a in arrays],
        scratch_shapes=[pltpu.SemaphoreType.DMA((n * per,)), pltpu.SemaphoreType.DMA((n * per,)), pltpu.SemaphoreType.DMA((n,))],
        compiler_params=pltpu.CompilerParams(has_side_effects=True),
    )(*arrays)


def _pair_reduce(a, name, rows):
    _, r, c = a.shape
    assert r % rows == 0
    n_steps = r // rows
    a5 = a.reshape(N_CHIPS, 2, r, c)
    core = lax.axis_index("c").astype(jnp.int32).reshape(1)

    def body(core_ref, keep_ref, send_ref, o_ref, land, send_sems, recv_sems, credits):
        i = pl.program_id(0)
        slot = i % 2
        x, y, cc = _mesh_position()
        sibling = (x, y, 1 - cc)

        @pl.when(i >= 2)
        def _():
            pl.semaphore_wait(credits.at[slot], 1)

        rdma = pltpu.make_async_remote_copy(
            src_ref=send_ref, dst_ref=land.at[slot], send_sem=send_sems.at[slot], recv_sem=recv_sems.at[slot],
            device_id=sibling, device_id_type=pl.DeviceIdType.MESH)
        rdma.start()
        rdma.wait_recv()
        o_ref[...] = (keep_ref[:, 0].astype(F32) + land[slot, :, 0].astype(F32)).astype(o_ref.dtype)
        rdma.wait_send()

        @pl.when(i + 2 < n_steps)
        def _():
            pl.semaphore_signal(credits.at[slot], inc=1, device_id=sibling, device_id_type=pl.DeviceIdType.MESH)

    grid_spec = pltpu.PrefetchScalarGridSpec(
        num_scalar_prefetch=1,
        grid=(n_steps,),
        in_specs=[pl.BlockSpec((N_CHIPS, 1, rows, c), lambda i, core_ref: (0, core_ref[0], i, 0)),
                  pl.BlockSpec((N_CHIPS, 1, rows, c), lambda i, core_ref: (0, 1 - core_ref[0], i, 0))],
        out_specs=pl.BlockSpec((N_CHIPS, rows, c), lambda i, core_ref: (0, i, 0)),
        scratch_shapes=[pltpu.VMEM((2, N_CHIPS, 1, rows, c), a.dtype), pltpu.SemaphoreType.DMA((2,)),
                        pltpu.SemaphoreType.DMA((2,)), pltpu.SemaphoreType.REGULAR((2,))],
    )
    return pl.pallas_call(
        body, name=name, grid_spec=grid_spec,
        out_shape=jax.ShapeDtypeStruct((N_CHIPS, r, c), a.dtype),
        compiler_params=pltpu.CompilerParams(dimension_semantics=("arbitrary",), vmem_limit_bytes=VMEM_LIMIT_BYTES,
                                             has_side_effects=True),
    )(core, a5, a5)


def _chip_exchange(arrays, name):
    n = len(arrays)
    per = N_CHIPS - 1

    def body(*refs):
        ins, outs = refs[:n], refs[n:2 * n]
        send_sems, recv_sems, local_sems = refs[2 * n:]
        x, y, c = _mesh_position()
        me_q = 2 * x + y
        local = [pltpu.make_async_copy(ins[i].at[me_q], outs[i].at[me_q], local_sems.at[i]) for i in range(n)]
        for cp in local:
            cp.start()
        started = []
        for j, (px, py) in enumerate(_other_chips(x, y)):
            for i in range(n):
                cp = pltpu.make_async_remote_copy(
                    src_ref=ins[i].at[2 * px + py], dst_ref=outs[i].at[me_q], send_sem=send_sems.at[i * per + j],
                    recv_sem=recv_sems.at[i * per + j], device_id=(px, py, c), device_id_type=pl.DeviceIdType.MESH)
                cp.start()
                started.append(cp)
        for j, (px, py) in enumerate(_other_chips(x, y)):
            for i in range(n):
                pltpu.make_async_remote_copy(
                    src_ref=ins[i].at[2 * px + py], dst_ref=outs[i].at[2 * px + py], send_sem=send_sems.at[i * per + j],
                    recv_sem=recv_sems.at[i * per + j], device_id=(px, py, c), device_id_type=pl.DeviceIdType.MESH).wait_recv()
        for cp in started:
            cp.wait_send()
        for cp in local:
            cp.wait()

    any_spec = pl.BlockSpec(memory_space=pl.ANY)
    return pl.pallas_call(
        body, name=name,
        in_specs=[any_spec] * n,
        out_specs=[any_spec] * n,
        out_shape=[jax.ShapeDtypeStruct(a.shape, a.dtype) for a in arrays],
        scratch_shapes=[pltpu.SemaphoreType.DMA((n * per,)), pltpu.SemaphoreType.DMA((n * per,)), pltpu.SemaphoreType.DMA((n,))],
        compiler_params=pltpu.CompilerParams(has_side_effects=True),
    )(*arrays)


def _chip_exchange_start(arrays, name):
    n = len(arrays)
    per = N_CHIPS - 1
    hbm = pl.BlockSpec(memory_space=pltpu.HBM)
    sem = pl.BlockSpec(memory_space=pltpu.SEMAPHORE)

    def body(*refs):
        ins, lands = refs[:n], refs[n:2 * n]
        send_sems, recv_sems = refs[2 * n], refs[2 * n + 1]
        token = refs[-1]
        x, y, c = _mesh_position()
        me_q = 2 * x + y
        for j, (px, py) in enumerate(_other_chips(x, y)):
            for i in range(n):
                pltpu.make_async_remote_copy(
                    src_ref=ins[i].at[2 * px + py], dst_ref=lands[i].at[me_q], send_sem=send_sems.at[i * per + j],
                    recv_sem=recv_sems.at[i * per + j], device_id=(px, py, c), device_id_type=pl.DeviceIdType.MESH).start()
        token[...] = jnp.zeros_like(token)

    buffers = [pltpu.HBM(a.shape, a.dtype) for a in arrays]
    res = pl.pallas_call(
        body, name=name,
        out_shape=(pltpu.SemaphoreType.DMA((n * per,)), pltpu.SemaphoreType.DMA((n * per,)), *buffers, *buffers,
                   jax.ShapeDtypeStruct((8, BLK), F32)),
        in_specs=[hbm] * (2 * n),
        out_specs=(sem, sem, *([hbm] * (2 * n)), pl.BlockSpec(memory_space=pltpu.VMEM)),
        input_output_aliases={i: 2 + i for i in range(2 * n)},
        compiler_params=pltpu.CompilerParams(has_side_effects=pltpu.SideEffectType.DATAFLOW_SIDE_EFFECTING),
    )(*[pltpu.with_memory_space_constraint(a, pltpu.HBM) for a in arrays],
      *[pltpu.with_memory_space_constraint(lax.empty(a.shape, a.dtype), pltpu.HBM) for a in arrays])
    return res[0], res[1], res[2:2 + n], res[2 + n:2 + 2 * n], res[-1]


def _chip_exchange_wait(send_sems, recv_sems, sources, lands, after, name):
    n = len(sources)
    per = N_CHIPS - 1
    hbm = pl.BlockSpec(memory_space=pltpu.HBM)
    sem = pl.BlockSpec(memory_space=pltpu.SEMAPHORE)

    def body(*refs):
        ins, zones = refs[:n], refs[n:2 * n]
        send, recv = refs[2 * n], refs[2 * n + 1]
        x, y, c = _mesh_position()
        for j, (px, py) in enumerate(_other_chips(x, y)):
            for i in range(n):
                cp = pltpu.make_async_remote_copy(
                    src_ref=ins[i].at[2 * px + py], dst_ref=zones[i].at[2 * px + py], send_sem=send.at[i * per + j],
                    recv_sem=recv.at[i * per + j], device_id=(px, py, c), device_id_type=pl.DeviceIdType.MESH)
                cp.wait_send()
                cp.wait_recv()

    buffers = [pltpu.HBM(a.shape, a.dtype) for a in sources]
    res = pl.pallas_call(
        body, name=name,
        out_shape=(*buffers, *buffers),
        in_specs=[hbm] * (2 * n) + [sem, sem, pl.BlockSpec(memory_space=pl.ANY)],
        out_specs=[hbm] * (2 * n),
        input_output_aliases={i: i for i in range(2 * n)},
        compiler_params=pltpu.CompilerParams(has_side_effects=pltpu.SideEffectType.DATAFLOW_SIDE_EFFECTING),
    )(*sources, *lands, send_sems, recv_sems, after)
    return res[:n], res[n:]


def _adam_reduce(parts, w, m, v, name, rows):
    r, c = w.shape
    n_parts = parts.shape[0]
    assert r % rows == 0
    c1 = 1.0 - ADAM_B1 ** ADAM_STEP
    c2 = 1.0 - ADAM_B2 ** ADAM_STEP

    def body(p_ref, w_ref, m_ref, v_ref, g_ref, d_ref, nm_ref, nv_ref):
        g = p_ref[0].astype(F32)
        for s in range(1, n_parts):
            g = g + p_ref[s].astype(F32)
        g_ref[...] = g
        nm = ADAM_B1 * m_ref[...] + (1.0 - ADAM_B1) * g
        nv = ADAM_B2 * v_ref[...] + (1.0 - ADAM_B2) * (g * g)
        nm_ref[...] = nm
        nv_ref[...] = nv
        d_ref[...] = -ADAM_LR * ((nm / c1) / (jnp.sqrt(nv / c2) + ADAM_EPS) + ADAM_WD * w_ref[...])

    blk = pl.BlockSpec((rows, c), lambda i: (i, 0))
    shape = jax.ShapeDtypeStruct((r, c), F32)
    return pl.pallas_call(
        body, name=name,
        grid=(r // rows,),
        in_specs=[pl.BlockSpec((n_parts, rows, c), lambda i: (0, i, 0)), blk, blk, blk],
        out_specs=[blk] * 4,
        out_shape=[shape] * 4,
        compiler_params=_params("parallel"),
    )(parts, w, m, v)


def _lane_total(rows8):
    def body(a_ref, o_ref):
        o_ref[...] = _sum_all(a_ref[...])

    return pl.pallas_call(body, name="loss_total", out_shape=jax.ShapeDtypeStruct((1, 1), F32))(rows8)


def _permute_w_in(w):
    rows = w.shape[0]
    n_pairs = N_GROUPS * HEAD_PAIRS
    qkv = w[:, :V_END].reshape(rows, 3, n_pairs, BLK).transpose(0, 2, 1, 3).reshape(rows, V_END)
    return jnp.concatenate(
        [w[:, DT_END:GMERGE_END], w[:, GATT_END:Z_END], w[:, Z_END:XBC_END], w[:, GMERGE_END:], w[:, V_END:GATT_END],
         w[:, XBC_END:DT_END], jnp.zeros((rows, QKV_OFF - R_DT - SSM_HEADS), w.dtype), qkv], axis=1)


def _unpermute_w_in(g):
    rows = g.shape[0]
    n_pairs = N_GROUPS * HEAD_PAIRS
    qkv = g[:, QKV_OFF:].reshape(rows, n_pairs, 3, BLK).transpose(0, 2, 1, 3).reshape(rows, V_END)
    return jnp.concatenate(
        [qkv, g[:, R_GATT:R_GATT + GROUP_COLS], g[:, R_Z:R_Z + D_INNER], g[:, R_XBC:R_XBC + CONV_DIM],
         g[:, R_DT:R_DT + SSM_HEADS], g[:, R_GM:R_GM + 2 * D_MODEL], g[:, R_GPLE:R_GPLE + D_MODEL]], axis=1)


def _column_runs():
    runs = [(DT_END, R_GM, 2 * D_MODEL), (GATT_END, R_Z, D_INNER), (Z_END, R_XBC, CONV_DIM), (GMERGE_END, R_GPLE, D_MODEL),
            (V_END, R_GATT, GROUP_COLS), (XBC_END, R_DT, SSM_HEADS)]
    for pair in range(N_GROUPS * HEAD_PAIRS):
        for part in range(3):
            runs.append((part * ATT_QKV + pair * BLK, QKV_OFF + (pair * 3 + part) * BLK, BLK))
    return runs


RELAYOUT_ROWS = 128


def _assemble_w_in(gathered):
    rows = gathered.shape[1]

    def pieces(ref_col, width):
        out = []
        while width > 0:
            s, o = divmod(ref_col, IN_SHARD)
            w = min(width, IN_SHARD - o)
            out.append((s, o, w))
            ref_col, width = ref_col + w, width - w
        return out

    blocks = {}
    for ref_col, col, width in _column_runs():
        for off in range(0, width, BLK):
            blocks[(col + off) // BLK] = (ref_col + off, min(BLK, width - off))

    def body(g_ref, o_ref):
        for blk in range(HCAT_COLS // BLK):
            lanes = pl.ds(blk * BLK, BLK)
            if blk not in blocks:
                o_ref[:, lanes] = jnp.zeros((RELAYOUT_ROWS, BLK), o_ref.dtype)
                continue
            ref_col, width = blocks[blk]
            parts = [g_ref[s, :, pl.ds(o, w)] for s, o, w in pieces(ref_col, width)]
            if width < BLK:
                parts.append(jnp.zeros((RELAYOUT_ROWS, BLK - width), o_ref.dtype))
            o_ref[:, lanes] = parts[0] if len(parts) == 1 else jnp.concatenate(parts, axis=1)

    return pl.pallas_call(
        body, name="assemble_w_in",
        grid=(rows // RELAYOUT_ROWS,),
        in_specs=[pl.BlockSpec((N_DEV, RELAYOUT_ROWS, IN_SHARD), lambda i: (0, i, 0))],
        out_specs=pl.BlockSpec((RELAYOUT_ROWS, HCAT_COLS), lambda i: (i, 0)),
        out_shape=jax.ShapeDtypeStruct((rows, HCAT_COLS), gathered.dtype),
        compiler_params=_params("parallel"),
    )(gathered)


def _scatter_w_in_grad(g):
    rows = g.shape[0]
    runs = sorted(_column_runs())

    def pieces(shard):
        lo, hi = shard * IN_SHARD, (shard + 1) * IN_SHARD
        out = []
        for ref_col, col, width in runs:
            a, b = max(lo, ref_col), min(hi, ref_col + width)
            if a < b:
                out.append((col + a - ref_col, b - a))
        return out

    def body(g_ref, o_ref):
        for shard in range(N_DEV):
            o_ref[shard] = jnp.concatenate([g_ref[:, pl.ds(col, w)] for col, w in pieces(shard)], axis=1)

    return pl.pallas_call(
        body, name="scatter_w_in_grad",
        grid=(rows // RELAYOUT_ROWS,),
        in_specs=[pl.BlockSpec((RELAYOUT_ROWS, HCAT_COLS), lambda i: (i, 0))],
        out_specs=pl.BlockSpec((N_DEV, RELAYOUT_ROWS, IN_SHARD), lambda i: (0, i, 0)),
        out_shape=jax.ShapeDtypeStruct((N_DEV, rows, IN_SHARD), g.dtype),
        compiler_params=_params("parallel"),
    )(g)


SMALL_ROWS = 80
_SMALL_LAYOUT = (("conv_b", CONV_DIM), ("dt_bias", BLK), ("a_log", BLK), ("d_skip", BLK), ("ssm_norm_w", D_INNER),
                 ("ln_g", D_MODEL), ("ln_b", D_MODEL), ("rel_bias", NUM_BUCKETS * ATT_HEADS), ("loss", D_MODEL))


def _pack_small(vals):
    flat = []
    for name, width in _SMALL_LAYOUT:
        v = vals.get(name)
        v = jnp.zeros((width,), F32) if v is None else v.reshape(-1).astype(F32)
        flat.append(jnp.pad(v, (0, width - v.shape[0])))
    flat = jnp.concatenate(flat)
    return jnp.pad(flat, (0, SMALL_ROWS * BLK - flat.shape[0])).reshape(SMALL_ROWS, BLK)


def _unpack_small(packed):
    flat = packed.reshape(-1)
    out, pos = {}, 0
    for name, width in _SMALL_LAYOUT:
        out[name] = flat[pos:pos + width]
        pos += width
    for name in ("dt_bias", "a_log", "d_skip"):
        out[name] = out[name][:SSM_HEADS]
    out["rel_bias"] = out["rel_bias"].reshape(NUM_BUCKETS, ATT_HEADS)
    return out


def _pack_cols(b_gate_part, conv_w_part):
    return jnp.concatenate([jnp.pad(b_gate_part, ((0, 5), (0, 0))), jnp.pad(conv_w_part, ((0, 4), (0, 0)))], axis=1)


def _pack_cols_all(b_gate_full, conv_w_full):
    bg = b_gate_full.reshape(3, N_DEV, BLK).transpose(1, 0, 2)
    cw = conv_w_full.reshape(CONV_WIDTH, N_DEV, CONV_DIM // N_DEV).transpose(1, 0, 2)
    return jnp.concatenate([jnp.pad(bg, ((0, 0), (0, 5), (0, 0))), jnp.pad(cw, ((0, 0), (0, 4), (0, 0)))], axis=2)


def _unpack_cols_all(packed):
    bg = packed[:, :3, :BLK].transpose(1, 0, 2).reshape(3, D_MODEL)
    cw = packed[:, :CONV_WIDTH, BLK:].transpose(1, 0, 2).reshape(CONV_WIDTH, CONV_DIM)
    return bg, cw


def _local_step(x, p, target, wp16, wb16, wo16, wple16, b_gate, conv_w, small, start_exchange):
    bsz, seq, _ = x.shape
    t = bsz * seq
    x2 = x.reshape(t, D_MODEL)
    x16 = x2.astype(BF16)
    p16 = p.reshape(t, PLE_DIM).astype(BF16)
    tgt2 = target.reshape(t, D_MODEL)
    b_gate8 = jnp.pad(b_gate, ((0, 5), (0, 0)))
    ln_gb = jnp.pad(jnp.stack([small["ln_g"], small["ln_b"]]), ((0, 6), (0, 0)))
    conv_b = small["conv_b"].reshape(1, CONV_DIM)
    norm_w = small["ssm_norm_w"].reshape(1, D_INNER)
    pad_heads = lambda v: jnp.pad(v, (0, BLK - SSM_HEADS))
    prow = jnp.pad(jnp.stack([pad_heads(small["dt_bias"]), pad_heads(small["a_log"]), pad_heads(small["d_skip"])]), ((0, 5), (0, 0)))
    pcol = prow.T
    wa16, wbb16 = wb16[:GROUP_COLS], wb16[GROUP_COLS:]

    rest = _matmul(x16, wp16, mode="nn", out_dtype=F32, name="inproj", tm=512, tn=2304, tk=D_MODEL, n_outer=True)
    rest3 = rest.reshape(bsz, seq, HCAT_COLS)
    biases, onehots, outs, lses = [], [], [], []
    for g, (_, dil) in enumerate(DILATED_PATTERNS):
        bias, onehot = _bias_matrix(small["rel_bias"][:, g * HEADS_PER_GROUP:(g + 1) * HEADS_PER_GROUP], dil)
        out3, lse3 = _attn_fwd(rest3, bias, g, dil, f"attn_fwd{g}")
        biases.append(bias), onehots.append(onehot)
        outs.append(out3.reshape(t, GROUP_COLS)), lses.append(lse3.reshape(t, GROUP_COLS))
    oa = _combine_fwd(outs, lses, rest)
    xa = _conv_fwd(rest3, conv_w, conv_b)
    dtr_t = jnp.swapaxes(rest3[:, :, R_DT:R_DT + BLK], 1, 2)
    y, s_in = _ssd_fwd(xa, rest3, dtr_t, prow, pcol)
    y2 = y.reshape(t, D_INNER)
    ys = _gatenorm_fwd(y2, rest, norm_w)
    y_a = _matmul(oa, wa16, mode="nn", out_dtype=F32, name="branch_a", tm=512, tn=D_MODEL, tk=GROUP_COLS)
    y_b = _matmul(ys, wbb16, mode="nn", out_dtype=F32, name="branch_b", tm=512, tn=D_MODEL, tk=D_INNER)
    merged = _merge_fwd(y_a, y_b, rest, b_gate8)
    mix = _matmul(merged, wo16, mode="nn", out_dtype=F32, name="out_proj", tm=512, tn=D_MODEL, tk=D_MODEL)
    pw = _matmul(p16, wple16, mode="nn", out_dtype=F32, name="ple_proj", tm=512, tn=D_MODEL, tk=PLE_DIM)
    d_pre, d_pre16, d_hcat, d_pw, stats = _final(x2, mix, pw, rest, b_gate8, ln_gb, tgt2)

    d_merged = _matmul(d_pre16, wo16, mode="nt", out_dtype=F32, name="d_merged", tm=512, tn=D_MODEL, tk=D_MODEL)
    g_w_out = _matmul(merged, d_pre16, mode="tn", out_dtype=BF16, name="g_w_out", tm=512, tn=D_MODEL, tk=1024)
    g_w_ple = _matmul(p16, d_pw, mode="tn", out_dtype=BF16, name="g_w_ple", tm=PLE_DIM, tn=D_MODEL, tk=1024)
    d_ya, d_yb, d_hcat, dbg01 = _merge_bwd(y_a, y_b, rest, b_gate8, d_merged, d_hcat)
    d_oa = _matmul(d_ya, wa16, mode="nt", out_dtype=F32, name="d_oa", tm=512, tn=GROUP_COLS, tk=D_MODEL)
    d_ys = _matmul(d_yb, wbb16, mode="nt", out_dtype=F32, name="d_ys", tm=512, tn=1024, tk=D_MODEL)
    g_wa = _matmul(oa, d_ya, mode="tn", out_dtype=BF16, name="g_w_branch_a", tm=GROUP_COLS, tn=D_MODEL, tk=1024)
    g_wb = _matmul(ys, d_yb, mode="tn", out_dtype=BF16, name="g_w_branch_b", tm=512, tn=D_MODEL, tk=1024)
    d_outs_dd_hc = _combine_bwd(outs, lses, rest, d_oa, d_hcat)
    d_outs, dds, d_hcat = d_outs_dd_hc[:3], d_outs_dd_hc[3:6], d_outs_dd_hc[6]
    d_y, d_hcat, d_nw = _gatenorm_bwd(y2, rest, norm_w, d_ys, d_hcat)
    d_xa, ddt, ddt_t, gprow, gpcol = _ssd_bwd(xa, rest3, dtr_t, prow, pcol, s_in, d_y.reshape(bsz, seq, D_INNER))
    d_hcat3, d_conv = _conv_bwd(rest3, conv_w, conv_b, d_xa, d_hcat.reshape(bsz, seq, HCAT_COLS))
    g_tables = []
    shape3 = (bsz, seq, GROUP_COLS)
    for g, (_, dil) in enumerate(DILATED_PATTERNS):
        d_hcat3, dbias = _attn_bwd(rest3, biases[g], lses[g].reshape(shape3), d_outs[g].reshape(shape3), dds[g].reshape(shape3),
                                   d_hcat3, g, dil, f"attn_bwd{g}")
        g_tables.append(_bias_grad(dbias, onehots[g], f"bias_grad{g}"))
    d_dt = (ddt + jnp.swapaxes(ddt_t, 1, 2)).reshape(t, BLK).astype(BF16)
    d_hcat = lax.dynamic_update_slice(d_hcat3.reshape(t, HCAT_COLS),
                                      jnp.concatenate([d_dt, jnp.zeros((t, QKV_OFF - R_DT - BLK), BF16)], axis=1), (0, R_DT))
    g_wp = _matmul(x16, d_hcat, mode="tn", out_dtype=BF16, name="g_w_in", tm=D_MODEL, tn=2304, tk=1024, n_outer=True)

    grads = dict(
        w_in=_scatter_w_in_grad(g_wp),
        b_gate=jnp.stack([dbg01[0], dbg01[1], stats[ST_BG2]]),
        conv_w=d_conv[:CONV_WIDTH],
        w_branch=jnp.concatenate([g_wa, g_wb], axis=0),
        w_out=g_w_out,
        w_ple=g_w_ple,
    )
    small_grads = dict(
        conv_b=d_conv[CONV_WIDTH],
        dt_bias=gprow[P_DTB, :SSM_HEADS] + gpcol[:SSM_HEADS, P_DTB],
        a_log=gprow[P_ALOG, :SSM_HEADS] + gpcol[:SSM_HEADS, P_ALOG],
        d_skip=gprow[P_DSKIP, :SSM_HEADS],
        ssm_norm_w=d_nw[0],
        ln_g=stats[ST_LNG],
        ln_b=stats[ST_LNB],
        rel_bias=jnp.concatenate(g_tables, axis=1),
        loss=stats[ST_LOSS],
    )
    in_flight, token = start_exchange(grads)
    grad_x = _matmul(d_hcat, wp16, mode="nt", out_dtype=F32, name="grad_x", tm=1024, tn=D_MODEL, tk=2304,
                     add=d_pre, add_scale=ALPHA, after=token)
    return grad_x.reshape(bsz, seq, D_MODEL), in_flight, small_grads


WEIGHT_ORDER = ("w_in", "b_gate", "conv_w", "conv_b", "dt_bias", "a_log", "d_skip", "ssm_norm_w", "w_branch", "w_out",
                "w_ple", "ln_g", "ln_b", "rel_bias")
SMALL_NAMES = ("conv_b", "dt_bias", "a_log", "d_skip", "ssm_norm_w", "ln_g", "ln_b", "rel_bias")


def kernel(x, p, w_in, b_gate, conv_w, conv_b, dt_bias, a_log, d_skip, ssm_norm_w, w_branch, w_out, w_ple, ln_g, ln_b, rel_bias, loss_target, m_w_in, m_b_gate, m_conv_w, m_conv_b, m_dt_bias, m_a_log, m_d_skip, m_ssm_norm_w, m_w_branch, m_w_out, m_w_ple, m_ln_g, m_ln_b, m_rel_bias, v_w_in, v_b_gate, v_conv_w, v_conv_b, v_dt_bias, v_a_log, v_d_skip, v_ssm_norm_w, v_w_branch, v_w_out, v_w_ple, v_ln_g, v_ln_b, v_rel_bias):
    given = dict(w_in=w_in, b_gate=b_gate, conv_w=conv_w, conv_b=conv_b, dt_bias=dt_bias, a_log=a_log, d_skip=d_skip,
                 ssm_norm_w=ssm_norm_w, w_branch=w_branch, w_out=w_out, w_ple=w_ple, ln_g=ln_g, ln_b=ln_b)
    moments_m = dict(w_in=m_w_in, b_gate=m_b_gate, conv_w=m_conv_w, conv_b=m_conv_b, dt_bias=m_dt_bias, a_log=m_a_log,
                     d_skip=m_d_skip, ssm_norm_w=m_ssm_norm_w, w_branch=m_w_branch, w_out=m_w_out, w_ple=m_w_ple,
                     ln_g=m_ln_g, ln_b=m_ln_b)
    moments_v = dict(w_in=v_w_in, b_gate=v_b_gate, conv_w=v_conv_w, conv_b=v_conv_b, dt_bias=v_dt_bias, a_log=v_a_log,
                     d_skip=v_d_skip, ssm_norm_w=v_ssm_norm_w, w_branch=v_w_branch, w_out=v_w_out, w_ple=v_w_ple,
                     ln_g=v_ln_g, ln_b=v_ln_b)
    w = {k: a[0] for k, a in given.items()} | {"rel_bias": rel_bias}
    mm = {k: a[0] for k, a in moments_m.items()} | {"rel_bias": m_rel_bias}
    vv = {k: a[0] for k, a in moments_v.items()} | {"rel_bias": v_rel_bias}

    gathered = _gather_two_level(
        [w["w_in"].astype(BF16), w["w_branch"].astype(BF16), w["w_out"].astype(BF16), w["w_ple"].astype(BF16),
         _pack_cols(w["b_gate"], w["conv_w"])], "gather_weights")
    wp16 = _assemble_w_in(gathered[0])
    wb16 = gathered[1].reshape(BRANCH_ROWS, D_MODEL)
    wo16 = gathered[2].reshape(D_MODEL, D_MODEL)
    wple16 = gathered[3].transpose(1, 0, 2).reshape(PLE_DIM, D_MODEL)
    b_gate_full, conv_w_full = _unpack_cols_all(gathered[4])
    small = {k: w[k] for k in SMALL_NAMES}

    def start_exchange(grads):
        big = [grads["w_in"],
               grads["w_branch"].astype(BF16).reshape(N_DEV, BRANCH_ROWS // N_DEV, D_MODEL),
               grads["w_out"].astype(BF16).reshape(N_DEV, D_MODEL // N_DEV, D_MODEL),
               grads["w_ple"].astype(BF16).reshape(PLE_DIM, N_DEV, BLK).transpose(1, 0, 2)]
        sums = [_pair_reduce(a, f"pair_reduce{i}", rows) for i, (a, rows) in enumerate(zip(big, (128, 176, 128, 256)))]
        send_sems, recv_sems, sources, lands, token = _chip_exchange_start(sums, "chip_exchange_start")
        cols_all = _pack_cols_all(grads["b_gate"], grads["conv_w"])
        return (send_sems, recv_sems, sources, lands, cols_all), token

    grad_x, in_flight, small_grads = _local_step(x, p[0], loss_target, wp16, wb16, wo16, wple16, b_gate_full, conv_w_full,
                                                 small, start_exchange)
    send_sems, recv_sems, sources, lands, cols_all = in_flight
    sources, zones = _chip_exchange_wait(send_sems, recv_sems, sources, lands, grad_x, "chip_exchange_wait")
    me_q = 2 * lax.axis_index("x") + lax.axis_index("y")
    parts = [lax.dynamic_update_slice_in_dim(z, lax.dynamic_slice_in_dim(s, me_q, 1, axis=0), me_q, axis=0)
             for z, s in zip(zones, sources)]
    small_parts = _exchange([cols_all, _pack_small(small_grads)], [True, False], "exchange_small")

    out = {}
    out["w_in"] = _adam_reduce(parts[0], w["w_in"], mm["w_in"], vv["w_in"], "adam_w_in", 128)
    out["w_branch"] = _adam_reduce(parts[1], w["w_branch"], mm["w_branch"], vv["w_branch"], "adam_w_branch", 176)
    out["w_out"] = _adam_reduce(parts[2], w["w_out"], mm["w_out"], vv["w_out"], "adam_w_out", 128)
    out["w_ple"] = _adam_reduce(parts[3], w["w_ple"], mm["w_ple"], vv["w_ple"], "adam_w_ple", 256)
    cols = _adam_reduce(small_parts[0], _pack_cols(w["b_gate"], w["conv_w"]), _pack_cols(mm["b_gate"], mm["conv_w"]),
                        _pack_cols(vv["b_gate"], vv["conv_w"]), "adam_cols", 8)
    out["b_gate"] = [a[:3, :BLK] for a in cols]
    out["conv_w"] = [a[:CONV_WIDTH, BLK:] for a in cols]
    packed = _adam_reduce(small_parts[1], _pack_small({k: w[k] for k in SMALL_NAMES}), _pack_small({k: mm[k] for k in SMALL_NAMES}),
                          _pack_small({k: vv[k] for k in SMALL_NAMES}), "adam_small", SMALL_ROWS)
    unpacked = [_unpack_small(a) for a in packed]
    for k in SMALL_NAMES:
        out[k] = [u[k] for u in unpacked]
    loss_rows = unpacked[0]["loss"].reshape(D_MODEL // BLK, BLK)
    loss = _lane_total(loss_rows).reshape(())

    def shaped(k, a):
        return a if k == "rel_bias" else a[None]

    results = [loss, grad_x]
    for i in range(4):
        results += [shaped(k, out[k][i]) for k in WEIGHT_ORDER]
    return tuple(results)
```

```python
import functools
import math

import jax
import jax.numpy as jnp
from jax import lax
from jax.experimental import pallas as pl
from jax.experimental.pallas import tpu as pltpu

F32 = jnp.float32
BF16 = jnp.bfloat16

N_DEV = 8
D_MODEL = 1024
SEQ = 2048
HEAD_DIM = 64
HEADS_PER_GROUP = 12
DILATED_PATTERNS = ((128, 1), (512, 4), (2048, 16))
N_GROUPS = 3
ATT_HEADS = N_GROUPS * HEADS_PER_GROUP
GROUP_COLS = HEADS_PER_GROUP * HEAD_DIM
ATT_QKV = ATT_HEADS * HEAD_DIM
BLK = 128
NUM_BUCKETS = 32
MAX_DISTANCE = 2048
D_INNER = 2048
SSM_HEADS = 32
SSM_GROUPS = 4
HEADS_PER_SSM_GROUP = SSM_HEADS // SSM_GROUPS
D_STATE = 128
CONV_WIDTH = 4
CONV_DIM = D_INNER + 2 * SSM_GROUPS * D_STATE
PLE_DIM = 256
ALPHA = 2.0 ** 0.25
LN_EPS = 1e-5
RMS_EPS = 1e-5
IN_COLS = 15904
IN_SHARD = IN_COLS // N_DEV
BRANCH_ROWS = GROUP_COLS + D_INNER

Q_END = ATT_QKV
K_END = 2 * ATT_QKV
V_END = 3 * ATT_QKV
GATT_END = V_END + GROUP_COLS
Z_END = GATT_END + D_INNER
XBC_END = Z_END + CONV_DIM
DT_END = XBC_END + SSM_HEADS
GMERGE_END = DT_END + 2 * D_MODEL

R_GM, R_Z, R_XBC, R_GPLE, R_GATT, R_DT = 0, 2048, 4096, 7168, 8192, 8960
QKV_OFF = 9216
HEAD_PAIRS = GROUP_COLS // BLK
QKV_G = 3 * GROUP_COLS
HCAT_COLS = QKV_OFF + N_GROUPS * QKV_G

ADAM_LR, ADAM_B1, ADAM_B2, ADAM_EPS, ADAM_WD, ADAM_STEP = 0.001, 0.9, 0.999, 1e-08, 0.01, 10

VMEM_LIMIT_BYTES = 56 * 1024 * 1024


def _params(*semantics):
    return pltpu.CompilerParams(dimension_semantics=semantics, vmem_limit_bytes=VMEM_LIMIT_BYTES)


def _sigmoid(v):
    return 1.0 / (1.0 + jnp.exp(-v))


def _silu_and_grad(v):
    s = _sigmoid(v)
    return v * s, s * (1.0 + v * (1.0 - s))


def _matmul(a, b, *, mode, out_dtype, name, tm, tn, tk, n_off=0, n=None, add=None, add_scale=1.0, n_outer=False, after=None):
    if mode == "nn":
        (m, k), n_full = a.shape, b.shape[1]
        assert b.shape[0] == k
    elif mode == "nt":
        (m, k), n_full = a.shape, b.shape[0]
        assert b.shape[1] == k
    else:
        (k, m), n_full = a.shape, b.shape[1]
        assert b.shape[0] == k
    n = n_full if n is None else n
    assert m % tm == 0 and n % tn == 0 and k % tk == 0 and n_off % tn == 0, (name, m, n, k)
    nk = k // tk
    jo = n_off // tn
    dims = {"nn": (((1,), (0,)), ((), ())), "nt": (((1,), (1,)), ((), ())), "tn": (((0,), (0,)), ((), ()))}[mode]

    def body(*refs):
        a_ref, b_ref = refs[:2]
        add_ref = refs[2] if add is not None else None
        o_ref = refs[2 + (add is not None) + (after is not None)]
        acc_ref = refs[-1] if nk > 1 else None
        prod = lax.dot_general(a_ref[...].astype(BF16), b_ref[...].astype(BF16), dims, preferred_element_type=F32)

        def finish(total):
            if add_ref is not None:
                total = total + add_scale * add_ref[...]
            o_ref[...] = total.astype(out_dtype)

        if nk == 1:
            finish(prod)
        else:
            kk = pl.program_id(2)

            @pl.when(kk == 0)
            def _():
                acc_ref[...] = prod

            @pl.when(jnp.logical_and(kk > 0, kk < nk - 1))
            def _():
                acc_ref[...] += prod

            @pl.when(kk == nk - 1)
            def _():
                finish(acc_ref[...] + prod)

    def ij(f):
        return (lambda g0, g1, kk: f(g1, g0, kk)) if n_outer else f

    if mode == "nn":
        a_spec = pl.BlockSpec((tm, tk), ij(lambda i, j, kk: (i, kk)))
        b_spec = pl.BlockSpec((tk, tn), ij(lambda i, j, kk: (kk, j + jo)))
    elif mode == "nt":
        a_spec = pl.BlockSpec((tm, tk), ij(lambda i, j, kk: (i, kk)))
        b_spec = pl.BlockSpec((tn, tk), ij(lambda i, j, kk: (j, kk)))
    else:
        a_spec = pl.BlockSpec((tk, tm), ij(lambda i, j, kk: (kk, i)))
        b_spec = pl.BlockSpec((tk, tn), ij(lambda i, j, kk: (kk, j)))
    in_specs = [a_spec, b_spec]
    args = [a, b]
    if add is not None:
        in_specs.append(pl.BlockSpec((tm, tn), ij(lambda i, j, kk: (i, j))))
        args.append(add)
    if after is not None:
        in_specs.append(pl.BlockSpec((8, BLK), lambda g0, g1, kk: (0, 0)))
        args.append(after)
    return pl.pallas_call(
        body, name=name,
        grid=(n // tn, m // tm, nk) if n_outer else (m // tm, n // tn, nk),
        in_specs=in_specs,
        out_specs=pl.BlockSpec((tm, tn), ij(lambda i, j, kk: (i, j))),
        out_shape=jax.ShapeDtypeStruct((m, n), out_dtype),
        scratch_shapes=[pltpu.VMEM((tm, tn), F32)] if nk > 1 else [],
        compiler_params=_params("parallel", "parallel", "arbitrary"),
    )(*args)


UNITS_PER_ITER = 4


def _band_mask(first):
    qi = lax.broadcasted_iota(jnp.int32, (BLK, BLK if first else 2 * BLK), 0)
    kj = lax.broadcasted_iota(jnp.int32, (BLK, BLK if first else 2 * BLK), 1)
    delta = qi - kj if first else qi + BLK - kj
    return jnp.logical_and(delta >= 0, delta <= BLK)


def _attn_specs(seq, g):
    qkv = [pl.BlockSpec((1, seq, BLK), functools.partial(
        lambda hp, b, part: (b, 0, QKV_OFF // BLK + (g * HEAD_PAIRS + hp) * 3 + part), part=part)) for part in range(3)]
    one = pl.BlockSpec((1, seq, BLK), lambda hp, b: (b, 0, hp))
    bias = pl.BlockSpec((2, BLK, 2 * BLK), lambda hp, b: (hp, 0, 0))
    return qkv, one, bias


def _attn_rows(dil, r, n, first):
    start = r + (dil * BLK) * n
    if dil == 1:
        start = pl.multiple_of(start, BLK)
        rows = pl.ds(start, BLK)
        keys = rows if first else pl.ds(pl.multiple_of(start - BLK, BLK), 2 * BLK)
    else:
        rows = pl.ds(start, BLK, stride=dil)
        keys = rows if first else pl.ds(start - dil * BLK, 2 * BLK, stride=dil)
    return rows, keys


def _attn_schedule(dil, nb, unit):
    def blocks_of(r):
        unit(r, 0, True)
        for n in range(1, UNITS_PER_ITER):
            unit(r, n, False)
        if nb > UNITS_PER_ITER:
            def more(i, carry):
                for jj in range(UNITS_PER_ITER):
                    unit(r, i * UNITS_PER_ITER + jj, False)
                return carry
            lax.fori_loop(1, nb // UNITS_PER_ITER, more, 0)

    if nb >= UNITS_PER_ITER:
        assert nb % UNITS_PER_ITER == 0
        if dil == 1:
            blocks_of(0)
        else:
            def per_residue(r, carry):
                blocks_of(r)
                return carry
            lax.fori_loop(0, dil, per_residue, 0)
    else:
        per_iter = UNITS_PER_ITER // nb
        assert UNITS_PER_ITER % nb == 0 and dil % per_iter == 0

        def residues(i, carry):
            for jj in range(per_iter):
                for n in range(nb):
                    unit(i * per_iter + jj, n, n == 0)
            return carry
        lax.fori_loop(0, dil // per_iter, residues, 0)


def _attn_fwd(hcat3, bias, g, dil, name):
    bsz, seq, _ = hcat3.shape
    nb = seq // dil // BLK
    scale = HEAD_DIM ** -0.5

    def body(q_ref, k_ref, v_ref, bias_ref, o_ref, lse_ref):
        masks = {True: _band_mask(True), False: _band_mask(False)}

        def unit(r, n, first):
            rows, keys = _attn_rows(dil, r, n, first)
            q2 = q_ref[0, rows, :].astype(BF16)
            k2 = k_ref[0, keys, :].astype(BF16)
            v2 = v_ref[0, keys, :].astype(BF16)
            outs, lses = [], []
            for j in range(2):
                lanes = slice(j * HEAD_DIM, (j + 1) * HEAD_DIM)
                bias_j = bias_ref[j, :, BLK:] if first else bias_ref[j]
                s = _nt(q2[:, lanes], k2[:, lanes]) * scale + bias_j
                s = jnp.where(masks[first], s, -jnp.inf)
                mx = jnp.max(s, axis=-1, keepdims=True)
                e = jnp.exp(s - mx)
                den = jnp.sum(e, axis=-1, keepdims=True)
                outs.append(jnp.dot(e.astype(BF16), v2[:, lanes], preferred_element_type=F32) / den)
                lses.append(jnp.broadcast_to(mx + jnp.log(den), (BLK, HEAD_DIM)))
            o_ref[0, rows, :] = jnp.concatenate(outs, axis=1)
            lse_ref[0, rows, :] = jnp.concatenate(lses, axis=1)

        _attn_schedule(dil, nb, unit)

    qkv_specs, one, bias_spec = _attn_specs(seq, g)
    shape = jax.ShapeDtypeStruct((bsz, seq, GROUP_COLS), F32)
    return pl.pallas_call(
        body, name=name,
        grid=(HEAD_PAIRS, bsz),
        in_specs=qkv_specs + [bias_spec],
        out_specs=[one, one],
        out_shape=[shape, shape],
        compiler_params=_params("parallel", "parallel"),
    )(hcat3, hcat3, hcat3, bias)


def _attn_bwd(hcat3, bias, lse, d_out, dd, d_hcat3, g, dil, name):
    bsz, seq, _ = hcat3.shape
    nb = seq // dil // BLK
    scale = HEAD_DIM ** -0.5

    def body(q_ref, k_ref, v_ref, bias_ref, lse_ref, do_ref, dd_ref, _, dqkv_ref, dbias_ref, dq_acc, dk_acc, dv_acc):
        @pl.when(pl.program_id(1) == 0)
        def _():
            dbias_ref[...] = jnp.zeros_like(dbias_ref)

        dk_acc[...] = jnp.zeros_like(dk_acc)
        dv_acc[...] = jnp.zeros_like(dv_acc)
        masks = {True: _band_mask(True), False: _band_mask(False)}

        def unit(r, n, first):
            rows, keys = _attn_rows(dil, r, n, first)
            q2 = q_ref[0, rows, :].astype(BF16)
            k2 = k_ref[0, keys, :].astype(BF16)
            v2 = v_ref[0, keys, :].astype(BF16)
            do2 = do_ref[0, rows, :].astype(BF16)
            lse2 = lse_ref[0, rows, :]
            dd2 = dd_ref[0, rows, :]
            dqs, dks, dvs = [], [], []
            for j in range(2):
                lanes = slice(j * HEAD_DIM, (j + 1) * HEAD_DIM)
                q, kb, vb, do = q2[:, lanes], k2[:, lanes], v2[:, lanes], do2[:, lanes]
                delta = jnp.sum(dd2[:, lanes], axis=-1, keepdims=True)
                bias_j = bias_ref[j, :, BLK:] if first else bias_ref[j]
                s = _nt(q, kb) * scale + bias_j
                p = jnp.where(masks[first], jnp.exp(s - lse2[:, j * HEAD_DIM:j * HEAD_DIM + 1]), 0.0)
                ds = p * (_nt(do, vb) - delta)
                ds16 = ds.astype(BF16)
                dqs.append(jnp.dot(ds16, kb, preferred_element_type=F32) * scale)
                dks.append(_tn(ds16, q) * scale)
                dvs.append(_tn(p.astype(BF16), do))
                if first:
                    dbias_ref[j, :, BLK:] += ds
                else:
                    dbias_ref[j] += ds
            dq_acc[rows, :] = jnp.concatenate(dqs, axis=1)
            dk_acc[keys, :] += jnp.concatenate(dks, axis=1)
            dv_acc[keys, :] += jnp.concatenate(dvs, axis=1)

        _attn_schedule(dil, nb, unit)
        dqkv_ref[0, :, 0:BLK] = dq_acc[...].astype(BF16)
        dqkv_ref[0, :, BLK:2 * BLK] = dk_acc[...].astype(BF16)
        dqkv_ref[0, :, 2 * BLK:3 * BLK] = dv_acc[...].astype(BF16)

    qkv_specs, one, bias_spec = _attn_specs(seq, g)
    return pl.pallas_call(
        body, name=name,
        grid=(HEAD_PAIRS, bsz),
        in_specs=qkv_specs + [bias_spec, one, one, one, pl.BlockSpec(memory_space=pl.ANY)],
        out_specs=[pl.BlockSpec((1, seq, 3 * BLK), lambda hp, b: (b, 0, QKV_OFF // (3 * BLK) + g * HEAD_PAIRS + hp)), bias_spec],
        out_shape=[jax.ShapeDtypeStruct(d_hcat3.shape, d_hcat3.dtype), jax.ShapeDtypeStruct((HEADS_PER_GROUP, BLK, 2 * BLK), F32)],
        input_output_aliases={7: 0},
        scratch_shapes=[pltpu.VMEM((seq, BLK), F32)] * 3,
        compiler_params=_params("arbitrary", "arbitrary"),
    )(hcat3, hcat3, hcat3, bias, lse, d_out, dd, d_hcat3)


def _t5_buckets(dil):
    import numpy as np
    qi = np.arange(BLK)[:, None]
    kj = np.arange(2 * BLK)[None, :]
    dist = np.maximum(qi + BLK - kj, 0) * dil
    max_exact = NUM_BUCKETS // 2
    d_f = np.maximum(dist, 1).astype(np.float32)
    large = max_exact + (np.log(d_f / np.float32(max_exact)) / np.float32(math.log(MAX_DISTANCE / max_exact))
                         * np.float32(NUM_BUCKETS - max_exact)).astype(np.int32)
    large = np.minimum(large, NUM_BUCKETS - 1)
    return np.where(dist < max_exact, dist, large).astype(np.int32).reshape(-1)


def _bias_matrix(table_g, dil):
    buckets = jnp.asarray(_t5_buckets(dil))
    onehot = (buckets[None, :] == lax.broadcasted_iota(jnp.int32, (NUM_BUCKETS, 1), 0)).astype(F32)
    tk = 4096

    def body(t_ref, oh_ref, o_ref):
        o_ref[...] = _dot_hi(t_ref[...], oh_ref[...])

    bias = pl.pallas_call(
        body, name=f"bias_matrix{dil}",
        grid=(onehot.shape[1] // tk,),
        in_specs=[pl.BlockSpec((HEADS_PER_GROUP, NUM_BUCKETS), lambda kk: (0, 0)), pl.BlockSpec((NUM_BUCKETS, tk), lambda kk: (0, kk))],
        out_specs=pl.BlockSpec((HEADS_PER_GROUP, tk), lambda kk: (0, kk)),
        out_shape=jax.ShapeDtypeStruct((HEADS_PER_GROUP, onehot.shape[1]), F32),
        compiler_params=_params("parallel"),
    )(table_g.T, onehot)
    return bias.reshape(HEADS_PER_GROUP, BLK, 2 * BLK), onehot


def _bias_grad(dbias, onehot, name):
    flat = dbias.reshape(HEADS_PER_GROUP, 2 * BLK * BLK)
    tk = 4096

    def body(oh_ref, g_ref, o_ref):
        @pl.when(pl.program_id(0) == 0)
        def _():
            o_ref[...] = jnp.zeros_like(o_ref)
        o_ref[...] += lax.dot_general(oh_ref[...], g_ref[...], (((1,), (1,)), ((), ())),
                                      preferred_element_type=F32, precision=lax.Precision.HIGHEST)

    return pl.pallas_call(
        body, name=name,
        grid=(flat.shape[1] // tk,),
        in_specs=[pl.BlockSpec((NUM_BUCKETS, tk), lambda kk: (0, kk)), pl.BlockSpec((HEADS_PER_GROUP, tk), lambda kk: (0, kk))],
        out_specs=pl.BlockSpec((NUM_BUCKETS, HEADS_PER_GROUP), lambda kk: (0, 0)),
        out_shape=jax.ShapeDtypeStruct((NUM_BUCKETS, HEADS_PER_GROUP), F32),
        compiler_params=_params("arbitrary"),
    )(onehot, flat)


CONV_ROWS = 128
HALO = 8


def _conv_chunks(seq, fn):
    n = seq // CONV_ROWS
    fn(0, True, n == 1)
    if n > 2:
        def step(i, carry):
            fn(pl.multiple_of(i * CONV_ROWS, CONV_ROWS), False, False)
            return carry
        lax.fori_loop(1, n - 1, step, 0, unroll=7)
    if n > 1:
        fn((n - 1) * CONV_ROWS, False, True)


def _load_with_halo(ref, r0, first, last, after):
    lo = 0 if first else HALO
    hi = HALO if (after and not last) else 0
    v = ref[0, pl.ds(r0 - lo, CONV_ROWS + lo + hi), :]
    parts = ([jnp.zeros((HALO, v.shape[1]), v.dtype)] if first else []) + [v]
    if after and last:
        parts.append(jnp.zeros((HALO, v.shape[1]), v.dtype))
    return v if len(parts) == 1 else jnp.concatenate(parts, axis=0)


def _conv_pre(u_ext, n_rows, w_ref, b_ref):
    acc = b_ref[0:1, :] + w_ref[CONV_WIDTH - 1:CONV_WIDTH, :] * u_ext[HALO:HALO + n_rows]
    for s in range(1, CONV_WIDTH):
        acc = acc + w_ref[CONV_WIDTH - 1 - s:CONV_WIDTH - s, :] * u_ext[HALO - s:HALO - s + n_rows]
    return acc


def _conv_fwd(rest3, conv_w, conv_b):
    bsz, seq, _ = rest3.shape

    def body(u_ref, w_ref, b_ref, o_ref):
        def chunk(r0, first, last):
            pre = _conv_pre(_load_with_halo(u_ref, r0, first, last, False), CONV_ROWS, w_ref, b_ref)
            o_ref[0, pl.ds(r0, CONV_ROWS), :] = pre * _sigmoid(pre)
        _conv_chunks(seq, chunk)

    return pl.pallas_call(
        body, name="conv_fwd",
        grid=(bsz, CONV_DIM // BLK),
        in_specs=[pl.BlockSpec((1, seq, BLK), lambda b, c: (b, 0, R_XBC // BLK + c)),
                  pl.BlockSpec((CONV_WIDTH, BLK), lambda b, c: (0, c)),
                  pl.BlockSpec((1, BLK), lambda b, c: (0, c))],
        out_specs=pl.BlockSpec((1, seq, BLK), lambda b, c: (b, 0, c)),
        out_shape=jax.ShapeDtypeStruct((bsz, seq, CONV_DIM), F32),
        compiler_params=_params("parallel", "parallel"),
    )(rest3, conv_w, conv_b)


def _conv_bwd(rest3, conv_w, conv_b, d_act, d_hcat3):
    bsz, seq, _ = rest3.shape

    def body(u_ref, w_ref, b_ref, g_ref, _, du_ref, dw_ref):
        @pl.when(pl.program_id(1) == 0)
        def _():
            dw_ref[...] = jnp.zeros_like(dw_ref)

        def chunk(r0, first, last):
            ext = CONV_ROWS + HALO
            u_ext = _load_with_halo(u_ref, r0, first, last, True)
            g_ext = _load_with_halo(g_ref, r0, True, last, True)[HALO:]
            d_pre = g_ext * _silu_and_grad(_conv_pre(u_ext, ext, w_ref, b_ref))[1]
            d_now = d_pre[:CONV_ROWS]
            du = w_ref[CONV_WIDTH - 1:CONV_WIDTH, :] * d_now
            dw_ref[CONV_WIDTH - 1:CONV_WIDTH, :] += jnp.sum(d_now * u_ext[HALO:HALO + CONV_ROWS], axis=0, keepdims=True)
            for s in range(1, CONV_WIDTH):
                du = du + w_ref[CONV_WIDTH - 1 - s:CONV_WIDTH - s, :] * d_pre[s:s + CONV_ROWS]
                dw_ref[CONV_WIDTH - 1 - s:CONV_WIDTH - s, :] += jnp.sum(d_now * u_ext[HALO - s:HALO - s + CONV_ROWS],
                                                                      axis=0, keepdims=True)
            dw_ref[CONV_WIDTH:CONV_WIDTH + 1, :] += jnp.sum(d_now, axis=0, keepdims=True)
            du_ref[0, pl.ds(r0, CONV_ROWS), :] = du.astype(BF16)
        _conv_chunks(seq, chunk)

    return pl.pallas_call(
        body, name="conv_bwd",
        grid=(CONV_DIM // BLK, bsz),
        in_specs=[pl.BlockSpec((1, seq, BLK), lambda c, b: (b, 0, R_XBC // BLK + c)),
                  pl.BlockSpec((CONV_WIDTH, BLK), lambda c, b: (0, c)),
                  pl.BlockSpec((1, BLK), lambda c, b: (0, c)),
                  pl.BlockSpec((1, seq, BLK), lambda c, b: (b, 0, c)),
                  pl.BlockSpec(memory_space=pl.ANY)],
        out_specs=[pl.BlockSpec((1, seq, BLK), lambda c, b: (b, 0, R_XBC // BLK + c)),
                   pl.BlockSpec((8, BLK), lambda c, b: (0, c))],
        out_shape=[jax.ShapeDtypeStruct(d_hcat3.shape, d_hcat3.dtype), jax.ShapeDtypeStruct((8, CONV_DIM), F32)],
        input_output_aliases={4: 0},
        compiler_params=_params("parallel", "arbitrary"),
    )(rest3, conv_w, conv_b, d_act, d_hcat3)


P_DTB, P_ALOG, P_DSKIP = 0, 1, 2


def _softplus(v):
    return jnp.maximum(v, 0.0) + jnp.log(1.0 + jnp.exp(-jnp.abs(v)))


def _dot_hi(a, b):
    return jnp.dot(a, b, preferred_element_type=F32, precision=lax.Precision.HIGHEST)


def _nt(a, b):
    return lax.dot_general(a, b, (((1,), (1,)), ((), ())), preferred_element_type=F32)


def _tn(a, b):
    return lax.dot_general(a, b, (((0,), (0,)), ((), ())), preferred_element_type=F32)


def _sum_all(v):
    return jnp.sum(jnp.sum(v, axis=0, keepdims=True), axis=1, keepdims=True)


def _ssd_decays(dtr, dtr_t, prow_ref, pcol_ref):
    ri = lax.broadcasted_iota(jnp.int32, (BLK, BLK), 0)
    ci = lax.broadcasted_iota(jnp.int32, (BLK, BLK), 1)
    tri = (ri >= ci).astype(F32)
    tri_u = (ri <= ci).astype(F32)
    pre = dtr + prow_ref[P_DTB:P_DTB + 1, :]
    dt = _softplus(pre)
    ah_row = -jnp.exp(prow_ref[P_ALOG:P_ALOG + 1, :])
    acs = _dot_hi(tri, dt * ah_row)
    pre_t = dtr_t + pcol_ref[:, P_DTB:P_DTB + 1]
    dt_t = _softplus(pre_t)
    ah_col = -jnp.exp(pcol_ref[:, P_ALOG:P_ALOG + 1])
    acs_t = _dot_hi(dt_t * ah_col, tri_u)
    return dict(tri=tri, tri_u=tri_u, pre=pre, dt=dt, ah_row=ah_row, acs=acs, pre_t=pre_t, dt_t=dt_t, ah_col=ah_col,
                acs_t=acs_t, causal=ri >= ci, last_row=ri[:, 0:1] == BLK - 1)


def _ssd_head(h, d, x_ref, s_in, g_mat):
    col = d["acs"][:, h:h + 1]
    row = d["acs_t"][h:h + 1, :]
    lm = jnp.exp(jnp.where(d["causal"], col - row, -jnp.inf))
    m = g_mat * lm
    xh = x_ref[0, :, pl.ds(h * HEAD_DIM, HEAD_DIM)]
    dtc = d["dt"][:, h:h + 1]
    xd = xh * dtc
    e = jnp.exp(col)
    clast = d["acs"][BLK - 1:BLK, h:h + 1]
    f = jnp.exp(clast - col)
    return dict(col=col, lm=lm, m=m, xh=xh, dtc=dtc, xd=xd, e=e, ecl=jnp.exp(clast), f=f, xf=xd * f)


def _ssd_specs(nc, rev):
    cidx = (lambda c: nc - 1 - c) if rev else (lambda c: c)
    x_spec = pl.BlockSpec((1, BLK, D_INNER), lambda b, c: (b, cidx(c), 0))
    bm_spec = pl.BlockSpec((1, BLK, SSM_GROUPS * D_STATE), lambda b, c: (b, cidx(c), D_INNER // (SSM_GROUPS * D_STATE)))
    cm_spec = pl.BlockSpec((1, BLK, SSM_GROUPS * D_STATE), lambda b, c: (b, cidx(c), D_INNER // (SSM_GROUPS * D_STATE) + 1))
    dt_spec = pl.BlockSpec((1, BLK, BLK), lambda b, c: (b, cidx(c), R_DT // BLK))
    dtt_spec = pl.BlockSpec((1, BLK, BLK), lambda b, c: (b, 0, cidx(c)))
    prow_spec = pl.BlockSpec((8, BLK), lambda b, c: (0, 0))
    pcol_spec = pl.BlockSpec((BLK, 8), lambda b, c: (0, 0))
    st_spec = pl.BlockSpec((1, 1, SSM_HEADS, HEAD_DIM, D_STATE), lambda b, c: (b, cidx(c), 0, 0, 0))
    y_spec = pl.BlockSpec((1, BLK, D_INNER), lambda b, c: (b, cidx(c), 0))
    return x_spec, bm_spec, cm_spec, dt_spec, dtt_spec, prow_spec, pcol_spec, st_spec, y_spec


def _ssd_fwd(xa, rest3, dtr_t, prow, pcol):
    bsz, seq, _ = xa.shape
    nc = seq // BLK

    def body(x_ref, bm_ref, cm_ref, dt_ref, dtt_ref, prow_ref, pcol_ref, y_ref, sin_ref, s_ref):
        @pl.when(pl.program_id(1) == 0)
        def _():
            s_ref[...] = jnp.zeros_like(s_ref)

        d = _ssd_decays(dt_ref[0], dtt_ref[0], prow_ref, pcol_ref)
        for g in range(SSM_GROUPS):
            lanes = pl.ds(g * D_STATE, D_STATE)
            bg = bm_ref[0, :, lanes].astype(BF16)
            cg = cm_ref[0, :, lanes].astype(BF16)
            g_mat = _nt(cg, bg)
            for hh in range(HEADS_PER_SSM_GROUP):
                h = g * HEADS_PER_SSM_GROUP + hh
                s_in = s_ref[h]
                sin_ref[0, 0, h] = s_in
                q = _ssd_head(h, d, x_ref, s_in, g_mat)
                y_diag = jnp.dot(q["m"].astype(BF16), q["xd"].astype(BF16), preferred_element_type=F32)
                y_off = _nt(cg, s_in.astype(BF16)) * q["e"]
                s_ref[h] = s_in * q["ecl"] + _tn(q["xf"].astype(BF16), bg)
                y_ref[0, :, pl.ds(h * HEAD_DIM, HEAD_DIM)] = y_diag + y_off + prow_ref[P_DSKIP:P_DSKIP + 1, h:h + 1] * q["xh"]

    x_spec, bm_spec, cm_spec, dt_spec, dtt_spec, prow_spec, pcol_spec, st_spec, y_spec = _ssd_specs(nc, False)
    return pl.pallas_call(
        body, name="ssd_fwd",
        grid=(bsz, nc),
        in_specs=[x_spec, bm_spec, cm_spec, dt_spec, dtt_spec, prow_spec, pcol_spec],
        out_specs=[y_spec, st_spec],
        out_shape=[jax.ShapeDtypeStruct((bsz, seq, D_INNER), F32),
                   jax.ShapeDtypeStruct((bsz, nc, SSM_HEADS, HEAD_DIM, D_STATE), F32)],
        scratch_shapes=[pltpu.VMEM((SSM_HEADS, HEAD_DIM, D_STATE), F32)],
        compiler_params=_params("parallel", "arbitrary"),
    )(xa, xa, xa, rest3, dtr_t, prow, pcol)


def _ssd_bwd(xa, rest3, dtr_t, prow, pcol, s_in_all, dy):
    bsz, seq, _ = xa.shape
    nc = seq // BLK

    def body(x_ref, bm_ref, cm_ref, dt_ref, dtt_ref, prow_ref, pcol_ref, sin_ref, dy_ref,
             dxa_ref, ddt_ref, ddtt_ref, gprow_ref, gpcol_ref, ds_ref, dc_ref, ddtc_ref, drt_ref):
        first = jnp.logical_and(pl.program_id(0) == 0, pl.program_id(1) == 0)

        @pl.when(first)
        def _():
            gprow_ref[...] = jnp.zeros_like(gprow_ref)
            gpcol_ref[...] = jnp.zeros_like(gpcol_ref)

        @pl.when(pl.program_id(1) == 0)
        def _():
            ds_ref[...] = jnp.zeros_like(ds_ref)

        dc_ref[...] = jnp.zeros_like(dc_ref)
        ddtc_ref[...] = jnp.zeros_like(ddtc_ref)
        drt_ref[...] = jnp.zeros_like(drt_ref)
        d = _ssd_decays(dt_ref[0], dtt_ref[0], prow_ref, pcol_ref)
        for g in range(SSM_GROUPS):
            lanes = pl.ds(g * D_STATE, D_STATE)
            bg = bm_ref[0, :, lanes].astype(BF16)
            cg = cm_ref[0, :, lanes].astype(BF16)
            g_mat = _nt(cg, bg)
            d_g = jnp.zeros((BLK, BLK), F32)
            d_bg = jnp.zeros((BLK, D_STATE), F32)
            d_cg = jnp.zeros((BLK, D_STATE), F32)
            for hh in range(HEADS_PER_SSM_GROUP):
                h = g * HEADS_PER_SSM_GROUP + hh
                head_lanes = pl.ds(h * HEAD_DIM, HEAD_DIM)
                s_in = sin_ref[0, 0, h]
                s_in16 = s_in.astype(BF16)
                q = _ssd_head(h, d, x_ref, s_in, g_mat)
                m16, xd16 = q["m"].astype(BF16), q["xd"].astype(BF16)
                d_y = dy_ref[0, :, head_lanes]
                d_y16 = d_y.astype(BF16)
                d_so = ds_ref[h]
                d_so16 = d_so.astype(BF16)
                d_x = prow_ref[P_DSKIP:P_DSKIP + 1, h:h + 1] * d_y
                gprow_ref[P_DSKIP:P_DSKIP + 1, h:h + 1] += _sum_all(d_y * q["xh"])
                d_m = _nt(d_y16, xd16)
                d_xd = _tn(m16, d_y16)
                w = d_m * q["m"]
                d_g = d_g + d_m * q["lm"]
                d_col = jnp.sum(w, axis=1, keepdims=True)
                drt_ref[h:h + 1, :] = -jnp.sum(w, axis=0, keepdims=True)
                qmat = _nt(cg, s_in16)
                d_q16 = (d_y * q["e"]).astype(BF16)
                d_col = d_col + jnp.sum(d_y * qmat, axis=1, keepdims=True) * q["e"]
                d_cg = d_cg + jnp.dot(d_q16, s_in16, preferred_element_type=F32)
                d_sin = _tn(d_q16, cg) + d_so * q["ecl"]
                d_clast = _sum_all(d_so * s_in) * q["ecl"]
                d_xf = _nt(bg, d_so16)
                d_bg = d_bg + jnp.dot(q["xf"].astype(BF16), d_so16, preferred_element_type=F32)
                d_xd = d_xd + d_xf * q["f"]
                d_f = jnp.sum(d_xf * q["xd"], axis=1, keepdims=True) * q["f"]
                d_clast = d_clast + jnp.sum(d_f, axis=0, keepdims=True)
                d_col = d_col - d_f + jnp.where(d["last_row"], d_clast, 0.0)
                dxa_ref[0, :, head_lanes] = d_x + d_xd * q["dtc"]
                dc_ref[:, h:h + 1] = d_col
                ddtc_ref[:, h:h + 1] = jnp.sum(d_xd * q["xh"], axis=1, keepdims=True)
                ds_ref[h] = d_sin
            d_g16 = d_g.astype(BF16)
            dxa_ref[0, :, pl.ds(D_INNER + g * D_STATE, D_STATE)] = d_bg + _tn(d_g16, cg)
            dxa_ref[0, :, pl.ds(D_INNER + (SSM_GROUPS + g) * D_STATE, D_STATE)] = d_cg + jnp.dot(d_g16, bg, preferred_element_type=F32)
        d_a = _dot_hi(d["tri_u"], dc_ref[...])
        d_pre = (ddtc_ref[...] + d_a * d["ah_row"]) * _sigmoid(d["pre"])
        ddt_ref[0] = d_pre
        gprow_ref[P_DTB:P_DTB + 1, :] += jnp.sum(d_pre, axis=0, keepdims=True)
        gprow_ref[P_ALOG:P_ALOG + 1, :] += jnp.sum(d_a * d["dt"], axis=0, keepdims=True) * d["ah_row"]
        d_at = _dot_hi(drt_ref[...], d["tri"])
        d_pre_t = d_at * d["ah_col"] * _sigmoid(d["pre_t"])
        ddtt_ref[0] = d_pre_t
        gpcol_ref[:, P_DTB:P_DTB + 1] += jnp.sum(d_pre_t, axis=1, keepdims=True)
        gpcol_ref[:, P_ALOG:P_ALOG + 1] += jnp.sum(d_at * d["dt_t"], axis=1, keepdims=True) * d["ah_col"]

    x_spec, bm_spec, cm_spec, dt_spec, dtt_spec, prow_spec, pcol_spec, st_spec, y_spec = _ssd_specs(nc, True)
    return pl.pallas_call(
        body, name="ssd_bwd",
        grid=(bsz, nc),
        in_specs=[x_spec, bm_spec, cm_spec, dt_spec, dtt_spec, prow_spec, pcol_spec, st_spec, y_spec],
        out_specs=[pl.BlockSpec((1, BLK, CONV_DIM), lambda b, c: (b, nc - 1 - c, 0)),
                   pl.BlockSpec((1, BLK, BLK), lambda b, c: (b, nc - 1 - c, 0)),
                   pl.BlockSpec((1, BLK, BLK), lambda b, c: (b, 0, nc - 1 - c)),
                   prow_spec, pcol_spec],
        out_shape=[jax.ShapeDtypeStruct((bsz, seq, CONV_DIM), F32),
                   jax.ShapeDtypeStruct((bsz, seq, BLK), F32),
                   jax.ShapeDtypeStruct((bsz, BLK, seq), F32),
                   jax.ShapeDtypeStruct((8, BLK), F32),
                   jax.ShapeDtypeStruct((BLK, 8), F32)],
        scratch_shapes=[pltpu.VMEM((SSM_HEADS, HEAD_DIM, D_STATE), F32), pltpu.VMEM((BLK, BLK), F32),
                        pltpu.VMEM((BLK, BLK), F32), pltpu.VMEM((BLK, BLK), F32)],
        compiler_params=_params("arbitrary", "arbitrary"),
    )(xa, xa, xa, rest3, dtr_t, prow, pcol, s_in_all, dy)


GROUP_W = HEADS_PER_SSM_GROUP * HEAD_DIM


def _select_matrix(shape, g, head_axis, per_head):
    h = lax.broadcasted_iota(jnp.int32, shape, head_axis)
    j = lax.broadcasted_iota(jnp.int32, shape, 1 - head_axis)
    return (h == g * HEADS_PER_SSM_GROUP + lax.shift_right_logical(j, per_head.bit_length() - 1)).astype(BF16)


def _split16(v, terms):
    parts, rem = [], v
    for _ in range(terms):
        p = rem.astype(BF16)
        parts.append(p)
        rem = rem - p.astype(F32)
    return parts


def _sel_dot(a, b, terms=2):
    if a.dtype == BF16:
        return sum(jnp.dot(a, p, preferred_element_type=F32) for p in _split16(b, terms))
    return sum(jnp.dot(p, b, preferred_element_type=F32) for p in _split16(a, terms))


def _ssd_group(g, d, e_all, f_all, ecl_b, x_ref, prow_ref):
    spread = _select_matrix((BLK, GROUP_W), g, 0, HEAD_DIM)
    gather = _select_matrix((GROUP_W, BLK), g, 1, HEAD_DIM)
    xg = x_ref[0, :, pl.ds(g * GROUP_W, GROUP_W)]
    dt_g = _sel_dot(d["dt"], spread)
    e_g = _sel_dot(e_all, spread)
    f_g = _sel_dot(f_all, spread)
    dsk_g = _sel_dot(prow_ref[...], spread, 3)[P_DSKIP:P_DSKIP + 1, :]
    sc_g = _sel_dot(gather, ecl_b, 3)
    xd = xg * dt_g
    return dict(spread=spread, gather=gather, xg=xg, dt_g=dt_g, e_g=e_g, f_g=f_g, dsk_g=dsk_g, sc_g=sc_g, xd=xd,
                xd16=xd.astype(BF16), xf16=(xd * f_g).astype(BF16))


def _ssd_common(d):
    e_all = jnp.exp(d["acs"])
    f_all = jnp.exp(d["acs"][BLK - 1:BLK, :] - d["acs"])
    ecl_b = jnp.broadcast_to(jnp.exp(d["acs_t"][:, BLK - 1:BLK]), (BLK, BLK))
    return e_all, f_all, ecl_b


def _ssd_mask_decay(d, h, g_mat):
    col = d["acs"][:, h:h + 1]
    row = d["acs_t"][h:h + 1, :]
    lm = jnp.exp(jnp.where(d["causal"], col - row, -jnp.inf))
    return lm, g_mat * lm


def _ssd_state_spec(nc, rev):
    cidx = (lambda c: nc - 1 - c) if rev else (lambda c: c)
    return pl.BlockSpec((1, 1, SSM_GROUPS, GROUP_W, D_STATE), lambda b, c: (b, cidx(c), 0, 0, 0))


def _ssd_fwd(xa, rest3, dtr_t, prow, pcol):
    bsz, seq, _ = xa.shape
    nc = seq // BLK

    def body(x_ref, bm_ref, cm_ref, dt_ref, dtt_ref, prow_ref, pcol_ref, y_ref, sin_ref, s_ref):
        @pl.when(pl.program_id(1) == 0)
        def _():
            s_ref[...] = jnp.zeros_like(s_ref)

        d = _ssd_decays(dt_ref[0], dtt_ref[0], prow_ref, pcol_ref)
        e_all, f_all, ecl_b = _ssd_common(d)
        for g in range(SSM_GROUPS):
            lanes = pl.ds(g * D_STATE, D_STATE)
            bg = bm_ref[0, :, lanes].astype(BF16)
            cg = cm_ref[0, :, lanes].astype(BF16)
            g_mat = _nt(cg, bg)
            q = _ssd_group(g, d, e_all, f_all, ecl_b, x_ref, prow_ref)
            s_in = s_ref[g]
            sin_ref[0, 0, g] = s_in
            y_diag = []
            for j in range(HEADS_PER_SSM_GROUP):
                _, m = _ssd_mask_decay(d, g * HEADS_PER_SSM_GROUP + j, g_mat)
                y_diag.append(jnp.dot(m.astype(BF16), q["xd16"][:, j * HEAD_DIM:(j + 1) * HEAD_DIM], preferred_element_type=F32))
            y_off = _nt(cg, s_in.astype(BF16)) * q["e_g"]
            y_ref[0, :, pl.ds(g * GROUP_W, GROUP_W)] = jnp.concatenate(y_diag, axis=1) + y_off + q["dsk_g"] * q["xg"]
            s_ref[g] = s_in * q["sc_g"] + _tn(q["xf16"], bg)

    x_spec, bm_spec, cm_spec, dt_spec, dtt_spec, prow_spec, pcol_spec, _, y_spec = _ssd_specs(nc, False)
    return pl.pallas_call(
        body, name="ssd_fwd",
        grid=(bsz, nc),
        in_specs=[x_spec, bm_spec, cm_spec, dt_spec, dtt_spec, prow_spec, pcol_spec],
        out_specs=[y_spec, _ssd_state_spec(nc, False)],
        out_shape=[jax.ShapeDtypeStruct((bsz, seq, D_INNER), F32),
                   jax.ShapeDtypeStruct((bsz, nc, SSM_GROUPS, GROUP_W, D_STATE), F32)],
        scratch_shapes=[pltpu.VMEM((SSM_GROUPS, GROUP_W, D_STATE), F32)],
        compiler_params=_params("parallel", "arbitrary"),
    )(xa, xa, xa, rest3, dtr_t, prow, pcol)


def _ssd_bwd(xa, rest3, dtr_t, prow, pcol, s_in_all, dy):
    bsz, seq, _ = xa.shape
    nc = seq // BLK

    def body(x_ref, bm_ref, cm_ref, dt_ref, dtt_ref, prow_ref, pcol_ref, sin_ref, dy_ref,
             dxa_ref, ddt_ref, ddtt_ref, gprow_ref, gpcol_ref, ds_ref, drt_ref):
        first = jnp.logical_and(pl.program_id(0) == 0, pl.program_id(1) == 0)

        @pl.when(first)
        def _():
            gprow_ref[...] = jnp.zeros_like(gprow_ref)
            gpcol_ref[...] = jnp.zeros_like(gpcol_ref)

        @pl.when(pl.program_id(1) == 0)
        def _():
            ds_ref[...] = jnp.zeros_like(ds_ref)

        drt_ref[...] = jnp.zeros_like(drt_ref)
        d = _ssd_decays(dt_ref[0], dtt_ref[0], prow_ref, pcol_ref)
        e_all, f_all, ecl_b = _ssd_common(d)
        d_c = jnp.zeros((BLK, BLK), F32)
        d_dtc = jnp.zeros((BLK, BLK), F32)
        d_clast_col = jnp.zeros((BLK, 1), F32)
        skip_rows = lax.broadcasted_iota(jnp.int32, (8, GROUP_W), 0) == P_DSKIP
        for g in range(SSM_GROUPS):
            lanes = pl.ds(g * D_STATE, D_STATE)
            bg = bm_ref[0, :, lanes].astype(BF16)
            cg = cm_ref[0, :, lanes].astype(BF16)
            g_mat = _nt(cg, bg)
            q = _ssd_group(g, d, e_all, f_all, ecl_b, x_ref, prow_ref)
            s_in = sin_ref[0, 0, g]
            s_in16 = s_in.astype(BF16)
            d_y = dy_ref[0, :, pl.ds(g * GROUP_W, GROUP_W)]
            d_y16 = d_y.astype(BF16)
            d_so = ds_ref[g]
            d_so16 = d_so.astype(BF16)
            d_g = jnp.zeros((BLK, BLK), F32)
            ws, d_xds = [], []
            for j in range(HEADS_PER_SSM_GROUP):
                h = g * HEADS_PER_SSM_GROUP + j
                head = slice(j * HEAD_DIM, (j + 1) * HEAD_DIM)
                lm, m = _ssd_mask_decay(d, h, g_mat)
                d_m = _nt(d_y16[:, head], q["xd16"][:, head])
                d_xds.append(_tn(m.astype(BF16), d_y16[:, head]))
                w = d_m * m
                d_g = d_g + d_m * lm
                drt_ref[h:h + 1, :] = -jnp.sum(w, axis=0, keepdims=True)
                ws.append(w)
            d_c = d_c + _sel_dot(jnp.concatenate(ws, axis=1), _select_matrix((HEADS_PER_SSM_GROUP * BLK, BLK), g, 1, BLK))
            d_xd = jnp.concatenate(d_xds, axis=1)
            d_g16 = d_g.astype(BF16)
            qmat = _nt(cg, s_in16)
            d_q16 = (d_y * q["e_g"]).astype(BF16)
            d_cg = jnp.dot(d_q16, s_in16, preferred_element_type=F32) + jnp.dot(d_g16, bg, preferred_element_type=F32)
            d_sin = _tn(d_q16, cg) + d_so * q["sc_g"]
            d_clast_col = d_clast_col + jnp.sum(_sel_dot(q["spread"], d_so * s_in * q["sc_g"]), axis=1, keepdims=True)
            d_xf = _nt(bg, d_so16)
            d_bg = jnp.dot(q["xf16"], d_so16, preferred_element_type=F32) + _tn(d_g16, cg)
            d_xd = d_xd + d_xf * q["f_g"]
            r_e = _sel_dot(d_y * qmat * q["e_g"], q["gather"])
            r_f = _sel_dot(d_xf * q["xd"] * q["f_g"], q["gather"])
            d_c = d_c + r_e - r_f + jnp.where(d["last_row"], jnp.sum(r_f, axis=0, keepdims=True), 0.0)
            d_dtc = d_dtc + _sel_dot(d_xd * q["xg"], q["gather"])
            skip_sum = jnp.where(skip_rows, jnp.sum(d_y * q["xg"], axis=0, keepdims=True), 0.0)
            gprow_ref[...] += _sel_dot(skip_sum, q["gather"])
            dxa_ref[0, :, pl.ds(g * GROUP_W, GROUP_W)] = q["dsk_g"] * d_y + d_xd * q["dt_g"]
            dxa_ref[0, :, pl.ds(D_INNER + g * D_STATE, D_STATE)] = d_bg
            dxa_ref[0, :, pl.ds(D_INNER + (SSM_GROUPS + g) * D_STATE, D_STATE)] = d_cg
            ds_ref[g] = d_sin
        drt_ref[:, BLK - 1:BLK] += d_clast_col
        d_a = _dot_hi(d["tri_u"], d_c)
        d_pre = (d_dtc + d_a * d["ah_row"]) * _sigmoid(d["pre"])
        ddt_ref[0] = d_pre
        gprow_ref[P_DTB:P_DTB + 1, :] += jnp.sum(d_pre, axis=0, keepdims=True)
        gprow_ref[P_ALOG:P_ALOG + 1, :] += jnp.sum(d_a * d["dt"], axis=0, keepdims=True) * d["ah_row"]
        d_at = _dot_hi(drt_ref[...], d["tri"])
        d_pre_t = d_at * d["ah_col"] * _sigmoid(d["pre_t"])
        ddtt_ref[0] = d_pre_t
        gpcol_ref[:, P_DTB:P_DTB + 1] += jnp.sum(d_pre_t, axis=1, keepdims=True)
        gpcol_ref[:, P_ALOG:P_ALOG + 1] += jnp.sum(d_at * d["dt_t"], axis=1, keepdims=True) * d["ah_col"]

    x_spec, bm_spec, cm_spec, dt_spec, dtt_spec, prow_spec, pcol_spec, _, y_spec = _ssd_specs(nc, True)
    return pl.pallas_call(
        body, name="ssd_bwd",
        grid=(bsz, nc),
        in_specs=[x_spec, bm_spec, cm_spec, dt_spec, dtt_spec, prow_spec, pcol_spec, _ssd_state_spec(nc, True), y_spec],
        out_specs=[pl.BlockSpec((1, BLK, CONV_DIM), lambda b, c: (b, nc - 1 - c, 0)),
                   pl.BlockSpec((1, BLK, BLK), lambda b, c: (b, nc - 1 - c, 0)),
                   pl.BlockSpec((1, BLK, BLK), lambda b, c: (b, 0, nc - 1 - c)),
                   prow_spec, pcol_spec],
        out_shape=[jax.ShapeDtypeStruct((bsz, seq, CONV_DIM), F32),
                   jax.ShapeDtypeStruct((bsz, seq, BLK), F32),
                   jax.ShapeDtypeStruct((bsz, BLK, seq), F32),
                   jax.ShapeDtypeStruct((8, BLK), F32),
                   jax.ShapeDtypeStruct((BLK, 8), F32)],
        scratch_shapes=[pltpu.VMEM((SSM_GROUPS, GROUP_W, D_STATE), F32), pltpu.VMEM((BLK, BLK), F32)],
        compiler_params=_params("arbitrary", "arbitrary"),
    )(xa, xa, xa, rest3, dtr_t, prow, pcol, s_in_all, dy)


ROW_TILE = 256
CMB_COLS = 256
RMS_COLS = D_INNER // SSM_GROUPS


ROW_CHUNK = 32
WIDE_ROW_CHUNK = 16


def _row_chunks(fn, chunk=ROW_CHUNK, n_rows=ROW_TILE):
    def step(i, carry):
        fn(pl.ds(pl.multiple_of(i * chunk, chunk), chunk))
        return carry
    lax.fori_loop(0, n_rows // chunk, step, 0, unroll=8)


def _combine_weights(l_refs, rows):
    ls = [r[rows, :] for r in l_refs]
    mx = jnp.maximum(jnp.maximum(ls[0], ls[1]), ls[2])
    es = [jnp.exp(l - mx) for l in ls]
    inv = 1.0 / (es[0] + es[1] + es[2])
    return [e * inv for e in es]


CMB_ROWS = 1024


def _combine_specs():
    a = pl.BlockSpec((CMB_ROWS, CMB_COLS), lambda i, j: (i, j))
    gatt = pl.BlockSpec((CMB_ROWS, CMB_COLS), lambda i, j: (i, R_GATT // CMB_COLS + j))
    return a, gatt


def _combine_fwd(outs, lses, rest):
    t = rest.shape[0]

    def body(o0, o1, o2, l0, l1, l2, ga_ref, oa_ref):
        def chunk(rows):
            ws = _combine_weights((l0, l1, l2), rows)
            o = ws[0] * o0[rows, :] + ws[1] * o1[rows, :] + ws[2] * o2[rows, :]
            oa_ref[rows, :] = (o * _silu_and_grad(ga_ref[rows, :])[0]).astype(BF16)
        _row_chunks(chunk, ROW_CHUNK, CMB_ROWS)

    a, gatt = _combine_specs()
    return pl.pallas_call(
        body, name="combine_fwd",
        grid=(t // CMB_ROWS, GROUP_COLS // CMB_COLS),
        in_specs=[a] * 6 + [gatt],
        out_specs=a,
        out_shape=jax.ShapeDtypeStruct((t, GROUP_COLS), BF16),
        compiler_params=_params("parallel", "parallel"),
    )(*outs, *lses, rest)


def _combine_bwd(outs, lses, rest, d_oa, d_hcat):
    t = rest.shape[0]

    def body(o0, o1, o2, l0, l1, l2, ga_ref, doa_ref, _, do0, do1, do2, dd0, dd1, dd2, dga_ref):
        def chunk(rows):
            ws = _combine_weights((l0, l1, l2), rows)
            o = ws[0] * o0[rows, :] + ws[1] * o1[rows, :] + ws[2] * o2[rows, :]
            sg, dsg = _silu_and_grad(ga_ref[rows, :])
            d_oa_v = doa_ref[rows, :]
            d_o = d_oa_v * sg
            dga_ref[rows, :] = (d_oa_v * o * dsg).astype(BF16)
            for w, do_ref, dd_ref in zip(ws, (do0, do1, do2), (dd0, dd1, dd2)):
                d_out = w * d_o
                do_ref[rows, :] = d_out
                dd_ref[rows, :] = d_out * o
        _row_chunks(chunk, ROW_CHUNK, CMB_ROWS)

    a, gatt = _combine_specs()
    s32 = jax.ShapeDtypeStruct((t, GROUP_COLS), F32)
    return pl.pallas_call(
        body, name="combine_bwd",
        grid=(t // CMB_ROWS, GROUP_COLS // CMB_COLS),
        in_specs=[a] * 6 + [gatt, a, pl.BlockSpec(memory_space=pl.ANY)],
        out_specs=[a] * 6 + [gatt],
        out_shape=[s32, s32, s32, s32, s32, s32, jax.ShapeDtypeStruct(d_hcat.shape, d_hcat.dtype)],
        input_output_aliases={8: 6},
        compiler_params=_params("parallel", "parallel"),
    )(*outs, *lses, rest, d_oa, d_hcat)


def _gatenorm_fwd(y, rest, norm_w):
    t = rest.shape[0]

    def body(y_ref, z_ref, w_ref, o_ref):
        def chunk(rows):
            u = y_ref[rows, :] * _silu_and_grad(z_ref[rows, :])[0]
            rs = lax.rsqrt(jnp.mean(u * u, axis=-1, keepdims=True) + RMS_EPS)
            o_ref[rows, :] = (u * rs * w_ref[...]).astype(BF16)
        _row_chunks(chunk, WIDE_ROW_CHUNK)

    return pl.pallas_call(
        body, name="gatenorm_fwd",
        grid=(t // ROW_TILE, SSM_GROUPS),
        in_specs=[pl.BlockSpec((ROW_TILE, RMS_COLS), lambda i, j: (i, j)),
                  pl.BlockSpec((ROW_TILE, RMS_COLS), lambda i, j: (i, R_Z // RMS_COLS + j)),
                  pl.BlockSpec((1, RMS_COLS), lambda i, j: (0, j))],
        out_specs=pl.BlockSpec((ROW_TILE, RMS_COLS), lambda i, j: (i, j)),
        out_shape=jax.ShapeDtypeStruct((t, D_INNER), BF16),
        compiler_params=_params("parallel", "parallel"),
    )(y, rest, norm_w)


def _gatenorm_bwd(y, rest, norm_w, d_ys, d_hcat):
    t = rest.shape[0]

    def body(y_ref, z_ref, w_ref, g_ref, _, dy_ref, dz_ref, dw_ref):
        @pl.when(pl.program_id(1) == 0)
        def _():
            dw_ref[...] = jnp.zeros_like(dw_ref)

        def chunk(rows):
            yv = y_ref[rows, :]
            sz, dsz = _silu_and_grad(z_ref[rows, :])
            u = yv * sz
            rs = lax.rsqrt(jnp.mean(u * u, axis=-1, keepdims=True) + RMS_EPS)
            un = u * rs
            g = g_ref[rows, :]
            dw_ref[0:1, :] += jnp.sum(g * un, axis=0, keepdims=True)
            d_un = g * w_ref[...]
            d_u = rs * (d_un - un * jnp.mean(d_un * un, axis=-1, keepdims=True))
            dy_ref[rows, :] = d_u * sz
            dz_ref[rows, :] = (d_u * yv * dsz).astype(BF16)
        _row_chunks(chunk, WIDE_ROW_CHUNK)

    blk = pl.BlockSpec((ROW_TILE, RMS_COLS), lambda j, i: (i, j))
    z_blk = pl.BlockSpec((ROW_TILE, RMS_COLS), lambda j, i: (i, R_Z // RMS_COLS + j))
    return pl.pallas_call(
        body, name="gatenorm_bwd",
        grid=(SSM_GROUPS, t // ROW_TILE),
        in_specs=[blk, z_blk, pl.BlockSpec((1, RMS_COLS), lambda j, i: (0, j)), blk, pl.BlockSpec(memory_space=pl.ANY)],
        out_specs=[blk, z_blk, pl.BlockSpec((8, RMS_COLS), lambda j, i: (0, j))],
        out_shape=[jax.ShapeDtypeStruct((t, D_INNER), F32), jax.ShapeDtypeStruct(d_hcat.shape, d_hcat.dtype),
                   jax.ShapeDtypeStruct((8, D_INNER), F32)],
        input_output_aliases={4: 1},
        compiler_params=_params("parallel", "arbitrary"),
    )(y, rest, norm_w, d_ys, d_hcat)


def _row_specs():
    full = pl.BlockSpec((ROW_TILE, D_MODEL), lambda i: (i, 0))
    vec = pl.BlockSpec((8, D_MODEL), lambda i: (0, 0))
    at = lambda off: pl.BlockSpec((ROW_TILE, D_MODEL), lambda i: (i, off // D_MODEL))
    return full, vec, at


def _merge_fwd(y_a, y_b, rest, b_gate):
    t = rest.shape[0]

    def body(ya_ref, yb_ref, ga_ref, gb_ref, bg_ref, o_ref):
        def chunk(rows):
            sa = _sigmoid(ga_ref[rows, :] + bg_ref[0:1, :])
            sb = _sigmoid(gb_ref[rows, :] + bg_ref[1:2, :])
            o_ref[rows, :] = (sa * ya_ref[rows, :] + sb * yb_ref[rows, :]).astype(BF16)
        _row_chunks(chunk, WIDE_ROW_CHUNK)

    full, vec, at = _row_specs()
    return pl.pallas_call(
        body, name="merge_fwd",
        grid=(t // ROW_TILE,),
        in_specs=[full, full, at(R_GM), at(R_GM + D_MODEL), vec],
        out_specs=full,
        out_shape=jax.ShapeDtypeStruct((t, D_MODEL), BF16),
        compiler_params=_params("parallel"),
    )(y_a, y_b, rest, rest, b_gate)


def _merge_bwd(y_a, y_b, rest, b_gate, d_merged, d_hcat):
    t = rest.shape[0]

    def body(ya_ref, yb_ref, ga_ref, gb_ref, bg_ref, dm_ref, _, dya_ref, dyb_ref, dg_ref, dbg_ref):
        @pl.when(pl.program_id(0) == 0)
        def _():
            dbg_ref[...] = jnp.zeros_like(dbg_ref)

        def chunk(rows):
            dm = dm_ref[rows, :]
            for row, y_ref, g_ref, dy_ref in ((0, ya_ref, ga_ref, dya_ref), (1, yb_ref, gb_ref, dyb_ref)):
                s = _sigmoid(g_ref[rows, :] + bg_ref[row:row + 1, :])
                dy_ref[rows, :] = (dm * s).astype(BF16)
                dg = dm * y_ref[rows, :] * s * (1.0 - s)
                dg_ref[rows, row * D_MODEL:(row + 1) * D_MODEL] = dg.astype(BF16)
                dbg_ref[row:row + 1, :] += jnp.sum(dg, axis=0, keepdims=True)
        _row_chunks(chunk, WIDE_ROW_CHUNK)

    full, vec, at = _row_specs()
    s16 = jax.ShapeDtypeStruct((t, D_MODEL), BF16)
    return pl.pallas_call(
        body, name="merge_bwd",
        grid=(t // ROW_TILE,),
        in_specs=[full, full, at(R_GM), at(R_GM + D_MODEL), vec, full, pl.BlockSpec(memory_space=pl.ANY)],
        out_specs=[full, full, pl.BlockSpec((ROW_TILE, 2 * D_MODEL), lambda i: (i, R_GM // (2 * D_MODEL))), vec],
        out_shape=[s16, s16, jax.ShapeDtypeStruct(d_hcat.shape, d_hcat.dtype), jax.ShapeDtypeStruct((8, D_MODEL), F32)],
        input_output_aliases={6: 2},
        compiler_params=_params("arbitrary"),
    )(y_a, y_b, rest, rest, b_gate, d_merged, d_hcat)


ST_LNG, ST_LNB, ST_BG2, ST_LOSS = 0, 1, 2, 3


def _final(x, mix, pw, rest, b_gate, ln_gb, target):
    t = rest.shape[0]

    def body(x_ref, mix_ref, pw_ref, gp_ref, bg_ref, ln_ref, tgt_ref, dpre_ref, dpre16_ref, dgp_ref, dpw_ref, st_ref):
        @pl.when(pl.program_id(0) == 0)
        def _():
            st_ref[...] = jnp.zeros_like(st_ref)

        def chunk(rows):
            sp = _sigmoid(gp_ref[rows, :] + bg_ref[2:3, :])
            pw = pw_ref[rows, :]
            pre = ALPHA * x_ref[rows, :] + mix_ref[rows, :] + sp * pw
            xc = pre - jnp.mean(pre, axis=-1, keepdims=True)
            rstd = lax.rsqrt(jnp.mean(xc * xc, axis=-1, keepdims=True) + LN_EPS)
            xhat = xc * rstd
            gain = ln_ref[0:1, :]
            err = xhat * gain + ln_ref[1:2, :] - tgt_ref[rows, :]
            d_yo = err * (1.0 / D_MODEL)
            d_xhat = d_yo * gain
            d_pre = rstd * (d_xhat - jnp.mean(d_xhat, axis=-1, keepdims=True)
                            - xhat * jnp.mean(d_xhat * xhat, axis=-1, keepdims=True))
            dpre_ref[rows, :] = d_pre
            dpre16_ref[rows, :] = d_pre.astype(BF16)
            dgp = d_pre * pw * sp * (1.0 - sp)
            dgp_ref[rows, :] = dgp.astype(BF16)
            dpw_ref[rows, :] = (d_pre * sp).astype(BF16)
            st_ref[ST_LNG:ST_LNG + 1, :] += jnp.sum(d_yo * xhat, axis=0, keepdims=True)
            st_ref[ST_LNB:ST_LNB + 1, :] += jnp.sum(d_yo, axis=0, keepdims=True)
            st_ref[ST_BG2:ST_BG2 + 1, :] += jnp.sum(dgp, axis=0, keepdims=True)
            st_ref[ST_LOSS:ST_LOSS + 1, :] += jnp.sum(err * err, axis=0, keepdims=True) * (0.5 / D_MODEL)
        _row_chunks(chunk, WIDE_ROW_CHUNK)

    full, vec, at = _row_specs()
    s16 = jax.ShapeDtypeStruct((t, D_MODEL), BF16)
    return pl.pallas_call(
        body, name="final",
        grid=(t // ROW_TILE,),
        in_specs=[full, full, full, at(R_GPLE), vec, vec, full],
        out_specs=[full, full, at(R_GPLE), full, vec],
        out_shape=[jax.ShapeDtypeStruct((t, D_MODEL), F32), s16, jax.ShapeDtypeStruct((t, HCAT_COLS), BF16), s16,
                   jax.ShapeDtypeStruct((8, D_MODEL), F32)],
        compiler_params=_params("arbitrary"),
    )(x, mix, pw, rest, b_gate, ln_gb, target)


def _mesh_position():
    return lax.axis_index("x"), lax.axis_index("y"), lax.axis_index("c")


def _flip(pos, k):
    x, y, c = pos
    return ((1 - x) if k & 4 else x, (1 - y) if k & 2 else y, (1 - c) if k & 1 else c)


def _linear(pos):
    return 4 * pos[0] + 2 * pos[1] + pos[2]


def _exchange(arrays, scatter, name):
    n = len(arrays)

    def body(*refs):
        ins, outs = refs[:n], refs[n:2 * n]
        send_sems, recv_sems, local_sems = refs[2 * n:]
        me = _mesh_position()
        me_i = _linear(me)

        def src_for(i, dest_i):
            return ins[i].at[dest_i] if scatter[i] else ins[i]

        local = [pltpu.make_async_copy(src_for(i, me_i), outs[i].at[me_i], local_sems.at[i]) for i in range(n)]
        for cp in local:
            cp.start()
        started = []
        for k in range(1, N_DEV):
            peer = _flip(me, k)
            peer_i = _linear(peer)
            for i in range(n):
                sem = i * (N_DEV - 1) + k - 1
                cp = pltpu.make_async_remote_copy(
                    src_ref=src_for(i, peer_i), dst_ref=outs[i].at[me_i], send_sem=send_sems.at[sem],
                    recv_sem=recv_sems.at[sem], device_id=peer, device_id_type=pl.DeviceIdType.MESH)
                cp.start()
                started.append(cp)
        for k in range(1, N_DEV):
            peer = _flip(me, k)
            peer_i = _linear(peer)
            for i in range(n):
                sem = i * (N_DEV - 1) + k - 1
                pltpu.make_async_remote_copy(
                    src_ref=src_for(i, peer_i), dst_ref=outs[i].at[peer_i], send_sem=send_sems.at[sem],
                    recv_sem=recv_sems.at[sem], device_id=peer, device_id_type=pl.DeviceIdType.MESH).wait_recv()
        for cp in started:
            cp.wait_send()
        for cp in local:
            cp.wait()

    any_spec = pl.BlockSpec(memory_space=pl.ANY)
    out_shape = [jax.ShapeDtypeStruct(a.shape if s else (N_DEV,) + a.shape, a.dtype) for a, s in zip(arrays, scatter)]
    return pl.pallas_call(
        body, name=name,
        in_specs=[any_spec] * n,
        out_specs=[any_spec] * n,
        out_shape=out_shape,
        scratch_shapes=[pltpu.SemaphoreType.DMA((n * (N_DEV - 1),)), pltpu.SemaphoreType.DMA((n * (N_DEV - 1),)),
                        pltpu.SemaphoreType.DMA((n,))],
        compiler_params=pltpu.CompilerParams(has_side_effects=True),
    )(*arrays)


N_CHIPS = N_DEV // 2


def _other_chips(x, y):
    return [(1 - x, y), (x, 1 - y), (1 - x, 1 - y)]


def _gather_two_level(arrays, name):
    n = len(arrays)
    per = N_DEV - 1

    def body(*refs):
        ins, outs = refs[:n], refs[n:2 * n]
        send_sems, recv_sems, local_sems = refs[2 * n:]
        x, y, c = _mesh_position()
        me, sibling = (x, y, c), (x, y, 1 - c)
        chips = _other_chips(x, y)

        def copy(i, k, block, to, src=None):
            slot = outs[i].at[_linear(block)]
            return pltpu.make_async_remote_copy(
                src_ref=slot if src is None else src, dst_ref=slot, send_sem=send_sems.at[i * per + k],
                recv_sem=recv_sems.at[i * per + k], device_id=to, device_id_type=pl.DeviceIdType.MESH)

        local = [pltpu.make_async_copy(ins[i], outs[i].at[_linear(me)], local_sems.at[i]) for i in range(n)]
        for cp in local:
            cp.start()
        started = []
        for i in range(n):
            first = [copy(i, 0, me, sibling, src=ins[i])]
            first += [copy(i, 1 + j, me, (*chip, c), src=ins[i]) for j, chip in enumerate(chips)]
            for cp in first:
                cp.start()
            started += first
        for j, chip in enumerate(chips):
            for i in range(n):
                copy(i, 1 + j, (*chip, c), me).wait_recv()
                passed = copy(i, 4 + j, (*chip, c), sibling)
                passed.start()
                started.append(passed)
        for i in range(n):
            copy(i, 0, sibling, me).wait_recv()
            for j, chip in enumerate(chips):
                copy(i, 4 + j, (*chip, 1 - c), me).wait_recv()
        for cp in started:
            cp.wait_send()
        for cp in local:
            cp.wait()

    any_spec = pl.BlockSpec(memory_space=pl.ANY)
    return pl.pallas_call(
        body, name=name,
        in_specs=[any_spec] * n,
        out_specs=[any_spec] * n,
        out_shape=[jax.ShapeDtypeStruct((N_DEV,) + a.shape, a.dtype) for a in arrays],
        scratch_shapes=[pltpu.SemaphoreType.DMA((n * per,)), pltpu.SemaphoreType.DMA((n * per,)), pltpu.SemaphoreType.DMA((n,))],
        compiler_params=pltpu.CompilerParams(has_side_effects=True),
    )(*arrays)


def _pair_reduce(a, name, rows):
    _, r, c = a.shape
    assert r % rows == 0
    n_steps = r // rows
    a5 = a.reshape(N_CHIPS, 2, r, c)
    core = lax.axis_index("c").astype(jnp.int32).reshape(1)

    def body(core_ref, keep_ref, send_ref, o_ref, land, send_sems, recv_sems, credits):
        i = pl.program_id(0)
        slot = i % 2
        x, y, cc = _mesh_position()
        sibling = (x, y, 1 - cc)

        @pl.when(i >= 2)
        def _():
            pl.semaphore_wait(credits.at[slot], 1)

        rdma = pltpu.make_async_remote_copy(
            src_ref=send_ref, dst_ref=land.at[slot], send_sem=send_sems.at[slot], recv_sem=recv_sems.at[slot],
            device_id=sibling, device_id_type=pl.DeviceIdType.MESH)
        rdma.start()
        rdma.wait_recv()
        o_ref[...] = (keep_ref[:, 0].astype(F32) + land[slot, :, 0].astype(F32)).astype(o_ref.dtype)
        rdma.wait_send()

        @pl.when(i + 2 < n_steps)
        def _():
            pl.semaphore_signal(credits.at[slot], inc=1, device_id=sibling, device_id_type=pl.DeviceIdType.MESH)

    grid_spec = pltpu.PrefetchScalarGridSpec(
        num_scalar_prefetch=1,
        grid=(n_steps,),
        in_specs=[pl.BlockSpec((N_CHIPS, 1, rows, c), lambda i, core_ref: (0, core_ref[0], i, 0)),
                  pl.BlockSpec((N_CHIPS, 1, rows, c), lambda i, core_ref: (0, 1 - core_ref[0], i, 0))],
        out_specs=pl.BlockSpec((N_CHIPS, rows, c), lambda i, core_ref: (0, i, 0)),
        scratch_shapes=[pltpu.VMEM((2, N_CHIPS, 1, rows, c), a.dtype), pltpu.SemaphoreType.DMA((2,)),
                        pltpu.SemaphoreType.DMA((2,)), pltpu.SemaphoreType.REGULAR((2,))],
    )
    return pl.pallas_call(
        body, name=name, grid_spec=grid_spec,
        out_shape=jax.ShapeDtypeStruct((N_CHIPS, r, c), a.dtype),
        compiler_params=pltpu.CompilerParams(dimension_semantics=("arbitrary",), vmem_limit_bytes=VMEM_LIMIT_BYTES,
                                             has_side_effects=True),
    )(core, a5, a5)


def _chip_exchange(arrays, name):
    n = len(arrays)
    per = N_CHIPS - 1

    def body(*refs):
        ins, outs = refs[:n], refs[n:2 * n]
        send_sems, recv_sems, local_sems = refs[2 * n:]
        x, y, c = _mesh_position()
        me_q = 2 * x + y
        local = [pltpu.make_async_copy(ins[i].at[me_q], outs[i].at[me_q], local_sems.at[i]) for i in range(n)]
        for cp in local:
            cp.start()
        started = []
        for j, (px, py) in enumerate(_other_chips(x, y)):
            for i in range(n):
                cp = pltpu.make_async_remote_copy(
                    src_ref=ins[i].at[2 * px + py], dst_ref=outs[i].at[me_q], send_sem=send_sems.at[i * per + j],
                    recv_sem=recv_sems.at[i * per + j], device_id=(px, py, c), device_id_type=pl.DeviceIdType.MESH)
                cp.start()
                started.append(cp)
        for j, (px, py) in enumerate(_other_chips(x, y)):
            for i in range(n):
                pltpu.make_async_remote_copy(
                    src_ref=ins[i].at[2 * px + py], dst_ref=outs[i].at[2 * px + py], send_sem=send_sems.at[i * per + j],
                    recv_sem=recv_sems.at[i * per + j], device_id=(px, py, c), device_id_type=pl.DeviceIdType.MESH).wait_recv()
        for cp in started:
            cp.wait_send()
        for cp in local:
            cp.wait()

    any_spec = pl.BlockSpec(memory_space=pl.ANY)
    return pl.pallas_call(
        body, name=name,
        in_specs=[any_spec] * n,
        out_specs=[any_spec] * n,
        out_shape=[jax.ShapeDtypeStruct(a.shape, a.dtype) for a in arrays],
        scratch_shapes=[pltpu.SemaphoreType.DMA((n * per,)), pltpu.SemaphoreType.DMA((n * per,)), pltpu.SemaphoreType.DMA((n,))],
        compiler_params=pltpu.CompilerParams(has_side_effects=True),
    )(*arrays)


def _chip_exchange_start(arrays, name):
    n = len(arrays)
    per = N_CHIPS - 1
    hbm = pl.BlockSpec(memory_space=pltpu.HBM)
    sem = pl.BlockSpec(memory_space=pltpu.SEMAPHORE)

    def body(*refs):
        ins, lands = refs[:n], refs[n:2 * n]
        send_sems, recv_sems = refs[2 * n], refs[2 * n + 1]
        token = refs[-1]
        x, y, c = _mesh_position()
        me_q = 2 * x + y
        for j, (px, py) in enumerate(_other_chips(x, y)):
            for i in range(n):
                pltpu.make_async_remote_copy(
                    src_ref=ins[i].at[2 * px + py], dst_ref=lands[i].at[me_q], send_sem=send_sems.at[i * per + j],
                    recv_sem=recv_sems.at[i * per + j], device_id=(px, py, c), device_id_type=pl.DeviceIdType.MESH).start()
        token[...] = jnp.zeros_like(token)

    buffers = [pltpu.HBM(a.shape, a.dtype) for a in arrays]
    res = pl.pallas_call(
        body, name=name,
        out_shape=(pltpu.SemaphoreType.DMA((n * per,)), pltpu.SemaphoreType.DMA((n * per,)), *buffers, *buffers,
                   jax.ShapeDtypeStruct((8, BLK), F32)),
        in_specs=[hbm] * (2 * n),
        out_specs=(sem, sem, *([hbm] * (2 * n)), pl.BlockSpec(memory_space=pltpu.VMEM)),
        input_output_aliases={i: 2 + i for i in range(2 * n)},
        compiler_params=pltpu.CompilerParams(has_side_effects=pltpu.SideEffectType.DATAFLOW_SIDE_EFFECTING),
    )(*[pltpu.with_memory_space_constraint(a, pltpu.HBM) for a in arrays],
      *[pltpu.with_memory_space_constraint(lax.empty(a.shape, a.dtype), pltpu.HBM) for a in arrays])
    return res[0], res[1], res[2:2 + n], res[2 + n:2 + 2 * n], res[-1]


def _chip_exchange_wait(send_sems, recv_sems, sources, lands, after, name):
    n = len(sources)
    per = N_CHIPS - 1
    hbm = pl.BlockSpec(memory_space=pltpu.HBM)
    sem = pl.BlockSpec(memory_space=pltpu.SEMAPHORE)

    def body(*refs):
        ins, zones = refs[:n], refs[n:2 * n]
        send, recv = refs[2 * n], refs[2 * n + 1]
        x, y, c = _mesh_position()
        for j, (px, py) in enumerate(_other_chips(x, y)):
            for i in range(n):
                cp = pltpu.make_async_remote_copy(
                    src_ref=ins[i].at[2 * px + py], dst_ref=zones[i].at[2 * px + py], send_sem=send.at[i * per + j],
                    recv_sem=recv.at[i * per + j], device_id=(px, py, c), device_id_type=pl.DeviceIdType.MESH)
                cp.wait_send()
                cp.wait_recv()

    buffers = [pltpu.HBM(a.shape, a.dtype) for a in sources]
    res = pl.pallas_call(
        body, name=name,
        out_shape=(*buffers, *buffers),
        in_specs=[hbm] * (2 * n) + [sem, sem, pl.BlockSpec(memory_space=pl.ANY)],
        out_specs=[hbm] * (2 * n),
        input_output_aliases={i: i for i in range(2 * n)},
        compiler_params=pltpu.CompilerParams(has_side_effects=pltpu.SideEffectType.DATAFLOW_SIDE_EFFECTING),
    )(*sources, *lands, send_sems, recv_sems, after)
    return res[:n], res[n:]


def _adam_reduce(parts, w, m, v, name, rows):
    r, c = w.shape
    n_parts = parts.shape[0]
    assert r % rows == 0
    c1 = 1.0 - ADAM_B1 ** ADAM_STEP
    c2 = 1.0 - ADAM_B2 ** ADAM_STEP

    def body(p_ref, w_ref, m_ref, v_ref, g_ref, d_ref, nm_ref, nv_ref):
        g = p_ref[0].astype(F32)
        for s in range(1, n_parts):
            g = g + p_ref[s].astype(F32)
        g_ref[...] = g
        nm = ADAM_B1 * m_ref[...] + (1.0 - ADAM_B1) * g
        nv = ADAM_B2 * v_ref[...] + (1.0 - ADAM_B2) * (g * g)
        nm_ref[...] = nm
        nv_ref[...] = nv
        d_ref[...] = -ADAM_LR * ((nm / c1) / (jnp.sqrt(nv / c2) + ADAM_EPS) + ADAM_WD * w_ref[...])

    blk = pl.BlockSpec((rows, c), lambda i: (i, 0))
    shape = jax.ShapeDtypeStruct((r, c), F32)
    return pl.pallas_call(
        body, name=name,
        grid=(r // rows,),
        in_specs=[pl.BlockSpec((n_parts, rows, c), lambda i: (0, i, 0)), blk, blk, blk],
        out_specs=[blk] * 4,
        out_shape=[shape] * 4,
        compiler_params=_params("parallel"),
    )(parts, w, m, v)


def _lane_total(rows8):
    def body(a_ref, o_ref):
        o_ref[...] = _sum_all(a_ref[...])

    return pl.pallas_call(body, name="loss_total", out_shape=jax.ShapeDtypeStruct((1, 1), F32))(rows8)


def _permute_w_in(w):
    rows = w.shape[0]
    n_pairs = N_GROUPS * HEAD_PAIRS
    qkv = w[:, :V_END].reshape(rows, 3, n_pairs, BLK).transpose(0, 2, 1, 3).reshape(rows, V_END)
    return jnp.concatenate(
        [w[:, DT_END:GMERGE_END], w[:, GATT_END:Z_END], w[:, Z_END:XBC_END], w[:, GMERGE_END:], w[:, V_END:GATT_END],
         w[:, XBC_END:DT_END], jnp.zeros((rows, QKV_OFF - R_DT - SSM_HEADS), w.dtype), qkv], axis=1)


def _unpermute_w_in(g):
    rows = g.shape[0]
    n_pairs = N_GROUPS * HEAD_PAIRS
    qkv = g[:, QKV_OFF:].reshape(rows, n_pairs, 3, BLK).transpose(0, 2, 1, 3).reshape(rows, V_END)
    return jnp.concatenate(
        [qkv, g[:, R_GATT:R_GATT + GROUP_COLS], g[:, R_Z:R_Z + D_INNER], g[:, R_XBC:R_XBC + CONV_DIM],
         g[:, R_DT:R_DT + SSM_HEADS], g[:, R_GM:R_GM + 2 * D_MODEL], g[:, R_GPLE:R_GPLE + D_MODEL]], axis=1)


def _column_runs():
    runs = [(DT_END, R_GM, 2 * D_MODEL), (GATT_END, R_Z, D_INNER), (Z_END, R_XBC, CONV_DIM), (GMERGE_END, R_GPLE, D_MODEL),
            (V_END, R_GATT, GROUP_COLS), (XBC_END, R_DT, SSM_HEADS)]
    for pair in range(N_GROUPS * HEAD_PAIRS):
        for part in range(3):
            runs.append((part * ATT_QKV + pair * BLK, QKV_OFF + (pair * 3 + part) * BLK, BLK))
    return runs


RELAYOUT_ROWS = 128


def _assemble_w_in(gathered):
    rows = gathered.shape[1]

    def pieces(ref_col, width):
        out = []
        while width > 0:
            s, o = divmod(ref_col, IN_SHARD)
            w = min(width, IN_SHARD - o)
            out.append((s, o, w))
            ref_col, width = ref_col + w, width - w
        return out

    blocks = {}
    for ref_col, col, width in _column_runs():
        for off in range(0, width, BLK):
            blocks[(col + off) // BLK] = (ref_col + off, min(BLK, width - off))

    def body(g_ref, o_ref):
        for blk in range(HCAT_COLS // BLK):
            lanes = pl.ds(blk * BLK, BLK)
            if blk not in blocks:
                o_ref[:, lanes] = jnp.zeros((RELAYOUT_ROWS, BLK), o_ref.dtype)
                continue
            ref_col, width = blocks[blk]
            parts = [g_ref[s, :, pl.ds(o, w)] for s, o, w in pieces(ref_col, width)]
            if width < BLK:
                parts.append(jnp.zeros((RELAYOUT_ROWS, BLK - width), o_ref.dtype))
            o_ref[:, lanes] = parts[0] if len(parts) == 1 else jnp.concatenate(parts, axis=1)

    return pl.pallas_call(
        body, name="assemble_w_in",
        grid=(rows // RELAYOUT_ROWS,),
        in_specs=[pl.BlockSpec((N_DEV, RELAYOUT_ROWS, IN_SHARD), lambda i: (0, i, 0))],
        out_specs=pl.BlockSpec((RELAYOUT_ROWS, HCAT_COLS), lambda i: (i, 0)),
        out_shape=jax.ShapeDtypeStruct((rows, HCAT_COLS), gathered.dtype),
        compiler_params=_params("parallel"),
    )(gathered)


def _scatter_w_in_grad(g):
    rows = g.shape[0]
    runs = sorted(_column_runs())

    def pieces(shard):
        lo, hi = shard * IN_SHARD, (shard + 1) * IN_SHARD
        out = []
        for ref_col, col, width in runs:
            a, b = max(lo, ref_col), min(hi, ref_col + width)
            if a < b:
                out.append((col + a - ref_col, b - a))
        return out

    def body(g_ref, o_ref):
        for shard in range(N_DEV):
            o_ref[shard] = jnp.concatenate([g_ref[:, pl.ds(col, w)] for col, w in pieces(shard)], axis=1)

    return pl.pallas_call(
        body, name="scatter_w_in_grad",
        grid=(rows // RELAYOUT_ROWS,),
        in_specs=[pl.BlockSpec((RELAYOUT_ROWS, HCAT_COLS), lambda i: (i, 0))],
        out_specs=pl.BlockSpec((N_DEV, RELAYOUT_ROWS, IN_SHARD), lambda i: (0, i, 0)),
        out_shape=jax.ShapeDtypeStruct((N_DEV, rows, IN_SHARD), g.dtype),
        compiler_params=_params("parallel"),
    )(g)


SMALL_ROWS = 80
_SMALL_LAYOUT = (("conv_b", CONV_DIM), ("dt_bias", BLK), ("a_log", BLK), ("d_skip", BLK), ("ssm_norm_w", D_INNER),
                 ("ln_g", D_MODEL), ("ln_b", D_MODEL), ("rel_bias", NUM_BUCKETS * ATT_HEADS), ("loss", D_MODEL))


def _pack_small(vals):
    flat = []
    for name, width in _SMALL_LAYOUT:
        v = vals.get(name)
        v = jnp.zeros((width,), F32) if v is None else v.reshape(-1).astype(F32)
        flat.append(jnp.pad(v, (0, width - v.shape[0])))
    flat = jnp.concatenate(flat)
    return jnp.pad(flat, (0, SMALL_ROWS * BLK - flat.shape[0])).reshape(SMALL_ROWS, BLK)


def _unpack_small(packed):
    flat = packed.reshape(-1)
    out, pos = {}, 0
    for name, width in _SMALL_LAYOUT:
        out[name] = flat[pos:pos + width]
        pos += width
    for name in ("dt_bias", "a_log", "d_skip"):
        out[name] = out[name][:SSM_HEADS]
    out["rel_bias"] = out["rel_bias"].reshape(NUM_BUCKETS, ATT_HEADS)
    return out


def _pack_cols(b_gate_part, conv_w_part):
    return jnp.concatenate([jnp.pad(b_gate_part, ((0, 5), (0, 0))), jnp.pad(conv_w_part, ((0, 4), (0, 0)))], axis=1)


def _pack_cols_all(b_gate_full, conv_w_full):
    bg = b_gate_full.reshape(3, N_DEV, BLK).transpose(1, 0, 2)
    cw = conv_w_full.reshape(CONV_WIDTH, N_DEV, CONV_DIM // N_DEV).transpose(1, 0, 2)
    return jnp.concatenate([jnp.pad(bg, ((0, 0), (0, 5), (0, 0))), jnp.pad(cw, ((0, 0), (0, 4), (0, 0)))], axis=2)


def _unpack_cols_all(packed):
    bg = packed[:, :3, :BLK].transpose(1, 0, 2).reshape(3, D_MODEL)
    cw = packed[:, :CONV_WIDTH, BLK:].transpose(1, 0, 2).reshape(CONV_WIDTH, CONV_DIM)
    return bg, cw


def _local_step(x, p, target, wp16, wb16, wo16, wple16, b_gate, conv_w, small, start_exchange):
    bsz, seq, _ = x.shape
    t = bsz * seq
    x2 = x.reshape(t, D_MODEL)
    x16 = x2.astype(BF16)
    p16 = p.reshape(t, PLE_DIM).astype(BF16)
    tgt2 = target.reshape(t, D_MODEL)
    b_gate8 = jnp.pad(b_gate, ((0, 5), (0, 0)))
    ln_gb = jnp.pad(jnp.stack([small["ln_g"], small["ln_b"]]), ((0, 6), (0, 0)))
    conv_b = small["conv_b"].reshape(1, CONV_DIM)
    norm_w = small["ssm_norm_w"].reshape(1, D_INNER)
    pad_heads = lambda v: jnp.pad(v, (0, BLK - SSM_HEADS))
    prow = jnp.pad(jnp.stack([pad_heads(small["dt_bias"]), pad_heads(small["a_log"]), pad_heads(small["d_skip"])]), ((0, 5), (0, 0)))
    pcol = prow.T
    wa16, wbb16 = wb16[:GROUP_COLS], wb16[GROUP_COLS:]

    rest = _matmul(x16, wp16, mode="nn", out_dtype=F32, name="inproj", tm=512, tn=2304, tk=D_MODEL, n_outer=True)
    rest3 = rest.reshape(bsz, seq, HCAT_COLS)
    biases, onehots, outs, lses = [], [], [], []
    for g, (_, dil) in enumerate(DILATED_PATTERNS):
        bias, onehot = _bias_matrix(small["rel_bias"][:, g * HEADS_PER_GROUP:(g + 1) * HEADS_PER_GROUP], dil)
        out3, lse3 = _attn_fwd(rest3, bias, g, dil, f"attn_fwd{g}")
        biases.append(bias), onehots.append(onehot)
        outs.append(out3.reshape(t, GROUP_COLS)), lses.append(lse3.reshape(t, GROUP_COLS))
    oa = _combine_fwd(outs, lses, rest)
    xa = _conv_fwd(rest3, conv_w, conv_b)
    dtr_t = jnp.swapaxes(rest3[:, :, R_DT:R_DT + BLK], 1, 2)
    y, s_in = _ssd_fwd(xa, rest3, dtr_t, prow, pcol)
    y2 = y.reshape(t, D_INNER)
    ys = _gatenorm_fwd(y2, rest, norm_w)
    y_a = _matmul(oa, wa16, mode="nn", out_dtype=F32, name="branch_a", tm=512, tn=D_MODEL, tk=GROUP_COLS)
    y_b = _matmul(ys, wbb16, mode="nn", out_dtype=F32, name="branch_b", tm=512, tn=D_MODEL, tk=D_INNER)
    merged = _merge_fwd(y_a, y_b, rest, b_gate8)
    mix = _matmul(merged, wo16, mode="nn", out_dtype=F32, name="out_proj", tm=512, tn=D_MODEL, tk=D_MODEL)
    pw = _matmul(p16, wple16, mode="nn", out_dtype=F32, name="ple_proj", tm=512, tn=D_MODEL, tk=PLE_DIM)
    d_pre, d_pre16, d_hcat, d_pw, stats = _final(x2, mix, pw, rest, b_gate8, ln_gb, tgt2)

    d_merged = _matmul(d_pre16, wo16, mode="nt", out_dtype=F32, name="d_merged", tm=512, tn=D_MODEL, tk=D_MODEL)
    g_w_out = _matmul(merged, d_pre16, mode="tn", out_dtype=BF16, name="g_w_out", tm=512, tn=D_MODEL, tk=1024)
    g_w_ple = _matmul(p16, d_pw, mode="tn", out_dtype=BF16, name="g_w_ple", tm=PLE_DIM, tn=D_MODEL, tk=1024)
    d_ya, d_yb, d_hcat, dbg01 = _merge_bwd(y_a, y_b, rest, b_gate8, d_merged, d_hcat)
    d_oa = _matmul(d_ya, wa16, mode="nt", out_dtype=F32, name="d_oa", tm=512, tn=GROUP_COLS, tk=D_MODEL)
    d_ys = _matmul(d_yb, wbb16, mode="nt", out_dtype=F32, name="d_ys", tm=512, tn=1024, tk=D_MODEL)
    g_wa = _matmul(oa, d_ya, mode="tn", out_dtype=BF16, name="g_w_branch_a", tm=GROUP_COLS, tn=D_MODEL, tk=1024)
    g_wb = _matmul(ys, d_yb, mode="tn", out_dtype=BF16, name="g_w_branch_b", tm=512, tn=D_MODEL, tk=1024)
    d_outs_dd_hc = _combine_bwd(outs, lses, rest, d_oa, d_hcat)
    d_outs, dds, d_hcat = d_outs_dd_hc[:3], d_outs_dd_hc[3:6], d_outs_dd_hc[6]
    d_y, d_hcat, d_nw = _gatenorm_bwd(y2, rest, norm_w, d_ys, d_hcat)
    d_xa, ddt, ddt_t, gprow, gpcol = _ssd_bwd(xa, rest3, dtr_t, prow, pcol, s_in, d_y.reshape(bsz, seq, D_INNER))
    d_hcat3, d_conv = _conv_bwd(rest3, conv_w, conv_b, d_xa, d_hcat.reshape(bsz, seq, HCAT_COLS))
    g_tables = []
    shape3 = (bsz, seq, GROUP_COLS)
    for g, (_, dil) in enumerate(DILATED_PATTERNS):
        d_hcat3, dbias = _attn_bwd(rest3, biases[g], lses[g].reshape(shape3), d_outs[g].reshape(shape3), dds[g].reshape(shape3),
                                   d_hcat3, g, dil, f"attn_bwd{g}")
        g_tables.append(_bias_grad(dbias, onehots[g], f"bias_grad{g}"))
    d_dt = (ddt + jnp.swapaxes(ddt_t, 1, 2)).reshape(t, BLK).astype(BF16)
    d_hcat = lax.dynamic_update_slice(d_hcat3.reshape(t, HCAT_COLS),
                                      jnp.concatenate([d_dt, jnp.zeros((t, QKV_OFF - R_DT - BLK), BF16)], axis=1), (0, R_DT))
    g_wp = _matmul(x16, d_hcat, mode="tn", out_dtype=BF16, name="g_w_in", tm=D_MODEL, tn=2304, tk=1024, n_outer=True)

    grads = dict(
        w_in=_scatter_w_in_grad(g_wp),
        b_gate=jnp.stack([dbg01[0], dbg01[1], stats[ST_BG2]]),
        conv_w=d_conv[:CONV_WIDTH],
        w_branch=jnp.concatenate([g_wa, g_wb], axis=0),
        w_out=g_w_out,
        w_ple=g_w_ple,
    )
    small_grads = dict(
        conv_b=d_conv[CONV_WIDTH],
        dt_bias=gprow[P_DTB, :SSM_HEADS] + gpcol[:SSM_HEADS, P_DTB],
        a_log=gprow[P_ALOG, :SSM_HEADS] + gpcol[:SSM_HEADS, P_ALOG],
        d_skip=gprow[P_DSKIP, :SSM_HEADS],
        ssm_norm_w=d_nw[0],
        ln_g=stats[ST_LNG],
        ln_b=stats[ST_LNB],
        rel_bias=jnp.concatenate(g_tables, axis=1),
        loss=stats[ST_LOSS],
    )
    in_flight, token = start_exchange(grads)
    grad_x = _matmul(d_hcat, wp16, mode="nt", out_dtype=F32, name="grad_x", tm=1024, tn=D_MODEL, tk=2304,
                     add=d_pre, add_scale=ALPHA, after=token)
    return grad_x.reshape(bsz, seq, D_MODEL), in_flight, small_grads


WEIGHT_ORDER = ("w_in", "b_gate", "conv_w", "conv_b", "dt_bias", "a_log", "d_skip", "ssm_norm_w", "w_branch", "w_out",
                "w_ple", "ln_g", "ln_b", "rel_bias")
SMALL_NAMES = ("conv_b", "dt_bias", "a_log", "d_skip", "ssm_norm_w", "ln_g", "ln_b", "rel_bias")


def kernel(x, p, w_in, b_gate, conv_w, conv_b, dt_bias, a_log, d_skip, ssm_norm_w, w_branch, w_out, w_ple, ln_g, ln_b, rel_bias, loss_target, m_w_in, m_b_gate, m_conv_w, m_conv_b, m_dt_bias, m_a_log, m_d_skip, m_ssm_norm_w, m_w_branch, m_w_out, m_w_ple, m_ln_g, m_ln_b, m_rel_bias, v_w_in, v_b_gate, v_conv_w, v_conv_b, v_dt_bias, v_a_log, v_d_skip, v_ssm_norm_w, v_w_branch, v_w_out, v_w_ple, v_ln_g, v_ln_b, v_rel_bias):
    given = dict(w_in=w_in, b_gate=b_gate, conv_w=conv_w, conv_b=conv_b, dt_bias=dt_bias, a_log=a_log, d_skip=d_skip,
                 ssm_norm_w=ssm_norm_w, w_branch=w_branch, w_out=w_out, w_ple=w_ple, ln_g=ln_g, ln_b=ln_b)
    moments_m = dict(w_in=m_w_in, b_gate=m_b_gate, conv_w=m_conv_w, conv_b=m_conv_b, dt_bias=m_dt_bias, a_log=m_a_log,
                     d_skip=m_d_skip, ssm_norm_w=m_ssm_norm_w, w_branch=m_w_branch, w_out=m_w_out, w_ple=m_w_ple,
                     ln_g=m_ln_g, ln_b=m_ln_b)
    moments_v = dict(w_in=v_w_in, b_gate=v_b_gate, conv_w=v_conv_w, conv_b=v_conv_b, dt_bias=v_dt_bias, a_log=v_a_log,
                     d_skip=v_d_skip, ssm_norm_w=v_ssm_norm_w, w_branch=v_w_branch, w_out=v_w_out, w_ple=v_w_ple,
                     ln_g=v_ln_g, ln_b=v_ln_b)
    w = {k: a[0] for k, a in given.items()} | {"rel_bias": rel_bias}
    mm = {k: a[0] for k, a in moments_m.items()} | {"rel_bias": m_rel_bias}
    vv = {k: a[0] for k, a in moments_v.items()} | {"rel_bias": v_rel_bias}

    gathered = _gather_two_level(
        [w["w_in"].astype(BF16), w["w_branch"].astype(BF16), w["w_out"].astype(BF16), w["w_ple"].astype(BF16),
         _pack_cols(w["b_gate"], w["conv_w"])], "gather_weights")
    wp16 = _assemble_w_in(gathered[0])
    wb16 = gathered[1].reshape(BRANCH_ROWS, D_MODEL)
    wo16 = gathered[2].reshape(D_MODEL, D_MODEL)
    wple16 = gathered[3].transpose(1, 0, 2).reshape(PLE_DIM, D_MODEL)
    b_gate_full, conv_w_full = _unpack_cols_all(gathered[4])
    small = {k: w[k] for k in SMALL_NAMES}

    def start_exchange(grads):
        big = [grads["w_in"],
               grads["w_branch"].astype(BF16).reshape(N_DEV, BRANCH_ROWS // N_DEV, D_MODEL),
               grads["w_out"].astype(BF16).reshape(N_DEV, D_MODEL // N_DEV, D_MODEL),
               grads["w_ple"].astype(BF16).reshape(PLE_DIM, N_DEV, BLK).transpose(1, 0, 2)]
        sums = [_pair_reduce(a, f"pair_reduce{i}", rows) for i, (a, rows) in enumerate(zip(big, (128, 176, 128, 256)))]
        send_sems, recv_sems, sources, lands, token = _chip_exchange_start(sums, "chip_exchange_start")
        cols_all = _pack_cols_all(grads["b_gate"], grads["conv_w"])
        return (send_sems, recv_sems, sources, lands, cols_all), token

    grad_x, in_flight, small_grads = _local_step(x, p[0], loss_target, wp16, wb16, wo16, wple16, b_gate_full, conv_w_full,
                                                 small, start_exchange)
    send_sems, recv_sems, sources, lands, cols_all = in_flight
    sources, zones = _chip_exchange_wait(send_sems, recv_sems, sources, lands, grad_x, "chip_exchange_wait")
    me_q = 2 * lax.axis_index("x") + lax.axis_index("y")
    parts = [lax.dynamic_update_slice_in_dim(z, lax.dynamic_slice_in_dim(s, me_q, 1, axis=0), me_q, axis=0)
             for z, s in zip(zones, sources)]
    small_parts = _exchange([cols_all, _pack_small(small_grads)], [True, False], "exchange_small")

    out = {}
    out["w_in"] = _adam_reduce(parts[0], w["w_in"], mm["w_in"], vv["w_in"], "adam_w_in", 128)
    out["w_branch"] = _adam_reduce(parts[1], w["w_branch"], mm["w_branch"], vv["w_branch"], "adam_w_branch", 176)
    out["w_out"] = _adam_reduce(parts[2], w["w_out"], mm["w_out"], vv["w_out"], "adam_w_out", 128)
    out["w_ple"] = _adam_reduce(parts[3], w["w_ple"], mm["w_ple"], vv["w_ple"], "adam_w_ple", 256)
    cols = _adam_reduce(small_parts[0], _pack_cols(w["b_gate"], w["conv_w"]), _pack_cols(mm["b_gate"], mm["conv_w"]),
                        _pack_cols(vv["b_gate"], vv["conv_w"]), "adam_cols", 8)
    out["b_gate"] = [a[:3, :BLK] for a in cols]
    out["conv_w"] = [a[:CONV_WIDTH, BLK:] for a in cols]
    packed = _adam_reduce(small_parts[1], _pack_small({k: w[k] for k in SMALL_NAMES}), _pack_small({k: mm[k] for k in SMALL_NAMES}),
                          _pack_small({k: vv[k] for k in SMALL_NAMES}), "adam_small", SMALL_ROWS)
    unpacked = [_unpack_small(a) for a in packed]
    for k in SMALL_NAMES:
        out[k] = [u[k] for u in unpacked]
    loss_rows = unpacked[0]["loss"].reshape(D_MODEL // BLK, BLK)
    loss = _lane_total(loss_rows).reshape(())

    def shaped(k, a):
        return a if k == "rel_bias" else a[None]

    results = [loss, grad_x]
    for i in range(4):
        results += [shaped(k, out[k][i]) for k in WEIGHT_ORDER]
    return tuple(results)
```

```python
import functools
import math

import jax
import jax.numpy as jnp
from jax import lax
from jax.experimental import pallas as pl
from jax.experimental.pallas import tpu as pltpu

F32 = jnp.float32
BF16 = jnp.bfloat16

N_DEV = 8
D_MODEL = 1024
SEQ = 2048
HEAD_DIM = 64
HEADS_PER_GROUP = 12
DILATED_PATTERNS = ((128, 1), (512, 4), (2048, 16))
N_GROUPS = 3
ATT_HEADS = N_GROUPS * HEADS_PER_GROUP
GROUP_COLS = HEADS_PER_GROUP * HEAD_DIM
ATT_QKV = ATT_HEADS * HEAD_DIM
BLK = 128
NUM_BUCKETS = 32
MAX_DISTANCE = 2048
D_INNER = 2048
SSM_HEADS = 32
SSM_GROUPS = 4
HEADS_PER_SSM_GROUP = SSM_HEADS // SSM_GROUPS
D_STATE = 128
CONV_WIDTH = 4
CONV_DIM = D_INNER + 2 * SSM_GROUPS * D_STATE
PLE_DIM = 256
ALPHA = 2.0 ** 0.25
LN_EPS = 1e-5
RMS_EPS = 1e-5
IN_COLS = 15904
IN_SHARD = IN_COLS // N_DEV
BRANCH_ROWS = GROUP_COLS + D_INNER

Q_END = ATT_QKV
K_END = 2 * ATT_QKV
V_END = 3 * ATT_QKV
GATT_END = V_END + GROUP_COLS
Z_END = GATT_END + D_INNER
XBC_END = Z_END + CONV_DIM
DT_END = XBC_END + SSM_HEADS
GMERGE_END = DT_END + 2 * D_MODEL

R_GM, R_Z, R_XBC, R_GPLE, R_GATT, R_DT = 0, 2048, 4096, 7168, 8192, 8960
QKV_OFF = 9216
HEAD_PAIRS = GROUP_COLS // BLK
QKV_G = 3 * GROUP_COLS
HCAT_COLS = QKV_OFF + N_GROUPS * QKV_G

ADAM_LR, ADAM_B1, ADAM_B2, ADAM_EPS, ADAM_WD, ADAM_STEP = 0.001, 0.9, 0.999, 1e-08, 0.01, 10

VMEM_LIMIT_BYTES = 56 * 1024 * 1024


def _params(*semantics):
    return pltpu.CompilerParams(dimension_semantics=semantics, vmem_limit_bytes=VMEM_LIMIT_BYTES)


def _sigmoid(v):
    return 1.0 / (1.0 + jnp.exp(-v))


def _silu_and_grad(v):
    s = _sigmoid(v)
    return v * s, s * (1.0 + v * (1.0 - s))


def _matmul(a, b, *, mode, out_dtype, name, tm, tn, tk, n_off=0, n=None, add=None, add_scale=1.0, n_outer=False, after=None):
    if mode == "nn":
        (m, k), n_full = a.shape, b.shape[1]
        assert b.shape[0] == k
    elif mode == "nt":
        (m, k), n_full = a.shape, b.shape[0]
        assert b.shape[1] == k
    else:
        (k, m), n_full = a.shape, b.shape[1]
        assert b.shape[0] == k
    n = n_full if n is None else n
    assert m % tm == 0 and n % tn == 0 and k % tk == 0 and n_off % tn == 0, (name, m, n, k)
    nk = k // tk
    jo = n_off // tn
    dims = {"nn": (((1,), (0,)), ((), ())), "nt": (((1,), (1,)), ((), ())), "tn": (((0,), (0,)), ((), ()))}[mode]

    def body(*refs):
        a_ref, b_ref = refs[:2]
        add_ref = refs[2] if add is not None else None
        o_ref = refs[2 + (add is not None) + (after is not None)]
        acc_ref = refs[-1] if nk > 1 else None
        prod = lax.dot_general(a_ref[...].astype(BF16), b_ref[...].astype(BF16), dims, preferred_element_type=F32)

        def finish(total):
            if add_ref is not None:
                total = total + add_scale * add_ref[...]
            o_ref[...] = total.astype(out_dtype)

        if nk == 1:
            finish(prod)
        else:
            kk = pl.program_id(2)

            @pl.when(kk == 0)
            def _():
                acc_ref[...] = prod

            @pl.when(jnp.logical_and(kk > 0, kk < nk - 1))
            def _():
                acc_ref[...] += prod

            @pl.when(kk == nk - 1)
            def _():
                finish(acc_ref[...] + prod)

    def ij(f):
        return (lambda g0, g1, kk: f(g1, g0, kk)) if n_outer else f

    if mode == "nn":
        a_spec = pl.BlockSpec((tm, tk), ij(lambda i, j, kk: (i, kk)))
        b_spec = pl.BlockSpec((tk, tn), ij(lambda i, j, kk: (kk, j + jo)))
    elif mode == "nt":
        a_spec = pl.BlockSpec((tm, tk), ij(lambda i, j, kk: (i, kk)))
        b_spec = pl.BlockSpec((tn, tk), ij(lambda i, j, kk: (j, kk)))
    else:
        a_spec = pl.BlockSpec((tk, tm), ij(lambda i, j, kk: (kk, i)))
        b_spec = pl.BlockSpec((tk, tn), ij(lambda i, j, kk: (kk, j)))
    in_specs = [a_spec, b_spec]
    args = [a, b]
    if add is not None:
        in_specs.append(pl.BlockSpec((tm, tn), ij(lambda i, j, kk: (i, j))))
        args.append(add)
    if after is not None:
        in_specs.append(pl.BlockSpec((8, BLK), lambda g0, g1, kk: (0, 0)))
        args.append(after)
    return pl.pallas_call(
        body, name=name,
        grid=(n // tn, m // tm, nk) if n_outer else (m // tm, n // tn, nk),
        in_specs=in_specs,
        out_specs=pl.BlockSpec((tm, tn), ij(lambda i, j, kk: (i, j))),
        out_shape=jax.ShapeDtypeStruct((m, n), out_dtype),
        scratch_shapes=[pltpu.VMEM((tm, tn), F32)] if nk > 1 else [],
        compiler_params=_params("parallel", "parallel", "arbitrary"),
    )(*args)


UNITS_PER_ITER = 4


def _band_mask(first):
    qi = lax.broadcasted_iota(jnp.int32, (BLK, BLK if first else 2 * BLK), 0)
    kj = lax.broadcasted_iota(jnp.int32, (BLK, BLK if first else 2 * BLK), 1)
    delta = qi - kj if first else qi + BLK - kj
    return jnp.logical_and(delta >= 0, delta <= BLK)


def _attn_specs(seq, g):
    qkv = [pl.BlockSpec((1, seq, BLK), functools.partial(
        lambda hp, b, part: (b, 0, QKV_OFF // BLK + (g * HEAD_PAIRS + hp) * 3 + part), part=part)) for part in range(3)]
    one = pl.BlockSpec((1, seq, BLK), lambda hp, b: (b, 0, hp))
    bias = pl.BlockSpec((2, BLK, 2 * BLK), lambda hp, b: (hp, 0, 0))
    return qkv, one, bias


def _attn_rows(dil, r, n, first):
    start = r + (dil * BLK) * n
    if dil == 1:
        start = pl.multiple_of(start, BLK)
        rows = pl.ds(start, BLK)
        keys = rows if first else pl.ds(pl.multiple_of(start - BLK, BLK), 2 * BLK)
    else:
        rows = pl.ds(start, BLK, stride=dil)
        keys = rows if first else pl.ds(start - dil * BLK, 2 * BLK, stride=dil)
    return rows, keys


def _attn_schedule(dil, nb, unit):
    def blocks_of(r):
        unit(r, 0, True)
        for n in range(1, UNITS_PER_ITER):
            unit(r, n, False)
        if nb > UNITS_PER_ITER:
            def more(i, carry):
                for jj in range(UNITS_PER_ITER):
                    unit(r, i * UNITS_PER_ITER + jj, False)
                return carry
            lax.fori_loop(1, nb // UNITS_PER_ITER, more, 0)

    if nb >= UNITS_PER_ITER:
        assert nb % UNITS_PER_ITER == 0
        if dil == 1:
            blocks_of(0)
        else:
            def per_residue(r, carry):
                blocks_of(r)
                return carry
            lax.fori_loop(0, dil, per_residue, 0)
    else:
        per_iter = UNITS_PER_ITER // nb
        assert UNITS_PER_ITER % nb == 0 and dil % per_iter == 0

        def residues(i, carry):
            for jj in range(per_iter):
                for n in range(nb):
                    unit(i * per_iter + jj, n, n == 0)
            return carry
        lax.fori_loop(0, dil // per_iter, residues, 0)


def _attn_fwd(hcat3, bias, g, dil, name):
    bsz, seq, _ = hcat3.shape
    nb = seq // dil // BLK
    scale = HEAD_DIM ** -0.5

    def body(q_ref, k_ref, v_ref, bias_ref, o_ref, lse_ref):
        masks = {True: _band_mask(True), False: _band_mask(False)}

        def unit(r, n, first):
            rows, keys = _attn_rows(dil, r, n, first)
            q2 = q_ref[0, rows, :].astype(BF16)
            k2 = k_ref[0, keys, :].astype(BF16)
            v2 = v_ref[0, keys, :].astype(BF16)
            outs, lses = [], []
            for j in range(2):
                lanes = slice(j * HEAD_DIM, (j + 1) * HEAD_DIM)
                bias_j = bias_ref[j, :, BLK:] if first else bias_ref[j]
                s = _nt(q2[:, lanes], k2[:, lanes]) * scale + bias_j
                s = jnp.where(masks[first], s, -jnp.inf)
                mx = jnp.max(s, axis=-1, keepdims=True)
                e = jnp.exp(s - mx)
                den = jnp.sum(e, axis=-1, keepdims=True)
                outs.append(jnp.dot(e.astype(BF16), v2[:, lanes], preferred_element_type=F32) / den)
                lses.append(jnp.broadcast_to(mx + jnp.log(den), (BLK, HEAD_DIM)))
            o_ref[0, rows, :] = jnp.concatenate(outs, axis=1)
            lse_ref[0, rows, :] = jnp.concatenate(lses, axis=1)

        _attn_schedule(dil, nb, unit)

    qkv_specs, one, bias_spec = _attn_specs(seq, g)
    shape = jax.ShapeDtypeStruct((bsz, seq, GROUP_COLS), F32)
    return pl.pallas_call(
        body, name=name,
        grid=(HEAD_PAIRS, bsz),
        in_specs=qkv_specs + [bias_spec],
        out_specs=[one, one],
        out_shape=[shape, shape],
        compiler_params=_params("parallel", "parallel"),
    )(hcat3, hcat3, hcat3, bias)


def _attn_bwd(hcat3, bias, lse, d_out, dd, d_hcat3, g, dil, name):
    bsz, seq, _ = hcat3.shape
    nb = seq // dil // BLK
    scale = HEAD_DIM ** -0.5

    def body(q_ref, k_ref, v_ref, bias_ref, lse_ref, do_ref, dd_ref, _, dqkv_ref, dbias_ref, dq_acc, dk_acc, dv_acc):
        @pl.when(pl.program_id(1) == 0)
        def _():
            dbias_ref[...] = jnp.zeros_like(dbias_ref)

        dk_acc[...] = jnp.zeros_like(dk_acc)
        dv_acc[...] = jnp.zeros_like(dv_acc)
        masks = {True: _band_mask(True), False: _band_mask(False)}

        def unit(r, n, first):
            rows, keys = _attn_rows(dil, r, n, first)
            q2 = q_ref[0, rows, :].astype(BF16)
            k2 = k_ref[0, keys, :].astype(BF16)
            v2 = v_ref[0, keys, :].astype(BF16)
            do2 = do_ref[0, rows, :].astype(BF16)
            lse2 = lse_ref[0, rows, :]
            dd2 = dd_ref[0, rows, :]
            dqs, dks, dvs = [], [], []
            for j in range(2):
                lanes = slice(j * HEAD_DIM, (j + 1) * HEAD_DIM)
                q, kb, vb, do = q2[:, lanes], k2[:, lanes], v2[:, lanes], do2[:, lanes]
                delta = jnp.sum(dd2[:, lanes], axis=-1, keepdims=True)
                bias_j = bias_ref[j, :, BLK:] if first else bias_ref[j]
                s = _nt(q, kb) * scale + bias_j
                p = jnp.where(masks[first], jnp.exp(s - lse2[:, j * HEAD_DIM:j * HEAD_DIM + 1]), 0.0)
                ds = p * (_nt(do, vb) - delta)
                ds16 = ds.astype(BF16)
                dqs.append(jnp.dot(ds16, kb, preferred_element_type=F32) * scale)
                dks.append(_tn(ds16, q) * scale)
                dvs.append(_tn(p.astype(BF16), do))
                if first:
                    dbias_ref[j, :, BLK:] += ds
                else:
                    dbias_ref[j] += ds
            dq_acc[rows, :] = jnp.concatenate(dqs, axis=1)
            dk_acc[keys, :] += jnp.concatenate(dks, axis=1)
            dv_acc[keys, :] += jnp.concatenate(dvs, axis=1)

        _attn_schedule(dil, nb, unit)
        dqkv_ref[0, :, 0:BLK] = dq_acc[...].astype(BF16)
        dqkv_ref[0, :, BLK:2 * BLK] = dk_acc[...].astype(BF16)
        dqkv_ref[0, :, 2 * BLK:3 * BLK] = dv_acc[...].astype(BF16)

    qkv_specs, one, bias_spec = _attn_specs(seq, g)
    return pl.pallas_call(
        body, name=name,
        grid=(HEAD_PAIRS, bsz),
        in_specs=qkv_specs + [bias_spec, one, one, one, pl.BlockSpec(memory_space=pl.ANY)],
        out_specs=[pl.BlockSpec((1, seq, 3 * BLK), lambda hp, b: (b, 0, QKV_OFF // (3 * BLK) + g * HEAD_PAIRS + hp)), bias_spec],
        out_shape=[jax.ShapeDtypeStruct(d_hcat3.shape, d_hcat3.dtype), jax.ShapeDtypeStruct((HEADS_PER_GROUP, BLK, 2 * BLK), F32)],
        input_output_aliases={7: 0},
        scratch_shapes=[pltpu.VMEM((seq, BLK), F32)] * 3,
        compiler_params=_params("arbitrary", "arbitrary"),
    )(hcat3, hcat3, hcat3, bias, lse, d_out, dd, d_hcat3)


def _t5_buckets(dil):
    import numpy as np
    qi = np.arange(BLK)[:, None]
    kj = np.arange(2 * BLK)[None, :]
    dist = np.maximum(qi + BLK - kj, 0) * dil
    max_exact = NUM_BUCKETS // 2
    d_f = np.maximum(dist, 1).astype(np.float32)
    large = max_exact + (np.log(d_f / np.float32(max_exact)) / np.float32(math.log(MAX_DISTANCE / max_exact))
                         * np.float32(NUM_BUCKETS - max_exact)).astype(np.int32)
    large = np.minimum(large, NUM_BUCKETS - 1)
    return np.where(dist < max_exact, dist, large).astype(np.int32).reshape(-1)


def _bias_matrix(table_g, dil):
    buckets = jnp.asarray(_t5_buckets(dil))
    onehot = (buckets[None, :] == lax.broadcasted_iota(jnp.int32, (NUM_BUCKETS, 1), 0)).astype(F32)
    tk = 4096

    def body(t_ref, oh_ref, o_ref):
        o_ref[...] = _dot_hi(t_ref[...], oh_ref[...])

    bias = pl.pallas_call(
        body, name=f"bias_matrix{dil}",
        grid=(onehot.shape[1] // tk,),
        in_specs=[pl.BlockSpec((HEADS_PER_GROUP, NUM_BUCKETS), lambda kk: (0, 0)), pl.BlockSpec((NUM_BUCKETS, tk), lambda kk: (0, kk))],
        out_specs=pl.BlockSpec((HEADS_PER_GROUP, tk), lambda kk: (0, kk)),
        out_shape=jax.ShapeDtypeStruct((HEADS_PER_GROUP, onehot.shape[1]), F32),
        compiler_params=_params("parallel"),
    )(table_g.T, onehot)
    return bias.reshape(HEADS_PER_GROUP, BLK, 2 * BLK), onehot


def _bias_grad(dbias, onehot, name):
    flat = dbias.reshape(HEADS_PER_GROUP, 2 * BLK * BLK)
    tk = 4096

    def body(oh_ref, g_ref, o_ref):
        @pl.when(pl.program_id(0) == 0)
        def _():
            o_ref[...] = jnp.zeros_like(o_ref)
        o_ref[...] += lax.dot_general(oh_ref[...], g_ref[...], (((1,), (1,)), ((), ())),
                                      preferred_element_type=F32, precision=lax.Precision.HIGHEST)

    return pl.pallas_call(
        body, name=name,
        grid=(flat.shape[1] // tk,),
        in_specs=[pl.BlockSpec((NUM_BUCKETS, tk), lambda kk: (0, kk)), pl.BlockSpec((HEADS_PER_GROUP, tk), lambda kk: (0, kk))],
        out_specs=pl.BlockSpec((NUM_BUCKETS, HEADS_PER_GROUP), lambda kk: (0, 0)),
        out_shape=jax.ShapeDtypeStruct((NUM_BUCKETS, HEADS_PER_GROUP), F32),
        compiler_params=_params("arbitrary"),
    )(onehot, flat)


CONV_ROWS = 128
HALO = 8


def _conv_chunks(seq, fn):
    n = seq // CONV_ROWS
    fn(0, True, n == 1)
    if n > 2:
        def step(i, carry):
            fn(pl.multiple_of(i * CONV_ROWS, CONV_ROWS), False, False)
            return carry
        lax.fori_loop(1, n - 1, step, 0, unroll=7)
    if n > 1:
        fn((n - 1) * CONV_ROWS, False, True)


def _load_with_halo(ref, r0, first, last, after):
    lo = 0 if first else HALO
    hi = HALO if (after and not last) else 0
    v = ref[0, pl.ds(r0 - lo, CONV_ROWS + lo + hi), :]
    parts = ([jnp.zeros((HALO, v.shape[1]), v.dtype)] if first else []) + [v]
    if after and last:
        parts.append(jnp.zeros((HALO, v.shape[1]), v.dtype))
    return v if len(parts) == 1 else jnp.concatenate(parts, axis=0)


def _conv_pre(u_ext, n_rows, w_ref, b_ref):
    acc = b_ref[0:1, :] + w_ref[CONV_WIDTH - 1:CONV_WIDTH, :] * u_ext[HALO:HALO + n_rows]
    for s in range(1, CONV_WIDTH):
        acc = acc + w_ref[CONV_WIDTH - 1 - s:CONV_WIDTH - s, :] * u_ext[HALO - s:HALO - s + n_rows]
    return acc


def _conv_fwd(rest3, conv_w, conv_b):
    bsz, seq, _ = rest3.shape

    def body(u_ref, w_ref, b_ref, o_ref):
        def chunk(r0, first, last):
            pre = _conv_pre(_load_with_halo(u_ref, r0, first, last, False), CONV_ROWS, w_ref, b_ref)
            o_ref[0, pl.ds(r0, CONV_ROWS), :] = pre * _sigmoid(pre)
        _conv_chunks(seq, chunk)

    return pl.pallas_call(
        body, name="conv_fwd",
        grid=(bsz, CONV_DIM // BLK),
        in_specs=[pl.BlockSpec((1, seq, BLK), lambda b, c: (b, 0, R_XBC // BLK + c)),
                  pl.BlockSpec((CONV_WIDTH, BLK), lambda b, c: (0, c)),
                  pl.BlockSpec((1, BLK), lambda b, c: (0, c))],
        out_specs=pl.BlockSpec((1, seq, BLK), lambda b, c: (b, 0, c)),
        out_shape=jax.ShapeDtypeStruct((bsz, seq, CONV_DIM), F32),
        compiler_params=_params("parallel", "parallel"),
    )(rest3, conv_w, conv_b)


def _conv_bwd(rest3, conv_w, conv_b, d_act, d_hcat3):
    bsz, seq, _ = rest3.shape

    def body(u_ref, w_ref, b_ref, g_ref, _, du_ref, dw_ref):
        @pl.when(pl.program_id(1) == 0)
        def _():
            dw_ref[...] = jnp.zeros_like(dw_ref)

        def chunk(r0, first, last):
            ext = CONV_ROWS + HALO
            u_ext = _load_with_halo(u_ref, r0, first, last, True)
            g_ext = _load_with_halo(g_ref, r0, True, last, True)[HALO:]
            d_pre = g_ext * _silu_and_grad(_conv_pre(u_ext, ext, w_ref, b_ref))[1]
            d_now = d_pre[:CONV_ROWS]
            du = w_ref[CONV_WIDTH - 1:CONV_WIDTH, :] * d_now
            dw_ref[CONV_WIDTH - 1:CONV_WIDTH, :] += jnp.sum(d_now * u_ext[HALO:HALO + CONV_ROWS], axis=0, keepdims=True)
            for s in range(1, CONV_WIDTH):
                du = du + w_ref[CONV_WIDTH - 1 - s:CONV_WIDTH - s, :] * d_pre[s:s + CONV_ROWS]
                dw_ref[CONV_WIDTH - 1 - s:CONV_WIDTH - s, :] += jnp.sum(d_now * u_ext[HALO - s:HALO - s + CONV_ROWS],
                                                                      axis=0, keepdims=True)
            dw_ref[CONV_WIDTH:CONV_WIDTH + 1, :] += jnp.sum(d_now, axis=0, keepdims=True)
            du_ref[0, pl.ds(r0, CONV_ROWS), :] = du.astype(BF16)
        _conv_chunks(seq, chunk)

    return pl.pallas_call(
        body, name="conv_bwd",
        grid=(CONV_DIM // BLK, bsz),
        in_specs=[pl.BlockSpec((1, seq, BLK), lambda c, b: (b, 0, R_XBC // BLK + c)),
                  pl.BlockSpec((CONV_WIDTH, BLK), lambda c, b: (0, c)),
                  pl.BlockSpec((1, BLK), lambda c, b: (0, c)),
                  pl.BlockSpec((1, seq, BLK), lambda c, b: (b, 0, c)),
                  pl.BlockSpec(memory_space=pl.ANY)],
        out_specs=[pl.BlockSpec((1, seq, BLK), lambda c, b: (b, 0, R_XBC // BLK + c)),
                   pl.BlockSpec((8, BLK), lambda c, b: (0, c))],
        out_shape=[jax.ShapeDtypeStruct(d_hcat3.shape, d_hcat3.dtype), jax.ShapeDtypeStruct((8, CONV_DIM), F32)],
        input_output_aliases={4: 0},
        compiler_params=_params("parallel", "arbitrary"),
    )(rest3, conv_w, conv_b, d_act, d_hcat3)


P_DTB, P_ALOG, P_DSKIP = 0, 1, 2


def _softplus(v):
    return jnp.maximum(v, 0.0) + jnp.log(1.0 + jnp.exp(-jnp.abs(v)))


def _dot_hi(a, b):
    return jnp.dot(a, b, preferred_element_type=F32, precision=lax.Precision.HIGHEST)


def _nt(a, b):
    return lax.dot_general(a, b, (((1,), (1,)), ((), ())), preferred_element_type=F32)


def _tn(a, b):
    return lax.dot_general(a, b, (((0,), (0,)), ((), ())), preferred_element_type=F32)


def _sum_all(v):
    return jnp.sum(jnp.sum(v, axis=0, keepdims=True), axis=1, keepdims=True)


def _ssd_decays(dtr, dtr_t, prow_ref, pcol_ref):
    ri = lax.broadcasted_iota(jnp.int32, (BLK, BLK), 0)
    ci = lax.broadcasted_iota(jnp.int32, (BLK, BLK), 1)
    tri = (ri >= ci).astype(F32)
    tri_u = (ri <= ci).astype(F32)
    pre = dtr + prow_ref[P_DTB:P_DTB + 1, :]
    dt = _softplus(pre)
    ah_row = -jnp.exp(prow_ref[P_ALOG:P_ALOG + 1, :])
    acs = _dot_hi(tri, dt * ah_row)
    pre_t = dtr_t + pcol_ref[:, P_DTB:P_DTB + 1]
    dt_t = _softplus(pre_t)
    ah_col = -jnp.exp(pcol_ref[:, P_ALOG:P_ALOG + 1])
    acs_t = _dot_hi(dt_t * ah_col, tri_u)
    return dict(tri=tri, tri_u=tri_u, pre=pre, dt=dt, ah_row=ah_row, acs=acs, pre_t=pre_t, dt_t=dt_t, ah_col=ah_col,
                acs_t=acs_t, causal=ri >= ci, last_row=ri[:, 0:1] == BLK - 1)


def _ssd_head(h, d, x_ref, s_in, g_mat):
    col = d["acs"][:, h:h + 1]
    row = d["acs_t"][h:h + 1, :]
    lm = jnp.exp(jnp.where(d["causal"], col - row, -jnp.inf))
    m = g_mat * lm
    xh = x_ref[0, :, pl.ds(h * HEAD_DIM, HEAD_DIM)]
    dtc = d["dt"][:, h:h + 1]
    xd = xh * dtc
    e = jnp.exp(col)
    clast = d["acs"][BLK - 1:BLK, h:h + 1]
    f = jnp.exp(clast - col)
    return dict(col=col, lm=lm, m=m, xh=xh, dtc=dtc, xd=xd, e=e, ecl=jnp.exp(clast), f=f, xf=xd * f)


def _ssd_specs(nc, rev):
    cidx = (lambda c: nc - 1 - c) if rev else (lambda c: c)
    x_spec = pl.BlockSpec((1, BLK, D_INNER), lambda b, c: (b, cidx(c), 0))
    bm_spec = pl.BlockSpec((1, BLK, SSM_GROUPS * D_STATE), lambda b, c: (b, cidx(c), D_INNER // (SSM_GROUPS * D_STATE)))
    cm_spec = pl.BlockSpec((1, BLK, SSM_GROUPS * D_STATE), lambda b, c: (b, cidx(c), D_INNER // (SSM_GROUPS * D_STATE) + 1))
    dt_spec = pl.BlockSpec((1, BLK, BLK), lambda b, c: (b, cidx(c), R_DT // BLK))
    dtt_spec = pl.BlockSpec((1, BLK, BLK), lambda b, c: (b, 0, cidx(c)))
    prow_spec = pl.BlockSpec((8, BLK), lambda b, c: (0, 0))
    pcol_spec = pl.BlockSpec((BLK, 8), lambda b, c: (0, 0))
    st_spec = pl.BlockSpec((1, 1, SSM_HEADS, HEAD_DIM, D_STATE), lambda b, c: (b, cidx(c), 0, 0, 0))
    y_spec = pl.BlockSpec((1, BLK, D_INNER), lambda b, c: (b, cidx(c), 0))
    return x_spec, bm_spec, cm_spec, dt_spec, dtt_spec, prow_spec, pcol_spec, st_spec, y_spec


def _ssd_fwd(xa, rest3, dtr_t, prow, pcol):
    bsz, seq, _ = xa.shape
    nc = seq // BLK

    def body(x_ref, bm_ref, cm_ref, dt_ref, dtt_ref, prow_ref, pcol_ref, y_ref, sin_ref, s_ref):
        @pl.when(pl.program_id(1) == 0)
        def _():
            s_ref[...] = jnp.zeros_like(s_ref)

        d = _ssd_decays(dt_ref[0], dtt_ref[0], prow_ref, pcol_ref)
        for g in range(SSM_GROUPS):
            lanes = pl.ds(g * D_STATE, D_STATE)
            bg = bm_ref[0, :, lanes].astype(BF16)
            cg = cm_ref[0, :, lanes].astype(BF16)
            g_mat = _nt(cg, bg)
            for hh in range(HEADS_PER_SSM_GROUP):
                h = g * HEADS_PER_SSM_GROUP + hh
                s_in = s_ref[h]
                sin_ref[0, 0, h] = s_in
                q = _ssd_head(h, d, x_ref, s_in, g_mat)
                y_diag = jnp.dot(q["m"].astype(BF16), q["xd"].astype(BF16), preferred_element_type=F32)
                y_off = _nt(cg, s_in.astype(BF16)) * q["e"]
                s_ref[h] = s_in * q["ecl"] + _tn(q["xf"].astype(BF16), bg)
                y_ref[0, :, pl.ds(h * HEAD_DIM, HEAD_DIM)] = y_diag + y_off + prow_ref[P_DSKIP:P_DSKIP + 1, h:h + 1] * q["xh"]

    x_spec, bm_spec, cm_spec, dt_spec, dtt_spec, prow_spec, pcol_spec, st_spec, y_spec = _ssd_specs(nc, False)
    return pl.pallas_call(
        body, name="ssd_fwd",
        grid=(bsz, nc),
        in_specs=[x_spec, bm_spec, cm_spec, dt_spec, dtt_spec, prow_spec, pcol_spec],
        out_specs=[y_spec, st_spec],
        out_shape=[jax.ShapeDtypeStruct((bsz, seq, D_INNER), F32),
                   jax.ShapeDtypeStruct((bsz, nc, SSM_HEADS, HEAD_DIM, D_STATE), F32)],
        scratch_shapes=[pltpu.VMEM((SSM_HEADS, HEAD_DIM, D_STATE), F32)],
        compiler_params=_params("parallel", "arbitrary"),
    )(xa, xa, xa, rest3, dtr_t, prow, pcol)


def _ssd_bwd(xa, rest3, dtr_t, prow, pcol, s_in_all, dy):
    bsz, seq, _ = xa.shape
    nc = seq // BLK

    def body(x_ref, bm_ref, cm_ref, dt_ref, dtt_ref, prow_ref, pcol_ref, sin_ref, dy_ref,
             dxa_ref, ddt_ref, ddtt_ref, gprow_ref, gpcol_ref, ds_ref, dc_ref, ddtc_ref, drt_ref):
        first = jnp.logical_and(pl.program_id(0) == 0, pl.program_id(1) == 0)

        @pl.when(first)
        def _():
            gprow_ref[...] = jnp.zeros_like(gprow_ref)
            gpcol_ref[...] = jnp.zeros_like(gpcol_ref)

        @pl.when(pl.program_id(1) == 0)
        def _():
            ds_ref[...] = jnp.zeros_like(ds_ref)

        dc_ref[...] = jnp.zeros_like(dc_ref)
        ddtc_ref[...] = jnp.zeros_like(ddtc_ref)
        drt_ref[...] = jnp.zeros_like(drt_ref)
        d = _ssd_decays(dt_ref[0], dtt_ref[0], prow_ref, pcol_ref)
        for g in range(SSM_GROUPS):
            lanes = pl.ds(g * D_STATE, D_STATE)
            bg = bm_ref[0, :, lanes].astype(BF16)
            cg = cm_ref[0, :, lanes].astype(BF16)
            g_mat = _nt(cg, bg)
            d_g = jnp.zeros((BLK, BLK), F32)
            d_bg = jnp.zeros((BLK, D_STATE), F32)
            d_cg = jnp.zeros((BLK, D_STATE), F32)
            for hh in range(HEADS_PER_SSM_GROUP):
                h = g * HEADS_PER_SSM_GROUP + hh
                head_lanes = pl.ds(h * HEAD_DIM, HEAD_DIM)
                s_in = sin_ref[0, 0, h]
                s_in16 = s_in.astype(BF16)
                q = _ssd_head(h, d, x_ref, s_in, g_mat)
                m16, xd16 = q["m"].astype(BF16), q["xd"].astype(BF16)
                d_y = dy_ref[0, :, head_lanes]
                d_y16 = d_y.astype(BF16)
                d_so = ds_ref[h]
                d_so16 = d_so.astype(BF16)
                d_x = prow_ref[P_DSKIP:P_DSKIP + 1, h:h + 1] * d_y
                gprow_ref[P_DSKIP:P_DSKIP + 1, h:h + 1] += _sum_all(d_y * q["xh"])
                d_m = _nt(d_y16, xd16)
                d_xd = _tn(m16, d_y16)
                w = d_m * q["m"]
                d_g = d_g + d_m * q["lm"]
                d_col = jnp.sum(w, axis=1, keepdims=True)
                drt_ref[h:h + 1, :] = -jnp.sum(w, axis=0, keepdims=True)
                qmat = _nt(cg, s_in16)
                d_q16 = (d_y * q["e"]).astype(BF16)
                d_col = d_col + jnp.sum(d_y * qmat, axis=1, keepdims=True) * q["e"]
                d_cg = d_cg + jnp.dot(d_q16, s_in16, preferred_element_type=F32)
                d_sin = _tn(d_q16, cg) + d_so * q["ecl"]
                d_clast = _sum_all(d_so * s_in) * q["ecl"]
                d_xf = _nt(bg, d_so16)
                d_bg = d_bg + jnp.dot(q["xf"].astype(BF16), d_so16, preferred_element_type=F32)
                d_xd = d_xd + d_xf * q["f"]
                d_f = jnp.sum(d_xf * q["xd"], axis=1, keepdims=True) * q["f"]
                d_clast = d_clast + jnp.sum(d_f, axis=0, keepdims=True)
                d_col = d_col - d_f + jnp.where(d["last_row"], d_clast, 0.0)
                dxa_ref[0, :, head_lanes] = d_x + d_xd * q["dtc"]
                dc_ref[:, h:h + 1] = d_col
                ddtc_ref[:, h:h + 1] = jnp.sum(d_xd * q["xh"], axis=1, keepdims=True)
                ds_ref[h] = d_sin
            d_g16 = d_g.astype(BF16)
            dxa_ref[0, :, pl.ds(D_INNER + g * D_STATE, D_STATE)] = d_bg + _tn(d_g16, cg)
            dxa_ref[0, :, pl.ds(D_INNER + (SSM_GROUPS + g) * D_STATE, D_STATE)] = d_cg + jnp.dot(d_g16, bg, preferred_element_type=F32)
        d_a = _dot_hi(d["tri_u"], dc_ref[...])
        d_pre = (ddtc_ref[...] + d_a * d["ah_row"]) * _sigmoid(d["pre"])
        ddt_ref[0] = d_pre
        gprow_ref[P_DTB:P_DTB + 1, :] += jnp.sum(d_pre, axis=0, keepdims=True)
        gprow_ref[P_ALOG:P_ALOG + 1, :] += jnp.sum(d_a * d["dt"], axis=0, keepdims=True) * d["ah_row"]
        d_at = _dot_hi(drt_ref[...], d["tri"])
        d_pre_t = d_at * d["ah_col"] * _sigmoid(d["pre_t"])
        ddtt_ref[0] = d_pre_t
        gpcol_ref[:, P_DTB:P_DTB + 1] += jnp.sum(d_pre_t, axis=1, keepdims=True)
        gpcol_ref[:, P_ALOG:P_ALOG + 1] += jnp.sum(d_at * d["dt_t"], axis=1, keepdims=True) * d["ah_col"]

    x_spec, bm_spec, cm_spec, dt_spec, dtt_spec, prow_spec, pcol_spec, st_spec, y_spec = _ssd_specs(nc, True)
    return pl.pallas_call(
        body, name="ssd_bwd",
        grid=(bsz, nc),
        in_specs=[x_spec, bm_spec, cm_spec, dt_spec, dtt_spec, prow_spec, pcol_spec, st_spec, y_spec],
        out_specs=[pl.BlockSpec((1, BLK, CONV_DIM), lambda b, c: (b, nc - 1 - c, 0)),
                   pl.BlockSpec((1, BLK, BLK), lambda b, c: (b, nc - 1 - c, 0)),
                   pl.BlockSpec((1, BLK, BLK), lambda b, c: (b, 0, nc - 1 - c)),
                   prow_spec, pcol_spec],
        out_shape=[jax.ShapeDtypeStruct((bsz, seq, CONV_DIM), F32),
                   jax.ShapeDtypeStruct((bsz, seq, BLK), F32),
                   jax.ShapeDtypeStruct((bsz, BLK, seq), F32),
                   jax.ShapeDtypeStruct((8, BLK), F32),
                   jax.ShapeDtypeStruct((BLK, 8), F32)],
        scratch_shapes=[pltpu.VMEM((SSM_HEADS, HEAD_DIM, D_STATE), F32), pltpu.VMEM((BLK, BLK), F32),
                        pltpu.VMEM((BLK, BLK), F32), pltpu.VMEM((BLK, BLK), F32)],
        compiler_params=_params("arbitrary", "arbitrary"),
    )(xa, xa, xa, rest3, dtr_t, prow, pcol, s_in_all, dy)


GROUP_W = HEADS_PER_SSM_GROUP * HEAD_DIM


def _select_matrix(shape, g, head_axis, per_head):
    h = lax.broadcasted_iota(jnp.int32, shape, head_axis)
    j = lax.broadcasted_iota(jnp.int32, shape, 1 - head_axis)
    return (h == g * HEADS_PER_SSM_GROUP + lax.shift_right_logical(j, per_head.bit_length() - 1)).astype(BF16)


def _split16(v, terms):
    parts, rem = [], v
    for _ in range(terms):
        p = rem.astype(BF16)
        parts.append(p)
        rem = rem - p.astype(F32)
    return parts


def _sel_dot(a, b, terms=2):
    if a.dtype == BF16:
        return sum(jnp.dot(a, p, preferred_element_type=F32) for p in _split16(b, terms))
    return sum(jnp.dot(p, b, preferred_element_type=F32) for p in _split16(a, terms))


def _ssd_group(g, d, e_all, f_all, ecl_b, x_ref, prow_ref):
    spread = _select_matrix((BLK, GROUP_W), g, 0, HEAD_DIM)
    gather = _select_matrix((GROUP_W, BLK), g, 1, HEAD_DIM)
    xg = x_ref[0, :, pl.ds(g * GROUP_W, GROUP_W)]
    dt_g = _sel_dot(d["dt"], spread)
    e_g = _sel_dot(e_all, spread)
    f_g = _sel_dot(f_all, spread)
    dsk_g = _sel_dot(prow_ref[...], spread, 3)[P_DSKIP:P_DSKIP + 1, :]
    sc_g = _sel_dot(gather, ecl_b, 3)
    xd = xg * dt_g
    return dict(spread=spread, gather=gather, xg=xg, dt_g=dt_g, e_g=e_g, f_g=f_g, dsk_g=dsk_g, sc_g=sc_g, xd=xd,
                xd16=xd.astype(BF16), xf16=(xd * f_g).astype(BF16))


def _ssd_common(d):
    e_all = jnp.exp(d["acs"])
    f_all = jnp.exp(d["acs"][BLK - 1:BLK, :] - d["acs"])
    ecl_b = jnp.broadcast_to(jnp.exp(d["acs_t"][:, BLK - 1:BLK]), (BLK, BLK))
    return e_all, f_all, ecl_b


def _ssd_mask_decay(d, h, g_mat):
    col = d["acs"][:, h:h + 1]
    row = d["acs_t"][h:h + 1, :]
    lm = jnp.exp(jnp.where(d["causal"], col - row, -jnp.inf))
    return lm, g_mat * lm


def _ssd_state_spec(nc, rev):
    cidx = (lambda c: nc - 1 - c) if rev else (lambda c: c)
    return pl.BlockSpec((1, 1, SSM_GROUPS, GROUP_W, D_STATE), lambda b, c: (b, cidx(c), 0, 0, 0))


def _ssd_fwd(xa, rest3, dtr_t, prow, pcol):
    bsz, seq, _ = xa.shape
    nc = seq // BLK

    def body(x_ref, bm_ref, cm_ref, dt_ref, dtt_ref, prow_ref, pcol_ref, y_ref, sin_ref, s_ref):
        @pl.when(pl.program_id(1) == 0)
        def _():
            s_ref[...] = jnp.zeros_like(s_ref)

        d = _ssd_decays(dt_ref[0], dtt_ref[0], prow_ref, pcol_ref)
        e_all, f_all, ecl_b = _ssd_common(d)
        for g in range(SSM_GROUPS):
            lanes = pl.ds(g * D_STATE, D_STATE)
            bg = bm_ref[0, :, lanes].astype(BF16)
            cg = cm_ref[0, :, lanes].astype(BF16)
            g_mat = _nt(cg, bg)
            q = _ssd_group(g, d, e_all, f_all, ecl_b, x_ref, prow_ref)
            s_in = s_ref[g]
            sin_ref[0, 0, g] = s_in
            y_diag = []
            for j in range(HEADS_PER_SSM_GROUP):
                _, m = _ssd_mask_decay(d, g * HEADS_PER_SSM_GROUP + j, g_mat)
                y_diag.append(jnp.dot(m.astype(BF16), q["xd16"][:, j * HEAD_DIM:(j + 1) * HEAD_DIM], preferred_element_type=F32))
            y_off = _nt(cg, s_in.astype(BF16)) * q["e_g"]
            y_ref[0, :, pl.ds(g * GROUP_W, GROUP_W)] = jnp.concatenate(y_diag, axis=1) + y_off + q["dsk_g"] * q["xg"]
            s_ref[g] = s_in * q["sc_g"] + _tn(q["xf16"], bg)

    x_spec, bm_spec, cm_spec, dt_spec, dtt_spec, prow_spec, pcol_spec, _, y_spec = _ssd_specs(nc, False)
    return pl.pallas_call(
        body, name="ssd_fwd",
        grid=(bsz, nc),
        in_specs=[x_spec, bm_spec, cm_spec, dt_spec, dtt_spec, prow_spec, pcol_spec],
        out_specs=[y_spec, _ssd_state_spec(nc, False)],
        out_shape=[jax.ShapeDtypeStruct((bsz, seq, D_INNER), F32),
                   jax.ShapeDtypeStruct((bsz, nc, SSM_GROUPS, GROUP_W, D_STATE), F32)],
        scratch_shapes=[pltpu.VMEM((SSM_GROUPS, GROUP_W, D_STATE), F32)],
        compiler_params=_params("parallel", "arbitrary"),
    )(xa, xa, xa, rest3, dtr_t, prow, pcol)


def _ssd_bwd(xa, rest3, dtr_t, prow, pcol, s_in_all, dy):
    bsz, seq, _ = xa.shape
    nc = seq // BLK

    def body(x_ref, bm_ref, cm_ref, dt_ref, dtt_ref, prow_ref, pcol_ref, sin_ref, dy_ref,
             dxa_ref, ddt_ref, ddtt_ref, gprow_ref, gpcol_ref, ds_ref, drt_ref):
        first = jnp.logical_and(pl.program_id(0) == 0, pl.program_id(1) == 0)

        @pl.when(first)
        def _():
            gprow_ref[...] = jnp.zeros_like(gprow_ref)
            gpcol_ref[...] = jnp.zeros_like(gpcol_ref)

        @pl.when(pl.program_id(1) == 0)
        def _():
            ds_ref[...] = jnp.zeros_like(ds_ref)

        drt_ref[...] = jnp.zeros_like(drt_ref)
        d = _ssd_decays(dt_ref[0], dtt_ref[0], prow_ref, pcol_ref)
        e_all, f_all, ecl_b = _ssd_common(d)
        d_c = jnp.zeros((BLK, BLK), F32)
        d_dtc = jnp.zeros((BLK, BLK), F32)
        d_clast_col = jnp.zeros((BLK, 1), F32)
        skip_rows = lax.broadcasted_iota(jnp.int32, (8, GROUP_W), 0) == P_DSKIP
        for g in range(SSM_GROUPS):
            lanes = pl.ds(g * D_STATE, D_STATE)
            bg = bm_ref[0, :, lanes].astype(BF16)
            cg = cm_ref[0, :, lanes].astype(BF16)
            g_mat = _nt(cg, bg)
            q = _ssd_group(g, d, e_all, f_all, ecl_b, x_ref, prow_ref)
            s_in = sin_ref[0, 0, g]
            s_in16 = s_in.astype(BF16)
            d_y = dy_ref[0, :, pl.ds(g * GROUP_W, GROUP_W)]
            d_y16 = d_y.astype(BF16)
            d_so = ds_ref[g]
            d_so16 = d_so.astype(BF16)
            d_g = jnp.zeros((BLK, BLK), F32)
            ws, d_xds = [], []
            for j in range(HEADS_PER_SSM_GROUP):
                h = g * HEADS_PER_SSM_GROUP + j
                head = slice(j * HEAD_DIM, (j + 1) * HEAD_DIM)
                lm, m = _ssd_mask_decay(d, h, g_mat)
                d_m = _nt(d_y16[:, head], q["xd16"][:, head])
                d_xds.append(_tn(m.astype(BF16), d_y16[:, head]))
                w = d_m * m
                d_g = d_g + d_m * lm
                drt_ref[h:h + 1, :] = -jnp.sum(w, axis=0, keepdims=True)
                ws.append(w)
            d_c = d_c + _sel_dot(jnp.concatenate(ws, axis=1), _select_matrix((HEADS_PER_SSM_GROUP * BLK, BLK), g, 1, BLK))
            d_xd = jnp.concatenate(d_xds, axis=1)
            d_g16 = d_g.astype(BF16)
            qmat = _nt(cg, s_in16)
            d_q16 = (d_y * q["e_g"]).astype(BF16)
            d_cg = jnp.dot(d_q16, s_in16, preferred_element_type=F32) + jnp.dot(d_g16, bg, preferred_element_type=F32)
            d_sin = _tn(d_q16, cg) + d_so * q["sc_g"]
            d_clast_col = d_clast_col + jnp.sum(_sel_dot(q["spread"], d_so * s_in * q["sc_g"]), axis=1, keepdims=True)
            d_xf = _nt(bg, d_so16)
            d_bg = jnp.dot(q["xf16"], d_so16, preferred_element_type=F32) + _tn(d_g16, cg)
            d_xd = d_xd + d_xf * q["f_g"]
            r_e = _sel_dot(d_y * qmat * q["e_g"], q["gather"])
            r_f = _sel_dot(d_xf * q["xd"] * q["f_g"], q["gather"])
            d_c = d_c + r_e - r_f + jnp.where(d["last_row"], jnp.sum(r_f, axis=0, keepdims=True), 0.0)
            d_dtc = d_dtc + _sel_dot(d_xd * q["xg"], q["gather"])
            skip_sum = jnp.where(skip_rows, jnp.sum(d_y * q["xg"], axis=0, keepdims=True), 0.0)
            gprow_ref[...] += _sel_dot(skip_sum, q["gather"])
            dxa_ref[0, :, pl.ds(g * GROUP_W, GROUP_W)] = q["dsk_g"] * d_y + d_xd * q["dt_g"]
            dxa_ref[0, :, pl.ds(D_INNER + g * D_STATE, D_STATE)] = d_bg
            dxa_ref[0, :, pl.ds(D_INNER + (SSM_GROUPS + g) * D_STATE, D_STATE)] = d_cg
            ds_ref[g] = d_sin
        drt_ref[:, BLK - 1:BLK] += d_clast_col
        d_a = _dot_hi(d["tri_u"], d_c)
        d_pre = (d_dtc + d_a * d["ah_row"]) * _sigmoid(d["pre"])
        ddt_ref[0] = d_pre
        gprow_ref[P_DTB:P_DTB + 1, :] += jnp.sum(d_pre, axis=0, keepdims=True)
        gprow_ref[P_ALOG:P_ALOG + 1, :] += jnp.sum(d_a * d["dt"], axis=0, keepdims=True) * d["ah_row"]
        d_at = _dot_hi(drt_ref[...], d["tri"])
        d_pre_t = d_at * d["ah_col"] * _sigmoid(d["pre_t"])
        ddtt_ref[0] = d_pre_t
        gpcol_ref[:, P_DTB:P_DTB + 1] += jnp.sum(d_pre_t, axis=1, keepdims=True)
        gpcol_ref[:, P_ALOG:P_ALOG + 1] += jnp.sum(d_at * d["dt_t"], axis=1, keepdims=True) * d["ah_col"]

    x_spec, bm_spec, cm_spec, dt_spec, dtt_spec, prow_spec, pcol_spec, _, y_spec = _ssd_specs(nc, True)
    return pl.pallas_call(
        body, name="ssd_bwd",
        grid=(bsz, nc),
        in_specs=[x_spec, bm_spec, cm_spec, dt_spec, dtt_spec, prow_spec, pcol_spec, _ssd_state_spec(nc, True), y_spec],
        out_specs=[pl.BlockSpec((1, BLK, CONV_DIM), lambda b, c: (b, nc - 1 - c, 0)),
                   pl.BlockSpec((1, BLK, BLK), lambda b, c: (b, nc - 1 - c, 0)),
                   pl.BlockSpec((1, BLK, BLK), lambda b, c: (b, 0, nc - 1 - c)),
                   prow_spec, pcol_spec],
        out_shape=[jax.ShapeDtypeStruct((bsz, seq, CONV_DIM), F32),
                   jax.ShapeDtypeStruct((bsz, seq, BLK), F32),
                   jax.ShapeDtypeStruct((bsz, BLK, seq), F32),
                   jax.ShapeDtypeStruct((8, BLK), F32),
                   jax.ShapeDtypeStruct((BLK, 8), F32)],
        scratch_shapes=[pltpu.VMEM((SSM_GROUPS, GROUP_W, D_STATE), F32), pltpu.VMEM((BLK, BLK), F32)],
        compiler_params=_params("arbitrary", "arbitrary"),
    )(xa, xa, xa, rest3, dtr_t, prow, pcol, s_in_all, dy)


ROW_TILE = 256
CMB_COLS = 256
RMS_COLS = D_INNER // SSM_GROUPS


ROW_CHUNK = 32
WIDE_ROW_CHUNK = 16


def _row_chunks(fn, chunk=ROW_CHUNK, n_rows=ROW_TILE):
    def step(i, carry):
        fn(pl.ds(pl.multiple_of(i * chunk, chunk), chunk))
        return carry
    lax.fori_loop(0, n_rows // chunk, step, 0, unroll=8)


def _combine_weights(l_refs, rows):
    ls = [r[rows, :] for r in l_refs]
    mx = jnp.maximum(jnp.maximum(ls[0], ls[1]), ls[2])
    es = [jnp.exp(l - mx) for l in ls]
    inv = 1.0 / (es[0] + es[1] + es[2])
    return [e * inv for e in es]


CMB_ROWS = 1024


def _combine_specs():
    a = pl.BlockSpec((CMB_ROWS, CMB_COLS), lambda i, j: (i, j))
    gatt = pl.BlockSpec((CMB_ROWS, CMB_COLS), lambda i, j: (i, R_GATT // CMB_COLS + j))
    return a, gatt


def _combine_fwd(outs, lses, rest):
    t = rest.shape[0]

    def body(o0, o1, o2, l0, l1, l2, ga_ref, oa_ref):
        def chunk(rows):
            ws = _combine_weights((l0, l1, l2), rows)
            o = ws[0] * o0[rows, :] + ws[1] * o1[rows, :] + ws[2] * o2[rows, :]
            oa_ref[rows, :] = (o * _silu_and_grad(ga_ref[rows, :])[0]).astype(BF16)
        _row_chunks(chunk, ROW_CHUNK, CMB_ROWS)

    a, gatt = _combine_specs()
    return pl.pallas_call(
        body, name="combine_fwd",
        grid=(t // CMB_ROWS, GROUP_COLS // CMB_COLS),
        in_specs=[a] * 6 + [gatt],
        out_specs=a,
        out_shape=jax.ShapeDtypeStruct((t, GROUP_COLS), BF16),
        compiler_params=_params("parallel", "parallel"),
    )(*outs, *lses, rest)


def _combine_bwd(outs, lses, rest, d_oa, d_hcat):
    t = rest.shape[0]

    def body(o0, o1, o2, l0, l1, l2, ga_ref, doa_ref, _, do0, do1, do2, dd0, dd1, dd2, dga_ref):
        def chunk(rows):
            ws = _combine_weights((l0, l1, l2), rows)
            o = ws[0] * o0[rows, :] + ws[1] * o1[rows, :] + ws[2] * o2[rows, :]
            sg, dsg = _silu_and_grad(ga_ref[rows, :])
            d_oa_v = doa_ref[rows, :]
            d_o = d_oa_v * sg
            dga_ref[rows, :] = (d_oa_v * o * dsg).astype(BF16)
            for w, do_ref, dd_ref in zip(ws, (do0, do1, do2), (dd0, dd1, dd2)):
                d_out = w * d_o
                do_ref[rows, :] = d_out
                dd_ref[rows, :] = d_out * o
        _row_chunks(chunk, ROW_CHUNK, CMB_ROWS)

    a, gatt = _combine_specs()
    s32 = jax.ShapeDtypeStruct((t, GROUP_COLS), F32)
    return pl.pallas_call(
        body, name="combine_bwd",
        grid=(t // CMB_ROWS, GROUP_COLS // CMB_COLS),
        in_specs=[a] * 6 + [gatt, a, pl.BlockSpec(memory_space=pl.ANY)],
        out_specs=[a] * 6 + [gatt],
        out_shape=[s32, s32, s32, s32, s32, s32, jax.ShapeDtypeStruct(d_hcat.shape, d_hcat.dtype)],
        input_output_aliases={8: 6},
        compiler_params=_params("parallel", "parallel"),
    )(*outs, *lses, rest, d_oa, d_hcat)


def _gatenorm_fwd(y, rest, norm_w):
    t = rest.shape[0]

    def body(y_ref, z_ref, w_ref, o_ref):
        def chunk(rows):
            u = y_ref[rows, :] * _silu_and_grad(z_ref[rows, :])[0]
            rs = lax.rsqrt(jnp.mean(u * u, axis=-1, keepdims=True) + RMS_EPS)
            o_ref[rows, :] = (u * rs * w_ref[...]).astype(BF16)
        _row_chunks(chunk, WIDE_ROW_CHUNK, CMB_ROWS)

    return pl.pallas_call(
        body, name="gatenorm_fwd",
        grid=(t // CMB_ROWS, SSM_GROUPS),
        in_specs=[pl.BlockSpec((CMB_ROWS, RMS_COLS), lambda i, j: (i, j)),
                  pl.BlockSpec((CMB_ROWS, RMS_COLS), lambda i, j: (i, R_Z // RMS_COLS + j)),
                  pl.BlockSpec((1, RMS_COLS), lambda i, j: (0, j))],
        out_specs=pl.BlockSpec((CMB_ROWS, RMS_COLS), lambda i, j: (i, j)),
        out_shape=jax.ShapeDtypeStruct((t, D_INNER), BF16),
        compiler_params=_params("parallel", "parallel"),
    )(y, rest, norm_w)


def _gatenorm_bwd(y, rest, norm_w, d_ys, d_hcat):
    t = rest.shape[0]

    def body(y_ref, z_ref, w_ref, g_ref, _, dy_ref, dz_ref, dw_ref):
        @pl.when(pl.program_id(1) == 0)
        def _():
            dw_ref[...] = jnp.zeros_like(dw_ref)

        def chunk(rows):
            yv = y_ref[rows, :]
            sz, dsz = _silu_and_grad(z_ref[rows, :])
            u = yv * sz
            rs = lax.rsqrt(jnp.mean(u * u, axis=-1, keepdims=True) + RMS_EPS)
            un = u * rs
            g = g_ref[rows, :]
            dw_ref[0:1, :] += jnp.sum(g * un, axis=0, keepdims=True)
            d_un = g * w_ref[...]
            d_u = rs * (d_un - un * jnp.mean(d_un * un, axis=-1, keepdims=True))
            dy_ref[rows, :] = d_u * sz
            dz_ref[rows, :] = (d_u * yv * dsz).astype(BF16)
        _row_chunks(chunk, WIDE_ROW_CHUNK, CMB_ROWS)

    blk = pl.BlockSpec((CMB_ROWS, RMS_COLS), lambda j, i: (i, j))
    z_blk = pl.BlockSpec((CMB_ROWS, RMS_COLS), lambda j, i: (i, R_Z // RMS_COLS + j))
    return pl.pallas_call(
        body, name="gatenorm_bwd",
        grid=(SSM_GROUPS, t // CMB_ROWS),
        in_specs=[blk, z_blk, pl.BlockSpec((1, RMS_COLS), lambda j, i: (0, j)), blk, pl.BlockSpec(memory_space=pl.ANY)],
        out_specs=[blk, z_blk, pl.BlockSpec((8, RMS_COLS), lambda j, i: (0, j))],
        out_shape=[jax.ShapeDtypeStruct((t, D_INNER), F32), jax.ShapeDtypeStruct(d_hcat.shape, d_hcat.dtype),
                   jax.ShapeDtypeStruct((8, D_INNER), F32)],
        input_output_aliases={4: 1},
        compiler_params=_params("parallel", "arbitrary"),
    )(y, rest, norm_w, d_ys, d_hcat)


def _row_specs():
    full = pl.BlockSpec((ROW_TILE, D_MODEL), lambda i: (i, 0))
    vec = pl.BlockSpec((8, D_MODEL), lambda i: (0, 0))
    at = lambda off: pl.BlockSpec((ROW_TILE, D_MODEL), lambda i: (i, off // D_MODEL))
    return full, vec, at


def _merge_fwd(y_a, y_b, rest, b_gate):
    t = rest.shape[0]

    def body(ya_ref, yb_ref, ga_ref, gb_ref, bg_ref, o_ref):
        def chunk(rows):
            sa = _sigmoid(ga_ref[rows, :] + bg_ref[0:1, :])
            sb = _sigmoid(gb_ref[rows, :] + bg_ref[1:2, :])
            o_ref[rows, :] = (sa * ya_ref[rows, :] + sb * yb_ref[rows, :]).astype(BF16)
        _row_chunks(chunk, WIDE_ROW_CHUNK)

    full, vec, at = _row_specs()
    return pl.pallas_call(
        body, name="merge_fwd",
        grid=(t // ROW_TILE,),
        in_specs=[full, full, at(R_GM), at(R_GM + D_MODEL), vec],
        out_specs=full,
        out_shape=jax.ShapeDtypeStruct((t, D_MODEL), BF16),
        compiler_params=_params("parallel"),
    )(y_a, y_b, rest, rest, b_gate)


def _merge_bwd(y_a, y_b, rest, b_gate, d_merged, d_hcat):
    t = rest.shape[0]

    def body(ya_ref, yb_ref, ga_ref, gb_ref, bg_ref, dm_ref, _, dya_ref, dyb_ref, dg_ref, dbg_ref):
        @pl.when(pl.program_id(0) == 0)
        def _():
            dbg_ref[...] = jnp.zeros_like(dbg_ref)

        def chunk(rows):
            dm = dm_ref[rows, :]
            for row, y_ref, g_ref, dy_ref in ((0, ya_ref, ga_ref, dya_ref), (1, yb_ref, gb_ref, dyb_ref)):
                s = _sigmoid(g_ref[rows, :] + bg_ref[row:row + 1, :])
                dy_ref[rows, :] = (dm * s).astype(BF16)
                dg = dm * y_ref[rows, :] * s * (1.0 - s)
                dg_ref[rows, row * D_MODEL:(row + 1) * D_MODEL] = dg.astype(BF16)
                dbg_ref[row:row + 1, :] += jnp.sum(dg, axis=0, keepdims=True)
        _row_chunks(chunk, WIDE_ROW_CHUNK)

    full, vec, at = _row_specs()
    s16 = jax.ShapeDtypeStruct((t, D_MODEL), BF16)
    return pl.pallas_call(
        body, name="merge_bwd",
        grid=(t // ROW_TILE,),
        in_specs=[full, full, at(R_GM), at(R_GM + D_MODEL), vec, full, pl.BlockSpec(memory_space=pl.ANY)],
        out_specs=[full, full, pl.BlockSpec((ROW_TILE, 2 * D_MODEL), lambda i: (i, R_GM // (2 * D_MODEL))), vec],
        out_shape=[s16, s16, jax.ShapeDtypeStruct(d_hcat.shape, d_hcat.dtype), jax.ShapeDtypeStruct((8, D_MODEL), F32)],
        input_output_aliases={6: 2},
        compiler_params=_params("arbitrary"),
    )(y_a, y_b, rest, rest, b_gate, d_merged, d_hcat)


ST_LNG, ST_LNB, ST_BG2, ST_LOSS = 0, 1, 2, 3


def _final(x, mix, pw, rest, b_gate, ln_gb, target):
    t = rest.shape[0]

    def body(x_ref, mix_ref, pw_ref, gp_ref, bg_ref, ln_ref, tgt_ref, dpre_ref, dpre16_ref, dgp_ref, dpw_ref, st_ref):
        @pl.when(pl.program_id(0) == 0)
        def _():
            st_ref[...] = jnp.zeros_like(st_ref)

        def chunk(rows):
            sp = _sigmoid(gp_ref[rows, :] + bg_ref[2:3, :])
            pw = pw_ref[rows, :]
            pre = ALPHA * x_ref[rows, :] + mix_ref[rows, :] + sp * pw
            xc = pre - jnp.mean(pre, axis=-1, keepdims=True)
            rstd = lax.rsqrt(jnp.mean(xc * xc, axis=-1, keepdims=True) + LN_EPS)
            xhat = xc * rstd
            gain = ln_ref[0:1, :]
            err = xhat * gain + ln_ref[1:2, :] - tgt_ref[rows, :]
            d_yo = err * (1.0 / D_MODEL)
            d_xhat = d_yo * gain
            d_pre = rstd * (d_xhat - jnp.mean(d_xhat, axis=-1, keepdims=True)
                            - xhat * jnp.mean(d_xhat * xhat, axis=-1, keepdims=True))
            dpre_ref[rows, :] = d_pre
            dpre16_ref[rows, :] = d_pre.astype(BF16)
            dgp = d_pre * pw * sp * (1.0 - sp)
            dgp_ref[rows, :] = dgp.astype(BF16)
            dpw_ref[rows, :] = (d_pre * sp).astype(BF16)
            st_ref[ST_LNG:ST_LNG + 1, :] += jnp.sum(d_yo * xhat, axis=0, keepdims=True)
            st_ref[ST_LNB:ST_LNB + 1, :] += jnp.sum(d_yo, axis=0, keepdims=True)
            st_ref[ST_BG2:ST_BG2 + 1, :] += jnp.sum(dgp, axis=0, keepdims=True)
            st_ref[ST_LOSS:ST_LOSS + 1, :] += jnp.sum(err * err, axis=0, keepdims=True) * (0.5 / D_MODEL)
        _row_chunks(chunk, WIDE_ROW_CHUNK)

    full, vec, at = _row_specs()
    s16 = jax.ShapeDtypeStruct((t, D_MODEL), BF16)
    return pl.pallas_call(
        body, name="final",
        grid=(t // ROW_TILE,),
        in_specs=[full, full, full, at(R_GPLE), vec, vec, full],
        out_specs=[full, full, at(R_GPLE), full, vec],
        out_shape=[jax.ShapeDtypeStruct((t, D_MODEL), F32), s16, jax.ShapeDtypeStruct((t, HCAT_COLS), BF16), s16,
                   jax.ShapeDtypeStruct((8, D_MODEL), F32)],
        compiler_params=_params("arbitrary"),
    )(x, mix, pw, rest, b_gate, ln_gb, target)


def _mesh_position():
    return lax.axis_index("x"), lax.axis_index("y"), lax.axis_index("c")


def _flip(pos, k):
    x, y, c = pos
    return ((1 - x) if k & 4 else x, (1 - y) if k & 2 else y, (1 - c) if k & 1 else c)


def _linear(pos):
    return 4 * pos[0] + 2 * pos[1] + pos[2]


def _exchange(arrays, scatter, name):
    n = len(arrays)

    def body(*refs):
        ins, outs = refs[:n], refs[n:2 * n]
        send_sems, recv_sems, local_sems = refs[2 * n:]
        me = _mesh_position()
        me_i = _linear(me)

        def src_for(i, dest_i):
            return ins[i].at[dest_i] if scatter[i] else ins[i]

        local = [pltpu.make_async_copy(src_for(i, me_i), outs[i].at[me_i], local_sems.at[i]) for i in range(n)]
        for cp in local:
            cp.start()
        started = []
        for k in range(1, N_DEV):
            peer = _flip(me, k)
            peer_i = _linear(peer)
            for i in range(n):
                sem = i * (N_DEV - 1) + k - 1
                cp = pltpu.make_async_remote_copy(
                    src_ref=src_for(i, peer_i), dst_ref=outs[i].at[me_i], send_sem=send_sems.at[sem],
                    recv_sem=recv_sems.at[sem], device_id=peer, device_id_type=pl.DeviceIdType.MESH)
                cp.start()
                started.append(cp)
        for k in range(1, N_DEV):
            peer = _flip(me, k)
            peer_i = _linear(peer)
            for i in range(n):
                sem = i * (N_DEV - 1) + k - 1
                pltpu.make_async_remote_copy(
                    src_ref=src_for(i, peer_i), dst_ref=outs[i].at[peer_i], send_sem=send_sems.at[sem],
                    recv_sem=recv_sems.at[sem], device_id=peer, device_id_type=pl.DeviceIdType.MESH).wait_recv()
        for cp in started:
            cp.wait_send()
        for cp in local:
            cp.wait()

    any_spec = pl.BlockSpec(memory_space=pl.ANY)
    out_shape = [jax.ShapeDtypeStruct(a.shape if s else (N_DEV,) + a.shape, a.dtype) for a, s in zip(arrays, scatter)]
    return pl.pallas_call(
        body, name=name,
        in_specs=[any_spec] * n,
        out_specs=[any_spec] * n,
        out_shape=out_shape,
        scratch_shapes=[pltpu.SemaphoreType.DMA((n * (N_DEV - 1),)), pltpu.SemaphoreType.DMA((n * (N_DEV - 1),)),
                        pltpu.SemaphoreType.DMA((n,))],
        compiler_params=pltpu.CompilerParams(has_side_effects=True),
    )(*arrays)


N_CHIPS = N_DEV // 2


def _other_chips(x, y):
    return [(1 - x, y), (x, 1 - y), (1 - x, 1 - y)]


def _gather_two_level(arrays, name):
    n = len(arrays)
    per = N_DEV - 1

    def body(*refs):
        ins, outs = refs[:n], refs[n:2 * n]
        send_sems, recv_sems, local_sems = refs[2 * n:]
        x, y, c = _mesh_position()
        me, sibling = (x, y, c), (x, y, 1 - c)
        chips = _other_chips(x, y)

        def copy(i, k, block, to, src=None):
            slot = outs[i].at[_linear(block)]
            return pltpu.make_async_remote_copy(
                src_ref=slot if src is None else src, dst_ref=slot, send_sem=send_sems.at[i * per + k],
                recv_sem=recv_sems.at[i * per + k], device_id=to, device_id_type=pl.DeviceIdType.MESH)

        local = [pltpu.make_async_copy(ins[i], outs[i].at[_linear(me)], local_sems.at[i]) for i in range(n)]
        for cp in local:
            cp.start()
        started = []
        for i in range(n):
            first = [copy(i, 0, me, sibling, src=ins[i])]
            first += [copy(i, 1 + j, me, (*chip, c), src=ins[i]) for j, chip in enumerate(chips)]
            for cp in first:
                cp.start()
            started += first
        for j, chip in enumerate(chips):
            for i in range(n):
                copy(i, 1 + j, (*chip, c), me).wait_recv()
                passed = copy(i, 4 + j, (*chip, c), sibling)
                passed.start()
                started.append(passed)
        for i in range(n):
            copy(i, 0, sibling, me).wait_recv()
            for j, chip in enumerate(chips):
                copy(i, 4 + j, (*chip, 1 - c), me).wait_recv()
        for cp in started:
            cp.wait_send()
        for cp in local:
            cp.wait()

    any_spec = pl.BlockSpec(memory_space=pl.ANY)
    return pl.pallas_call(
        body, name=name,
        in_specs=[any_spec] * n,
        out_specs=[any_spec] * n,
        out_shape=[jax.ShapeDtypeStruct((N_DEV,) + a.shape, a.dtype) for a in arrays],
        scratch_shapes=[pltpu.SemaphoreType.DMA((n * per,)), pltpu.SemaphoreType.DMA((n * per,)), pltpu.SemaphoreType.DMA((n,))],
        compiler_params=pltpu.CompilerParams(has_side_effects=True),
    )(*arrays)


def _pair_reduce(a, name, rows):
    _, r, c = a.shape
    assert r % rows == 0
    n_steps = r // rows
    a5 = a.reshape(N_CHIPS, 2, r, c)
    core = lax.axis_index("c").astype(jnp.int32).reshape(1)

    def body(core_ref, keep_ref, send_ref, o_ref, land, send_sems, recv_sems, credits):
        i = pl.program_id(0)
        slot = i % 2
        x, y, cc = _mesh_position()
        sibling = (x, y, 1 - cc)

        @pl.when(i >= 2)
        def _():
            pl.semaphore_wait(credits.at[slot], 1)

        rdma = pltpu.make_async_remote_copy(
            src_ref=send_ref, dst_ref=land.at[slot], send_sem=send_sems.at[slot], recv_sem=recv_sems.at[slot],
            device_id=sibling, device_id_type=pl.DeviceIdType.MESH)
        rdma.start()
        rdma.wait_recv()
        o_ref[...] = (keep_ref[:, 0].astype(F32) + land[slot, :, 0].astype(F32)).astype(o_ref.dtype)
        rdma.wait_send()

        @pl.when(i + 2 < n_steps)
        def _():
            pl.semaphore_signal(credits.at[slot], inc=1, device_id=sibling, device_id_type=pl.DeviceIdType.MESH)

    grid_spec = pltpu.PrefetchScalarGridSpec(
        num_scalar_prefetch=1,
        grid=(n_steps,),
        in_specs=[pl.BlockSpec((N_CHIPS, 1, rows, c), lambda i, core_ref: (0, core_ref[0], i, 0)),
                  pl.BlockSpec((N_CHIPS, 1, rows, c), lambda i, core_ref: (0, 1 - core_ref[0], i, 0))],
        out_specs=pl.BlockSpec((N_CHIPS, rows, c), lambda i, core_ref: (0, i, 0)),
        scratch_shapes=[pltpu.VMEM((2, N_CHIPS, 1, rows, c), a.dtype), pltpu.SemaphoreType.DMA((2,)),
                        pltpu.SemaphoreType.DMA((2,)), pltpu.SemaphoreType.REGULAR((2,))],
    )
    return pl.pallas_call(
        body, name=name, grid_spec=grid_spec,
        out_shape=jax.ShapeDtypeStruct((N_CHIPS, r, c), a.dtype),
        compiler_params=pltpu.CompilerParams(dimension_semantics=("arbitrary",), vmem_limit_bytes=VMEM_LIMIT_BYTES,
                                             has_side_effects=True),
    )(core, a5, a5)


def _chip_exchange(arrays, name):
    n = len(arrays)
    per = N_CHIPS - 1

    def body(*refs):
        ins, outs = refs[:n], refs[n:2 * n]
        send_sems, recv_sems, local_sems = refs[2 * n:]
        x, y, c = _mesh_position()
        me_q = 2 * x + y
        local = [pltpu.make_async_copy(ins[i].at[me_q], outs[i].at[me_q], local_sems.at[i]) for i in range(n)]
        for cp in local:
            cp.start()
        started = []
        for j, (px, py) in enumerate(_other_chips(x, y)):
            for i in range(n):
                cp = pltpu.make_async_remote_copy(
                    src_ref=ins[i].at[2 * px + py], dst_ref=outs[i].at[me_q], send_sem=send_sems.at[i * per + j],
                    recv_sem=recv_sems.at[i * per + j], device_id=(px, py, c), device_id_type=pl.DeviceIdType.MESH)
                cp.start()
                started.append(cp)
        for j, (px, py) in enumerate(_other_chips(x, y)):
            for i in range(n):
                pltpu.make_async_remote_copy(
                    src_ref=ins[i].at[2 * px + py], dst_ref=outs[i].at[2 * px + py], send_sem=send_sems.at[i * per + j],
                    recv_sem=recv_sems.at[i * per + j], device_id=(px, py, c), device_id_type=pl.DeviceIdType.MESH).wait_recv()
        for cp in started:
            cp.wait_send()
        for cp in local:
            cp.wait()

    any_spec = pl.BlockSpec(memory_space=pl.ANY)
    return pl.pallas_call(
        body, name=name,
        in_specs=[any_spec] * n,
        out_specs=[any_spec] * n,
        out_shape=[jax.ShapeDtypeStruct(a.shape, a.dtype) for a in arrays],
        scratch_shapes=[pltpu.SemaphoreType.DMA((n * per,)), pltpu.SemaphoreType.DMA((n * per,)), pltpu.SemaphoreType.DMA((n,))],
        compiler_params=pltpu.CompilerParams(has_side_effects=True),
    )(*arrays)


def _chip_exchange_start(arrays, name):
    n = len(arrays)
    per = N_CHIPS - 1
    hbm = pl.BlockSpec(memory_space=pltpu.HBM)
    sem = pl.BlockSpec(memory_space=pltpu.SEMAPHORE)

    def body(*refs):
        ins, lands = refs[:n], refs[n:2 * n]
        send_sems, recv_sems = refs[2 * n], refs[2 * n + 1]
        token = refs[-1]
        x, y, c = _mesh_position()
        me_q = 2 * x + y
        for j, (px, py) in enumerate(_other_chips(x, y)):
            for i in range(n):
                pltpu.make_async_remote_copy(
                    src_ref=ins[i].at[2 * px + py], dst_ref=lands[i].at[me_q], send_sem=send_sems.at[i * per + j],
                    recv_sem=recv_sems.at[i * per + j], device_id=(px, py, c), device_id_type=pl.DeviceIdType.MESH).start()
        token[...] = jnp.zeros_like(token)

    buffers = [pltpu.HBM(a.shape, a.dtype) for a in arrays]
    res = pl.pallas_call(
        body, name=name,
        out_shape=(pltpu.SemaphoreType.DMA((n * per,)), pltpu.SemaphoreType.DMA((n * per,)), *buffers, *buffers,
                   jax.ShapeDtypeStruct((8, BLK), F32)),
        in_specs=[hbm] * (2 * n),
        out_specs=(sem, sem, *([hbm] * (2 * n)), pl.BlockSpec(memory_space=pltpu.VMEM)),
        input_output_aliases={i: 2 + i for i in range(2 * n)},
        compiler_params=pltpu.CompilerParams(has_side_effects=pltpu.SideEffectType.DATAFLOW_SIDE_EFFECTING),
    )(*[pltpu.with_memory_space_constraint(a, pltpu.HBM) for a in arrays],
      *[pltpu.with_memory_space_constraint(lax.empty(a.shape, a.dtype), pltpu.HBM) for a in arrays])
    return res[0], res[1], res[2:2 + n], res[2 + n:2 + 2 * n], res[-1]


def _chip_exchange_wait(send_sems, recv_sems, sources, lands, after, name):
    n = len(sources)
    per = N_CHIPS - 1
    hbm = pl.BlockSpec(memory_space=pltpu.HBM)
    sem = pl.BlockSpec(memory_space=pltpu.SEMAPHORE)

    def body(*refs):
        ins, zones = refs[:n], refs[n:2 * n]
        send, recv = refs[2 * n], refs[2 * n + 1]
        x, y, c = _mesh_position()
        for j, (px, py) in enumerate(_other_chips(x, y)):
            for i in range(n):
                cp = pltpu.make_async_remote_copy(
                    src_ref=ins[i].at[2 * px + py], dst_ref=zones[i].at[2 * px + py], send_sem=send.at[i * per + j],
                    recv_sem=recv.at[i * per + j], device_id=(px, py, c), device_id_type=pl.DeviceIdType.MESH)
                cp.wait_send()
                cp.wait_recv()

    buffers = [pltpu.HBM(a.shape, a.dtype) for a in sources]
    res = pl.pallas_call(
        body, name=name,
        out_shape=(*buffers, *buffers),
        in_specs=[hbm] * (2 * n) + [sem, sem, pl.BlockSpec(memory_space=pl.ANY)],
        out_specs=[hbm] * (2 * n),
        input_output_aliases={i: i for i in range(2 * n)},
        compiler_params=pltpu.CompilerParams(has_side_effects=pltpu.SideEffectType.DATAFLOW_SIDE_EFFECTING),
    )(*sources, *lands, send_sems, recv_sems, after)
    return res[:n], res[n:]


def _adam_reduce(parts, w, m, v, name, rows):
    r, c = w.shape
    n_parts = parts.shape[0]
    assert r % rows == 0
    c1 = 1.0 - ADAM_B1 ** ADAM_STEP
    c2 = 1.0 - ADAM_B2 ** ADAM_STEP

    def body(p_ref, w_ref, m_ref, v_ref, g_ref, d_ref, nm_ref, nv_ref):
        g = p_ref[0].astype(F32)
        for s in range(1, n_parts):
            g = g + p_ref[s].astype(F32)
        g_ref[...] = g
        nm = ADAM_B1 * m_ref[...] + (1.0 - ADAM_B1) * g
        nv = ADAM_B2 * v_ref[...] + (1.0 - ADAM_B2) * (g * g)
        nm_ref[...] = nm
        nv_ref[...] = nv
        d_ref[...] = -ADAM_LR * ((nm / c1) / (jnp.sqrt(nv / c2) + ADAM_EPS) + ADAM_WD * w_ref[...])

    blk = pl.BlockSpec((rows, c), lambda i: (i, 0))
    shape = jax.ShapeDtypeStruct((r, c), F32)
    return pl.pallas_call(
        body, name=name,
        grid=(r // rows,),
        in_specs=[pl.BlockSpec((n_parts, rows, c), lambda i: (0, i, 0)), blk, blk, blk],
        out_specs=[blk] * 4,
        out_shape=[shape] * 4,
        compiler_params=_params("parallel"),
    )(parts, w, m, v)


def _lane_total(rows8):
    def body(a_ref, o_ref):
        o_ref[...] = _sum_all(a_ref[...])

    return pl.pallas_call(body, name="loss_total", out_shape=jax.ShapeDtypeStruct((1, 1), F32))(rows8)


def _permute_w_in(w):
    rows = w.shape[0]
    n_pairs = N_GROUPS * HEAD_PAIRS
    qkv = w[:, :V_END].reshape(rows, 3, n_pairs, BLK).transpose(0, 2, 1, 3).reshape(rows, V_END)
    return jnp.concatenate(
        [w[:, DT_END:GMERGE_END], w[:, GATT_END:Z_END], w[:, Z_END:XBC_END], w[:, GMERGE_END:], w[:, V_END:GATT_END],
         w[:, XBC_END:DT_END], jnp.zeros((rows, QKV_OFF - R_DT - SSM_HEADS), w.dtype), qkv], axis=1)


def _unpermute_w_in(g):
    rows = g.shape[0]
    n_pairs = N_GROUPS * HEAD_PAIRS
    qkv = g[:, QKV_OFF:].reshape(rows, n_pairs, 3, BLK).transpose(0, 2, 1, 3).reshape(rows, V_END)
    return jnp.concatenate(
        [qkv, g[:, R_GATT:R_GATT + GROUP_COLS], g[:, R_Z:R_Z + D_INNER], g[:, R_XBC:R_XBC + CONV_DIM],
         g[:, R_DT:R_DT + SSM_HEADS], g[:, R_GM:R_GM + 2 * D_MODEL], g[:, R_GPLE:R_GPLE + D_MODEL]], axis=1)


def _column_runs():
    runs = [(DT_END, R_GM, 2 * D_MODEL), (GATT_END, R_Z, D_INNER), (Z_END, R_XBC, CONV_DIM), (GMERGE_END, R_GPLE, D_MODEL),
            (V_END, R_GATT, GROUP_COLS), (XBC_END, R_DT, SSM_HEADS)]
    for pair in range(N_GROUPS * HEAD_PAIRS):
        for part in range(3):
            runs.append((part * ATT_QKV + pair * BLK, QKV_OFF + (pair * 3 + part) * BLK, BLK))
    return runs


RELAYOUT_ROWS = 128


def _assemble_w_in(gathered):
    rows = gathered.shape[1]

    def pieces(ref_col, width):
        out = []
        while width > 0:
            s, o = divmod(ref_col, IN_SHARD)
            w = min(width, IN_SHARD - o)
            out.append((s, o, w))
            ref_col, width = ref_col + w, width - w
        return out

    blocks = {}
    for ref_col, col, width in _column_runs():
        for off in range(0, width, BLK):
            blocks[(col + off) // BLK] = (ref_col + off, min(BLK, width - off))

    def body(g_ref, o_ref):
        for blk in range(HCAT_COLS // BLK):
            lanes = pl.ds(blk * BLK, BLK)
            if blk not in blocks:
                o_ref[:, lanes] = jnp.zeros((RELAYOUT_ROWS, BLK), o_ref.dtype)
                continue
            ref_col, width = blocks[blk]
            parts = [g_ref[s, :, pl.ds(o, w)] for s, o, w in pieces(ref_col, width)]
            if width < BLK:
                parts.append(jnp.zeros((RELAYOUT_ROWS, BLK - width), o_ref.dtype))
            o_ref[:, lanes] = parts[0] if len(parts) == 1 else jnp.concatenate(parts, axis=1)

    return pl.pallas_call(
        body, name="assemble_w_in",
        grid=(rows // RELAYOUT_ROWS,),
        in_specs=[pl.BlockSpec((N_DEV, RELAYOUT_ROWS, IN_SHARD), lambda i: (0, i, 0))],
        out_specs=pl.BlockSpec((RELAYOUT_ROWS, HCAT_COLS), lambda i: (i, 0)),
        out_shape=jax.ShapeDtypeStruct((rows, HCAT_COLS), gathered.dtype),
        compiler_params=_params("parallel"),
    )(gathered)


def _scatter_w_in_grad(g):
    rows = g.shape[0]
    runs = sorted(_column_runs())

    def pieces(shard):
        lo, hi = shard * IN_SHARD, (shard + 1) * IN_SHARD
        out = []
        for ref_col, col, width in runs:
            a, b = max(lo, ref_col), min(hi, ref_col + width)
            if a < b:
                out.append((col + a - ref_col, b - a))
        return out

    def body(g_ref, o_ref):
        for shard in range(N_DEV):
            o_ref[shard] = jnp.concatenate([g_ref[:, pl.ds(col, w)] for col, w in pieces(shard)], axis=1)

    return pl.pallas_call(
        body, name="scatter_w_in_grad",
        grid=(rows // RELAYOUT_ROWS,),
        in_specs=[pl.BlockSpec((RELAYOUT_ROWS, HCAT_COLS), lambda i: (i, 0))],
        out_specs=pl.BlockSpec((N_DEV, RELAYOUT_ROWS, IN_SHARD), lambda i: (0, i, 0)),
        out_shape=jax.ShapeDtypeStruct((N_DEV, rows, IN_SHARD), g.dtype),
        compiler_params=_params("parallel"),
    )(g)


SMALL_ROWS = 80
_SMALL_LAYOUT = (("conv_b", CONV_DIM), ("dt_bias", BLK), ("a_log", BLK), ("d_skip", BLK), ("ssm_norm_w", D_INNER),
                 ("ln_g", D_MODEL), ("ln_b", D_MODEL), ("rel_bias", NUM_BUCKETS * ATT_HEADS), ("loss", D_MODEL))


def _pack_small(vals):
    flat = []
    for name, width in _SMALL_LAYOUT:
        v = vals.get(name)
        v = jnp.zeros((width,), F32) if v is None else v.reshape(-1).astype(F32)
        flat.append(jnp.pad(v, (0, width - v.shape[0])))
    flat = jnp.concatenate(flat)
    return jnp.pad(flat, (0, SMALL_ROWS * BLK - flat.shape[0])).reshape(SMALL_ROWS, BLK)


def _unpack_small(packed):
    flat = packed.reshape(-1)
    out, pos = {}, 0
    for name, width in _SMALL_LAYOUT:
        out[name] = flat[pos:pos + width]
        pos += width
    for name in ("dt_bias", "a_log", "d_skip"):
        out[name] = out[name][:SSM_HEADS]
    out["rel_bias"] = out["rel_bias"].reshape(NUM_BUCKETS, ATT_HEADS)
    return out


def _pack_cols(b_gate_part, conv_w_part):
    return jnp.concatenate([jnp.pad(b_gate_part, ((0, 5), (0, 0))), jnp.pad(conv_w_part, ((0, 4), (0, 0)))], axis=1)


def _pack_cols_all(b_gate_full, conv_w_full):
    bg = b_gate_full.reshape(3, N_DEV, BLK).transpose(1, 0, 2)
    cw = conv_w_full.reshape(CONV_WIDTH, N_DEV, CONV_DIM // N_DEV).transpose(1, 0, 2)
    return jnp.concatenate([jnp.pad(bg, ((0, 0), (0, 5), (0, 0))), jnp.pad(cw, ((0, 0), (0, 4), (0, 0)))], axis=2)


def _unpack_cols_all(packed):
    bg = packed[:, :3, :BLK].transpose(1, 0, 2).reshape(3, D_MODEL)
    cw = packed[:, :CONV_WIDTH, BLK:].transpose(1, 0, 2).reshape(CONV_WIDTH, CONV_DIM)
    return bg, cw


def _local_step(x, p, target, wp16, wb16, wo16, wple16, b_gate, conv_w, small, start_exchange):
    bsz, seq, _ = x.shape
    t = bsz * seq
    x2 = x.reshape(t, D_MODEL)
    x16 = x2.astype(BF16)
    p16 = p.reshape(t, PLE_DIM).astype(BF16)
    tgt2 = target.reshape(t, D_MODEL)
    b_gate8 = jnp.pad(b_gate, ((0, 5), (0, 0)))
    ln_gb = jnp.pad(jnp.stack([small["ln_g"], small["ln_b"]]), ((0, 6), (0, 0)))
    conv_b = small["conv_b"].reshape(1, CONV_DIM)
    norm_w = small["ssm_norm_w"].reshape(1, D_INNER)
    pad_heads = lambda v: jnp.pad(v, (0, BLK - SSM_HEADS))
    prow = jnp.pad(jnp.stack([pad_heads(small["dt_bias"]), pad_heads(small["a_log"]), pad_heads(small["d_skip"])]), ((0, 5), (0, 0)))
    pcol = prow.T
    wa16, wbb16 = wb16[:GROUP_COLS], wb16[GROUP_COLS:]

    rest = _matmul(x16, wp16, mode="nn", out_dtype=F32, name="inproj", tm=512, tn=2304, tk=D_MODEL, n_outer=True)
    rest3 = rest.reshape(bsz, seq, HCAT_COLS)
    biases, onehots, outs, lses = [], [], [], []
    for g, (_, dil) in enumerate(DILATED_PATTERNS):
        bias, onehot = _bias_matrix(small["rel_bias"][:, g * HEADS_PER_GROUP:(g + 1) * HEADS_PER_GROUP], dil)
        out3, lse3 = _attn_fwd(rest3, bias, g, dil, f"attn_fwd{g}")
        biases.append(bias), onehots.append(onehot)
        outs.append(out3.reshape(t, GROUP_COLS)), lses.append(lse3.reshape(t, GROUP_COLS))
    oa = _combine_fwd(outs, lses, rest)
    xa = _conv_fwd(rest3, conv_w, conv_b)
    dtr_t = jnp.swapaxes(rest3[:, :, R_DT:R_DT + BLK], 1, 2)
    y, s_in = _ssd_fwd(xa, rest3, dtr_t, prow, pcol)
    y2 = y.reshape(t, D_INNER)
    ys = _gatenorm_fwd(y2, rest, norm_w)
    y_a = _matmul(oa, wa16, mode="nn", out_dtype=F32, name="branch_a", tm=512, tn=D_MODEL, tk=GROUP_COLS)
    y_b = _matmul(ys, wbb16, mode="nn", out_dtype=F32, name="branch_b", tm=512, tn=D_MODEL, tk=D_INNER)
    merged = _merge_fwd(y_a, y_b, rest, b_gate8)
    mix = _matmul(merged, wo16, mode="nn", out_dtype=F32, name="out_proj", tm=512, tn=D_MODEL, tk=D_MODEL)
    pw = _matmul(p16, wple16, mode="nn", out_dtype=F32, name="ple_proj", tm=512, tn=D_MODEL, tk=PLE_DIM)
    d_pre, d_pre16, d_hcat, d_pw, stats = _final(x2, mix, pw, rest, b_gate8, ln_gb, tgt2)

    d_merged = _matmul(d_pre16, wo16, mode="nt", out_dtype=F32, name="d_merged", tm=512, tn=D_MODEL, tk=D_MODEL)
    g_w_out = _matmul(merged, d_pre16, mode="tn", out_dtype=BF16, name="g_w_out", tm=512, tn=D_MODEL, tk=1024)
    g_w_ple = _matmul(p16, d_pw, mode="tn", out_dtype=BF16, name="g_w_ple", tm=PLE_DIM, tn=D_MODEL, tk=1024)
    d_ya, d_yb, d_hcat, dbg01 = _merge_bwd(y_a, y_b, rest, b_gate8, d_merged, d_hcat)
    d_oa = _matmul(d_ya, wa16, mode="nt", out_dtype=F32, name="d_oa", tm=512, tn=GROUP_COLS, tk=D_MODEL)
    d_ys = _matmul(d_yb, wbb16, mode="nt", out_dtype=F32, name="d_ys", tm=512, tn=1024, tk=D_MODEL)
    g_wa = _matmul(oa, d_ya, mode="tn", out_dtype=BF16, name="g_w_branch_a", tm=GROUP_COLS, tn=D_MODEL, tk=1024)
    g_wb = _matmul(ys, d_yb, mode="tn", out_dtype=BF16, name="g_w_branch_b", tm=512, tn=D_MODEL, tk=1024)
    d_outs_dd_hc = _combine_bwd(outs, lses, rest, d_oa, d_hcat)
    d_outs, dds, d_hcat = d_outs_dd_hc[:3], d_outs_dd_hc[3:6], d_outs_dd_hc[6]
    d_y, d_hcat, d_nw = _gatenorm_bwd(y2, rest, norm_w, d_ys, d_hcat)
    d_xa, ddt, ddt_t, gprow, gpcol = _ssd_bwd(xa, rest3, dtr_t, prow, pcol, s_in, d_y.reshape(bsz, seq, D_INNER))
    d_hcat3, d_conv = _conv_bwd(rest3, conv_w, conv_b, d_xa, d_hcat.reshape(bsz, seq, HCAT_COLS))
    g_tables = []
    shape3 = (bsz, seq, GROUP_COLS)
    for g, (_, dil) in enumerate(DILATED_PATTERNS):
        d_hcat3, dbias = _attn_bwd(rest3, biases[g], lses[g].reshape(shape3), d_outs[g].reshape(shape3), dds[g].reshape(shape3),
                                   d_hcat3, g, dil, f"attn_bwd{g}")
        g_tables.append(_bias_grad(dbias, onehots[g], f"bias_grad{g}"))
    d_dt = (ddt + jnp.swapaxes(ddt_t, 1, 2)).reshape(t, BLK).astype(BF16)
    d_hcat = lax.dynamic_update_slice(d_hcat3.reshape(t, HCAT_COLS),
                                      jnp.concatenate([d_dt, jnp.zeros((t, QKV_OFF - R_DT - BLK), BF16)], axis=1), (0, R_DT))
    g_wp = _matmul(x16, d_hcat, mode="tn", out_dtype=BF16, name="g_w_in", tm=D_MODEL, tn=2304, tk=1024, n_outer=True)

    grads = dict(
        w_in=_scatter_w_in_grad(g_wp),
        b_gate=jnp.stack([dbg01[0], dbg01[1], stats[ST_BG2]]),
        conv_w=d_conv[:CONV_WIDTH],
        w_branch=jnp.concatenate([g_wa, g_wb], axis=0),
        w_out=g_w_out,
        w_ple=g_w_ple,
    )
    small_grads = dict(
        conv_b=d_conv[CONV_WIDTH],
        dt_bias=gprow[P_DTB, :SSM_HEADS] + gpcol[:SSM_HEADS, P_DTB],
        a_log=gprow[P_ALOG, :SSM_HEADS] + gpcol[:SSM_HEADS, P_ALOG],
        d_skip=gprow[P_DSKIP, :SSM_HEADS],
        ssm_norm_w=d_nw[0],
        ln_g=stats[ST_LNG],
        ln_b=stats[ST_LNB],
        rel_bias=jnp.concatenate(g_tables, axis=1),
        loss=stats[ST_LOSS],
    )
    in_flight, token = start_exchange(grads)
    grad_x = _matmul(d_hcat, wp16, mode="nt", out_dtype=F32, name="grad_x", tm=1024, tn=D_MODEL, tk=2304,
                     add=d_pre, add_scale=ALPHA, after=token)
    return grad_x.reshape(bsz, seq, D_MODEL), in_flight, small_grads


WEIGHT_ORDER = ("w_in", "b_gate", "conv_w", "conv_b", "dt_bias", "a_log", "d_skip", "ssm_norm_w", "w_branch", "w_out",
                "w_ple", "ln_g", "ln_b", "rel_bias")
SMALL_NAMES = ("conv_b", "dt_bias", "a_log", "d_skip", "ssm_norm_w", "ln_g", "ln_b", "rel_bias")


def kernel(x, p, w_in, b_gate, conv_w, conv_b, dt_bias, a_log, d_skip, ssm_norm_w, w_branch, w_out, w_ple, ln_g, ln_b, rel_bias, loss_target, m_w_in, m_b_gate, m_conv_w, m_conv_b, m_dt_bias, m_a_log, m_d_skip, m_ssm_norm_w, m_w_branch, m_w_out, m_w_ple, m_ln_g, m_ln_b, m_rel_bias, v_w_in, v_b_gate, v_conv_w, v_conv_b, v_dt_bias, v_a_log, v_d_skip, v_ssm_norm_w, v_w_branch, v_w_out, v_w_ple, v_ln_g, v_ln_b, v_rel_bias):
    given = dict(w_in=w_in, b_gate=b_gate, conv_w=conv_w, conv_b=conv_b, dt_bias=dt_bias, a_log=a_log, d_skip=d_skip,
                 ssm_norm_w=ssm_norm_w, w_branch=w_branch, w_out=w_out, w_ple=w_ple, ln_g=ln_g, ln_b=ln_b)
    moments_m = dict(w_in=m_w_in, b_gate=m_b_gate, conv_w=m_conv_w, conv_b=m_conv_b, dt_bias=m_dt_bias, a_log=m_a_log,
                     d_skip=m_d_skip, ssm_norm_w=m_ssm_norm_w, w_branch=m_w_branch, w_out=m_w_out, w_ple=m_w_ple,
                     ln_g=m_ln_g, ln_b=m_ln_b)
    moments_v = dict(w_in=v_w_in, b_gate=v_b_gate, conv_w=v_conv_w, conv_b=v_conv_b, dt_bias=v_dt_bias, a_log=v_a_log,
                     d_skip=v_d_skip, ssm_norm_w=v_ssm_norm_w, w_branch=v_w_branch, w_out=v_w_out, w_ple=v_w_ple,
                     ln_g=v_ln_g, ln_b=v_ln_b)
    w = {k: a[0] for k, a in given.items()} | {"rel_bias": rel_bias}
    mm = {k: a[0] for k, a in moments_m.items()} | {"rel_bias": m_rel_bias}
    vv = {k: a[0] for k, a in moments_v.items()} | {"rel_bias": v_rel_bias}

    gathered = _gather_two_level(
        [w["w_in"].astype(BF16), w["w_branch"].astype(BF16), w["w_out"].astype(BF16), w["w_ple"].astype(BF16),
         _pack_cols(w["b_gate"], w["conv_w"])], "gather_weights")
    wp16 = _assemble_w_in(gathered[0])
    wb16 = gathered[1].reshape(BRANCH_ROWS, D_MODEL)
    wo16 = gathered[2].reshape(D_MODEL, D_MODEL)
    wple16 = gathered[3].transpose(1, 0, 2).reshape(PLE_DIM, D_MODEL)
    b_gate_full, conv_w_full = _unpack_cols_all(gathered[4])
    small = {k: w[k] for k in SMALL_NAMES}

    def start_exchange(grads):
        big = [grads["w_in"],
               grads["w_branch"].astype(BF16).reshape(N_DEV, BRANCH_ROWS // N_DEV, D_MODEL),
               grads["w_out"].astype(BF16).reshape(N_DEV, D_MODEL // N_DEV, D_MODEL),
               grads["w_ple"].astype(BF16).reshape(PLE_DIM, N_DEV, BLK).transpose(1, 0, 2)]
        sums = [_pair_reduce(a, f"pair_reduce{i}", rows) for i, (a, rows) in enumerate(zip(big, (128, 176, 128, 256)))]
        send_sems, recv_sems, sources, lands, token = _chip_exchange_start(sums, "chip_exchange_start")
        cols_all = _pack_cols_all(grads["b_gate"], grads["conv_w"])
        return (send_sems, recv_sems, sources, lands, cols_all), token

    grad_x, in_flight, small_grads = _local_step(x, p[0], loss_target, wp16, wb16, wo16, wple16, b_gate_full, conv_w_full,
                                                 small, start_exchange)
    send_sems, recv_sems, sources, lands, cols_all = in_flight
    sources, zones = _chip_exchange_wait(send_sems, recv_sems, sources, lands, grad_x, "chip_exchange_wait")
    me_q = 2 * lax.axis_index("x") + lax.axis_index("y")
    parts = [lax.dynamic_update_slice_in_dim(z, lax.dynamic_slice_in_dim(s, me_q, 1, axis=0), me_q, axis=0)
             for z, s in zip(zones, sources)]
    small_parts = _exchange([cols_all, _pack_small(small_grads)], [True, False], "exchange_small")

    out = {}
    out["w_in"] = _adam_reduce(parts[0], w["w_in"], mm["w_in"], vv["w_in"], "adam_w_in", 128)
    out["w_branch"] = _adam_reduce(parts[1], w["w_branch"], mm["w_branch"], vv["w_branch"], "adam_w_branch", 176)
    out["w_out"] = _adam_reduce(parts[2], w["w_out"], mm["w_out"], vv["w_out"], "adam_w_out", 128)
    out["w_ple"] = _adam_reduce(parts[3], w["w_ple"], mm["w_ple"], vv["w_ple"], "adam_w_ple", 256)
    cols = _adam_reduce(small_parts[0], _pack_cols(w["b_gate"], w["conv_w"]), _pack_cols(mm["b_gate"], mm["conv_w"]),
                        _pack_cols(vv["b_gate"], vv["conv_w"]), "adam_cols", 8)
    out["b_gate"] = [a[:3, :BLK] for a in cols]
    out["conv_w"] = [a[:CONV_WIDTH, BLK:] for a in cols]
    packed = _adam_reduce(small_parts[1], _pack_small({k: w[k] for k in SMALL_NAMES}), _pack_small({k: mm[k] for k in SMALL_NAMES}),
                          _pack_small({k: vv[k] for k in SMALL_NAMES}), "adam_small", SMALL_ROWS)
    unpacked = [_unpack_small(a) for a in packed]
    for k in SMALL_NAMES:
        out[k] = [u[k] for u in unpacked]
    loss_rows = unpacked[0]["loss"].reshape(D_MODEL // BLK, BLK)
    loss = _lane_total(loss_rows).reshape(())

    def shaped(k, a):
        return a if k == "rel_bias" else a[None]

    results = [loss, grad_x]
    for i in range(4):
        results += [shaped(k, out[k][i]) for k in WEIGHT_ORDER]
    return tuple(results)
```
